```python
import jax, jax.numpy as jnp
from jax import lax
import numpy as np

D_MODEL = 1024
BATCH = 8
SEQ = 4096
DEPTH = 2

MIX_WIDTH = D_MODEL
MLA_HEADS = 8
MLA_V = (MIX_WIDTH // 2) // MLA_HEADS
MLA_NOPE = 64
MLA_ROPE = 32
MLA_Q_LORA = D_MODEL // 4
MLA_KV_LORA = D_MODEL // 8
FOX_HEADS = 8
FOX_HEAD_DIM = (MIX_WIDTH // 2) // FOX_HEADS
FOX_W = FOX_HEADS * FOX_HEAD_DIM
IN_SPLITS = (MLA_Q_LORA, MLA_KV_LORA, MLA_ROPE, FOX_W, FOX_W, FOX_W, FOX_HEADS)
IN_COLS = sum(IN_SPLITS)
Q_BLOCK = 128
ROPE_THETA = 10000.0
D_FF = D_MODEL * 7 // 2
N_EXPERTS = 8
TOP_K = 2
EPS = 1e-6

kernel_name = "hybrid_mla_fox_adaln_moe_trunk"


def rms_norm(x, g):
    xf = x.astype(jnp.float32)
    y = xf * lax.rsqrt(jnp.mean(xf * xf, axis=-1, keepdims=True) + EPS)
    return (y * g.astype(jnp.float32)).astype(x.dtype)


def rope(x, positions):
    half = x.shape[-1] // 2
    inv_freq = ROPE_THETA ** (-jnp.arange(half, dtype=jnp.float32) / half)
    ang = positions.astype(jnp.float32)[..., None] * inv_freq
    cos, sin = jnp.cos(ang)[:, :, None, :], jnp.sin(ang)[:, :, None, :]
    xf = x.astype(jnp.float32)
    x1, x2 = xf[..., :half], xf[..., half:]
    return jnp.concatenate([x1 * cos - x2 * sin, x1 * sin + x2 * cos], axis=-1).astype(x.dtype)


def causal_block_attention(score_fn, v):
    B, S, H, dv = v.shape
    n_blocks = S // Q_BLOCK
    kpos = jnp.arange(S)

    def one_block(i):
        start = i * Q_BLOCK
        s = score_fn(start)
        qpos = start + jnp.arange(Q_BLOCK)
        s = jnp.where(kpos[None, :] <= qpos[:, None], s, -jnp.inf)
        p = jax.nn.softmax(s, axis=-1)
        return jnp.einsum('bhqk,bkhd->bqhd', p.astype(v.dtype), v)

    out = lax.map(one_block, jnp.arange(n_blocks))
    return out.transpose(1, 0, 2, 3, 4).reshape(B, S, H, dv)


def hybrid_mixer(h, positions, w_in, q_norm_g, w_uq, kv_norm_g, w_ukv,
                 fox_forget_b, mla_out_g, fox_out_g, w_o):
    B, S, _ = h.shape
    proj = h @ w_in
    cuts = [int(v) for v in np.cumsum(IN_SPLITS)[:-1]]
    c_q, c_kv, k_r, f_q, f_k, f_v, f_logit = jnp.split(proj, cuts, axis=-1)

    q = (rms_norm(c_q, q_norm_g) @ w_uq).reshape(B, S, MLA_HEADS, MLA_NOPE + MLA_ROPE)
    q_nope = q[..., :MLA_NOPE]
    q_rope = rope(q[..., MLA_NOPE:], positions)
    kv = (rms_norm(c_kv, kv_norm_g) @ w_ukv).reshape(B, S, MLA_HEADS, MLA_NOPE + MLA_V)
    k_nope, v_mla = kv[..., :MLA_NOPE], kv[..., MLA_NOPE:]
    k_rope = rope(k_r[:, :, None, :], positions)[:, :, 0, :]
    mla_scale = (MLA_NOPE + MLA_ROPE) ** -0.5

    def mla_scores(start):
        qn = lax.dynamic_slice_in_dim(q_nope, start, Q_BLOCK, axis=1)
        qr = lax.dynamic_slice_in_dim(q_rope, start, Q_BLOCK, axis=1)
        s = (jnp.einsum('bqhd,bkhd->bhqk', qn, k_nope, preferred_element_type=jnp.float32)
             + jnp.einsum('bqhr,bkr->bhqk', qr, k_rope, preferred_element_type=jnp.float32))
        return s * mla_scale

    o_mla = causal_block_attention(mla_scores, v_mla).reshape(B, S, MLA_HEADS * MLA_V)

    fq = f_q.reshape(B, S, FOX_HEADS, FOX_HEAD_DIM)
    fk = f_k.reshape(B, S, FOX_HEADS, FOX_HEAD_DIM)
    fv = f_v.reshape(B, S, FOX_HEADS, FOX_HEAD_DIM)
    log_f = jax.nn.log_sigmoid(f_logit.astype(jnp.float32) + fox_forget_b.astype(jnp.float32))
    F = jnp.cumsum(log_f, axis=1).transpose(0, 2, 1)
    fox_scale = FOX_HEAD_DIM ** -0.5

    def fox_scores(start):
        qb = lax.dynamic_slice_in_dim(fq, start, Q_BLOCK, axis=1)
        Fq = lax.dynamic_slice_in_dim(F, start, Q_BLOCK, axis=2)
        s = jnp.einsum('bqhd,bkhd->bhqk', qb, fk, preferred_element_type=jnp.float32)
        return s * fox_scale + Fq[..., None] - F[:, :, None, :]

    o_fox = causal_block_attention(fox_scores, fv).reshape(B, S, FOX_W)

    o = jnp.concatenate([rms_norm(o_mla, mla_out_g), rms_norm(o_fox, fox_out_g)], axis=-1)
    return o @ w_o


def swiglu(h, w_gate, w_up, w_down):
    return (jax.nn.silu(h @ w_gate) * (h @ w_up)) @ w_down


def moe_swiglu(h, router_w, w_gate, w_up, w_down):
    B, S, D = h.shape
    t = h.reshape(B * S, D)
    logits = (t @ router_w).astype(jnp.float32)
    top_v, top_i = lax.top_k(logits, TOP_K)
    top_w = jax.nn.softmax(top_v, axis=-1)
    gates = jnp.sum(jax.nn.one_hot(top_i, N_EXPERTS, dtype=jnp.float32) * top_w[..., None], axis=1)
    gates = gates.astype(t.dtype)
    y = jnp.zeros_like(t)
    for e in range(N_EXPERTS):
        y = y + gates[:, e:e + 1] * swiglu(t, w_gate[e], w_up[e], w_down[e])
    return y.reshape(B, S, D)


def modulate(h, shift, scale):
    return h * (1 + scale[:, None, :]) + shift[:, None, :]


def setup_inputs(seed: int = 0) -> dict:
    key = jax.random.key(seed)
    ks = jax.random.split(key, 24)
    L, LD, LM = DEPTH, (DEPTH + 1) // 2, DEPTH // 2
    D, F, E = D_MODEL, D_FF, N_EXPERTS

    def nrm(k, shape, scale):
        return jax.random.normal(k, shape, jnp.float32) * scale

    def gain(k, shape):
        return 1.0 + 0.05 * jax.random.normal(k, shape, jnp.float32)

    positions = (jnp.arange(SEQ, dtype=jnp.int32)[None, :]
                 + jax.random.randint(ks[2], (BATCH, 1), 0, 1024, dtype=jnp.int32))
    return {
        "x": nrm(ks[0], (BATCH, SEQ, D), 1.0),
        "c": nrm(ks[1], (BATCH, D), 1.0),
        "positions": positions,
        "ada_w": nrm(ks[3], (L, D, 6 * D), 0.5 * D ** -0.5),
        "ada_b": nrm(ks[4], (L, 6 * D), 0.02),
        "attn_norm_g": gain(ks[5], (L, D)),
        "w_in": nrm(ks[6], (L, D, IN_COLS), D ** -0.5),
        "q_norm_g": gain(ks[7], (L, MLA_Q_LORA)),
        "w_uq": nrm(ks[8], (L, MLA_Q_LORA, MLA_HEADS * (MLA_NOPE + MLA_ROPE)), MLA_Q_LORA ** -0.5),
        "kv_norm_g": gain(ks[9], (L, MLA_KV_LORA)),
        "w_ukv": nrm(ks[10], (L, MLA_KV_LORA, MLA_HEADS * (MLA_NOPE + MLA_V)), MLA_KV_LORA ** -0.5),
        "fox_forget_b": jax.random.uniform(ks[11], (L, FOX_HEADS), jnp.float32, 1.0, 4.0),
        "mla_out_g": gain(ks[12], (L, MLA_HEADS * MLA_V)),
        "fox_out_g": gain(ks[13], (L, FOX_W)),
        "w_o": nrm(ks[14], (L, MIX_WIDTH, D), MIX_WIDTH ** -0.5),
        "ffn_norm_g": gain(ks[15], (L, D)),
        "dense_w_gate": nrm(ks[16], (LD, D, F), D ** -0.5),
        "dense_w_up": nrm(ks[17], (LD, D, F), D ** -0.5),
        "dense_w_down": nrm(ks[18], (LD, F, D), F ** -0.5),
        "router_w": nrm(ks[19], (LM, D, E), D ** -0.5),
        "moe_w_gate": nrm(ks[20], (LM, E, D, F), D ** -0.5),
        "moe_w_up": nrm(ks[21], (LM, E, D, F), D ** -0.5),
        "moe_w_down": nrm(ks[22], (LM, E, F, D), F ** -0.5),
        "final_norm_g": gain(ks[23], (D,)),
    }


def reference(x, c, positions, ada_w, ada_b, attn_norm_g, w_in, q_norm_g, w_uq,
              kv_norm_g, w_ukv, fox_forget_b, mla_out_g, fox_out_g, w_o, ffn_norm_g,
              dense_w_gate, dense_w_up, dense_w_down, router_w, moe_w_gate, moe_w_up,
              moe_w_down, final_norm_g):
    c_act = jax.nn.silu(c)
    for l in range(DEPTH):
        mod = c_act @ ada_w[l] + ada_b[l]
        sh_a, sc_a, g_a, sh_f, sc_f, g_f = jnp.split(mod, 6, axis=-1)
        h = modulate(rms_norm(x, attn_norm_g[l]), sh_a, sc_a)
        mix = hybrid_mixer(h, positions, w_in[l], q_norm_g[l], w_uq[l], kv_norm_g[l],
                           w_ukv[l], fox_forget_b[l], mla_out_g[l], fox_out_g[l], w_o[l])
        x = x + g_a[:, None, :] * mix
        h = modulate(rms_norm(x, ffn_norm_g[l]), sh_f, sc_f)
        j = l // 2
        if l % 2 == 0:
            f = swiglu(h, dense_w_gate[j], dense_w_up[j], dense_w_down[j])
        else:
            f = moe_swiglu(h, router_w[j], moe_w_gate[j], moe_w_up[j], moe_w_down[j])
        x = x + g_f[:, None, :] * f
    return rms_norm(x, final_norm_g)
```

```python
import functools

import numpy as np
import jax
import jax.numpy as jnp
from jax import lax
from jax.experimental import pallas as pl
from jax.experimental.pallas import tpu as pltpu

F32 = jnp.float32
BF16 = jnp.bfloat16

D_MODEL = 1024
MLA_HEADS = 8
MLA_V = 64
MLA_NOPE = 64
MLA_ROPE = 32
MLA_Q_LORA = 256
MLA_KV_LORA = 128
FOX_HEADS = 8
FOX_HEAD_DIM = 64
FOX_W = FOX_HEADS * FOX_HEAD_DIM
ROPE_THETA = 10000.0
D_FF = 3584
N_EXPERTS = 8
EPS = 1e-6
N_HEADS = MLA_HEADS + FOX_HEADS
HALF_ROPE = MLA_ROPE // 2

LANES = 128
VMEM_LIMIT = 56 * 1024 * 1024

ROPE_LO = MLA_NOPE
AUG_LO = FOX_HEAD_DIM
CTRL_LOGIT = 96
CTRL_ONE = 120

C_Q0 = 0
C_KV0 = C_Q0 + MLA_Q_LORA
C_KRA0 = C_KV0 + MLA_KV_LORA
C_KRB0 = C_KRA0 + LANES
C_FQ0 = C_KRB0 + LANES
C_FK0 = C_FQ0 + FOX_W
C_FV0 = C_FK0 + FOX_W
C_END = C_FV0 + FOX_W

NEG_BIG = -1e30


def _rms(v):
    return v * lax.rsqrt(jnp.mean(v * v, axis=-1, keepdims=True) + EPS)


def _sigmoid(v):
    return 1.0 / (1.0 + jnp.exp(-v))


def _split3(v):
    hi = v.astype(BF16).astype(F32)
    r = v - hi
    mid = r.astype(BF16).astype(F32)
    lo = (r - mid).astype(BF16).astype(F32)
    return hi, mid, lo


def _adaln_kernel(c_ref, w_ref, b_ref, o_ref):
    c = c_ref[...]
    ca = (c * _sigmoid(c)).astype(BF16)
    o_ref[0] = jnp.dot(ca, w_ref[0].astype(BF16), preferred_element_type=F32) + b_ref[0]


def _adaln(c, ada_w, ada_b):
    L, D, N = ada_w.shape
    B = c.shape[0]
    tn = 1536
    return pl.pallas_call(
        _adaln_kernel,
        grid=(L, N // tn),
        in_specs=[
            pl.BlockSpec((B, D), lambda l, j: (0, 0)),
            pl.BlockSpec((1, D, tn), lambda l, j: (l, 0, j)),
            pl.BlockSpec((1, 1, tn), lambda l, j: (l, 0, j)),
        ],
        out_specs=pl.BlockSpec((1, B, tn), lambda l, j: (l, 0, j)),
        out_shape=jax.ShapeDtypeStruct((L, B, N), F32),
        compiler_params=pltpu.CompilerParams(
            dimension_semantics=("arbitrary", "arbitrary"), vmem_limit_bytes=VMEM_LIMIT),
    )(c, ada_w, ada_b.reshape(L, 1, N))


def _rope_tab_kernel(pos_ref, freq_ref, cos_ref, sin_ref):
    ang = freq_ref[...] * pos_ref[0]
    cos_ref[0] = jnp.cos(ang)
    sin_ref[0] = jnp.sin(ang)


def _rope_tables(positions):
    B, S = positions.shape
    half = HALF_ROPE
    inv_freq = ROPE_THETA ** (-jnp.arange(half, dtype=F32) / half)
    pos = positions.astype(F32).reshape(B, 1, S)
    cos_t, sin_t = pl.pallas_call(
        _rope_tab_kernel,
        grid=(B,),
        in_specs=[
            pl.BlockSpec((1, 1, S), lambda b: (b, 0, 0)),
            pl.BlockSpec((half, 1), lambda b: (0, 0)),
        ],
        out_specs=[pl.BlockSpec((1, half, S), lambda b: (b, 0, 0))] * 2,
        out_shape=[jax.ShapeDtypeStruct((B, half, S), F32)] * 2,
        compiler_params=pltpu.CompilerParams(dimension_semantics=("arbitrary",)),
    )(pos, inv_freq.reshape(half, 1))
    cos = jnp.transpose(cos_t, (0, 2, 1))
    sin = jnp.transpose(sin_t, (0, 2, 1))
    ones = jnp.ones((B, S, MLA_NOPE), F32)
    z_lo = jnp.zeros((B, S, MLA_NOPE), F32)
    z_hi = jnp.zeros((B, S, LANES - MLA_NOPE - MLA_ROPE), F32)
    ctab = jnp.concatenate([ones, cos, cos, z_hi], axis=-1)
    stab = jnp.concatenate([z_lo, -sin, sin, z_hi], axis=-1)
    return ctab, stab


def _take_cols(w, idx):
    w_ext = jnp.concatenate([w, jnp.zeros((w.shape[0], 1), w.dtype)], axis=1)
    idx = np.where(idx < 0, w.shape[1], idx)
    return w_ext[:, idx]


def _perm_w_in():
    cq = 0
    ckv = cq + MLA_Q_LORA
    kr = ckv + MLA_KV_LORA
    fq = kr + MLA_ROPE
    fk = fq + FOX_W
    fv = fk + FOX_W
    fl = fv + FOX_W
    idx = -np.ones((C_END,), np.int64)
    idx[C_Q0:C_Q0 + MLA_Q_LORA] = cq + np.arange(MLA_Q_LORA)
    idx[C_KV0:C_KV0 + MLA_KV_LORA] = ckv + np.arange(MLA_KV_LORA)
    h = HALF_ROPE
    idx[C_KRA0 + ROPE_LO:C_KRA0 + ROPE_LO + MLA_ROPE] = kr + np.arange(MLA_ROPE)
    idx[C_KRA0 + CTRL_LOGIT:C_KRA0 + CTRL_LOGIT + FOX_HEADS] = fl + np.arange(FOX_HEADS)
    idx[C_KRB0 + ROPE_LO:C_KRB0 + ROPE_LO + h] = kr + h + np.arange(h)
    idx[C_KRB0 + ROPE_LO + h:C_KRB0 + ROPE_LO + MLA_ROPE] = kr + np.arange(h)
    idx[C_FQ0:C_FQ0 + FOX_W] = fq + np.arange(FOX_W)
    idx[C_FK0:C_FK0 + FOX_W] = fk + np.arange(FOX_W)
    idx[C_FV0:C_FV0 + FOX_W] = fv + np.arange(FOX_W)
    return idx


def _perm_w_uq():
    per = MLA_NOPE + MLA_ROPE
    ia = -np.ones((MLA_HEADS * LANES,), np.int64)
    ib = -np.ones((MLA_HEADS * LANES,), np.int64)
    h2 = HALF_ROPE
    for h in range(MLA_HEADS):
        ia[h * LANES:h * LANES + per] = h * per + np.arange(per)
        ib[h * LANES + ROPE_LO:h * LANES + ROPE_LO + h2] = h * per + MLA_NOPE + h2 + np.arange(h2)
        ib[h * LANES + ROPE_LO + h2:h * LANES + ROPE_LO + MLA_ROPE] = h * per + MLA_NOPE + np.arange(h2)
    return ia, ib


def _perm_w_ukv():
    per = MLA_NOPE + MLA_V
    ik = -np.ones((MLA_HEADS * LANES,), np.int64)
    iv = np.zeros((MLA_HEADS * MLA_V,), np.int64)
    for h in range(MLA_HEADS):
        ik[h * LANES:h * LANES + MLA_NOPE] = h * per + np.arange(MLA_NOPE)
        iv[h * MLA_V:(h + 1) * MLA_V] = h * per + MLA_NOPE + np.arange(MLA_V)
    return np.concatenate([ik, iv])


def _aug_placement():
    e = np.zeros((LANES, 2 * FOX_HEADS * LANES), np.float32)
    koff = FOX_HEADS * LANES
    for h in range(FOX_HEADS):
        for p in range(3):
            src = CTRL_LOGIT + 8 * p + h
            e[src, h * LANES + AUG_LO + p] = 1.0
            e[CTRL_ONE, h * LANES + AUG_LO + 3 + p] = -1.0
            e[CTRL_ONE, koff + h * LANES + AUG_LO + p] = 1.0
            e[src, koff + h * LANES + AUG_LO + 3 + p] = 1.0
    return e


def _inproj_kernel(x_ref, mod_ref, g_ref, win_ref, qg_ref, kvg_ref, wuqa_ref, wuqb_ref, wukv_ref,
                   fb_ref, ct_ref, st_ref, tri_ref, eqk_ref,
                   qa_ref, ka_ref, vt_ref, carry_ref, *, ts):
    si = pl.program_id(1)

    @pl.when(si == 0)
    def _():
        carry_ref[...] = jnp.zeros_like(carry_ref)

    x = x_ref[0]
    shift = mod_ref[0, 0:1, :]
    scale = mod_ref[0, 1:2, :]
    h = ((_rms(x) * g_ref[...]) * (1.0 + scale) + shift).astype(BF16)
    proj = jnp.dot(h, win_ref[...], preferred_element_type=F32)

    ctab = ct_ref[0]
    stab = st_ref[0]
    lane = lax.broadcasted_iota(jnp.int32, (ts, LANES), 1)

    c_q = proj[:, C_Q0:C_Q0 + MLA_Q_LORA]
    cqn = (_rms(c_q) * qg_ref[...]).astype(BF16)
    qa = jnp.dot(cqn, wuqa_ref[...], preferred_element_type=F32)
    qb = jnp.dot(cqn, wuqb_ref[...], preferred_element_type=F32)
    mla_scale = (MLA_NOPE + MLA_ROPE) ** -0.5
    for hh in range(MLA_HEADS):
        sl = slice(hh * LANES, (hh + 1) * LANES)
        qa_ref[0, hh] = ((qa[:, sl] * ctab + qb[:, sl] * stab) * mla_scale).astype(BF16)

    c_kv = proj[:, C_KV0:C_KV0 + MLA_KV_LORA]
    ckvn = (_rms(c_kv) * kvg_ref[...]).astype(BF16)
    kv = jnp.dot(ckvn, wukv_ref[...], preferred_element_type=F32)
    kra = proj[:, C_KRA0:C_KRA0 + LANES]
    krb = proj[:, C_KRB0:C_KRB0 + LANES]
    krope = kra * ctab + krb * stab
    for hh in range(MLA_HEADS):
        ka_ref[0, hh] = (kv[:, hh * LANES:(hh + 1) * LANES] + krope).astype(BF16)
    v_mla = kv[:, MLA_HEADS * LANES:]

    ctrl = (lane >= CTRL_LOGIT) & (lane < CTRL_LOGIT + FOX_HEADS)
    fl = kra + fb_ref[...]
    lsig = jnp.minimum(fl, 0.0) - jnp.log1p(jnp.exp(-jnp.abs(fl)))
    lf = jnp.where(ctrl, lsig, 0.0)
    hi, mid, lo = _split3(lf)
    p1 = (hi + pltpu.roll(mid, 8, 1) + pltpu.roll(lo, 16, 1)).astype(BF16)
    cs = jnp.dot(tri_ref[...], p1, preferred_element_type=F32)
    cs = cs + pltpu.roll(cs, LANES - 8, 1) + pltpu.roll(cs, LANES - 16, 1)
    fcum = jnp.where(ctrl, cs, 0.0) + carry_ref[...]
    carry_ref[...] = fcum[ts - 1:ts, :]

    hi, mid, lo = _split3(fcum)
    p2 = hi + pltpu.roll(mid, 8, 1) + pltpu.roll(lo, 16, 1)
    p2 = jnp.where(lane == CTRL_ONE, 1.0, p2).astype(BF16)
    aug = jnp.dot(p2, eqk_ref[...], preferred_element_type=F32)

    fq = proj[:, C_FQ0:C_FQ0 + FOX_W]
    fk = proj[:, C_FK0:C_FK0 + FOX_W]
    low = lane < FOX_HEAD_DIM
    koff = FOX_HEADS * LANES
    for j in range(FOX_HEADS // 2):
        sl = slice(j * LANES, (j + 1) * LANES)
        for src, off, dst in ((fq, 0, qa_ref), (fk, koff, ka_ref)):
            blk = src[:, sl]
            blk_r = pltpu.roll(blk, FOX_HEAD_DIM, 1)
            h0 = 2 * j
            a0 = aug[:, off + h0 * LANES:off + (h0 + 1) * LANES]
            a1 = aug[:, off + (h0 + 1) * LANES:off + (h0 + 2) * LANES]
            dst[0, MLA_HEADS + h0] = jnp.where(low, blk, a0).astype(BF16)
            dst[0, MLA_HEADS + h0 + 1] = jnp.where(low, blk_r, a1).astype(BF16)

    fv = proj[:, C_FV0:C_FV0 + FOX_W]
    vt_ref[0, 0, 0:MLA_HEADS * MLA_V, :] = v_mla.T.astype(BF16)
    vt_ref[0, 0, MLA_HEADS * MLA_V:, :] = fv.T.astype(BF16)


def _inproj(x, mod_l, g, win, qg, kvg, wuqa, wuqb, wukv, fb_row, ctab, stab, tri, eqk, *, ts):
    B, S, D = x.shape
    nst = S // ts
    const2 = lambda b, s: (0, 0)
    kern = functools.partial(_inproj_kernel, ts=ts)
    return pl.pallas_call(
        kern,
        grid=(B, nst),
        in_specs=[
            pl.BlockSpec((1, ts, D), lambda b, s: (b, s, 0)),
            pl.BlockSpec((1, 6, D), lambda b, s: (b, 0, 0)),
            pl.BlockSpec((1, D), const2),
            pl.BlockSpec(win.shape, const2),
            pl.BlockSpec(qg.shape, const2),
            pl.BlockSpec(kvg.shape, const2),
            pl.BlockSpec(wuqa.shape, const2),
            pl.BlockSpec(wuqb.shape, const2),
            pl.BlockSpec(wukv.shape, const2),
            pl.BlockSpec(fb_row.shape, const2),
            pl.BlockSpec((1, ts, LANES), lambda b, s: (b, s, 0)),
            pl.BlockSpec((1, ts, LANES), lambda b, s: (b, s, 0)),
            pl.BlockSpec(tri.shape, const2),
            pl.BlockSpec(eqk.shape, const2),
        ],
        out_specs=[
            pl.BlockSpec((1, N_HEADS, ts, LANES), lambda b, s: (b, 0, s, 0)),
            pl.BlockSpec((1, N_HEADS, ts, LANES), lambda b, s: (b, 0, s, 0)),
            pl.BlockSpec((1, 1, N_HEADS * MLA_V, ts), lambda b, s: (b, s, 0, 0)),
        ],
        out_shape=[
            jax.ShapeDtypeStruct((B, N_HEADS, S, LANES), BF16),
            jax.ShapeDtypeStruct((B, N_HEADS, S, LANES), BF16),
            jax.ShapeDtypeStruct((B, nst, N_HEADS * MLA_V, ts), BF16),
        ],
        scratch_shapes=[pltpu.VMEM((1, LANES), F32)],
        compiler_params=pltpu.CompilerParams(
            dimension_semantics=("arbitrary", "arbitrary"), vmem_limit_bytes=VMEM_LIMIT),
    )(x, mod_l, g, win, qg, kvg, wuqa, wuqb, wukv, fb_row, ctab, stab, tri, eqk)


def _attn_kernel(q_ref, k_ref, vt_ref, o_ref, *, tq, tk, hp):
    qi = pl.program_id(2)
    q0 = qi * tq
    n_full = (q0 + 1) // tk
    n_end = (q0 + tq + tk - 1) // tk
    qs = [q_ref[0, h] for h in range(hp)]
    dv = MLA_V

    def tile(j, carry, masked):
        out = []
        for h in range(hp):
            m, l, acc = carry[h]
            kt = k_ref[0, h, pl.ds(pl.multiple_of(j * tk, tk), tk), :]
            st = lax.dot_general(kt, qs[h], (((1,), (1,)), ((), ())),
                                 preferred_element_type=F32)
            if masked:
                kpos = j * tk + lax.broadcasted_iota(jnp.int32, (tk, tq), 0)
                qpos = q0 + lax.broadcasted_iota(jnp.int32, (tk, tq), 1)
                st = jnp.where(kpos <= qpos, st, NEG_BIG)
            m_new = jnp.maximum(m, jnp.max(st, axis=0, keepdims=True))
            alpha = jnp.exp(m - m_new)
            p = jnp.exp(st - m_new)
            l_new = alpha * l + jnp.sum(p, axis=0, keepdims=True)
            vt = vt_ref[0, j, h * dv:(h + 1) * dv, :]
            acc_new = alpha * acc + jnp.dot(vt, p.astype(BF16), preferred_element_type=F32)
            out.append((m_new, l_new, acc_new))
        return tuple(out)

    init = tuple((jnp.full((1, tq), NEG_BIG, F32), jnp.zeros((1, tq), F32),
                  jnp.zeros((dv, tq), F32)) for _ in range(hp))
    carry = lax.fori_loop(0, n_full, lambda j, c: tile(j, c, False), init)
    carry = lax.fori_loop(n_full, n_end, lambda j, c: tile(j, c, True), carry)
    for h in range(hp):
        _, l, acc = carry[h]
        o_ref[0, h * dv:(h + 1) * dv, :] = acc / l


def _attention(qa, ka, vt, *, tq, tk, hp):
    B, H, S, _ = qa.shape
    nkt = vt.shape[1]
    assert S % tq == 0 and nkt * tk == S and H % hp == 0
    kern = functools.partial(_attn_kernel, tq=tq, tk=tk, hp=hp)
    return pl.pallas_call(
        kern,
        grid=(B, H // hp, S // tq),
        in_specs=[
            pl.BlockSpec((1, hp, tq, LANES), lambda b, g, i: (b, g, i, 0)),
            pl.BlockSpec((1, hp, S, LANES), lambda b, g, i: (b, g, 0, 0)),
            pl.BlockSpec((1, nkt, hp * MLA_V, tk), lambda b, g, i: (b, 0, g, 0)),
        ],
        out_specs=pl.BlockSpec((1, hp * MLA_V, tq), lambda b, g, i: (b, g, i)),
        out_shape=jax.ShapeDtypeStruct((B, H * MLA_V, S), F32),
        compiler_params=pltpu.CompilerParams(
            dimension_semantics=("arbitrary", "arbitrary", "arbitrary"),
            vmem_limit_bytes=VMEM_LIMIT),
    )(qa, ka, vt)


def _top2_gates(logits, lane):
    lg = jnp.where(lane < N_EXPERTS, logits, NEG_BIG)
    m1 = jnp.max(lg, axis=1, keepdims=True)
    i1 = jnp.min(jnp.where(lg == m1, lane, LANES), axis=1, keepdims=True)
    lg2 = jnp.where(lane == i1, NEG_BIG, lg)
    m2 = jnp.max(lg2, axis=1, keepdims=True)
    i2 = jnp.min(jnp.where(lg2 == m2, lane, LANES), axis=1, keepdims=True)
    e2 = jnp.exp(m2 - m1)
    den = 1.0 + e2
    return jnp.where(lane == i1, 1.0 / den, 0.0) + jnp.where(lane == i2, e2 / den, 0.0)


def _outproj_kernel(*refs, ts, with_router):
    if with_router:
        (ot_ref, x_ref, mod_ref, mg_ref, fg_ref, wo_ref, ng_ref, rw_ref,
         x1_ref, h2_ref, gates_ref) = refs
    else:
        ot_ref, x_ref, mod_ref, mg_ref, fg_ref, wo_ref, ng_ref, x1_ref, h2_ref = refs
    half = MLA_HEADS * MLA_V
    om = ot_ref[0, 0:half, :].T
    of = ot_ref[0, half:, :].T
    on = jnp.concatenate([_rms(om) * mg_ref[...], _rms(of) * fg_ref[...]], axis=1).astype(BF16)
    mix = jnp.dot(on, wo_ref[...], preferred_element_type=F32)
    x1 = x_ref[0] + mod_ref[0, 2:3, :] * mix
    x1_ref[0] = x1
    h2 = ((_rms(x1) * ng_ref[...]) * (1.0 + mod_ref[0, 4:5, :]) + mod_ref[0, 3:4, :]).astype(BF16)
    h2_ref[0] = h2
    if with_router:
        logits = jnp.dot(h2, rw_ref[...], preferred_element_type=F32)
        lane = lax.broadcasted_iota(jnp.int32, (ts, LANES), 1)
        gates_ref[0] = _top2_gates(logits, lane)


def _outproj(ot, x, mod_l, mg, fg, wo, ng, rw, *, ts):
    B, S, D = x.shape
    with_router = rw is not None
    const2 = lambda b, s: (0, 0)
    in_specs = [
        pl.BlockSpec((1, ot.shape[1], ts), lambda b, s: (b, 0, s)),
        pl.BlockSpec((1, ts, D), lambda b, s: (b, s, 0)),
        pl.BlockSpec((1, 6, D), lambda b, s: (b, 0, 0)),
        pl.BlockSpec(mg.shape, const2),
        pl.BlockSpec(fg.shape, const2),
        pl.BlockSpec(wo.shape, const2),
        pl.BlockSpec(ng.shape, const2),
    ]
    args = [ot, x, mod_l, mg, fg, wo, ng]
    out_specs = [pl.BlockSpec((1, ts, D), lambda b, s: (b, s, 0)),
                 pl.BlockSpec((1, ts, D), lambda b, s: (b, s, 0))]
    out_shape = [jax.ShapeDtypeStruct((B, S, D), F32), jax.ShapeDtypeStruct((B, S, D), BF16)]
    if with_router:
        in_specs.append(pl.BlockSpec(rw.shape, const2))
        args.append(rw)
        out_specs.append(pl.BlockSpec((1, ts, LANES), lambda b, s: (b, s, 0)))
        out_shape.append(jax.ShapeDtypeStruct((B, S, LANES), F32))
    kern = functools.partial(_outproj_kernel, ts=ts, with_router=with_router)
    return pl.pallas_call(
        kern,
        grid=(B, S // ts),
        in_specs=in_specs,
        out_specs=out_specs,
        out_shape=out_shape,
        compiler_params=pltpu.CompilerParams(
            dimension_semantics=("arbitrary", "arbitrary"), vmem_limit_bytes=VMEM_LIMIT),
    )(*args)


def _ffn_kernel(*refs, with_gates, final_norm):
    if with_gates:
        h_ref, wg_ref, wu_ref, wd_ref, x_ref, mod_ref, fin_ref, gates_ref, o_ref, acc_ref = refs
    else:
        h_ref, wg_ref, wu_ref, wd_ref, x_ref, mod_ref, fin_ref, o_ref, acc_ref = refs
    e = pl.program_id(1)
    f = pl.program_id(2)

    @pl.when((e == 0) & (f == 0))
    def _():
        acc_ref[...] = jnp.zeros_like(acc_ref)

    h = h_ref[...]
    g = jnp.dot(h, wg_ref[0], preferred_element_type=F32)
    u = jnp.dot(h, wu_ref[0], preferred_element_type=F32)
    a = g * _sigmoid(g) * u
    if with_gates:
        gates = gates_ref[...]
        lane = lax.broadcasted_iota(jnp.int32, gates.shape, 1)
        a = a * jnp.sum(jnp.where(lane == e, gates, 0.0), axis=1, keepdims=True)
    acc_ref[...] += jnp.dot(a.astype(BF16), wd_ref[0], preferred_element_type=F32)

    @pl.when((e == pl.num_programs(1) - 1) & (f == pl.num_programs(2) - 1))
    def _():
        x2 = x_ref[...] + mod_ref[0, 5:6, :] * acc_ref[...]
        if final_norm:
            x2 = _rms(x2) * fin_ref[...]
        o_ref[...] = x2


def _ffn(h2, wg, wu, wd, x1, mod_l, fin_g, gates, *, tm, tf, seq, final_norm):
    T, D = h2.shape
    E, _, F = wg.shape
    tiles_per_batch = seq // tm
    with_gates = gates is not None
    in_specs = [
        pl.BlockSpec((tm, D), lambda i, e, f: (i, 0)),
        pl.BlockSpec((1, D, tf), lambda i, e, f: (e, 0, f)),
        pl.BlockSpec((1, D, tf), lambda i, e, f: (e, 0, f)),
        pl.BlockSpec((1, tf, D), lambda i, e, f: (e, f, 0)),
        pl.BlockSpec((tm, D), lambda i, e, f: (i, 0)),
        pl.BlockSpec((1, 6, D), lambda i, e, f: (i // tiles_per_batch, 0, 0)),
        pl.BlockSpec((1, D), lambda i, e, f: (0, 0)),
    ]
    args = [h2, wg, wu, wd, x1, mod_l, fin_g]
    if with_gates:
        in_specs.append(pl.BlockSpec((tm, LANES), lambda i, e, f: (i, 0)))
        args.append(gates)
    kern = functools.partial(_ffn_kernel, with_gates=with_gates, final_norm=final_norm)
    return pl.pallas_call(
        kern,
        grid=(T // tm, E, F // tf),
        in_specs=in_specs,
        out_specs=pl.BlockSpec((tm, D), lambda i, e, f: (i, 0)),
        out_shape=jax.ShapeDtypeStruct((T, D), F32),
        scratch_shapes=[pltpu.VMEM((tm, D), F32)],
        compiler_params=pltpu.CompilerParams(
            dimension_semantics=("arbitrary", "arbitrary", "arbitrary"),
            vmem_limit_bytes=VMEM_LIMIT),
    )(*args)


def _tiles(S):
    ts = min(512, S)
    tq = min(256, S)
    tk = ts
    tm = min(1024, S)
    tf = 512
    return ts, tq, tk, tm, tf


def kernel(x, c, positions, ada_w, ada_b, attn_norm_g, w_in, q_norm_g, w_uq, kv_norm_g, w_ukv,
           fox_forget_b, mla_out_g, fox_out_g, w_o, ffn_norm_g, dense_w_gate, dense_w_up,
           dense_w_down, router_w, moe_w_gate, moe_w_up, moe_w_down, final_norm_g):
    B, S, D = x.shape
    L = ada_w.shape[0]
    ts, tq, tk, tm, tf = _tiles(S)

    mod = _adaln(c, ada_w, ada_b).reshape(L, B, 6, D)
    ctab, stab = _rope_tables(positions)

    idx_in = _perm_w_in()
    idx_qa, idx_qb = _perm_w_uq()
    idx_kv = _perm_w_ukv()
    col_scale = np.ones((C_END,), np.float32)
    col_scale[C_FQ0:C_FQ0 + FOX_W] = FOX_HEAD_DIM ** -0.5
    tri = jnp.asarray(np.tril(np.ones((ts, ts), np.float32)), BF16)
    eqk = jnp.asarray(_aug_placement(), BF16)

    for l in range(L):
        win = (_take_cols(w_in[l], idx_in) * col_scale).astype(BF16)
        wuqa = _take_cols(w_uq[l], idx_qa).astype(BF16)
        wuqb = _take_cols(w_uq[l], idx_qb).astype(BF16)
        wukv = _take_cols(w_ukv[l], idx_kv).astype(BF16)
        fb_row = jnp.zeros((1, LANES), F32).at[0, CTRL_LOGIT:CTRL_LOGIT + FOX_HEADS].set(
            fox_forget_b[l].astype(F32))
        qa, ka, vt = _inproj(
            x, mod[l], attn_norm_g[l].reshape(1, D), win, q_norm_g[l].reshape(1, -1),
            kv_norm_g[l].reshape(1, -1), wuqa, wuqb, wukv, fb_row, ctab, stab, tri, eqk, ts=ts)
        ot = _attention(qa, ka, vt, tq=tq, tk=tk, hp=2)

        j = l // 2
        is_moe = (l % 2 == 1)
        rw = None
        if is_moe:
            rw = jnp.pad(router_w[j], ((0, 0), (0, LANES - N_EXPERTS))).astype(BF16)
        outs = _outproj(ot, x, mod[l], mla_out_g[l].reshape(1, -1), fox_out_g[l].reshape(1, -1),
                        w_o[l].astype(BF16), ffn_norm_g[l].reshape(1, D), rw, ts=ts)
        x1, h2 = outs[0], outs[1]
        if is_moe:
            wg, wu, wd = moe_w_gate[j], moe_w_up[j], moe_w_down[j]
            gates = outs[2].reshape(B * S, LANES)
        else:
            wg, wu, wd = dense_w_gate[j][None], dense_w_up[j][None], dense_w_down[j][None]
            gates = None
        x = _ffn(h2.reshape(B * S, D), wg.astype(BF16), wu.astype(BF16), wd.astype(BF16),
                 x1.reshape(B * S, D), mod[l], final_norm_g.reshape(1, D), gates,
                 tm=tm, tf=tf, seq=S, final_norm=(l == L - 1)).reshape(B, S, D)
    return x
```

```python
import functools

import numpy as np
import jax
import jax.numpy as jnp
from jax import lax
from jax.experimental import pallas as pl
from jax.experimental.pallas import tpu as pltpu

F32 = jnp.float32
BF16 = jnp.bfloat16

D_MODEL = 1024
MLA_HEADS = 8
MLA_V = 64
MLA_NOPE = 64
MLA_ROPE = 32
MLA_Q_LORA = 256
MLA_KV_LORA = 128
FOX_HEADS = 8
FOX_HEAD_DIM = 64
FOX_W = FOX_HEADS * FOX_HEAD_DIM
ROPE_THETA = 10000.0
D_FF = 3584
N_EXPERTS = 8
EPS = 1e-6
N_HEADS = MLA_HEADS + FOX_HEADS
HALF_ROPE = MLA_ROPE // 2

LANES = 128
VMEM_LIMIT = 56 * 1024 * 1024

ROPE_LO = MLA_NOPE
AUG_LO = FOX_HEAD_DIM
CTRL_LOGIT = 96
CTRL_ONE = 120

C_Q0 = 0
C_KV0 = C_Q0 + MLA_Q_LORA
C_KRA0 = C_KV0 + MLA_KV_LORA
C_KRB0 = C_KRA0 + LANES
C_FQ0 = C_KRB0 + LANES
C_FK0 = C_FQ0 + FOX_W
C_FV0 = C_FK0 + FOX_W
C_END = C_FV0 + FOX_W

NEG_BIG = -1e30
LOG2E = 1.4426950408889634
ONES_ROWS = 16


def _rms(v):
    return v * lax.rsqrt(jnp.mean(v * v, axis=-1, keepdims=True) + EPS)


def _sigmoid(v):
    return 1.0 / (1.0 + jnp.exp(-v))


def _split3(v):
    hi = v.astype(BF16).astype(F32)
    r = v - hi
    mid = r.astype(BF16).astype(F32)
    lo = (r - mid).astype(BF16).astype(F32)
    return hi, mid, lo


def _adaln_kernel(c_ref, w_ref, b_ref, o_ref):
    c = c_ref[...]
    ca = (c * _sigmoid(c)).astype(BF16)
    o_ref[0] = jnp.dot(ca, w_ref[0].astype(BF16), preferred_element_type=F32) + b_ref[0]


def _adaln(c, ada_w, ada_b):
    L, D, N = ada_w.shape
    B = c.shape[0]
    tn = 1536
    return pl.pallas_call(
        _adaln_kernel,
        grid=(L, N // tn),
        in_specs=[
            pl.BlockSpec((B, D), lambda l, j: (0, 0)),
            pl.BlockSpec((1, D, tn), lambda l, j: (l, 0, j)),
            pl.BlockSpec((1, 1, tn), lambda l, j: (l, 0, j)),
        ],
        out_specs=pl.BlockSpec((1, B, tn), lambda l, j: (l, 0, j)),
        out_shape=jax.ShapeDtypeStruct((L, B, N), F32),
        compiler_params=pltpu.CompilerParams(
            dimension_semantics=("arbitrary", "arbitrary"), vmem_limit_bytes=VMEM_LIMIT),
    )(c, ada_w, ada_b.reshape(L, 1, N))


def _rope_tab_kernel(pos_ref, freq_ref, cos_ref, sin_ref):
    ang = freq_ref[...] * pos_ref[0]
    cos_ref[0] = jnp.cos(ang)
    sin_ref[0] = jnp.sin(ang)


def _rope_tables(positions):
    B, S = positions.shape
    half = HALF_ROPE
    inv_freq = ROPE_THETA ** (-jnp.arange(half, dtype=F32) / half)
    pos = positions.astype(F32).reshape(B, 1, S)
    cos_t, sin_t = pl.pallas_call(
        _rope_tab_kernel,
        grid=(B,),
        in_specs=[
            pl.BlockSpec((1, 1, S), lambda b: (b, 0, 0)),
            pl.BlockSpec((half, 1), lambda b: (0, 0)),
        ],
        out_specs=[pl.BlockSpec((1, half, S), lambda b: (b, 0, 0))] * 2,
        out_shape=[jax.ShapeDtypeStruct((B, half, S), F32)] * 2,
        compiler_params=pltpu.CompilerParams(dimension_semantics=("arbitrary",)),
    )(pos, inv_freq.reshape(half, 1))
    cos = jnp.transpose(cos_t, (0, 2, 1))
    sin = jnp.transpose(sin_t, (0, 2, 1))
    ones = jnp.ones((B, S, MLA_NOPE), F32)
    z_lo = jnp.zeros((B, S, MLA_NOPE), F32)
    z_hi = jnp.zeros((B, S, LANES - MLA_NOPE - MLA_ROPE), F32)
    ctab = jnp.concatenate([ones, cos, cos, z_hi], axis=-1)
    stab = jnp.concatenate([z_lo, -sin, sin, z_hi], axis=-1)
    return ctab, stab


def _take_cols(w, idx):
    w_ext = jnp.concatenate([w, jnp.zeros((w.shape[0], 1), w.dtype)], axis=1)
    idx = np.where(idx < 0, w.shape[1], idx)
    return w_ext[:, idx]


def _perm_w_in():
    cq = 0
    ckv = cq + MLA_Q_LORA
    kr = ckv + MLA_KV_LORA
    fq = kr + MLA_ROPE
    fk = fq + FOX_W
    fv = fk + FOX_W
    fl = fv + FOX_W
    idx = -np.ones((C_END,), np.int64)
    idx[C_Q0:C_Q0 + MLA_Q_LORA] = cq + np.arange(MLA_Q_LORA)
    idx[C_KV0:C_KV0 + MLA_KV_LORA] = ckv + np.arange(MLA_KV_LORA)
    h = HALF_ROPE
    idx[C_KRA0 + ROPE_LO:C_KRA0 + ROPE_LO + MLA_ROPE] = kr + np.arange(MLA_ROPE)
    idx[C_KRA0 + CTRL_LOGIT:C_KRA0 + CTRL_LOGIT + FOX_HEADS] = fl + np.arange(FOX_HEADS)
    idx[C_KRB0 + ROPE_LO:C_KRB0 + ROPE_LO + h] = kr + h + np.arange(h)
    idx[C_KRB0 + ROPE_LO + h:C_KRB0 + ROPE_LO + MLA_ROPE] = kr + np.arange(h)
    idx[C_FQ0:C_FQ0 + FOX_W] = fq + np.arange(FOX_W)
    idx[C_FK0:C_FK0 + FOX_W] = fk + np.arange(FOX_W)
    idx[C_FV0:C_FV0 + FOX_W] = fv + np.arange(FOX_W)
    return idx


def _perm_w_uq():
    per = MLA_NOPE + MLA_ROPE
    ia = -np.ones((MLA_HEADS * LANES,), np.int64)
    ib = -np.ones((MLA_HEADS * LANES,), np.int64)
    h2 = HALF_ROPE
    for h in range(MLA_HEADS):
        ia[h * LANES:h * LANES + per] = h * per + np.arange(per)
        ib[h * LANES + ROPE_LO:h * LANES + ROPE_LO + h2] = h * per + MLA_NOPE + h2 + np.arange(h2)
        ib[h * LANES + ROPE_LO + h2:h * LANES + ROPE_LO + MLA_ROPE] = h * per + MLA_NOPE + np.arange(h2)
    return ia, ib


def _perm_w_ukv():
    per = MLA_NOPE + MLA_V
    ik = -np.ones((MLA_HEADS * LANES,), np.int64)
    iv = np.zeros((MLA_HEADS * MLA_V,), np.int64)
    for h in range(MLA_HEADS):
        ik[h * LANES:h * LANES + MLA_NOPE] = h * per + np.arange(MLA_NOPE)
        iv[h * MLA_V:(h + 1) * MLA_V] = h * per + MLA_NOPE + np.arange(MLA_V)
    return np.concatenate([ik, iv])


def _aug_placement():
    e = np.zeros((LANES, 2 * FOX_HEADS * LANES), np.float32)
    koff = FOX_HEADS * LANES
    for h in range(FOX_HEADS):
        for p in range(3):
            src = CTRL_LOGIT + 8 * p + h
            e[src, h * LANES + AUG_LO + p] = 1.0
            e[CTRL_ONE, h * LANES + AUG_LO + 3 + p] = -1.0
            e[CTRL_ONE, koff + h * LANES + AUG_LO + p] = 1.0
            e[src, koff + h * LANES + AUG_LO + 3 + p] = 1.0
    return e


def _inproj_kernel(x_ref, mod_ref, g_ref, win_ref, qg_ref, kvg_ref, wuqa_ref, wuqb_ref, wukv_ref,
                   fb_ref, ct_ref, st_ref, tri_ref, eqk_ref,
                   qa_ref, ka_ref, vt_ref, carry_ref, *, ts):
    si = pl.program_id(1)

    @pl.when(si == 0)
    def _():
        carry_ref[...] = jnp.zeros_like(carry_ref)

    x = x_ref[0]
    shift = mod_ref[0, 0:1, :]
    scale = mod_ref[0, 1:2, :]
    h = ((_rms(x) * g_ref[...]) * (1.0 + scale) + shift).astype(BF16)
    proj = jnp.dot(h, win_ref[...], preferred_element_type=F32)

    ctab = ct_ref[0]
    stab = st_ref[0]
    lane = lax.broadcasted_iota(jnp.int32, (ts, LANES), 1)

    c_q = proj[:, C_Q0:C_Q0 + MLA_Q_LORA]
    cqn = (_rms(c_q) * qg_ref[...]).astype(BF16)
    qa = jnp.dot(cqn, wuqa_ref[...], preferred_element_type=F32)
    qb = jnp.dot(cqn, wuqb_ref[...], preferred_element_type=F32)
    mla_scale = (MLA_NOPE + MLA_ROPE) ** -0.5 * LOG2E
    for hh in range(MLA_HEADS):
        sl = slice(hh * LANES, (hh + 1) * LANES)
        qa_ref[0, hh] = ((qa[:, sl] * ctab + qb[:, sl] * stab) * mla_scale).astype(BF16)

    c_kv = proj[:, C_KV0:C_KV0 + MLA_KV_LORA]
    ckvn = (_rms(c_kv) * kvg_ref[...]).astype(BF16)
    kv = jnp.dot(ckvn, wukv_ref[...], preferred_element_type=F32)
    kra = proj[:, C_KRA0:C_KRA0 + LANES]
    krb = proj[:, C_KRB0:C_KRB0 + LANES]
    krope = kra * ctab + krb * stab
    for hh in range(MLA_HEADS):
        ka_ref[0, hh] = (kv[:, hh * LANES:(hh + 1) * LANES] + krope).astype(BF16)
    v_mla = kv[:, MLA_HEADS * LANES:]

    ctrl = (lane >= CTRL_LOGIT) & (lane < CTRL_LOGIT + FOX_HEADS)
    fl = kra + fb_ref[...]
    lsig = jnp.minimum(fl, 0.0) - jnp.log1p(jnp.exp(-jnp.abs(fl)))
    lf = jnp.where(ctrl, lsig, 0.0)
    hi, mid, lo = _split3(lf)
    p1 = (hi + pltpu.roll(mid, 8, 1) + pltpu.roll(lo, 16, 1)).astype(BF16)
    cs = jnp.dot(tri_ref[...], p1, preferred_element_type=F32)
    cs = cs + pltpu.roll(cs, LANES - 8, 1) + pltpu.roll(cs, LANES - 16, 1)
    fcum = jnp.where(ctrl, cs, 0.0) + carry_ref[...]
    carry_ref[...] = fcum[ts - 1:ts, :]

    hi, mid, lo = _split3(fcum * LOG2E)
    p2 = hi + pltpu.roll(mid, 8, 1) + pltpu.roll(lo, 16, 1)
    p2 = jnp.where(lane == CTRL_ONE, 1.0, p2).astype(BF16)
    aug = jnp.dot(p2, eqk_ref[...], preferred_element_type=F32)

    fq = proj[:, C_FQ0:C_FQ0 + FOX_W]
    fk = proj[:, C_FK0:C_FK0 + FOX_W]
    low = lane < FOX_HEAD_DIM
    koff = FOX_HEADS * LANES
    for j in range(FOX_HEADS // 2):
        sl = slice(j * LANES, (j + 1) * LANES)
        for src, off, dst in ((fq, 0, qa_ref), (fk, koff, ka_ref)):
            blk = src[:, sl] * LOG2E if dst is qa_ref else src[:, sl]
            blk_r = pltpu.roll(blk, FOX_HEAD_DIM, 1)
            h0 = 2 * j
            a0 = aug[:, off + h0 * LANES:off + (h0 + 1) * LANES]
            a1 = aug[:, off + (h0 + 1) * LANES:off + (h0 + 2) * LANES]
            dst[0, MLA_HEADS + h0] = jnp.where(low, blk, a0).astype(BF16)
            dst[0, MLA_HEADS + h0 + 1] = jnp.where(low, blk_r, a1).astype(BF16)

    fv = proj[:, C_FV0:C_FV0 + FOX_W]
    vt_ref[0, 0, 0:MLA_HEADS * MLA_V, :] = v_mla.T.astype(BF16)
    vt_ref[0, 0, MLA_HEADS * MLA_V:, :] = fv.T.astype(BF16)


def _inproj(x, mod_l, g, win, qg, kvg, wuqa, wuqb, wukv, fb_row, ctab, stab, tri, eqk, *, ts):
    B, S, D = x.shape
    nst = S // ts
    const2 = lambda b, s: (0, 0)
    kern = functools.partial(_inproj_kernel, ts=ts)
    return pl.pallas_call(
        kern,
        grid=(B, nst),
        in_specs=[
            pl.BlockSpec((1, ts, D), lambda b, s: (b, s, 0)),
            pl.BlockSpec((1, 6, D), lambda b, s: (b, 0, 0)),
            pl.BlockSpec((1, D), const2),
            pl.BlockSpec(win.shape, const2),
            pl.BlockSpec(qg.shape, const2),
            pl.BlockSpec(kvg.shape, const2),
            pl.BlockSpec(wuqa.shape, const2),
            pl.BlockSpec(wuqb.shape, const2),
            pl.BlockSpec(wukv.shape, const2),
            pl.BlockSpec(fb_row.shape, const2),
            pl.BlockSpec((1, ts, LANES), lambda b, s: (b, s, 0)),
            pl.BlockSpec((1, ts, LANES), lambda b, s: (b, s, 0)),
            pl.BlockSpec(tri.shape, const2),
            pl.BlockSpec(eqk.shape, const2),
        ],
        out_specs=[
            pl.BlockSpec((1, N_HEADS, ts, LANES), lambda b, s: (b, 0, s, 0)),
            pl.BlockSpec((1, N_HEADS, ts, LANES), lambda b, s: (b, 0, s, 0)),
            pl.BlockSpec((1, 1, N_HEADS * MLA_V, ts), lambda b, s: (b, s, 0, 0)),
        ],
        out_shape=[
            jax.ShapeDtypeStruct((B, N_HEADS, S, LANES), BF16),
            jax.ShapeDtypeStruct((B, N_HEADS, S, LANES), BF16),
            jax.ShapeDtypeStruct((B, nst, N_HEADS * MLA_V, ts), BF16),
        ],
        scratch_shapes=[pltpu.VMEM((1, LANES), F32)],
        compiler_params=pltpu.CompilerParams(
            dimension_semantics=("arbitrary", "arbitrary"), vmem_limit_bytes=VMEM_LIMIT),
    )(x, mod_l, g, win, qg, kvg, wuqa, wuqb, wukv, fb_row, ctab, stab, tri, eqk)


def _colmax8(st, groups):
    tk, tq = st.shape
    v = st.reshape(groups, tk // (8 * groups), 8, tq)
    return jnp.max(jnp.max(v, axis=1), axis=0)


def _attn_kernel(q_ref, k_ref, vt_ref, o_ref, s_scr, tmax_scr, m_scr, acc_scr, *, tq, tk, hp):
    qi = pl.program_id(2)
    q0 = qi * tq
    n_full = q0 // tk
    dv = MLA_V
    groups = 4
    ones = jnp.ones((ONES_ROWS, tk), BF16)

    def qk(j):
        return [lax.dot_general(k_ref[0, h, pl.ds(pl.multiple_of(j * tk, tk), tk), :],
                                q_ref[0, h], (((1,), (1,)), ((), ())),
                                preferred_element_type=F32) for h in range(hp)]

    def park(sts, j, masked):
        for h in range(hp):
            st = sts[h]
            if masked:
                kpos = j * tk + lax.broadcasted_iota(jnp.int32, (tk, tq), 0)
                qpos = q0 + lax.broadcasted_iota(jnp.int32, (tk, tq), 1)
                st = jnp.where(kpos <= qpos, st, NEG_BIG)
            s_scr[h] = st
            tmax_scr[h] = _colmax8(st, groups)

    def scores(j, masked):
        park(qk(j), j, masked)

    def update(j):
        for h in range(hp):
            m = m_scr[h]
            m_new = jnp.maximum(m, jnp.max(tmax_scr[h], axis=0, keepdims=True))
            alpha = jnp.exp2(m - m_new)
            p = jnp.exp2((s_scr[h] - m_new).astype(BF16))
            vt1 = jnp.concatenate([vt_ref[0, j, h * dv:(h + 1) * dv, :], ones], axis=0)
            acc_scr[h] = alpha * acc_scr[h] + jnp.dot(vt1, p, preferred_element_type=F32)
            m_scr[h] = m_new

    m_scr[...] = jnp.full(m_scr.shape, NEG_BIG, F32)
    acc_scr[...] = jnp.zeros(acc_scr.shape, F32)

    @pl.when(n_full == 0)
    def _():
        scores(0, True)

    @pl.when(n_full > 0)
    def _():
        scores(0, False)

    def body(t, c):
        sts = qk(t + 1)
        update(t)
        park(sts, t + 1, False)
        return c

    lax.fori_loop(0, n_full - 1, body, 0)

    @pl.when(n_full > 0)
    def _():
        sts = qk(n_full)
        update(n_full - 1)
        park(sts, n_full, True)

    update(n_full)
    for h in range(hp):
        acc = acc_scr[h]
        o_ref[0, h * dv:(h + 1) * dv, :] = acc[0:dv, :] / acc[dv:dv + 1, :]


def _attention(qa, ka, vt, *, tq, tk, hp):
    B, H, S, _ = qa.shape
    nkt = vt.shape[1]
    assert S % tq == 0 and nkt * tk == S and H % hp == 0 and tk % tq == 0
    kern = functools.partial(_attn_kernel, tq=tq, tk=tk, hp=hp)
    return pl.pallas_call(
        kern,
        grid=(B, H // hp, S // tq),
        in_specs=[
            pl.BlockSpec((1, hp, tq, LANES), lambda b, g, i: (b, g, i, 0)),
            pl.BlockSpec((1, hp, S, LANES), lambda b, g, i: (b, g, 0, 0)),
            pl.BlockSpec((1, nkt, hp * MLA_V, tk), lambda b, g, i: (b, 0, g, 0)),
        ],
        out_specs=pl.BlockSpec((1, hp * MLA_V, tq), lambda b, g, i: (b, g, i)),
        out_shape=jax.ShapeDtypeStruct((B, H * MLA_V, S), F32),
        scratch_shapes=[
            pltpu.VMEM((hp, tk, tq), F32),
            pltpu.VMEM((hp, 8, tq), F32),
            pltpu.VMEM((hp, 1, tq), F32),
            pltpu.VMEM((hp, MLA_V + ONES_ROWS, tq), F32),
        ],
        compiler_params=pltpu.CompilerParams(
            dimension_semantics=("arbitrary", "arbitrary", "arbitrary"),
            vmem_limit_bytes=VMEM_LIMIT),
    )(qa, ka, vt)


def _top2_gates(logits, lane):
    lg = jnp.where(lane < N_EXPERTS, logits, NEG_BIG)
    m1 = jnp.max(lg, axis=1, keepdims=True)
    i1 = jnp.min(jnp.where(lg == m1, lane, LANES), axis=1, keepdims=True)
    lg2 = jnp.where(lane == i1, NEG_BIG, lg)
    m2 = jnp.max(lg2, axis=1, keepdims=True)
    i2 = jnp.min(jnp.where(lg2 == m2, lane, LANES), axis=1, keepdims=True)
    e2 = jnp.exp(m2 - m1)
    den = 1.0 + e2
    return jnp.where(lane == i1, 1.0 / den, 0.0) + jnp.where(lane == i2, e2 / den, 0.0)


def _outproj_kernel(*refs, ts, with_router):
    if with_router:
        (ot_ref, x_ref, mod_ref, mg_ref, fg_ref, wo_ref, ng_ref, rw_ref,
         x1_ref, h2_ref, gates_ref) = refs
    else:
        ot_ref, x_ref, mod_ref, mg_ref, fg_ref, wo_ref, ng_ref, x1_ref, h2_ref = refs
    half = MLA_HEADS * MLA_V
    om = ot_ref[0, 0:half, :].T
    of = ot_ref[0, half:, :].T
    on = jnp.concatenate([_rms(om) * mg_ref[...], _rms(of) * fg_ref[...]], axis=1).astype(BF16)
    mix = jnp.dot(on, wo_ref[...], preferred_element_type=F32)
    x1 = x_ref[0] + mod_ref[0, 2:3, :] * mix
    x1_ref[0] = x1
    h2 = ((_rms(x1) * ng_ref[...]) * (1.0 + mod_ref[0, 4:5, :]) + mod_ref[0, 3:4, :]).astype(BF16)
    h2_ref[0] = h2
    if with_router:
        logits = jnp.dot(h2, rw_ref[...], preferred_element_type=F32)
        lane = lax.broadcasted_iota(jnp.int32, (ts, LANES), 1)
        gates_ref[0] = _top2_gates(logits, lane)


def _outproj(ot, x, mod_l, mg, fg, wo, ng, rw, *, ts):
    B, S, D = x.shape
    with_router = rw is not None
    const2 = lambda b, s: (0, 0)
    in_specs = [
        pl.BlockSpec((1, ot.shape[1], ts), lambda b, s: (b, 0, s)),
        pl.BlockSpec((1, ts, D), lambda b, s: (b, s, 0)),
        pl.BlockSpec((1, 6, D), lambda b, s: (b, 0, 0)),
        pl.BlockSpec(mg.shape, const2),
        pl.BlockSpec(fg.shape, const2),
        pl.BlockSpec(wo.shape, const2),
        pl.BlockSpec(ng.shape, const2),
    ]
    args = [ot, x, mod_l, mg, fg, wo, ng]
    out_specs = [pl.BlockSpec((1, ts, D), lambda b, s: (b, s, 0)),
                 pl.BlockSpec((1, ts, D), lambda b, s: (b, s, 0))]
    out_shape = [jax.ShapeDtypeStruct((B, S, D), F32), jax.ShapeDtypeStruct((B, S, D), BF16)]
    if with_router:
        in_specs.append(pl.BlockSpec(rw.shape, const2))
        args.append(rw)
        out_specs.append(pl.BlockSpec((1, ts, LANES), lambda b, s: (b, s, 0)))
        out_shape.append(jax.ShapeDtypeStruct((B, S, LANES), F32))
    kern = functools.partial(_outproj_kernel, ts=ts, with_router=with_router)
    return pl.pallas_call(
        kern,
        grid=(B, S // ts),
        in_specs=in_specs,
        out_specs=out_specs,
        out_shape=out_shape,
        compiler_params=pltpu.CompilerParams(
            dimension_semantics=("arbitrary", "arbitrary"), vmem_limit_bytes=VMEM_LIMIT),
    )(*args)


def _ffn_kernel(h_ref, wg_ref, wu_ref, wd_ref, x_ref, mod_ref, fin_ref, o_ref, acc_ref,
                *, final_norm):
    f = pl.program_id(1)

    @pl.when(f == 0)
    def _():
        acc_ref[...] = jnp.zeros_like(acc_ref)

    h = h_ref[...]
    g = jnp.dot(h, wg_ref[...], preferred_element_type=F32)
    u = jnp.dot(h, wu_ref[...], preferred_element_type=F32)
    a = g * _sigmoid(g) * u
    acc_ref[...] += jnp.dot(a.astype(BF16), wd_ref[...], preferred_element_type=F32)

    @pl.when(f == pl.num_programs(1) - 1)
    def _():
        x2 = x_ref[...] + mod_ref[0, 5:6, :] * acc_ref[...]
        if final_norm:
            x2 = _rms(x2) * fin_ref[...]
        o_ref[...] = x2


def _ffn(h2, wg, wu, wd, x1, mod_l, fin_g, *, tm, tf, seq, final_norm):
    T, D = h2.shape
    F = wg.shape[1]
    tiles_per_batch = seq // tm
    kern = functools.partial(_ffn_kernel, final_norm=final_norm)
    return pl.pallas_call(
        kern,
        grid=(T // tm, F // tf),
        in_specs=[
            pl.BlockSpec((tm, D), lambda i, f: (i, 0)),
            pl.BlockSpec((D, tf), lambda i, f: (0, f)),
            pl.BlockSpec((D, tf), lambda i, f: (0, f)),
            pl.BlockSpec((tf, D), lambda i, f: (f, 0)),
            pl.BlockSpec((tm, D), lambda i, f: (i, 0)),
            pl.BlockSpec((1, 6, D), lambda i, f: (i // tiles_per_batch, 0, 0)),
            pl.BlockSpec((1, D), lambda i, f: (0, 0)),
        ],
        out_specs=pl.BlockSpec((tm, D), lambda i, f: (i, 0)),
        out_shape=jax.ShapeDtypeStruct((T, D), F32),
        scratch_shapes=[pltpu.VMEM((tm, D), F32)],
        compiler_params=pltpu.CompilerParams(
            dimension_semantics=("arbitrary", "arbitrary"), vmem_limit_bytes=VMEM_LIMIT),
    )(h2, wg, wu, wd, x1, mod_l, fin_g)


def _moe_kernel(h_ref, gates_ref, lt_ref, wg_ref, wu_ref, wd_ref, x_ref, mod_ref, fin_ref, o_ref,
                rank_scr, rankt_scr, gatest_scr, xc_scr, gc_scr, yc_scr, nch_ref,
                *, tm, rc, final_norm):
    e = pl.program_id(1)
    f = pl.program_id(2)
    last_f = f == pl.num_programs(2) - 1

    @pl.when((e == 0) & (f == 0))
    def _():
        gates = gates_ref[...]
        sel = gates > 0.0
        selb = jnp.where(sel, 1.0, 0.0).astype(BF16)
        rank = jnp.dot(lt_ref[...], selb, preferred_element_type=F32)
        rank = jnp.where(sel, rank, -1.0)
        rank_scr[...] = rank
        rankt_scr[...] = rank.T
        gatest_scr[...] = gates.T
        o_ref[...] = jnp.zeros_like(o_ref)

    @pl.when(f == 0)
    def _():
        lane = lax.broadcasted_iota(jnp.int32, (tm, LANES), 1)
        cnt = jnp.sum(jnp.where((lane == e) & (rank_scr[...] >= 0.0), 1.0, 0.0))
        nch = (cnt.astype(jnp.int32) + (rc - 1)) // rc
        nch_ref[0] = nch
        rrow = rankt_scr[pl.ds(e, 1), :].astype(jnp.int32)
        grow = gatest_scr[pl.ds(e, 1), :]

        def gather(k, c):
            row = lax.broadcasted_iota(jnp.int32, (rc, tm), 0) + k * rc
            hit = rrow == row
            onehot = jnp.where(hit, 1.0, 0.0).astype(BF16)
            xc_scr[k] = jnp.dot(onehot, h_ref[...], preferred_element_type=F32).astype(BF16)
            gcol = jnp.sum(jnp.where(hit, grow, 0.0), axis=1, keepdims=True)
            gc_scr[k] = jnp.broadcast_to(gcol, (rc, LANES))
            yc_scr[k] = jnp.zeros((rc, yc_scr.shape[2]), F32)
            return c

        lax.fori_loop(0, nch, gather, 0)

    def expert(k, c):
        xk = xc_scr[k]
        g = jnp.dot(xk, wg_ref[0], preferred_element_type=F32)
        u = jnp.dot(xk, wu_ref[0], preferred_element_type=F32)
        a = g * _sigmoid(g) * u * gc_scr[k][:, 0:1]
        yc_scr[k] += jnp.dot(a.astype(BF16), wd_ref[0], preferred_element_type=F32)
        return c

    lax.fori_loop(0, nch_ref[0], expert, 0)

    @pl.when(last_f)
    def _():
        lane = lax.broadcasted_iota(jnp.int32, (tm, LANES), 1)
        rcol = jnp.sum(jnp.where(lane == e, rank_scr[...], 0.0), axis=1,
                       keepdims=True).astype(jnp.int32)

        def scatter(k, c):
            col = lax.broadcasted_iota(jnp.int32, (tm, rc), 1) + k * rc
            onehot_t = jnp.where(rcol == col, 1.0, 0.0).astype(BF16)
            o_ref[...] += jnp.dot(onehot_t, yc_scr[k].astype(BF16), preferred_element_type=F32)
            return c

        lax.fori_loop(0, nch_ref[0], scatter, 0)

    @pl.when((e == pl.num_programs(1) - 1) & last_f)
    def _():
        x2 = x_ref[...] + mod_ref[0, 5:6, :] * o_ref[...]
        if final_norm:
            x2 = _rms(x2) * fin_ref[...]
        o_ref[...] = x2


def _moe(h2, gates, wg, wu, wd, x1, mod_l, fin_g, *, tm, tf, rc, seq, final_norm):
    T, D = h2.shape
    E, _, F = wg.shape
    tiles_per_batch = seq // tm
    lt = jnp.asarray(np.tril(np.ones((tm, tm), np.float32), -1), BF16)
    kern = functools.partial(_moe_kernel, tm=tm, rc=rc, final_norm=final_norm)
    return pl.pallas_call(
        kern,
        grid=(T // tm, E, F // tf),
        in_specs=[
            pl.BlockSpec((tm, D), lambda i, e, f: (i, 0)),
            pl.BlockSpec((tm, LANES), lambda i, e, f: (i, 0)),
            pl.BlockSpec((tm, tm), lambda i, e, f: (0, 0)),
            pl.BlockSpec((1, D, tf), lambda i, e, f: (e, 0, f)),
            pl.BlockSpec((1, D, tf), lambda i, e, f: (e, 0, f)),
            pl.BlockSpec((1, tf, D), lambda i, e, f: (e, f, 0)),
            pl.BlockSpec((tm, D), lambda i, e, f: (i, 0)),
            pl.BlockSpec((1, 6, D), lambda i, e, f: (i // tiles_per_batch, 0, 0)),
            pl.BlockSpec((1, D), lambda i, e, f: (0, 0)),
        ],
        out_specs=pl.BlockSpec((tm, D), lambda i, e, f: (i, 0)),
        out_shape=jax.ShapeDtypeStruct((T, D), F32),
        scratch_shapes=[
            pltpu.VMEM((tm, LANES), F32),
            pltpu.VMEM((LANES, tm), F32),
            pltpu.VMEM((LANES, tm), F32),
            pltpu.VMEM((tm // rc, rc, D), BF16),
            pltpu.VMEM((tm // rc, rc, LANES), F32),
            pltpu.VMEM((tm // rc, rc, D), F32),
            pltpu.SMEM((1,), jnp.int32),
        ],
        compiler_params=pltpu.CompilerParams(
            dimension_semantics=("arbitrary", "arbitrary", "arbitrary"),
            vmem_limit_bytes=VMEM_LIMIT),
    )(h2, gates, lt, wg, wu, wd, x1, mod_l, fin_g)


def _tiles(S):
    ts = min(512, S)
    tq = min(256, S)
    tk = ts
    tm = min(1024, S)
    tf = 512
    rc = 256
    return ts, tq, tk, tm, tf, rc


def kernel(x, c, positions, ada_w, ada_b, attn_norm_g, w_in, q_norm_g, w_uq, kv_norm_g, w_ukv,
           fox_forget_b, mla_out_g, fox_out_g, w_o, ffn_norm_g, dense_w_gate, dense_w_up,
           dense_w_down, router_w, moe_w_gate, moe_w_up, moe_w_down, final_norm_g):
    B, S, D = x.shape
    L = ada_w.shape[0]
    ts, tq, tk, tm, tf, rc = _tiles(S)

    mod = _adaln(c, ada_w, ada_b).reshape(L, B, 6, D)
    ctab, stab = _rope_tables(positions)

    idx_in = _perm_w_in()
    idx_qa, idx_qb = _perm_w_uq()
    idx_kv = _perm_w_ukv()
    col_scale = np.ones((C_END,), np.float32)
    col_scale[C_FQ0:C_FQ0 + FOX_W] = FOX_HEAD_DIM ** -0.5
    tri = jnp.asarray(np.tril(np.ones((ts, ts), np.float32)), BF16)
    eqk = jnp.asarray(_aug_placement(), BF16)

    for l in range(L):
        win = (_take_cols(w_in[l], idx_in) * col_scale).astype(BF16)
        wuqa = _take_cols(w_uq[l], idx_qa).astype(BF16)
        wuqb = _take_cols(w_uq[l], idx_qb).astype(BF16)
        wukv = _take_cols(w_ukv[l], idx_kv).astype(BF16)
        fb_row = jnp.zeros((1, LANES), F32).at[0, CTRL_LOGIT:CTRL_LOGIT + FOX_HEADS].set(
            fox_forget_b[l].astype(F32))
        qa, ka, vt = _inproj(
            x, mod[l], attn_norm_g[l].reshape(1, D), win, q_norm_g[l].reshape(1, -1),
            kv_norm_g[l].reshape(1, -1), wuqa, wuqb, wukv, fb_row, ctab, stab, tri, eqk, ts=ts)
        ot = _attention(qa, ka, vt, tq=tq, tk=tk, hp=4)

        j = l // 2
        is_moe = (l % 2 == 1)
        rw = None
        if is_moe:
            rw = jnp.pad(router_w[j], ((0, 0), (0, LANES - N_EXPERTS))).astype(BF16)
        outs = _outproj(ot, x, mod[l], mla_out_g[l].reshape(1, -1), fox_out_g[l].reshape(1, -1),
                        w_o[l].astype(BF16), ffn_norm_g[l].reshape(1, D), rw, ts=ts)
        x1, h2 = outs[0].reshape(B * S, D), outs[1].reshape(B * S, D)
        fin_g = final_norm_g.reshape(1, D)
        last = l == L - 1
        if is_moe:
            x = _moe(h2, outs[2].reshape(B * S, LANES), moe_w_gate[j].astype(BF16),
                     moe_w_up[j].astype(BF16), moe_w_down[j].astype(BF16), x1, mod[l], fin_g,
                     tm=tm, tf=tf, rc=rc, seq=S, final_norm=last)
        else:
            x = _ffn(h2, dense_w_gate[j].astype(BF16), dense_w_up[j].astype(BF16),
                     dense_w_down[j].astype(BF16), x1, mod[l], fin_g,
                     tm=tm, tf=tf, seq=S, final_norm=last)
        x = x.reshape(B, S, D)
    return x
```

```python
import functools

import numpy as np
import jax
import jax.numpy as jnp
from jax import lax
from jax.experimental import pallas as pl
from jax.experimental.pallas import tpu as pltpu

F32 = jnp.float32
BF16 = jnp.bfloat16

D_MODEL = 1024
MLA_HEADS = 8
MLA_V = 64
MLA_NOPE = 64
MLA_ROPE = 32
MLA_Q_LORA = 256
MLA_KV_LORA = 128
FOX_HEADS = 8
FOX_HEAD_DIM = 64
FOX_W = FOX_HEADS * FOX_HEAD_DIM
ROPE_THETA = 10000.0
D_FF = 3584
N_EXPERTS = 8
EPS = 1e-6
N_HEADS = MLA_HEADS + FOX_HEADS
HALF_ROPE = MLA_ROPE // 2

LANES = 128
VMEM_LIMIT = 56 * 1024 * 1024

ROPE_LO = MLA_NOPE
AUG_LO = FOX_HEAD_DIM
CTRL_LOGIT = 96
CTRL_ONE = 120

C_Q0 = 0
C_KV0 = C_Q0 + MLA_Q_LORA
C_KRA0 = C_KV0 + MLA_KV_LORA
C_KRB0 = C_KRA0 + LANES
C_FQ0 = C_KRB0 + LANES
C_FK0 = C_FQ0 + FOX_W
C_FV0 = C_FK0 + FOX_W
C_END = C_FV0 + FOX_W

NEG_BIG = -1e30
LOG2E = 1.4426950408889634
ONES_ROWS = 16


def _rms(v):
    return v * lax.rsqrt(jnp.mean(v * v, axis=-1, keepdims=True) + EPS)


def _sigmoid(v):
    return 1.0 / (1.0 + jnp.exp(-v))


def _split3(v):
    hi = v.astype(BF16).astype(F32)
    r = v - hi
    mid = r.astype(BF16).astype(F32)
    lo = (r - mid).astype(BF16).astype(F32)
    return hi, mid, lo


def _adaln_kernel(c_ref, w_ref, b_ref, o_ref):
    c = c_ref[...]
    ca = (c * _sigmoid(c)).astype(BF16)
    o_ref[0] = jnp.dot(ca, w_ref[0].astype(BF16), preferred_element_type=F32) + b_ref[0]


def _adaln(c, ada_w, ada_b):
    L, D, N = ada_w.shape
    B = c.shape[0]
    tn = 1536
    return pl.pallas_call(
        _adaln_kernel,
        grid=(L, N // tn),
        in_specs=[
            pl.BlockSpec((B, D), lambda l, j: (0, 0)),
            pl.BlockSpec((1, D, tn), lambda l, j: (l, 0, j)),
            pl.BlockSpec((1, 1, tn), lambda l, j: (l, 0, j)),
        ],
        out_specs=pl.BlockSpec((1, B, tn), lambda l, j: (l, 0, j)),
        out_shape=jax.ShapeDtypeStruct((L, B, N), F32),
        compiler_params=pltpu.CompilerParams(
            dimension_semantics=("arbitrary", "arbitrary"), vmem_limit_bytes=VMEM_LIMIT),
    )(c, ada_w, ada_b.reshape(L, 1, N))


def _rope_tab_kernel(pos_ref, freq_ref, cos_ref, sin_ref):
    ang = freq_ref[...] * pos_ref[0]
    cos_ref[0] = jnp.cos(ang)
    sin_ref[0] = jnp.sin(ang)


def _rope_tables(positions):
    B, S = positions.shape
    half = HALF_ROPE
    inv_freq = ROPE_THETA ** (-jnp.arange(half, dtype=F32) / half)
    pos = positions.astype(F32).reshape(B, 1, S)
    cos_t, sin_t = pl.pallas_call(
        _rope_tab_kernel,
        grid=(B,),
        in_specs=[
            pl.BlockSpec((1, 1, S), lambda b: (b, 0, 0)),
            pl.BlockSpec((half, 1), lambda b: (0, 0)),
        ],
        out_specs=[pl.BlockSpec((1, half, S), lambda b: (b, 0, 0))] * 2,
        out_shape=[jax.ShapeDtypeStruct((B, half, S), F32)] * 2,
        compiler_params=pltpu.CompilerParams(dimension_semantics=("arbitrary",)),
    )(pos, inv_freq.reshape(half, 1))
    cos = jnp.transpose(cos_t, (0, 2, 1))
    sin = jnp.transpose(sin_t, (0, 2, 1))
    ones = jnp.ones((B, S, MLA_NOPE), F32)
    z_lo = jnp.zeros((B, S, MLA_NOPE), F32)
    z_hi = jnp.zeros((B, S, LANES - MLA_NOPE - MLA_ROPE), F32)
    ctab = jnp.concatenate([ones, cos, cos, z_hi], axis=-1)
    stab = jnp.concatenate([z_lo, -sin, sin, z_hi], axis=-1)
    return ctab, stab


def _take_cols(w, idx):
    w_ext = jnp.concatenate([w, jnp.zeros((w.shape[0], 1), w.dtype)], axis=1)
    idx = np.where(idx < 0, w.shape[1], idx)
    return w_ext[:, idx]


def _perm_w_in():
    cq = 0
    ckv = cq + MLA_Q_LORA
    kr = ckv + MLA_KV_LORA
    fq = kr + MLA_ROPE
    fk = fq + FOX_W
    fv = fk + FOX_W
    fl = fv + FOX_W
    idx = -np.ones((C_END,), np.int64)
    idx[C_Q0:C_Q0 + MLA_Q_LORA] = cq + np.arange(MLA_Q_LORA)
    idx[C_KV0:C_KV0 + MLA_KV_LORA] = ckv + np.arange(MLA_KV_LORA)
    h = HALF_ROPE
    idx[C_KRA0 + ROPE_LO:C_KRA0 + ROPE_LO + MLA_ROPE] = kr + np.arange(MLA_ROPE)
    idx[C_KRA0 + CTRL_LOGIT:C_KRA0 + CTRL_LOGIT + FOX_HEADS] = fl + np.arange(FOX_HEADS)
    idx[C_KRB0 + ROPE_LO:C_KRB0 + ROPE_LO + h] = kr + h + np.arange(h)
    idx[C_KRB0 + ROPE_LO + h:C_KRB0 + ROPE_LO + MLA_ROPE] = kr + np.arange(h)
    idx[C_FQ0:C_FQ0 + FOX_W] = fq + np.arange(FOX_W)
    idx[C_FK0:C_FK0 + FOX_W] = fk + np.arange(FOX_W)
    idx[C_FV0:C_FV0 + FOX_W] = fv + np.arange(FOX_W)
    return idx


def _perm_w_uq():
    per = MLA_NOPE + MLA_ROPE
    ia = -np.ones((MLA_HEADS * LANES,), np.int64)
    ib = -np.ones((MLA_HEADS * LANES,), np.int64)
    h2 = HALF_ROPE
    for h in range(MLA_HEADS):
        ia[h * LANES:h * LANES + per] = h * per + np.arange(per)
        ib[h * LANES + ROPE_LO:h * LANES + ROPE_LO + h2] = h * per + MLA_NOPE + h2 + np.arange(h2)
        ib[h * LANES + ROPE_LO + h2:h * LANES + ROPE_LO + MLA_ROPE] = h * per + MLA_NOPE + np.arange(h2)
    return ia, ib


def _perm_w_ukv():
    per = MLA_NOPE + MLA_V
    ik = -np.ones((MLA_HEADS * LANES,), np.int64)
    iv = np.zeros((MLA_HEADS * MLA_V,), np.int64)
    for h in range(MLA_HEADS):
        ik[h * LANES:h * LANES + MLA_NOPE] = h * per + np.arange(MLA_NOPE)
        iv[h * MLA_V:(h + 1) * MLA_V] = h * per + MLA_NOPE + np.arange(MLA_V)
    return np.concatenate([ik, iv])


def _aug_placement():
    e = np.zeros((LANES, 2 * FOX_HEADS * LANES), np.float32)
    koff = FOX_HEADS * LANES
    for h in range(FOX_HEADS):
        for p in range(3):
            src = CTRL_LOGIT + 8 * p + h
            e[src, h * LANES + AUG_LO + p] = 1.0
            e[CTRL_ONE, h * LANES + AUG_LO + 3 + p] = -1.0
            e[CTRL_ONE, koff + h * LANES + AUG_LO + p] = 1.0
            e[src, koff + h * LANES + AUG_LO + 3 + p] = 1.0
    return e


def _inproj_kernel(x_ref, mod_ref, g_ref, win_ref, qg_ref, kvg_ref, wuqa_ref, wuqb_ref, wukv_ref,
                   fb_ref, ct_ref, st_ref, tri_ref, eqk_ref,
                   qa_ref, ka_ref, vt_ref, carry_ref, *, ts):
    si = pl.program_id(1)

    @pl.when(si == 0)
    def _():
        carry_ref[...] = jnp.zeros_like(carry_ref)

    x = x_ref[0]
    shift = mod_ref[0, 0:1, :]
    scale = mod_ref[0, 1:2, :]
    h = ((_rms(x) * g_ref[...]) * (1.0 + scale) + shift).astype(BF16)
    proj = jnp.dot(h, win_ref[...], preferred_element_type=F32)

    ctab = ct_ref[0]
    stab = st_ref[0]
    lane = lax.broadcasted_iota(jnp.int32, (ts, LANES), 1)

    c_q = proj[:, C_Q0:C_Q0 + MLA_Q_LORA]
    cqn = (_rms(c_q) * qg_ref[...]).astype(BF16)
    qa = jnp.dot(cqn, wuqa_ref[...], preferred_element_type=F32)
    qb = jnp.dot(cqn, wuqb_ref[...], preferred_element_type=F32)
    mla_scale = (MLA_NOPE + MLA_ROPE) ** -0.5 * LOG2E
    for hh in range(MLA_HEADS):
        sl = slice(hh * LANES, (hh + 1) * LANES)
        qa_ref[0, hh] = ((qa[:, sl] * ctab + qb[:, sl] * stab) * mla_scale).astype(BF16)

    c_kv = proj[:, C_KV0:C_KV0 + MLA_KV_LORA]
    ckvn = (_rms(c_kv) * kvg_ref[...]).astype(BF16)
    kv = jnp.dot(ckvn, wukv_ref[...], preferred_element_type=F32)
    kra = proj[:, C_KRA0:C_KRA0 + LANES]
    krb = proj[:, C_KRB0:C_KRB0 + LANES]
    krope = kra * ctab + krb * stab
    for hh in range(MLA_HEADS):
        ka_ref[0, hh] = (kv[:, hh * LANES:(hh + 1) * LANES] + krope).astype(BF16)
    v_mla = kv[:, MLA_HEADS * LANES:]

    ctrl = (lane >= CTRL_LOGIT) & (lane < CTRL_LOGIT + FOX_HEADS)
    fl = kra + fb_ref[...]
    lsig = jnp.minimum(fl, 0.0) - jnp.log1p(jnp.exp(-jnp.abs(fl)))
    lf = jnp.where(ctrl, lsig, 0.0)
    hi, mid, lo = _split3(lf)
    p1 = (hi + pltpu.roll(mid, 8, 1) + pltpu.roll(lo, 16, 1)).astype(BF16)
    cs = jnp.dot(tri_ref[...], p1, preferred_element_type=F32)
    cs = cs + pltpu.roll(cs, LANES - 8, 1) + pltpu.roll(cs, LANES - 16, 1)
    fcum = jnp.where(ctrl, cs, 0.0) + carry_ref[...]
    carry_ref[...] = fcum[ts - 1:ts, :]

    hi, mid, lo = _split3(fcum * LOG2E)
    p2 = hi + pltpu.roll(mid, 8, 1) + pltpu.roll(lo, 16, 1)
    p2 = jnp.where(lane == CTRL_ONE, 1.0, p2).astype(BF16)
    aug = jnp.dot(p2, eqk_ref[...], preferred_element_type=F32)

    fq = proj[:, C_FQ0:C_FQ0 + FOX_W]
    fk = proj[:, C_FK0:C_FK0 + FOX_W]
    low = lane < FOX_HEAD_DIM
    koff = FOX_HEADS * LANES
    for j in range(FOX_HEADS // 2):
        sl = slice(j * LANES, (j + 1) * LANES)
        for src, off, dst in ((fq, 0, qa_ref), (fk, koff, ka_ref)):
            blk = src[:, sl] * LOG2E if dst is qa_ref else src[:, sl]
            blk_r = pltpu.roll(blk, FOX_HEAD_DIM, 1)
            h0 = 2 * j
            a0 = aug[:, off + h0 * LANES:off + (h0 + 1) * LANES]
            a1 = aug[:, off + (h0 + 1) * LANES:off + (h0 + 2) * LANES]
            dst[0, MLA_HEADS + h0] = jnp.where(low, blk, a0).astype(BF16)
            dst[0, MLA_HEADS + h0 + 1] = jnp.where(low, blk_r, a1).astype(BF16)

    fv = proj[:, C_FV0:C_FV0 + FOX_W]
    vt_ref[0, 0, 0:MLA_HEADS * MLA_V, :] = v_mla.T.astype(BF16)
    vt_ref[0, 0, MLA_HEADS * MLA_V:, :] = fv.T.astype(BF16)


def _inproj(x, mod_l, g, win, qg, kvg, wuqa, wuqb, wukv, fb_row, ctab, stab, tri, eqk, *, ts):
    B, S, D = x.shape
    nst = S // ts
    const2 = lambda b, s: (0, 0)
    kern = functools.partial(_inproj_kernel, ts=ts)
    return pl.pallas_call(
        kern,
        grid=(B, nst),
        in_specs=[
            pl.BlockSpec((1, ts, D), lambda b, s: (b, s, 0)),
            pl.BlockSpec((1, 6, D), lambda b, s: (b, 0, 0)),
            pl.BlockSpec((1, D), const2),
            pl.BlockSpec(win.shape, const2),
            pl.BlockSpec(qg.shape, const2),
            pl.BlockSpec(kvg.shape, const2),
            pl.BlockSpec(wuqa.shape, const2),
            pl.BlockSpec(wuqb.shape, const2),
            pl.BlockSpec(wukv.shape, const2),
            pl.BlockSpec(fb_row.shape, const2),
            pl.BlockSpec((1, ts, LANES), lambda b, s: (b, s, 0)),
            pl.BlockSpec((1, ts, LANES), lambda b, s: (b, s, 0)),
            pl.BlockSpec(tri.shape, const2),
            pl.BlockSpec(eqk.shape, const2),
        ],
        out_specs=[
            pl.BlockSpec((1, N_HEADS, ts, LANES), lambda b, s: (b, 0, s, 0)),
            pl.BlockSpec((1, N_HEADS, ts, LANES), lambda b, s: (b, 0, s, 0)),
            pl.BlockSpec((1, 1, N_HEADS * MLA_V, ts), lambda b, s: (b, s, 0, 0)),
        ],
        out_shape=[
            jax.ShapeDtypeStruct((B, N_HEADS, S, LANES), BF16),
            jax.ShapeDtypeStruct((B, N_HEADS, S, LANES), BF16),
            jax.ShapeDtypeStruct((B, nst, N_HEADS * MLA_V, ts), BF16),
        ],
        scratch_shapes=[pltpu.VMEM((1, LANES), F32)],
        compiler_params=pltpu.CompilerParams(
            dimension_semantics=("arbitrary", "arbitrary"), vmem_limit_bytes=VMEM_LIMIT),
    )(x, mod_l, g, win, qg, kvg, wuqa, wuqb, wukv, fb_row, ctab, stab, tri, eqk)


def _colmax8(st, groups):
    tk, tq = st.shape
    v = st.reshape(groups, tk // (8 * groups), 8, tq)
    return jnp.max(jnp.max(v, axis=1), axis=0)


def _attn_kernel(q_ref, k_ref, vt_ref, o_ref, s_scr, tmax_scr, m_scr, acc_scr,
                 *, tq, tk, hp, nq):
    dv = MLA_V
    groups = 4
    ones = jnp.ones((ONES_ROWS, tk), BF16)

    def qk(qi, j):
        q0 = pl.multiple_of(qi * tq, tq)
        return [lax.dot_general(k_ref[0, h, pl.ds(pl.multiple_of(j * tk, tk), tk), :],
                                q_ref[0, h, pl.ds(q0, tq), :], (((1,), (1,)), ((), ())),
                                preferred_element_type=F32) for h in range(hp)]

    def park(sts, qi, j, masked):
        for h in range(hp):
            st = sts[h]
            if masked:
                kpos = j * tk + lax.broadcasted_iota(jnp.int32, (tk, tq), 0)
                qpos = qi * tq + lax.broadcasted_iota(jnp.int32, (tk, tq), 1)
                st = jnp.where(kpos <= qpos, st, NEG_BIG)
            s_scr[h] = st
            tmax_scr[h] = _colmax8(st, groups)

    def update(j):
        for h in range(hp):
            m = m_scr[h]
            m_new = jnp.maximum(m, jnp.max(tmax_scr[h], axis=0, keepdims=True))
            alpha = jnp.exp2(m - m_new)
            p = jnp.exp2((s_scr[h] - m_new).astype(BF16))
            vt1 = jnp.concatenate([vt_ref[0, j, h * dv:(h + 1) * dv, :], ones], axis=0)
            acc_scr[h] = alpha * acc_scr[h] + jnp.dot(vt1, p, preferred_element_type=F32)
            m_scr[h] = m_new

    def reset_state():
        m_scr[...] = jnp.full(m_scr.shape, NEG_BIG, F32)
        acc_scr[...] = jnp.zeros(acc_scr.shape, F32)

    def finish(qi):
        q0 = pl.multiple_of(qi * tq, tq)
        for h in range(hp):
            acc = acc_scr[h]
            o_ref[0, h * dv:(h + 1) * dv, pl.ds(q0, tq)] = acc[0:dv, :] / acc[dv:dv + 1, :]
        reset_state()

    reset_state()
    park(qk(0, 0), 0, 0, True)

    def q_tile(qi, c):
        nf = (qi * tq) // tk

        def below_diagonal(t, c2):
            sts = qk(qi, t + 1)
            update(t)
            park(sts, qi, t + 1, False)
            return c2

        lax.fori_loop(0, nf - 1, below_diagonal, 0)

        @pl.when(nf > 0)
        def _():
            sts = qk(qi, nf)
            update(nf - 1)
            park(sts, qi, nf, True)

        nxt = qi + 1
        nxt_on_diagonal = (nxt * tq) // tk == 0

        @pl.when((nxt < nq) & nxt_on_diagonal)
        def _():
            sts = qk(nxt, 0)
            update(nf)
            finish(qi)
            park(sts, nxt, 0, True)

        @pl.when((nxt < nq) & jnp.logical_not(nxt_on_diagonal))
        def _():
            sts = qk(nxt, 0)
            update(nf)
            finish(qi)
            park(sts, nxt, 0, False)

        @pl.when(nxt == nq)
        def _():
            update(nf)
            finish(qi)

        return c

    lax.fori_loop(0, nq, q_tile, 0)


def _attention(qa, ka, vt, *, tq, tk, hp):
    B, H, S, _ = qa.shape
    nkt = vt.shape[1]
    assert S % tq == 0 and nkt * tk == S and H % hp == 0 and tk % tq == 0
    kern = functools.partial(_attn_kernel, tq=tq, tk=tk, hp=hp, nq=S // tq)
    return pl.pallas_call(
        kern,
        grid=(B, H // hp),
        in_specs=[
            pl.BlockSpec((1, hp, S, LANES), lambda b, g: (b, g, 0, 0)),
            pl.BlockSpec((1, hp, S, LANES), lambda b, g: (b, g, 0, 0)),
            pl.BlockSpec((1, nkt, hp * MLA_V, tk), lambda b, g: (b, 0, g, 0)),
        ],
        out_specs=pl.BlockSpec((1, hp * MLA_V, S), lambda b, g: (b, g, 0)),
        out_shape=jax.ShapeDtypeStruct((B, H * MLA_V, S), F32),
        scratch_shapes=[
            pltpu.VMEM((hp, tk, tq), F32),
            pltpu.VMEM((hp, 8, tq), F32),
            pltpu.VMEM((hp, 1, tq), F32),
            pltpu.VMEM((hp, MLA_V + ONES_ROWS, tq), F32),
        ],
        compiler_params=pltpu.CompilerParams(
            dimension_semantics=("arbitrary", "arbitrary"), vmem_limit_bytes=VMEM_LIMIT),
    )(qa, ka, vt)


def _top2_gates(logits, lane):
    lg = jnp.where(lane < N_EXPERTS, logits, NEG_BIG)
    m1 = jnp.max(lg, axis=1, keepdims=True)
    i1 = jnp.min(jnp.where(lg == m1, lane, LANES), axis=1, keepdims=True)
    lg2 = jnp.where(lane == i1, NEG_BIG, lg)
    m2 = jnp.max(lg2, axis=1, keepdims=True)
    i2 = jnp.min(jnp.where(lg2 == m2, lane, LANES), axis=1, keepdims=True)
    e2 = jnp.exp(m2 - m1)
    den = 1.0 + e2
    return jnp.where(lane == i1, 1.0 / den, 0.0) + jnp.where(lane == i2, e2 / den, 0.0)


def _outproj_kernel(*refs, ts, with_router):
    if with_router:
        (ot_ref, x_ref, mod_ref, mg_ref, fg_ref, wo_ref, ng_ref, rw_ref,
         x1_ref, h2_ref, gates_ref) = refs
    else:
        ot_ref, x_ref, mod_ref, mg_ref, fg_ref, wo_ref, ng_ref, x1_ref, h2_ref = refs
    half = MLA_HEADS * MLA_V
    om = ot_ref[0, 0:half, :].T
    of = ot_ref[0, half:, :].T
    on = jnp.concatenate([_rms(om) * mg_ref[...], _rms(of) * fg_ref[...]], axis=1).astype(BF16)
    mix = jnp.dot(on, wo_ref[...], preferred_element_type=F32)
    x1 = x_ref[0] + mod_ref[0, 2:3, :] * mix
    x1_ref[0] = x1
    h2 = ((_rms(x1) * ng_ref[...]) * (1.0 + mod_ref[0, 4:5, :]) + mod_ref[0, 3:4, :]).astype(BF16)
    h2_ref[0] = h2
    if with_router:
        logits = jnp.dot(h2, rw_ref[...], preferred_element_type=F32)
        lane = lax.broadcasted_iota(jnp.int32, (ts, LANES), 1)
        gates_ref[0] = _top2_gates(logits, lane)


def _outproj(ot, x, mod_l, mg, fg, wo, ng, rw, *, ts):
    B, S, D = x.shape
    with_router = rw is not None
    const2 = lambda b, s: (0, 0)
    in_specs = [
        pl.BlockSpec((1, ot.shape[1], ts), lambda b, s: (b, 0, s)),
        pl.BlockSpec((1, ts, D), lambda b, s: (b, s, 0)),
        pl.BlockSpec((1, 6, D), lambda b, s: (b, 0, 0)),
        pl.BlockSpec(mg.shape, const2),
        pl.BlockSpec(fg.shape, const2),
        pl.BlockSpec(wo.shape, const2),
        pl.BlockSpec(ng.shape, const2),
    ]
    args = [ot, x, mod_l, mg, fg, wo, ng]
    out_specs = [pl.BlockSpec((1, ts, D), lambda b, s: (b, s, 0)),
                 pl.BlockSpec((1, ts, D), lambda b, s: (b, s, 0))]
    out_shape = [jax.ShapeDtypeStruct((B, S, D), F32), jax.ShapeDtypeStruct((B, S, D), BF16)]
    if with_router:
        in_specs.append(pl.BlockSpec(rw.shape, const2))
        args.append(rw)
        out_specs.append(pl.BlockSpec((1, ts, LANES), lambda b, s: (b, s, 0)))
        out_shape.append(jax.ShapeDtypeStruct((B, S, LANES), F32))
    kern = functools.partial(_outproj_kernel, ts=ts, with_router=with_router)
    return pl.pallas_call(
        kern,
        grid=(B, S // ts),
        in_specs=in_specs,
        out_specs=out_specs,
        out_shape=out_shape,
        compiler_params=pltpu.CompilerParams(
            dimension_semantics=("arbitrary", "arbitrary"), vmem_limit_bytes=VMEM_LIMIT),
    )(*args)


def _fuse_gate_up(wg, wu, tf):
    E, D, F = wg.shape
    w = jnp.concatenate([wg.reshape(E, D, F // tf, tf), wu.reshape(E, D, F // tf, tf)], axis=-1)
    return jnp.transpose(w, (0, 2, 1, 3)).astype(BF16)


def _swiglu_tile(x, wgu, tf):
    gu = jnp.dot(x, wgu, preferred_element_type=F32)
    g = gu[:, :tf]
    return g * _sigmoid(g) * gu[:, tf:]


def _ffn_kernel(h_ref, wgu_ref, wd_ref, x_ref, mod_ref, fin_ref, o_ref, acc_ref,
                *, tf, final_norm):
    f = pl.program_id(1)

    @pl.when(f == 0)
    def _():
        acc_ref[...] = jnp.zeros_like(acc_ref)

    a = _swiglu_tile(h_ref[...], wgu_ref[0, 0], tf)
    acc_ref[...] += jnp.dot(a.astype(BF16), wd_ref[...], preferred_element_type=F32)

    @pl.when(f == pl.num_programs(1) - 1)
    def _():
        x2 = x_ref[...] + mod_ref[0, 5:6, :] * acc_ref[...]
        if final_norm:
            x2 = _rms(x2) * fin_ref[...]
        o_ref[...] = x2


def _ffn(h2, wgu, wd, x1, mod_l, fin_g, *, tm, tf, seq, final_norm):
    T, D = h2.shape
    F = wd.shape[0]
    tiles_per_batch = seq // tm
    kern = functools.partial(_ffn_kernel, tf=tf, final_norm=final_norm)
    return pl.pallas_call(
        kern,
        grid=(T // tm, F // tf),
        in_specs=[
            pl.BlockSpec((tm, D), lambda i, f: (i, 0)),
            pl.BlockSpec((1, 1, D, 2 * tf), lambda i, f: (0, f, 0, 0)),
            pl.BlockSpec((tf, D), lambda i, f: (f, 0)),
            pl.BlockSpec((tm, D), lambda i, f: (i, 0)),
            pl.BlockSpec((1, 6, D), lambda i, f: (i // tiles_per_batch, 0, 0)),
            pl.BlockSpec((1, D), lambda i, f: (0, 0)),
        ],
        out_specs=pl.BlockSpec((tm, D), lambda i, f: (i, 0)),
        out_shape=jax.ShapeDtypeStruct((T, D), F32),
        scratch_shapes=[pltpu.VMEM((tm, D), F32)],
        compiler_params=pltpu.CompilerParams(
            dimension_semantics=("arbitrary", "arbitrary"), vmem_limit_bytes=VMEM_LIMIT),
    )(h2, wgu, wd, x1, mod_l, fin_g)


def _moe_kernel(h_ref, gates_ref, lt_ref, wgu_ref, wd_ref, x_ref, mod_ref, fin_ref, o_ref,
                rank_scr, rankt_scr, gatest_scr, xc_scr, gc_scr, yc_scr, nch_ref,
                *, tm, tf, rc, final_norm):
    e = pl.program_id(1)
    f = pl.program_id(2)
    last_f = f == pl.num_programs(2) - 1

    @pl.when((e == 0) & (f == 0))
    def _():
        gates = gates_ref[...]
        sel = gates > 0.0
        selb = jnp.where(sel, 1.0, 0.0).astype(BF16)
        rank = jnp.dot(lt_ref[...], selb, preferred_element_type=F32)
        rank = jnp.where(sel, rank, -1.0)
        rank_scr[...] = rank
        rankt_scr[...] = rank.T
        gatest_scr[...] = gates.T
        o_ref[...] = jnp.zeros_like(o_ref)

    @pl.when(f == 0)
    def _():
        lane = lax.broadcasted_iota(jnp.int32, (tm, LANES), 1)
        cnt = jnp.sum(jnp.where((lane == e) & (rank_scr[...] >= 0.0), 1.0, 0.0))
        nch = (cnt.astype(jnp.int32) + (rc - 1)) // rc
        nch_ref[0] = nch
        rrow = rankt_scr[pl.ds(e, 1), :].astype(jnp.int32)
        grow = gatest_scr[pl.ds(e, 1), :]

        def gather(k, c):
            row = lax.broadcasted_iota(jnp.int32, (rc, tm), 0) + k * rc
            hit = rrow == row
            onehot = jnp.where(hit, 1.0, 0.0).astype(BF16)
            xc_scr[k] = jnp.dot(onehot, h_ref[...], preferred_element_type=F32).astype(BF16)
            gcol = jnp.sum(jnp.where(hit, grow, 0.0), axis=1, keepdims=True)
            gc_scr[k] = jnp.broadcast_to(gcol, (rc, LANES))
            yc_scr[k] = jnp.zeros((rc, yc_scr.shape[2]), F32)
            return c

        lax.fori_loop(0, nch, gather, 0)

    def expert(k, c):
        a = _swiglu_tile(xc_scr[k], wgu_ref[0, 0], tf) * gc_scr[k][:, 0:1]
        yc_scr[k] += jnp.dot(a.astype(BF16), wd_ref[0], preferred_element_type=F32)
        return c

    lax.fori_loop(0, nch_ref[0], expert, 0)

    @pl.when(last_f)
    def _():
        lane = lax.broadcasted_iota(jnp.int32, (tm, LANES), 1)
        rcol = jnp.sum(jnp.where(lane == e, rank_scr[...], 0.0), axis=1,
                       keepdims=True).astype(jnp.int32)

        def scatter(k, c):
            col = lax.broadcasted_iota(jnp.int32, (tm, rc), 1) + k * rc
            onehot_t = jnp.where(rcol == col, 1.0, 0.0).astype(BF16)
            o_ref[...] += jnp.dot(onehot_t, yc_scr[k].astype(BF16), preferred_element_type=F32)
            return c

        lax.fori_loop(0, nch_ref[0], scatter, 0)

    @pl.when((e == pl.num_programs(1) - 1) & last_f)
    def _():
        x2 = x_ref[...] + mod_ref[0, 5:6, :] * o_ref[...]
        if final_norm:
            x2 = _rms(x2) * fin_ref[...]
        o_ref[...] = x2


def _moe(h2, gates, wgu, wd, x1, mod_l, fin_g, *, tm, tf, rc, seq, final_norm):
    T, D = h2.shape
    E, F, _ = wd.shape
    tiles_per_batch = seq // tm
    lt = jnp.asarray(np.tril(np.ones((tm, tm), np.float32), -1), BF16)
    kern = functools.partial(_moe_kernel, tm=tm, tf=tf, rc=rc, final_norm=final_norm)
    return pl.pallas_call(
        kern,
        grid=(T // tm, E, F // tf),
        in_specs=[
            pl.BlockSpec((tm, D), lambda i, e, f: (i, 0)),
            pl.BlockSpec((tm, LANES), lambda i, e, f: (i, 0)),
            pl.BlockSpec((tm, tm), lambda i, e, f: (0, 0)),
            pl.BlockSpec((1, 1, D, 2 * tf), lambda i, e, f: (e, f, 0, 0)),
            pl.BlockSpec((1, tf, D), lambda i, e, f: (e, f, 0)),
            pl.BlockSpec((tm, D), lambda i, e, f: (i, 0)),
            pl.BlockSpec((1, 6, D), lambda i, e, f: (i // tiles_per_batch, 0, 0)),
            pl.BlockSpec((1, D), lambda i, e, f: (0, 0)),
        ],
        out_specs=pl.BlockSpec((tm, D), lambda i, e, f: (i, 0)),
        out_shape=jax.ShapeDtypeStruct((T, D), F32),
        scratch_shapes=[
            pltpu.VMEM((tm, LANES), F32),
            pltpu.VMEM((LANES, tm), F32),
            pltpu.VMEM((LANES, tm), F32),
            pltpu.VMEM((tm // rc, rc, D), BF16),
            pltpu.VMEM((tm // rc, rc, LANES), F32),
            pltpu.VMEM((tm // rc, rc, D), F32),
            pltpu.SMEM((1,), jnp.int32),
        ],
        compiler_params=pltpu.CompilerParams(
            dimension_semantics=("arbitrary", "arbitrary", "arbitrary"),
            vmem_limit_bytes=VMEM_LIMIT),
    )(h2, gates, lt, wgu, wd, x1, mod_l, fin_g)


def _tiles(S):
    ts = min(512, S)
    tq = min(256, S)
    tk = ts
    tm = min(1024, S)
    tf = 896
    rc = 256
    return ts, tq, tk, tm, tf, rc


def kernel(x, c, positions, ada_w, ada_b, attn_norm_g, w_in, q_norm_g, w_uq, kv_norm_g, w_ukv,
           fox_forget_b, mla_out_g, fox_out_g, w_o, ffn_norm_g, dense_w_gate, dense_w_up,
           dense_w_down, router_w, moe_w_gate, moe_w_up, moe_w_down, final_norm_g):
    B, S, D = x.shape
    L = ada_w.shape[0]
    ts, tq, tk, tm, tf, rc = _tiles(S)

    mod = _adaln(c, ada_w, ada_b).reshape(L, B, 6, D)
    ctab, stab = _rope_tables(positions)

    idx_in = _perm_w_in()
    idx_qa, idx_qb = _perm_w_uq()
    idx_kv = _perm_w_ukv()
    col_scale = np.ones((C_END,), np.float32)
    col_scale[C_FQ0:C_FQ0 + FOX_W] = FOX_HEAD_DIM ** -0.5
    tri = jnp.asarray(np.tril(np.ones((ts, ts), np.float32)), BF16)
    eqk = jnp.asarray(_aug_placement(), BF16)

    for l in range(L):
        win = (_take_cols(w_in[l], idx_in) * col_scale).astype(BF16)
        wuqa = _take_cols(w_uq[l], idx_qa).astype(BF16)
        wuqb = _take_cols(w_uq[l], idx_qb).astype(BF16)
        wukv = _take_cols(w_ukv[l], idx_kv).astype(BF16)
        fb_row = jnp.zeros((1, LANES), F32).at[0, CTRL_LOGIT:CTRL_LOGIT + FOX_HEADS].set(
            fox_forget_b[l].astype(F32))
        qa, ka, vt = _inproj(
            x, mod[l], attn_norm_g[l].reshape(1, D), win, q_norm_g[l].reshape(1, -1),
            kv_norm_g[l].reshape(1, -1), wuqa, wuqb, wukv, fb_row, ctab, stab, tri, eqk, ts=ts)
        ot = _attention(qa, ka, vt, tq=tq, tk=tk, hp=4)

        j = l // 2
        is_moe = (l % 2 == 1)
        rw = None
        if is_moe:
            rw = jnp.pad(router_w[j], ((0, 0), (0, LANES - N_EXPERTS))).astype(BF16)
        outs = _outproj(ot, x, mod[l], mla_out_g[l].reshape(1, -1), fox_out_g[l].reshape(1, -1),
                        w_o[l].astype(BF16), ffn_norm_g[l].reshape(1, D), rw, ts=ts)
        x1, h2 = outs[0].reshape(B * S, D), outs[1].reshape(B * S, D)
        fin_g = final_norm_g.reshape(1, D)
        last = l == L - 1
        if is_moe:
            x = _moe(h2, outs[2].reshape(B * S, LANES),
                     _fuse_gate_up(moe_w_gate[j], moe_w_up[j], tf), moe_w_down[j].astype(BF16),
                     x1, mod[l], fin_g, tm=tm, tf=tf, rc=rc, seq=S, final_norm=last)
        else:
            x = _ffn(h2, _fuse_gate_up(dense_w_gate[j][None], dense_w_up[j][None], tf),
                     dense_w_down[j].astype(BF16), x1, mod[l], fin_g,
                     tm=tm, tf=tf, seq=S, final_norm=last)
        x = x.reshape(B, S, D)
    return x
```

```python
import functools

import numpy as np
import jax
import jax.numpy as jnp
from jax import lax
from jax.experimental import pallas as pl
from jax.experimental.pallas import tpu as pltpu

F32 = jnp.float32
BF16 = jnp.bfloat16

D_MODEL = 1024
MLA_HEADS = 8
MLA_V = 64
MLA_NOPE = 64
MLA_ROPE = 32
MLA_Q_LORA = 256
MLA_KV_LORA = 128
FOX_HEADS = 8
FOX_HEAD_DIM = 64
FOX_W = FOX_HEADS * FOX_HEAD_DIM
ROPE_THETA = 10000.0
D_FF = 3584
N_EXPERTS = 8
EPS = 1e-6
N_HEADS = MLA_HEADS + FOX_HEADS
HALF_ROPE = MLA_ROPE // 2

LANES = 128
VMEM_LIMIT = 56 * 1024 * 1024

ROPE_LO = MLA_NOPE
AUG_LO = FOX_HEAD_DIM
CTRL_LOGIT = 96
CTRL_ONE = 120

C_Q0 = 0
C_KV0 = C_Q0 + MLA_Q_LORA
C_KRA0 = C_KV0 + MLA_KV_LORA
C_KRB0 = C_KRA0 + LANES
C_FQ0 = C_KRB0 + LANES
C_FK0 = C_FQ0 + FOX_W
C_FV0 = C_FK0 + FOX_W
C_END = C_FV0 + FOX_W

NEG_BIG = -1e30
LOG2E = 1.4426950408889634
ONES_ROWS = 16


def _rms(v):
    return v * lax.rsqrt(jnp.mean(v * v, axis=-1, keepdims=True) + EPS)


def _sigmoid(v):
    return 1.0 / (1.0 + jnp.exp(-v))


def _split3(v):
    hi = v.astype(BF16).astype(F32)
    r = v - hi
    mid = r.astype(BF16).astype(F32)
    lo = (r - mid).astype(BF16).astype(F32)
    return hi, mid, lo


def _adaln_kernel(c_ref, w_ref, b_ref, o_ref):
    c = c_ref[...]
    ca = (c * _sigmoid(c)).astype(BF16)
    o_ref[0] = jnp.dot(ca, w_ref[0].astype(BF16), preferred_element_type=F32) + b_ref[0]


def _adaln(c, ada_w, ada_b):
    L, D, N = ada_w.shape
    B = c.shape[0]
    tn = 1536
    return pl.pallas_call(
        _adaln_kernel,
        grid=(L, N // tn),
        in_specs=[
            pl.BlockSpec((B, D), lambda l, j: (0, 0)),
            pl.BlockSpec((1, D, tn), lambda l, j: (l, 0, j)),
            pl.BlockSpec((1, 1, tn), lambda l, j: (l, 0, j)),
        ],
        out_specs=pl.BlockSpec((1, B, tn), lambda l, j: (l, 0, j)),
        out_shape=jax.ShapeDtypeStruct((L, B, N), F32),
        compiler_params=pltpu.CompilerParams(
            dimension_semantics=("arbitrary", "arbitrary"), vmem_limit_bytes=VMEM_LIMIT),
    )(c, ada_w, ada_b.reshape(L, 1, N))


def _rope_tab_kernel(pos_ref, freq_ref, cos_ref, sin_ref):
    ang = freq_ref[...] * pos_ref[0]
    cos_ref[0] = jnp.cos(ang)
    sin_ref[0] = jnp.sin(ang)


def _rope_tables(positions):
    B, S = positions.shape
    half = HALF_ROPE
    inv_freq = ROPE_THETA ** (-jnp.arange(half, dtype=F32) / half)
    pos = positions.astype(F32).reshape(B, 1, S)
    cos_t, sin_t = pl.pallas_call(
        _rope_tab_kernel,
        grid=(B,),
        in_specs=[
            pl.BlockSpec((1, 1, S), lambda b: (b, 0, 0)),
            pl.BlockSpec((half, 1), lambda b: (0, 0)),
        ],
        out_specs=[pl.BlockSpec((1, half, S), lambda b: (b, 0, 0))] * 2,
        out_shape=[jax.ShapeDtypeStruct((B, half, S), F32)] * 2,
        compiler_params=pltpu.CompilerParams(dimension_semantics=("arbitrary",)),
    )(pos, inv_freq.reshape(half, 1))
    cos = jnp.transpose(cos_t, (0, 2, 1))
    sin = jnp.transpose(sin_t, (0, 2, 1))
    ones = jnp.ones((B, S, MLA_NOPE), F32)
    z_lo = jnp.zeros((B, S, MLA_NOPE), F32)
    z_hi = jnp.zeros((B, S, LANES - MLA_NOPE - MLA_ROPE), F32)
    ctab = jnp.concatenate([ones, cos, cos, z_hi], axis=-1)
    stab = jnp.concatenate([z_lo, -sin, sin, z_hi], axis=-1)
    return ctab, stab


def _take_cols(w, idx):
    pieces, i, n = [], 0, len(idx)
    while i < n:
        j = i + 1
        if idx[i] < 0:
            while j < n and idx[j] < 0:
                j += 1
            pieces.append(jnp.zeros((w.shape[0], j - i), w.dtype))
        else:
            while j < n and idx[j] == idx[j - 1] + 1:
                j += 1
            pieces.append(w[:, int(idx[i]):int(idx[i]) + (j - i)])
        i = j
    return jnp.concatenate(pieces, axis=1)


def _perm_w_in():
    cq = 0
    ckv = cq + MLA_Q_LORA
    kr = ckv + MLA_KV_LORA
    fq = kr + MLA_ROPE
    fk = fq + FOX_W
    fv = fk + FOX_W
    fl = fv + FOX_W
    idx = -np.ones((C_END,), np.int64)
    idx[C_Q0:C_Q0 + MLA_Q_LORA] = cq + np.arange(MLA_Q_LORA)
    idx[C_KV0:C_KV0 + MLA_KV_LORA] = ckv + np.arange(MLA_KV_LORA)
    h = HALF_ROPE
    idx[C_KRA0 + ROPE_LO:C_KRA0 + ROPE_LO + MLA_ROPE] = kr + np.arange(MLA_ROPE)
    idx[C_KRA0 + CTRL_LOGIT:C_KRA0 + CTRL_LOGIT + FOX_HEADS] = fl + np.arange(FOX_HEADS)
    idx[C_KRB0 + ROPE_LO:C_KRB0 + ROPE_LO + h] = kr + h + np.arange(h)
    idx[C_KRB0 + ROPE_LO + h:C_KRB0 + ROPE_LO + MLA_ROPE] = kr + np.arange(h)
    idx[C_FQ0:C_FQ0 + FOX_W] = fq + np.arange(FOX_W)
    idx[C_FK0:C_FK0 + FOX_W] = fk + np.arange(FOX_W)
    idx[C_FV0:C_FV0 + FOX_W] = fv + np.arange(FOX_W)
    return idx


def _perm_w_uq():
    per = MLA_NOPE + MLA_ROPE
    ia = -np.ones((MLA_HEADS * LANES,), np.int64)
    ib = -np.ones((MLA_HEADS * LANES,), np.int64)
    h2 = HALF_ROPE
    for h in range(MLA_HEADS):
        ia[h * LANES:h * LANES + per] = h * per + np.arange(per)
        ib[h * LANES + ROPE_LO:h * LANES + ROPE_LO + h2] = h * per + MLA_NOPE + h2 + np.arange(h2)
        ib[h * LANES + ROPE_LO + h2:h * LANES + ROPE_LO + MLA_ROPE] = h * per + MLA_NOPE + np.arange(h2)
    return ia, ib


def _perm_w_ukv():
    per = MLA_NOPE + MLA_V
    ik = -np.ones((MLA_HEADS * LANES,), np.int64)
    iv = np.zeros((MLA_HEADS * MLA_V,), np.int64)
    for h in range(MLA_HEADS):
        ik[h * LANES:h * LANES + MLA_NOPE] = h * per + np.arange(MLA_NOPE)
        iv[h * MLA_V:(h + 1) * MLA_V] = h * per + MLA_NOPE + np.arange(MLA_V)
    return np.concatenate([ik, iv])


def _aug_placement():
    e = np.zeros((LANES, 2 * FOX_HEADS * LANES), np.float32)
    koff = FOX_HEADS * LANES
    for h in range(FOX_HEADS):
        for p in range(3):
            src = CTRL_LOGIT + 8 * p + h
            e[src, h * LANES + AUG_LO + p] = 1.0
            e[CTRL_ONE, h * LANES + AUG_LO + 3 + p] = -1.0
            e[CTRL_ONE, koff + h * LANES + AUG_LO + p] = 1.0
            e[src, koff + h * LANES + AUG_LO + 3 + p] = 1.0
    return e


def _inproj_kernel(x_ref, mod_ref, g_ref, win_ref, qg_ref, kvg_ref, wuqa_ref, wuqb_ref, wukv_ref,
                   fb_ref, ct_ref, st_ref, tri_ref, eqk_ref,
                   qa_ref, ka_ref, vt_ref, carry_ref, *, ts):
    si = pl.program_id(1)

    @pl.when(si == 0)
    def _():
        carry_ref[...] = jnp.zeros_like(carry_ref)

    x = x_ref[0]
    shift = mod_ref[0, 0:1, :]
    scale = mod_ref[0, 1:2, :]
    h = ((_rms(x) * g_ref[...]) * (1.0 + scale) + shift).astype(BF16)
    proj = jnp.dot(h, win_ref[...], preferred_element_type=F32)

    ctab = ct_ref[0]
    stab = st_ref[0]
    lane = lax.broadcasted_iota(jnp.int32, (ts, LANES), 1)

    c_q = proj[:, C_Q0:C_Q0 + MLA_Q_LORA]
    cqn = (_rms(c_q) * qg_ref[...]).astype(BF16)
    qa = jnp.dot(cqn, wuqa_ref[...], preferred_element_type=F32)
    qb = jnp.dot(cqn, wuqb_ref[...], preferred_element_type=F32)
    mla_scale = (MLA_NOPE + MLA_ROPE) ** -0.5 * LOG2E
    for hh in range(MLA_HEADS):
        sl = slice(hh * LANES, (hh + 1) * LANES)
        qa_ref[0, hh] = ((qa[:, sl] * ctab + qb[:, sl] * stab) * mla_scale).astype(BF16)

    c_kv = proj[:, C_KV0:C_KV0 + MLA_KV_LORA]
    ckvn = (_rms(c_kv) * kvg_ref[...]).astype(BF16)
    kv = jnp.dot(ckvn, wukv_ref[...], preferred_element_type=F32)
    kra = proj[:, C_KRA0:C_KRA0 + LANES]
    krb = proj[:, C_KRB0:C_KRB0 + LANES]
    krope = kra * ctab + krb * stab
    for hh in range(MLA_HEADS):
        ka_ref[0, hh] = (kv[:, hh * LANES:(hh + 1) * LANES] + krope).astype(BF16)
    v_mla = kv[:, MLA_HEADS * LANES:]

    ctrl = (lane >= CTRL_LOGIT) & (lane < CTRL_LOGIT + FOX_HEADS)
    fl = kra + fb_ref[...]
    lsig = jnp.minimum(fl, 0.0) - jnp.log1p(jnp.exp(-jnp.abs(fl)))
    lf = jnp.where(ctrl, lsig, 0.0)
    hi, mid, lo = _split3(lf)
    p1 = (hi + pltpu.roll(mid, 8, 1) + pltpu.roll(lo, 16, 1)).astype(BF16)
    cs = jnp.dot(tri_ref[...], p1, preferred_element_type=F32)
    cs = cs + pltpu.roll(cs, LANES - 8, 1) + pltpu.roll(cs, LANES - 16, 1)
    fcum = jnp.where(ctrl, cs, 0.0) + carry_ref[...]
    carry_ref[...] = fcum[ts - 1:ts, :]

    hi, mid, lo = _split3(fcum * LOG2E)
    p2 = hi + pltpu.roll(mid, 8, 1) + pltpu.roll(lo, 16, 1)
    p2 = jnp.where(lane == CTRL_ONE, 1.0, p2).astype(BF16)
    aug = jnp.dot(p2, eqk_ref[...], preferred_element_type=F32)

    fq = proj[:, C_FQ0:C_FQ0 + FOX_W]
    fk = proj[:, C_FK0:C_FK0 + FOX_W]
    low = lane < FOX_HEAD_DIM
    koff = FOX_HEADS * LANES
    for j in range(FOX_HEADS // 2):
        sl = slice(j * LANES, (j + 1) * LANES)
        for src, off, dst in ((fq, 0, qa_ref), (fk, koff, ka_ref)):
            blk = src[:, sl] * LOG2E if dst is qa_ref else src[:, sl]
            blk_r = pltpu.roll(blk, FOX_HEAD_DIM, 1)
            h0 = 2 * j
            a0 = aug[:, off + h0 * LANES:off + (h0 + 1) * LANES]
            a1 = aug[:, off + (h0 + 1) * LANES:off + (h0 + 2) * LANES]
            dst[0, MLA_HEADS + h0] = jnp.where(low, blk, a0).astype(BF16)
            dst[0, MLA_HEADS + h0 + 1] = jnp.where(low, blk_r, a1).astype(BF16)

    fv = proj[:, C_FV0:C_FV0 + FOX_W]
    vt_ref[0, 0, 0:MLA_HEADS * MLA_V, :] = v_mla.T.astype(BF16)
    vt_ref[0, 0, MLA_HEADS * MLA_V:, :] = fv.T.astype(BF16)


def _inproj(x, mod_l, g, win, qg, kvg, wuqa, wuqb, wukv, fb_row, ctab, stab, tri, eqk, *, ts):
    B, S, D = x.shape
    nst = S // ts
    const2 = lambda b, s: (0, 0)
    kern = functools.partial(_inproj_kernel, ts=ts)
    return pl.pallas_call(
        kern,
        grid=(B, nst),
        in_specs=[
            pl.BlockSpec((1, ts, D), lambda b, s: (b, s, 0)),
            pl.BlockSpec((1, 6, D), lambda b, s: (b, 0, 0)),
            pl.BlockSpec((1, D), const2),
            pl.BlockSpec(win.shape, const2),
            pl.BlockSpec(qg.shape, const2),
            pl.BlockSpec(kvg.shape, const2),
            pl.BlockSpec(wuqa.shape, const2),
            pl.BlockSpec(wuqb.shape, const2),
            pl.BlockSpec(wukv.shape, const2),
            pl.BlockSpec(fb_row.shape, const2),
            pl.BlockSpec((1, ts, LANES), lambda b, s: (b, s, 0)),
            pl.BlockSpec((1, ts, LANES), lambda b, s: (b, s, 0)),
            pl.BlockSpec(tri.shape, const2),
            pl.BlockSpec(eqk.shape, const2),
        ],
        out_specs=[
            pl.BlockSpec((1, N_HEADS, ts, LANES), lambda b, s: (b, 0, s, 0)),
            pl.BlockSpec((1, N_HEADS, ts, LANES), lambda b, s: (b, 0, s, 0)),
            pl.BlockSpec((1, 1, N_HEADS * MLA_V, ts), lambda b, s: (b, s, 0, 0)),
        ],
        out_shape=[
            jax.ShapeDtypeStruct((B, N_HEADS, S, LANES), BF16),
            jax.ShapeDtypeStruct((B, N_HEADS, S, LANES), BF16),
            jax.ShapeDtypeStruct((B, nst, N_HEADS * MLA_V, ts), BF16),
        ],
        scratch_shapes=[pltpu.VMEM((1, LANES), F32)],
        compiler_params=pltpu.CompilerParams(
            dimension_semantics=("arbitrary", "arbitrary"), vmem_limit_bytes=VMEM_LIMIT),
    )(x, mod_l, g, win, qg, kvg, wuqa, wuqb, wukv, fb_row, ctab, stab, tri, eqk)


def _colmax8(st, groups):
    tk, tq = st.shape
    v = st.reshape(groups, tk // (8 * groups), 8, tq)
    return jnp.max(jnp.max(v, axis=1), axis=0)


def _attn_kernel(q_ref, k_ref, vt_ref, o_ref, s_scr, tmax_scr, m_scr, acc_scr,
                 *, tq, tk, hp, nq):
    dv = MLA_V
    groups = 4
    ones = jnp.ones((ONES_ROWS, tk), BF16)

    def qk(qi, j):
        q0 = pl.multiple_of(qi * tq, tq)
        return [lax.dot_general(k_ref[0, h, pl.ds(pl.multiple_of(j * tk, tk), tk), :],
                                q_ref[0, h, pl.ds(q0, tq), :], (((1,), (1,)), ((), ())),
                                preferred_element_type=F32) for h in range(hp)]

    def park(sts, qi, j, masked):
        for h in range(hp):
            st = sts[h]
            if masked:
                kpos = j * tk + lax.broadcasted_iota(jnp.int32, (tk, tq), 0)
                qpos = qi * tq + lax.broadcasted_iota(jnp.int32, (tk, tq), 1)
                st = jnp.where(kpos <= qpos, st, NEG_BIG)
            s_scr[h] = st
            tmax_scr[h] = _colmax8(st, groups)

    def update(j):
        for h in range(hp):
            m = m_scr[h]
            m_new = jnp.maximum(m, jnp.max(tmax_scr[h], axis=0, keepdims=True))
            alpha = jnp.exp2(m - m_new)
            p = jnp.exp2((s_scr[h] - m_new).astype(BF16))
            vt1 = jnp.concatenate([vt_ref[0, j, h * dv:(h + 1) * dv, :], ones], axis=0)
            acc_scr[h] = alpha * acc_scr[h] + jnp.dot(vt1, p, preferred_element_type=F32)
            m_scr[h] = m_new

    def reset_state():
        m_scr[...] = jnp.full(m_scr.shape, NEG_BIG, F32)
        acc_scr[...] = jnp.zeros(acc_scr.shape, F32)

    def finish(qi):
        q0 = pl.multiple_of(qi * tq, tq)
        for h in range(hp):
            acc = acc_scr[h]
            o_ref[0, h * dv:(h + 1) * dv, pl.ds(q0, tq)] = (
                acc[0:dv, :] / acc[dv:dv + 1, :]).astype(o_ref.dtype)
        reset_state()

    reset_state()
    park(qk(0, 0), 0, 0, True)

    def q_tile(qi, c):
        nf = (qi * tq) // tk

        def below_diagonal(t, c2):
            sts = qk(qi, t + 1)
            update(t)
            park(sts, qi, t + 1, False)
            return c2

        lax.fori_loop(0, nf - 1, below_diagonal, 0)

        @pl.when(nf > 0)
        def _():
            sts = qk(qi, nf)
            update(nf - 1)
            park(sts, qi, nf, True)

        nxt = qi + 1
        nxt_on_diagonal = (nxt * tq) // tk == 0

        @pl.when((nxt < nq) & nxt_on_diagonal)
        def _():
            sts = qk(nxt, 0)
            update(nf)
            finish(qi)
            park(sts, nxt, 0, True)

        @pl.when((nxt < nq) & jnp.logical_not(nxt_on_diagonal))
        def _():
            sts = qk(nxt, 0)
            update(nf)
            finish(qi)
            park(sts, nxt, 0, False)

        @pl.when(nxt == nq)
        def _():
            update(nf)
            finish(qi)

        return c

    lax.fori_loop(0, nq, q_tile, 0)


def _attention(qa, ka, vt, *, tq, tk, hp):
    B, H, S, _ = qa.shape
    nkt = vt.shape[1]
    assert S % tq == 0 and nkt * tk == S and H % hp == 0 and tk % tq == 0
    kern = functools.partial(_attn_kernel, tq=tq, tk=tk, hp=hp, nq=S // tq)
    return pl.pallas_call(
        kern,
        grid=(B, H // hp),
        in_specs=[
            pl.BlockSpec((1, hp, S, LANES), lambda b, g: (b, g, 0, 0)),
            pl.BlockSpec((1, hp, S, LANES), lambda b, g: (b, g, 0, 0)),
            pl.BlockSpec((1, nkt, hp * MLA_V, tk), lambda b, g: (b, 0, g, 0)),
        ],
        out_specs=pl.BlockSpec((1, hp * MLA_V, S), lambda b, g: (b, g, 0)),
        out_shape=jax.ShapeDtypeStruct((B, H * MLA_V, S), BF16),
        scratch_shapes=[
            pltpu.VMEM((hp, tk, tq), F32),
            pltpu.VMEM((hp, 8, tq), F32),
            pltpu.VMEM((hp, 1, tq), F32),
            pltpu.VMEM((hp, MLA_V + ONES_ROWS, tq), F32),
        ],
        compiler_params=pltpu.CompilerParams(
            dimension_semantics=("arbitrary", "arbitrary"), vmem_limit_bytes=VMEM_LIMIT),
    )(qa, ka, vt)


def _top2_gates(logits, lane):
    lg = jnp.where(lane < N_EXPERTS, logits, NEG_BIG)
    m1 = jnp.max(lg, axis=1, keepdims=True)
    i1 = jnp.min(jnp.where(lg == m1, lane, LANES), axis=1, keepdims=True)
    lg2 = jnp.where(lane == i1, NEG_BIG, lg)
    m2 = jnp.max(lg2, axis=1, keepdims=True)
    i2 = jnp.min(jnp.where(lg2 == m2, lane, LANES), axis=1, keepdims=True)
    e2 = jnp.exp(m2 - m1)
    den = 1.0 + e2
    return jnp.where(lane == i1, 1.0 / den, 0.0) + jnp.where(lane == i2, e2 / den, 0.0)


def _outproj_kernel(*refs, ts, with_router):
    if with_router:
        (ot_ref, x_ref, mod_ref, mg_ref, fg_ref, wo_ref, ng_ref, rw_ref,
         x1_ref, h2_ref, gates_ref) = refs
    else:
        ot_ref, x_ref, mod_ref, mg_ref, fg_ref, wo_ref, ng_ref, x1_ref, h2_ref = refs
    half = MLA_HEADS * MLA_V
    om = ot_ref[0, 0:half, :].astype(F32).T
    of = ot_ref[0, half:, :].astype(F32).T
    on = jnp.concatenate([_rms(om) * mg_ref[...], _rms(of) * fg_ref[...]], axis=1).astype(BF16)
    mix = jnp.dot(on, wo_ref[...], preferred_element_type=F32)
    x1 = x_ref[0] + mod_ref[0, 2:3, :] * mix
    x1_ref[0] = x1
    h2 = ((_rms(x1) * ng_ref[...]) * (1.0 + mod_ref[0, 4:5, :]) + mod_ref[0, 3:4, :]).astype(BF16)
    h2_ref[0] = h2
    if with_router:
        logits = jnp.dot(h2, rw_ref[...], preferred_element_type=F32)
        lane = lax.broadcasted_iota(jnp.int32, (ts, LANES), 1)
        gates_ref[0] = _top2_gates(logits, lane)


def _outproj(ot, x, mod_l, mg, fg, wo, ng, rw, *, ts):
    B, S, D = x.shape
    with_router = rw is not None
    const2 = lambda b, s: (0, 0)
    in_specs = [
        pl.BlockSpec((1, ot.shape[1], ts), lambda b, s: (b, 0, s)),
        pl.BlockSpec((1, ts, D), lambda b, s: (b, s, 0)),
        pl.BlockSpec((1, 6, D), lambda b, s: (b, 0, 0)),
        pl.BlockSpec(mg.shape, const2),
        pl.BlockSpec(fg.shape, const2),
        pl.BlockSpec(wo.shape, const2),
        pl.BlockSpec(ng.shape, const2),
    ]
    args = [ot, x, mod_l, mg, fg, wo, ng]
    out_specs = [pl.BlockSpec((1, ts, D), lambda b, s: (b, s, 0)),
                 pl.BlockSpec((1, ts, D), lambda b, s: (b, s, 0))]
    out_shape = [jax.ShapeDtypeStruct((B, S, D), F32), jax.ShapeDtypeStruct((B, S, D), BF16)]
    if with_router:
        in_specs.append(pl.BlockSpec(rw.shape, const2))
        args.append(rw)
        out_specs.append(pl.BlockSpec((1, ts, LANES), lambda b, s: (b, s, 0)))
        out_shape.append(jax.ShapeDtypeStruct((B, S, LANES), F32))
    kern = functools.partial(_outproj_kernel, ts=ts, with_router=with_router)
    return pl.pallas_call(
        kern,
        grid=(B, S // ts),
        in_specs=in_specs,
        out_specs=out_specs,
        out_shape=out_shape,
        compiler_params=pltpu.CompilerParams(
            dimension_semantics=("arbitrary", "arbitrary"), vmem_limit_bytes=VMEM_LIMIT),
    )(*args)


def _fuse_gate_up(wg, wu, tf):
    E, D, F = wg.shape
    w = jnp.concatenate([wg.reshape(E, D, F // tf, tf), wu.reshape(E, D, F // tf, tf)], axis=-1)
    return jnp.transpose(w, (0, 2, 1, 3)).astype(BF16)


def _swiglu_tile(x, wgu, tf):
    gu = jnp.dot(x, wgu, preferred_element_type=F32)
    g = gu[:, :tf]
    return g * _sigmoid(g) * gu[:, tf:]


def _ffn_kernel(h_ref, wgu_ref, wd_ref, x_ref, mod_ref, fin_ref, o_ref, acc_ref,
                *, tf, final_norm):
    f = pl.program_id(1)

    @pl.when(f == 0)
    def _():
        acc_ref[...] = jnp.zeros_like(acc_ref)

    a = _swiglu_tile(h_ref[...], wgu_ref[0, 0], tf)
    acc_ref[...] += jnp.dot(a.astype(BF16), wd_ref[...], preferred_element_type=F32)

    @pl.when(f == pl.num_programs(1) - 1)
    def _():
        x2 = x_ref[...] + mod_ref[0, 5:6, :] * acc_ref[...]
        if final_norm:
            x2 = _rms(x2) * fin_ref[...]
        o_ref[...] = x2


def _ffn(h2, wgu, wd, x1, mod_l, fin_g, *, tm, tf, seq, final_norm):
    T, D = h2.shape
    F = wd.shape[0]
    tiles_per_batch = seq // tm
    kern = functools.partial(_ffn_kernel, tf=tf, final_norm=final_norm)
    return pl.pallas_call(
        kern,
        grid=(T // tm, F // tf),
        in_specs=[
            pl.BlockSpec((tm, D), lambda i, f: (i, 0)),
            pl.BlockSpec((1, 1, D, 2 * tf), lambda i, f: (0, f, 0, 0)),
            pl.BlockSpec((tf, D), lambda i, f: (f, 0)),
            pl.BlockSpec((tm, D), lambda i, f: (i, 0)),
            pl.BlockSpec((1, 6, D), lambda i, f: (i // tiles_per_batch, 0, 0)),
            pl.BlockSpec((1, D), lambda i, f: (0, 0)),
        ],
        out_specs=pl.BlockSpec((tm, D), lambda i, f: (i, 0)),
        out_shape=jax.ShapeDtypeStruct((T, D), F32),
        scratch_shapes=[pltpu.VMEM((tm, D), F32)],
        compiler_params=pltpu.CompilerParams(
            dimension_semantics=("arbitrary", "arbitrary"), vmem_limit_bytes=VMEM_LIMIT),
    )(h2, wgu, wd, x1, mod_l, fin_g)


def _moe_kernel(h_ref, gates_ref, lt_ref, wgu_ref, wd_ref, x_ref, mod_ref, fin_ref, o_ref,
                rank_scr, rankt_scr, gatest_scr, xc_scr, gc_scr, yc_scr, nch_ref,
                *, tm, tf, rc, final_norm):
    rcp = xc_scr.shape[1]
    e = pl.program_id(1)
    f = pl.program_id(2)
    last_f = f == pl.num_programs(2) - 1

    @pl.when((e == 0) & (f == 0))
    def _():
        gates = gates_ref[...]
        sel = gates > 0.0
        selb = jnp.where(sel, 1.0, 0.0).astype(BF16)
        rank = jnp.dot(lt_ref[...], selb, preferred_element_type=F32)
        rank = jnp.where(sel, rank, -1.0)
        rank_scr[...] = rank
        rankt_scr[...] = rank.T
        gatest_scr[...] = gates.T
        o_ref[...] = jnp.zeros_like(o_ref)

    @pl.when(f == 0)
    def _():
        lane = lax.broadcasted_iota(jnp.int32, (tm, LANES), 1)
        cnt = jnp.sum(jnp.where((lane == e) & (rank_scr[...] >= 0.0), 1.0, 0.0))
        nch = (cnt.astype(jnp.int32) + (rc - 1)) // rc
        nch_ref[0] = nch
        rrow = rankt_scr[pl.ds(e, 1), :].astype(jnp.int32)
        grow = gatest_scr[pl.ds(e, 1), :]

        def gather(k, c):
            row = lax.broadcasted_iota(jnp.int32, (rc, tm), 0) + k * rc
            hit = rrow == row
            onehot = jnp.where(hit, 1.0, 0.0).astype(BF16)
            xc_scr[k, 0:rc] = jnp.dot(onehot, h_ref[...], preferred_element_type=F32).astype(BF16)
            gcol = jnp.sum(jnp.where(hit, grow, 0.0), axis=1, keepdims=True)
            gc_scr[k, 0:rc] = jnp.broadcast_to(gcol, (rc, LANES))
            yc_scr[k] = jnp.zeros(yc_scr.shape[1:], F32)
            return c

        lax.fori_loop(0, nch, gather, 0)

    def expert(k, c):
        a = _swiglu_tile(xc_scr[k, 0:rc], wgu_ref[0, 0], tf) * gc_scr[k, 0:rc, 0:1]
        yc_scr[k, 0:rc] += jnp.dot(a.astype(BF16), wd_ref[0], preferred_element_type=F32)
        return c

    lax.fori_loop(0, nch_ref[0], expert, 0)

    @pl.when(last_f)
    def _():
        lane = lax.broadcasted_iota(jnp.int32, (tm, LANES), 1)
        rcol = jnp.sum(jnp.where(lane == e, rank_scr[...], 0.0), axis=1,
                       keepdims=True).astype(jnp.int32)

        def scatter(k, c):
            col = lax.broadcasted_iota(jnp.int32, (tm, rcp), 1)
            hit = (rcol == col + k * rc) & (col < rc)
            onehot_t = jnp.where(hit, 1.0, 0.0).astype(BF16)
            o_ref[...] += jnp.dot(onehot_t, yc_scr[k].astype(BF16), preferred_element_type=F32)
            return c

        lax.fori_loop(0, nch_ref[0], scatter, 0)

    @pl.when((e == pl.num_programs(1) - 1) & last_f)
    def _():
        x2 = x_ref[...] + mod_ref[0, 5:6, :] * o_ref[...]
        if final_norm:
            x2 = _rms(x2) * fin_ref[...]
        o_ref[...] = x2


def _moe(h2, gates, wgu, wd, x1, mod_l, fin_g, *, tm, tf, rc, seq, final_norm):
    T, D = h2.shape
    E, F, _ = wd.shape
    tiles_per_batch = seq // tm
    lt = jnp.asarray(np.tril(np.ones((tm, tm), np.float32), -1), BF16)
    ncap = -(-tm // rc)
    rcp = -(-rc // LANES) * LANES
    kern = functools.partial(_moe_kernel, tm=tm, tf=tf, rc=rc, final_norm=final_norm)
    return pl.pallas_call(
        kern,
        grid=(T // tm, E, F // tf),
        in_specs=[
            pl.BlockSpec((tm, D), lambda i, e, f: (i, 0)),
            pl.BlockSpec((tm, LANES), lambda i, e, f: (i, 0)),
            pl.BlockSpec((tm, tm), lambda i, e, f: (0, 0)),
            pl.BlockSpec((1, 1, D, 2 * tf), lambda i, e, f: (e, f, 0, 0)),
            pl.BlockSpec((1, tf, D), lambda i, e, f: (e, f, 0)),
            pl.BlockSpec((tm, D), lambda i, e, f: (i, 0)),
            pl.BlockSpec((1, 6, D), lambda i, e, f: (i // tiles_per_batch, 0, 0)),
            pl.BlockSpec((1, D), lambda i, e, f: (0, 0)),
        ],
        out_specs=pl.BlockSpec((tm, D), lambda i, e, f: (i, 0)),
        out_shape=jax.ShapeDtypeStruct((T, D), F32),
        scratch_shapes=[
            pltpu.VMEM((tm, LANES), F32),
            pltpu.VMEM((LANES, tm), F32),
            pltpu.VMEM((LANES, tm), F32),
            pltpu.VMEM((ncap, rcp, D), BF16),
            pltpu.VMEM((ncap, rcp, LANES), F32),
            pltpu.VMEM((ncap, rcp, D), F32),
            pltpu.SMEM((1,), jnp.int32),
        ],
        compiler_params=pltpu.CompilerParams(
            dimension_semantics=("arbitrary", "arbitrary", "arbitrary"),
            vmem_limit_bytes=VMEM_LIMIT),
    )(h2, gates, lt, wgu, wd, x1, mod_l, fin_g)


def _tiles(S):
    ts = min(512, S)
    tq = min(512, S)
    tk = ts
    tm = min(1024, S)
    tf = 896
    rc = tm // 4 + 48
    assert rc % 16 == 0
    return ts, tq, tk, tm, tf, rc


def kernel(x, c, positions, ada_w, ada_b, attn_norm_g, w_in, q_norm_g, w_uq, kv_norm_g, w_ukv,
           fox_forget_b, mla_out_g, fox_out_g, w_o, ffn_norm_g, dense_w_gate, dense_w_up,
           dense_w_down, router_w, moe_w_gate, moe_w_up, moe_w_down, final_norm_g):
    B, S, D = x.shape
    L = ada_w.shape[0]
    ts, tq, tk, tm, tf, rc = _tiles(S)

    mod = _adaln(c, ada_w, ada_b).reshape(L, B, 6, D)
    ctab, stab = _rope_tables(positions)

    idx_in = _perm_w_in()
    idx_qa, idx_qb = _perm_w_uq()
    idx_kv = _perm_w_ukv()
    col_scale = np.ones((C_END,), np.float32)
    col_scale[C_FQ0:C_FQ0 + FOX_W] = FOX_HEAD_DIM ** -0.5
    tri = jnp.asarray(np.tril(np.ones((ts, ts), np.float32)), BF16)
    eqk = jnp.asarray(_aug_placement(), BF16)

    for l in range(L):
        win = (_take_cols(w_in[l], idx_in) * col_scale).astype(BF16)
        wuqa = _take_cols(w_uq[l], idx_qa).astype(BF16)
        wuqb = _take_cols(w_uq[l], idx_qb).astype(BF16)
        wukv = _take_cols(w_ukv[l], idx_kv).astype(BF16)
        fb_row = jnp.zeros((1, LANES), F32).at[0, CTRL_LOGIT:CTRL_LOGIT + FOX_HEADS].set(
            fox_forget_b[l].astype(F32))
        qa, ka, vt = _inproj(
            x, mod[l], attn_norm_g[l].reshape(1, D), win, q_norm_g[l].reshape(1, -1),
            kv_norm_g[l].reshape(1, -1), wuqa, wuqb, wukv, fb_row, ctab, stab, tri, eqk, ts=ts)
        ot = _attention(qa, ka, vt, tq=tq, tk=tk, hp=4)

        j = l // 2
        is_moe = (l % 2 == 1)
        rw = None
        if is_moe:
            rw = jnp.pad(router_w[j], ((0, 0), (0, LANES - N_EXPERTS))).astype(BF16)
        outs = _outproj(ot, x, mod[l], mla_out_g[l].reshape(1, -1), fox_out_g[l].reshape(1, -1),
                        w_o[l].astype(BF16), ffn_norm_g[l].reshape(1, D), rw, ts=ts)
        x1, h2 = outs[0].reshape(B * S, D), outs[1].reshape(B * S, D)
        fin_g = final_norm_g.reshape(1, D)
        last = l == L - 1
        if is_moe:
            x = _moe(h2, outs[2].reshape(B * S, LANES),
                     _fuse_gate_up(moe_w_gate[j], moe_w_up[j], tf), moe_w_down[j].astype(BF16),
                     x1, mod[l], fin_g, tm=tm, tf=tf, rc=rc, seq=S, final_norm=last)
        else:
            x = _ffn(h2, _fuse_gate_up(dense_w_gate[j][None], dense_w_up[j][None], tf),
                     dense_w_down[j].astype(BF16), x1, mod[l], fin_g,
                     tm=tm, tf=tf, seq=S, final_norm=last)
        x = x.reshape(B, S, D)
    return x
```

```python
import functools

import numpy as np
import jax
import jax.numpy as jnp
from jax import lax
from jax.experimental import pallas as pl
from jax.experimental.pallas import tpu as pltpu

F32 = jnp.float32
BF16 = jnp.bfloat16

D_MODEL = 1024
MLA_HEADS = 8
MLA_V = 64
MLA_NOPE = 64
MLA_ROPE = 32
MLA_Q_LORA = 256
MLA_KV_LORA = 128
FOX_HEADS = 8
FOX_HEAD_DIM = 64
FOX_W = FOX_HEADS * FOX_HEAD_DIM
ROPE_THETA = 10000.0
D_FF = 3584
N_EXPERTS = 8
EPS = 1e-6
N_HEADS = MLA_HEADS + FOX_HEADS
HALF_ROPE = MLA_ROPE // 2

LANES = 128
VMEM_LIMIT = 56 * 1024 * 1024

ROPE_LO = MLA_NOPE
AUG_LO = FOX_HEAD_DIM
CTRL_LOGIT = 96
CTRL_ONE = 120

C_Q0 = 0
C_KV0 = C_Q0 + MLA_Q_LORA
C_KRA0 = C_KV0 + MLA_KV_LORA
C_KRB0 = C_KRA0 + LANES
C_FQ0 = C_KRB0 + LANES
C_FK0 = C_FQ0 + FOX_W
C_FV0 = C_FK0 + FOX_W
C_END = C_FV0 + FOX_W

NEG_BIG = -1e30
LOG2E = 1.4426950408889634
ONES_ROWS = 16


def _rms(v):
    return v * lax.rsqrt(jnp.mean(v * v, axis=-1, keepdims=True) + EPS)


def _sigmoid(v):
    return 1.0 / (1.0 + jnp.exp(-v))


def _split3(v):
    hi = v.astype(BF16).astype(F32)
    r = v - hi
    mid = r.astype(BF16).astype(F32)
    lo = (r - mid).astype(BF16).astype(F32)
    return hi, mid, lo


def _adaln_kernel(c_ref, w_ref, b_ref, o_ref):
    c = c_ref[...]
    ca = (c * _sigmoid(c)).astype(BF16)
    o_ref[0] = jnp.dot(ca, w_ref[0].astype(BF16), preferred_element_type=F32) + b_ref[0]


def _adaln(c, ada_w, ada_b):
    L, D, N = ada_w.shape
    B = c.shape[0]
    tn = 1536
    return pl.pallas_call(
        _adaln_kernel,
        grid=(L, N // tn),
        in_specs=[
            pl.BlockSpec((B, D), lambda l, j: (0, 0)),
            pl.BlockSpec((1, D, tn), lambda l, j: (l, 0, j)),
            pl.BlockSpec((1, 1, tn), lambda l, j: (l, 0, j)),
        ],
        out_specs=pl.BlockSpec((1, B, tn), lambda l, j: (l, 0, j)),
        out_shape=jax.ShapeDtypeStruct((L, B, N), F32),
        compiler_params=pltpu.CompilerParams(
            dimension_semantics=("arbitrary", "arbitrary"), vmem_limit_bytes=VMEM_LIMIT),
    )(c, ada_w, ada_b.reshape(L, 1, N))


def _rope_tab_kernel(pos_ref, freq_ref, cos_ref, sin_ref):
    ang = freq_ref[...] * pos_ref[0]
    cos_ref[0] = jnp.cos(ang)
    sin_ref[0] = jnp.sin(ang)


def _rope_tables(positions):
    B, S = positions.shape
    half = HALF_ROPE
    inv_freq = ROPE_THETA ** (-jnp.arange(half, dtype=F32) / half)
    pos = positions.astype(F32).reshape(B, 1, S)
    cos_t, sin_t = pl.pallas_call(
        _rope_tab_kernel,
        grid=(B,),
        in_specs=[
            pl.BlockSpec((1, 1, S), lambda b: (b, 0, 0)),
            pl.BlockSpec((half, 1), lambda b: (0, 0)),
        ],
        out_specs=[pl.BlockSpec((1, half, S), lambda b: (b, 0, 0))] * 2,
        out_shape=[jax.ShapeDtypeStruct((B, half, S), F32)] * 2,
        compiler_params=pltpu.CompilerParams(dimension_semantics=("arbitrary",)),
    )(pos, inv_freq.reshape(half, 1))
    cos = jnp.transpose(cos_t, (0, 2, 1))
    sin = jnp.transpose(sin_t, (0, 2, 1))
    ones = jnp.ones((B, S, MLA_NOPE), F32)
    z_lo = jnp.zeros((B, S, MLA_NOPE), F32)
    z_hi = jnp.zeros((B, S, LANES - MLA_NOPE - MLA_ROPE), F32)
    ctab = jnp.concatenate([ones, cos, cos, z_hi], axis=-1)
    stab = jnp.concatenate([z_lo, -sin, sin, z_hi], axis=-1)
    return ctab, stab


def _take_cols(w, idx):
    pieces, i, n = [], 0, len(idx)
    while i < n:
        j = i + 1
        if idx[i] < 0:
            while j < n and idx[j] < 0:
                j += 1
            pieces.append(jnp.zeros((w.shape[0], j - i), w.dtype))
        else:
            while j < n and idx[j] == idx[j - 1] + 1:
                j += 1
            pieces.append(w[:, int(idx[i]):int(idx[i]) + (j - i)])
        i = j
    return jnp.concatenate(pieces, axis=1)


def _perm_w_in():
    cq = 0
    ckv = cq + MLA_Q_LORA
    kr = ckv + MLA_KV_LORA
    fq = kr + MLA_ROPE
    fk = fq + FOX_W
    fv = fk + FOX_W
    fl = fv + FOX_W
    idx = -np.ones((C_END,), np.int64)
    idx[C_Q0:C_Q0 + MLA_Q_LORA] = cq + np.arange(MLA_Q_LORA)
    idx[C_KV0:C_KV0 + MLA_KV_LORA] = ckv + np.arange(MLA_KV_LORA)
    h = HALF_ROPE
    idx[C_KRA0 + ROPE_LO:C_KRA0 + ROPE_LO + MLA_ROPE] = kr + np.arange(MLA_ROPE)
    idx[C_KRA0 + CTRL_LOGIT:C_KRA0 + CTRL_LOGIT + FOX_HEADS] = fl + np.arange(FOX_HEADS)
    idx[C_KRB0 + ROPE_LO:C_KRB0 + ROPE_LO + h] = kr + h + np.arange(h)
    idx[C_KRB0 + ROPE_LO + h:C_KRB0 + ROPE_LO + MLA_ROPE] = kr + np.arange(h)
    idx[C_FQ0:C_FQ0 + FOX_W] = fq + np.arange(FOX_W)
    idx[C_FK0:C_FK0 + FOX_W] = fk + np.arange(FOX_W)
    idx[C_FV0:C_FV0 + FOX_W] = fv + np.arange(FOX_W)
    return idx


def _perm_w_uq():
    per = MLA_NOPE + MLA_ROPE
    ia = -np.ones((MLA_HEADS * LANES,), np.int64)
    ib = -np.ones((MLA_HEADS * LANES,), np.int64)
    h2 = HALF_ROPE
    for h in range(MLA_HEADS):
        ia[h * LANES:h * LANES + per] = h * per + np.arange(per)
        ib[h * LANES + ROPE_LO:h * LANES + ROPE_LO + h2] = h * per + MLA_NOPE + h2 + np.arange(h2)
        ib[h * LANES + ROPE_LO + h2:h * LANES + ROPE_LO + MLA_ROPE] = h * per + MLA_NOPE + np.arange(h2)
    return ia, ib


def _perm_w_ukv():
    per = MLA_NOPE + MLA_V
    ik = -np.ones((MLA_HEADS * LANES,), np.int64)
    iv = np.zeros((MLA_HEADS * MLA_V,), np.int64)
    for h in range(MLA_HEADS):
        ik[h * LANES:h * LANES + MLA_NOPE] = h * per + np.arange(MLA_NOPE)
        iv[h * MLA_V:(h + 1) * MLA_V] = h * per + MLA_NOPE + np.arange(MLA_V)
    return np.concatenate([ik, iv])


def _aug_placement():
    e = np.zeros((LANES, 2 * FOX_HEADS * LANES), np.float32)
    koff = FOX_HEADS * LANES
    for h in range(FOX_HEADS):
        for p in range(3):
            src = CTRL_LOGIT + 8 * p + h
            e[src, h * LANES + AUG_LO + p] = 1.0
            e[CTRL_ONE, h * LANES + AUG_LO + 3 + p] = -1.0
            e[CTRL_ONE, koff + h * LANES + AUG_LO + p] = 1.0
            e[src, koff + h * LANES + AUG_LO + 3 + p] = 1.0
    return e


def _inproj_kernel(x_ref, mod_ref, g_ref, win_ref, qg_ref, kvg_ref, wuqa_ref, wuqb_ref, wukv_ref,
                   fb_ref, ct_ref, st_ref, tri_ref, eqk_ref,
                   qa_ref, ka_ref, vt_ref, carry_ref, *, ts):
    si = pl.program_id(1)

    @pl.when(si == 0)
    def _():
        carry_ref[...] = jnp.zeros_like(carry_ref)

    x = x_ref[0]
    shift = mod_ref[0, 0:1, :]
    scale = mod_ref[0, 1:2, :]
    h = ((_rms(x) * g_ref[...]) * (1.0 + scale) + shift).astype(BF16)
    proj = jnp.dot(h, win_ref[...], preferred_element_type=F32)

    ctab = ct_ref[0]
    stab = st_ref[0]
    lane = lax.broadcasted_iota(jnp.int32, (ts, LANES), 1)

    c_q = proj[:, C_Q0:C_Q0 + MLA_Q_LORA]
    cqn = (_rms(c_q) * qg_ref[...]).astype(BF16)
    qa = jnp.dot(cqn, wuqa_ref[...], preferred_element_type=F32)
    qb = jnp.dot(cqn, wuqb_ref[...], preferred_element_type=F32)
    mla_scale = (MLA_NOPE + MLA_ROPE) ** -0.5 * LOG2E
    for hh in range(MLA_HEADS):
        sl = slice(hh * LANES, (hh + 1) * LANES)
        qa_ref[0, hh] = ((qa[:, sl] * ctab + qb[:, sl] * stab) * mla_scale).astype(BF16)

    c_kv = proj[:, C_KV0:C_KV0 + MLA_KV_LORA]
    ckvn = (_rms(c_kv) * kvg_ref[...]).astype(BF16)
    kv = jnp.dot(ckvn, wukv_ref[...], preferred_element_type=F32)
    kra = proj[:, C_KRA0:C_KRA0 + LANES]
    krb = proj[:, C_KRB0:C_KRB0 + LANES]
    krope = kra * ctab + krb * stab
    for hh in range(MLA_HEADS):
        ka_ref[0, hh] = (kv[:, hh * LANES:(hh + 1) * LANES] + krope).astype(BF16)
    v_mla = kv[:, MLA_HEADS * LANES:]

    ctrl = (lane >= CTRL_LOGIT) & (lane < CTRL_LOGIT + FOX_HEADS)
    fl = kra + fb_ref[...]
    lsig = jnp.minimum(fl, 0.0) - jnp.log1p(jnp.exp(-jnp.abs(fl)))
    lf = jnp.where(ctrl, lsig, 0.0)
    hi, mid, lo = _split3(lf)
    p1 = (hi + pltpu.roll(mid, 8, 1) + pltpu.roll(lo, 16, 1)).astype(BF16)
    cs = jnp.dot(tri_ref[...], p1, preferred_element_type=F32)
    cs = cs + pltpu.roll(cs, LANES - 8, 1) + pltpu.roll(cs, LANES - 16, 1)
    fcum = jnp.where(ctrl, cs, 0.0) + carry_ref[...]
    carry_ref[...] = fcum[ts - 1:ts, :]

    hi, mid, lo = _split3(fcum * LOG2E)
    p2 = hi + pltpu.roll(mid, 8, 1) + pltpu.roll(lo, 16, 1)
    p2 = jnp.where(lane == CTRL_ONE, 1.0, p2).astype(BF16)
    aug = jnp.dot(p2, eqk_ref[...], preferred_element_type=F32)

    fq = proj[:, C_FQ0:C_FQ0 + FOX_W]
    fk = proj[:, C_FK0:C_FK0 + FOX_W]
    low = lane < FOX_HEAD_DIM
    koff = FOX_HEADS * LANES
    for j in range(FOX_HEADS // 2):
        sl = slice(j * LANES, (j + 1) * LANES)
        for src, off, dst in ((fq, 0, qa_ref), (fk, koff, ka_ref)):
            blk = src[:, sl] * LOG2E if dst is qa_ref else src[:, sl]
            blk_r = pltpu.roll(blk, FOX_HEAD_DIM, 1)
            h0 = 2 * j
            a0 = aug[:, off + h0 * LANES:off + (h0 + 1) * LANES]
            a1 = aug[:, off + (h0 + 1) * LANES:off + (h0 + 2) * LANES]
            dst[0, MLA_HEADS + h0] = jnp.where(low, blk, a0).astype(BF16)
            dst[0, MLA_HEADS + h0 + 1] = jnp.where(low, blk_r, a1).astype(BF16)

    fv = proj[:, C_FV0:C_FV0 + FOX_W]
    vt_ref[0, 0, 0:MLA_HEADS * MLA_V, :] = v_mla.T.astype(BF16)
    vt_ref[0, 0, MLA_HEADS * MLA_V:, :] = fv.T.astype(BF16)


def _inproj(x, mod_l, g, win, qg, kvg, wuqa, wuqb, wukv, fb_row, ctab, stab, tri, eqk, *, ts):
    B, S, D = x.shape
    nst = S // ts
    const2 = lambda b, s: (0, 0)
    kern = functools.partial(_inproj_kernel, ts=ts)
    return pl.pallas_call(
        kern,
        grid=(B, nst),
        in_specs=[
            pl.BlockSpec((1, ts, D), lambda b, s: (b, s, 0)),
            pl.BlockSpec((1, 6, D), lambda b, s: (b, 0, 0)),
            pl.BlockSpec((1, D), const2),
            pl.BlockSpec(win.shape, const2),
            pl.BlockSpec(qg.shape, const2),
            pl.BlockSpec(kvg.shape, const2),
            pl.BlockSpec(wuqa.shape, const2),
            pl.BlockSpec(wuqb.shape, const2),
            pl.BlockSpec(wukv.shape, const2),
            pl.BlockSpec(fb_row.shape, const2),
            pl.BlockSpec((1, ts, LANES), lambda b, s: (b, s, 0)),
            pl.BlockSpec((1, ts, LANES), lambda b, s: (b, s, 0)),
            pl.BlockSpec(tri.shape, const2),
            pl.BlockSpec(eqk.shape, const2),
        ],
        out_specs=[
            pl.BlockSpec((1, N_HEADS, ts, LANES), lambda b, s: (b, 0, s, 0)),
            pl.BlockSpec((1, N_HEADS, ts, LANES), lambda b, s: (b, 0, s, 0)),
            pl.BlockSpec((1, 1, N_HEADS * MLA_V, ts), lambda b, s: (b, s, 0, 0)),
        ],
        out_shape=[
            jax.ShapeDtypeStruct((B, N_HEADS, S, LANES), BF16),
            jax.ShapeDtypeStruct((B, N_HEADS, S, LANES), BF16),
            jax.ShapeDtypeStruct((B, nst, N_HEADS * MLA_V, ts), BF16),
        ],
        scratch_shapes=[pltpu.VMEM((1, LANES), F32)],
        compiler_params=pltpu.CompilerParams(
            dimension_semantics=("arbitrary", "arbitrary"), vmem_limit_bytes=VMEM_LIMIT),
    )(x, mod_l, g, win, qg, kvg, wuqa, wuqb, wukv, fb_row, ctab, stab, tri, eqk)


def _colmax8(st, groups):
    tk, tq = st.shape
    v = st.reshape(groups, tk // (8 * groups), 8, tq)
    return jnp.max(jnp.max(v, axis=1), axis=0)


def _attn_kernel(q_ref, k_ref, vt_ref, o_ref, s_scr, tmax_scr, m_scr, acc_scr,
                 *, tq, tk, hp, nq):
    dv = MLA_V
    groups = 4
    ones = jnp.ones((ONES_ROWS, tk), BF16)

    def qk(qi, j):
        q0 = pl.multiple_of(qi * tq, tq)
        return [lax.dot_general(k_ref[0, h, pl.ds(pl.multiple_of(j * tk, tk), tk), :],
                                q_ref[0, h, pl.ds(q0, tq), :], (((1,), (1,)), ((), ())),
                                preferred_element_type=F32) for h in range(hp)]

    def park(sts, qi, j, masked):
        for h in range(hp):
            st = sts[h]
            if masked:
                kpos = j * tk + lax.broadcasted_iota(jnp.int32, (tk, tq), 0)
                qpos = qi * tq + lax.broadcasted_iota(jnp.int32, (tk, tq), 1)
                st = jnp.where(kpos <= qpos, st, NEG_BIG)
            s_scr[h] = st
            tmax_scr[h] = _colmax8(st, groups)

    def update(j):
        for h in range(hp):
            m = m_scr[h]
            m_new = jnp.maximum(m, jnp.max(tmax_scr[h], axis=0, keepdims=True))
            alpha = jnp.exp2(m - m_new)
            p = jnp.exp2((s_scr[h] - m_new).astype(BF16))
            vt1 = jnp.concatenate([vt_ref[0, j, h * dv:(h + 1) * dv, :], ones], axis=0)
            acc_scr[h] = alpha * acc_scr[h] + jnp.dot(vt1, p, preferred_element_type=F32)
            m_scr[h] = m_new

    def reset_state():
        m_scr[...] = jnp.full(m_scr.shape, NEG_BIG, F32)
        acc_scr[...] = jnp.zeros(acc_scr.shape, F32)

    def finish(qi):
        q0 = pl.multiple_of(qi * tq, tq)
        for h in range(hp):
            acc = acc_scr[h]
            o_ref[0, h * dv:(h + 1) * dv, pl.ds(q0, tq)] = (
                acc[0:dv, :] / acc[dv:dv + 1, :]).astype(o_ref.dtype)
        reset_state()

    reset_state()
    park(qk(0, 0), 0, 0, True)

    def q_tile(qi, c):
        nf = (qi * tq) // tk

        def below_diagonal(t, c2):
            sts = qk(qi, t + 1)
            update(t)
            park(sts, qi, t + 1, False)
            return c2

        lax.fori_loop(0, nf - 1, below_diagonal, 0)

        @pl.when(nf > 0)
        def _():
            sts = qk(qi, nf)
            update(nf - 1)
            park(sts, qi, nf, True)

        nxt = qi + 1
        nxt_on_diagonal = (nxt * tq) // tk == 0

        @pl.when((nxt < nq) & nxt_on_diagonal)
        def _():
            sts = qk(nxt, 0)
            update(nf)
            finish(qi)
            park(sts, nxt, 0, True)

        @pl.when((nxt < nq) & jnp.logical_not(nxt_on_diagonal))
        def _():
            sts = qk(nxt, 0)
            update(nf)
            finish(qi)
            park(sts, nxt, 0, False)

        @pl.when(nxt == nq)
        def _():
            update(nf)
            finish(qi)

        return c

    lax.fori_loop(0, nq, q_tile, 0)


def _attention(qa, ka, vt, *, tq, tk, hp):
    B, H, S, _ = qa.shape
    nkt = vt.shape[1]
    assert S % tq == 0 and nkt * tk == S and H % hp == 0 and tk % tq == 0
    kern = functools.partial(_attn_kernel, tq=tq, tk=tk, hp=hp, nq=S // tq)
    return pl.pallas_call(
        kern,
        grid=(B, H // hp),
        in_specs=[
            pl.BlockSpec((1, hp, S, LANES), lambda b, g: (b, g, 0, 0)),
            pl.BlockSpec((1, hp, S, LANES), lambda b, g: (b, g, 0, 0)),
            pl.BlockSpec((1, nkt, hp * MLA_V, tk), lambda b, g: (b, 0, g, 0)),
        ],
        out_specs=pl.BlockSpec((1, hp * MLA_V, S), lambda b, g: (b, g, 0)),
        out_shape=jax.ShapeDtypeStruct((B, H * MLA_V, S), BF16),
        scratch_shapes=[
            pltpu.VMEM((hp, tk, tq), F32),
            pltpu.VMEM((hp, 8, tq), F32),
            pltpu.VMEM((hp, 1, tq), F32),
            pltpu.VMEM((hp, MLA_V + ONES_ROWS, tq), F32),
        ],
        compiler_params=pltpu.CompilerParams(
            dimension_semantics=("arbitrary", "arbitrary"), vmem_limit_bytes=VMEM_LIMIT),
    )(qa, ka, vt)


def _top2_gates(logits, lane):
    lg = jnp.where(lane < N_EXPERTS, logits, NEG_BIG)
    m1 = jnp.max(lg, axis=1, keepdims=True)
    i1 = jnp.min(jnp.where(lg == m1, lane, LANES), axis=1, keepdims=True)
    lg2 = jnp.where(lane == i1, NEG_BIG, lg)
    m2 = jnp.max(lg2, axis=1, keepdims=True)
    i2 = jnp.min(jnp.where(lg2 == m2, lane, LANES), axis=1, keepdims=True)
    e2 = jnp.exp(m2 - m1)
    den = 1.0 + e2
    return jnp.where(lane == i1, 1.0 / den, 0.0) + jnp.where(lane == i2, e2 / den, 0.0)


def _outproj_kernel(*refs, ts, with_router):
    if with_router:
        (ot_ref, x_ref, mod_ref, mg_ref, fg_ref, wo_ref, ng_ref, rw_ref,
         x1_ref, h2_ref, gates_ref) = refs
    else:
        ot_ref, x_ref, mod_ref, mg_ref, fg_ref, wo_ref, ng_ref, x1_ref, h2_ref = refs
    half = MLA_HEADS * MLA_V
    om = ot_ref[0, 0:half, :].astype(F32).T
    of = ot_ref[0, half:, :].astype(F32).T
    on = jnp.concatenate([_rms(om) * mg_ref[...], _rms(of) * fg_ref[...]], axis=1).astype(BF16)
    mix = jnp.dot(on, wo_ref[...], preferred_element_type=F32)
    x1 = x_ref[0] + mod_ref[0, 2:3, :] * mix
    x1_ref[0] = x1
    h2 = ((_rms(x1) * ng_ref[...]) * (1.0 + mod_ref[0, 4:5, :]) + mod_ref[0, 3:4, :]).astype(BF16)
    h2_ref[0] = h2
    if with_router:
        logits = jnp.dot(h2, rw_ref[...], preferred_element_type=F32)
        lane = lax.broadcasted_iota(jnp.int32, (ts, LANES), 1)
        gates_ref[0] = _top2_gates(logits, lane)


def _outproj(ot, x, mod_l, mg, fg, wo, ng, rw, *, ts):
    B, S, D = x.shape
    with_router = rw is not None
    const2 = lambda b, s: (0, 0)
    in_specs = [
        pl.BlockSpec((1, ot.shape[1], ts), lambda b, s: (b, 0, s)),
        pl.BlockSpec((1, ts, D), lambda b, s: (b, s, 0)),
        pl.BlockSpec((1, 6, D), lambda b, s: (b, 0, 0)),
        pl.BlockSpec(mg.shape, const2),
        pl.BlockSpec(fg.shape, const2),
        pl.BlockSpec(wo.shape, const2),
        pl.BlockSpec(ng.shape, const2),
    ]
    args = [ot, x, mod_l, mg, fg, wo, ng]
    out_specs = [pl.BlockSpec((1, ts, D), lambda b, s: (b, s, 0)),
                 pl.BlockSpec((1, ts, D), lambda b, s: (b, s, 0))]
    out_shape = [jax.ShapeDtypeStruct((B, S, D), F32), jax.ShapeDtypeStruct((B, S, D), BF16)]
    if with_router:
        in_specs.append(pl.BlockSpec(rw.shape, const2))
        args.append(rw)
        out_specs.append(pl.BlockSpec((1, ts, LANES), lambda b, s: (b, s, 0)))
        out_shape.append(jax.ShapeDtypeStruct((B, S, LANES), F32))
    kern = functools.partial(_outproj_kernel, ts=ts, with_router=with_router)
    return pl.pallas_call(
        kern,
        grid=(B, S // ts),
        in_specs=in_specs,
        out_specs=out_specs,
        out_shape=out_shape,
        compiler_params=pltpu.CompilerParams(
            dimension_semantics=("arbitrary", "arbitrary"), vmem_limit_bytes=VMEM_LIMIT),
    )(*args)


def _fuse_gate_up(wg, wu, tf):
    E, D, F = wg.shape
    w = jnp.concatenate([wg.reshape(E, D, F // tf, tf), wu.reshape(E, D, F // tf, tf)], axis=-1)
    return jnp.transpose(w, (0, 2, 1, 3)).astype(BF16)


def _swiglu_tile(x, wgu, tf):
    gu = jnp.dot(x, wgu, preferred_element_type=F32)
    g = gu[:, :tf]
    return g * _sigmoid(g) * gu[:, tf:]


def _ffn_kernel(h_ref, wgu_ref, wd_ref, x_ref, mod_ref, fin_ref, o_ref, acc_ref,
                *, tf, final_norm):
    f = pl.program_id(1)

    @pl.when(f == 0)
    def _():
        acc_ref[...] = jnp.zeros_like(acc_ref)

    a = _swiglu_tile(h_ref[...], wgu_ref[0, 0], tf)
    acc_ref[...] += jnp.dot(a.astype(BF16), wd_ref[...], preferred_element_type=F32)

    @pl.when(f == pl.num_programs(1) - 1)
    def _():
        x2 = x_ref[...] + mod_ref[0, 5:6, :] * acc_ref[...]
        if final_norm:
            x2 = _rms(x2) * fin_ref[...]
        o_ref[...] = x2


def _ffn(h2, wgu, wd, x1, mod_l, fin_g, *, tm, tf, seq, final_norm):
    T, D = h2.shape
    F = wd.shape[0]
    tiles_per_batch = seq // tm
    kern = functools.partial(_ffn_kernel, tf=tf, final_norm=final_norm)
    return pl.pallas_call(
        kern,
        grid=(T // tm, F // tf),
        in_specs=[
            pl.BlockSpec((tm, D), lambda i, f: (i, 0)),
            pl.BlockSpec((1, 1, D, 2 * tf), lambda i, f: (0, f, 0, 0)),
            pl.BlockSpec((tf, D), lambda i, f: (f, 0)),
            pl.BlockSpec((tm, D), lambda i, f: (i, 0)),
            pl.BlockSpec((1, 6, D), lambda i, f: (i // tiles_per_batch, 0, 0)),
            pl.BlockSpec((1, D), lambda i, f: (0, 0)),
        ],
        out_specs=pl.BlockSpec((tm, D), lambda i, f: (i, 0)),
        out_shape=jax.ShapeDtypeStruct((T, D), F32),
        scratch_shapes=[pltpu.VMEM((tm, D), F32)],
        compiler_params=pltpu.CompilerParams(
            dimension_semantics=("arbitrary", "arbitrary"), vmem_limit_bytes=VMEM_LIMIT),
    )(h2, wgu, wd, x1, mod_l, fin_g)


def _moe_kernel(h_ref, gates_ref, wg0_ref, wg1_ref, wu0_ref, wu1_ref, wd0_ref, wd1_ref,
                x_ref, mod_ref, fin_ref, o_ref,
                rank_scr, rankt_scr, gatest_scr, xc_scr, gc_scr, yc_scr, nch_ref,
                *, tm, tf, rc, final_norm):
    rcp = xc_scr.shape[1]
    dh = xc_scr.shape[2] // 2
    e = pl.program_id(1)
    f = pl.program_id(2)
    last_f = f == pl.num_programs(2) - 1

    @pl.when((e == 0) & (f == 0))
    def _():
        gates = gates_ref[...]
        sel = gates > 0.0
        selb = jnp.where(sel, 1.0, 0.0).astype(BF16)
        earlier = (lax.broadcasted_iota(jnp.int32, (tm, tm), 1)
                   < lax.broadcasted_iota(jnp.int32, (tm, tm), 0))
        rank = jnp.dot(jnp.where(earlier, 1.0, 0.0).astype(BF16), selb,
                       preferred_element_type=F32)
        rank = jnp.where(sel, rank, -1.0)
        rank_scr[...] = rank
        rankt_scr[...] = rank.T
        gatest_scr[...] = gates.T
        o_ref[...] = jnp.zeros_like(o_ref)

    @pl.when(f == 0)
    def _():
        lane = lax.broadcasted_iota(jnp.int32, (tm, LANES), 1)
        cnt = jnp.sum(jnp.where((lane == e) & (rank_scr[...] >= 0.0), 1.0, 0.0))
        nch = (cnt.astype(jnp.int32) + (rc - 1)) // rc
        nch_ref[0] = nch
        rrow = rankt_scr[pl.ds(e, 1), :].astype(jnp.int32)
        grow = gatest_scr[pl.ds(e, 1), :]

        def gather(k, c):
            row = lax.broadcasted_iota(jnp.int32, (rc, tm), 0) + k * rc
            hit = rrow == row
            onehot = jnp.where(hit, 1.0, 0.0).astype(BF16)
            xc_scr[k, 0:rc] = jnp.dot(onehot, h_ref[...], preferred_element_type=F32).astype(BF16)
            gcol = jnp.sum(jnp.where(hit, grow, 0.0), axis=1, keepdims=True)
            gc_scr[k, 0:rc] = jnp.broadcast_to(gcol, (rc, LANES))
            yc_scr[k] = jnp.zeros(yc_scr.shape[1:], F32)
            return c

        lax.fori_loop(0, nch, gather, 0)

    def expert(k, c):
        xa = xc_scr[k, 0:rc, 0:dh]
        xb = xc_scr[k, 0:rc, dh:]
        g = (jnp.dot(xa, wg0_ref[0, 0], preferred_element_type=F32)
             + jnp.dot(xb, wg1_ref[0, 0], preferred_element_type=F32))
        u = (jnp.dot(xa, wu0_ref[0, 0], preferred_element_type=F32)
             + jnp.dot(xb, wu1_ref[0, 0], preferred_element_type=F32))
        a = (g * _sigmoid(g) * u * gc_scr[k, 0:rc, 0:1]).astype(BF16)
        yc_scr[k, 0:rc, 0:dh] += jnp.dot(a, wd0_ref[0, 0, 0], preferred_element_type=F32)
        yc_scr[k, 0:rc, dh:] += jnp.dot(a, wd1_ref[0, 0, 0], preferred_element_type=F32)
        return c

    lax.fori_loop(0, nch_ref[0], expert, 0)

    @pl.when(last_f)
    def _():
        lane = lax.broadcasted_iota(jnp.int32, (tm, LANES), 1)
        rcol = jnp.sum(jnp.where(lane == e, rank_scr[...], 0.0), axis=1,
                       keepdims=True).astype(jnp.int32)

        def scatter(k, c):
            col = lax.broadcasted_iota(jnp.int32, (tm, rcp), 1)
            hit = (rcol == col + k * rc) & (col < rc)
            onehot_t = jnp.where(hit, 1.0, 0.0).astype(BF16)
            o_ref[...] += jnp.dot(onehot_t, yc_scr[k].astype(BF16), preferred_element_type=F32)
            return c

        lax.fori_loop(0, nch_ref[0], scatter, 0)

    @pl.when((e == pl.num_programs(1) - 1) & last_f)
    def _():
        x2 = x_ref[...] + mod_ref[0, 5:6, :] * o_ref[...]
        if final_norm:
            x2 = _rms(x2) * fin_ref[...]
        o_ref[...] = x2


def _tile_major(w, tf):
    E, D, F = w.shape
    return jnp.transpose(w.reshape(E, D, F // tf, tf), (0, 2, 1, 3)).astype(BF16)


def _col_halves(w, tf):
    E, F, D = w.shape
    return jnp.transpose(w.reshape(E, F // tf, tf, 2, D // 2), (0, 1, 3, 2, 4)).astype(BF16)


def _moe(h2, gates, wg, wu, wd, x1, mod_l, fin_g, *, tm, tf, rc, seq, final_norm):
    T, D = h2.shape
    E, NF = wd.shape[0], wd.shape[1]
    tiles_per_batch = seq // tm
    up_half = lambda p: pl.BlockSpec((1, 1, D // 2, tf), lambda i, e, f: (e, f, p, 0))
    down_half = lambda p: pl.BlockSpec((1, 1, 1, tf, D // 2), lambda i, e, f: (e, f, p, 0, 0))
    ncap = -(-tm // rc)
    rcp = -(-rc // LANES) * LANES
    kern = functools.partial(_moe_kernel, tm=tm, tf=tf, rc=rc, final_norm=final_norm)
    return pl.pallas_call(
        kern,
        grid=(T // tm, E, NF),
        in_specs=[
            pl.BlockSpec((tm, D), lambda i, e, f: (i, 0)),
            pl.BlockSpec((tm, LANES), lambda i, e, f: (i, 0)),
            up_half(0), up_half(1), up_half(0), up_half(1), down_half(0), down_half(1),
            pl.BlockSpec((tm, D), lambda i, e, f: (i, 0)),
            pl.BlockSpec((1, 6, D), lambda i, e, f: (i // tiles_per_batch, 0, 0)),
            pl.BlockSpec((1, D), lambda i, e, f: (0, 0)),
        ],
        out_specs=pl.BlockSpec((tm, D), lambda i, e, f: (i, 0)),
        out_shape=jax.ShapeDtypeStruct((T, D), F32),
        scratch_shapes=[
            pltpu.VMEM((tm, LANES), F32),
            pltpu.VMEM((LANES, tm), F32),
            pltpu.VMEM((LANES, tm), F32),
            pltpu.VMEM((ncap, rcp, D), BF16),
            pltpu.VMEM((ncap, rcp, LANES), F32),
            pltpu.VMEM((ncap, rcp, D), F32),
            pltpu.SMEM((1,), jnp.int32),
        ],
        compiler_params=pltpu.CompilerParams(
            dimension_semantics=("arbitrary", "arbitrary", "arbitrary"),
            vmem_limit_bytes=VMEM_LIMIT),
    )(h2, gates, wg, wg, wu, wu, wd, wd, x1, mod_l, fin_g)


def _tiles(S):
    ts = min(512, S)
    tq = min(512, S)
    tk = ts
    tm = min(1024, S)
    tf = 896
    rc = tm // 4 + 48
    assert rc % 16 == 0
    return ts, tq, tk, tm, tf, rc


def kernel(x, c, positions, ada_w, ada_b, attn_norm_g, w_in, q_norm_g, w_uq, kv_norm_g, w_ukv,
           fox_forget_b, mla_out_g, fox_out_g, w_o, ffn_norm_g, dense_w_gate, dense_w_up,
           dense_w_down, router_w, moe_w_gate, moe_w_up, moe_w_down, final_norm_g):
    B, S, D = x.shape
    L = ada_w.shape[0]
    ts, tq, tk, tm, tf, rc = _tiles(S)

    mod = _adaln(c, ada_w, ada_b).reshape(L, B, 6, D)
    ctab, stab = _rope_tables(positions)

    idx_in = _perm_w_in()
    idx_qa, idx_qb = _perm_w_uq()
    idx_kv = _perm_w_ukv()
    col_scale = np.ones((C_END,), np.float32)
    col_scale[C_FQ0:C_FQ0 + FOX_W] = FOX_HEAD_DIM ** -0.5
    tri = jnp.asarray(np.tril(np.ones((ts, ts), np.float32)), BF16)
    eqk = jnp.asarray(_aug_placement(), BF16)

    for l in range(L):
        win = (_take_cols(w_in[l], idx_in) * col_scale).astype(BF16)
        wuqa = _take_cols(w_uq[l], idx_qa).astype(BF16)
        wuqb = _take_cols(w_uq[l], idx_qb).astype(BF16)
        wukv = _take_cols(w_ukv[l], idx_kv).astype(BF16)
        fb_row = jnp.zeros((1, LANES), F32).at[0, CTRL_LOGIT:CTRL_LOGIT + FOX_HEADS].set(
            fox_forget_b[l].astype(F32))
        qa, ka, vt = _inproj(
            x, mod[l], attn_norm_g[l].reshape(1, D), win, q_norm_g[l].reshape(1, -1),
            kv_norm_g[l].reshape(1, -1), wuqa, wuqb, wukv, fb_row, ctab, stab, tri, eqk, ts=ts)
        ot = _attention(qa, ka, vt, tq=tq, tk=tk, hp=4)

        j = l // 2
        is_moe = (l % 2 == 1)
        rw = None
        if is_moe:
            rw = jnp.pad(router_w[j], ((0, 0), (0, LANES - N_EXPERTS))).astype(BF16)
        outs = _outproj(ot, x, mod[l], mla_out_g[l].reshape(1, -1), fox_out_g[l].reshape(1, -1),
                        w_o[l].astype(BF16), ffn_norm_g[l].reshape(1, D), rw, ts=ts)
        x1, h2 = outs[0].reshape(B * S, D), outs[1].reshape(B * S, D)
        fin_g = final_norm_g.reshape(1, D)
        last = l == L - 1
        if is_moe:
            x = _moe(h2, outs[2].reshape(B * S, LANES),
                     _tile_major(moe_w_gate[j], tf), _tile_major(moe_w_up[j], tf),
                     _col_halves(moe_w_down[j], tf),
                     x1, mod[l], fin_g, tm=tm, tf=tf, rc=rc, seq=S, final_norm=last)
        else:
            x = _ffn(h2, _fuse_gate_up(dense_w_gate[j][None], dense_w_up[j][None], tf),
                     dense_w_down[j].astype(BF16), x1, mod[l], fin_g,
                     tm=tm, tf=tf, seq=S, final_norm=last)
        x = x.reshape(B, S, D)
    return x
```

```python
import functools

import numpy as np
import jax
import jax.numpy as jnp
from jax import lax
from jax.experimental import pallas as pl
from jax.experimental.pallas import tpu as pltpu

F32 = jnp.float32
BF16 = jnp.bfloat16

D_MODEL = 1024
MLA_HEADS = 8
MLA_V = 64
MLA_NOPE = 64
MLA_ROPE = 32
MLA_Q_LORA = 256
MLA_KV_LORA = 128
FOX_HEADS = 8
FOX_HEAD_DIM = 64
FOX_W = FOX_HEADS * FOX_HEAD_DIM
ROPE_THETA = 10000.0
D_FF = 3584
N_EXPERTS = 8
EPS = 1e-6
N_HEADS = MLA_HEADS + FOX_HEADS
HALF_ROPE = MLA_ROPE // 2

LANES = 128
VMEM_LIMIT = 56 * 1024 * 1024

ROPE_LO = MLA_NOPE
AUG_LO = FOX_HEAD_DIM
CTRL_LOGIT = 96
CTRL_ONE = 120

C_Q0 = 0
C_KV0 = C_Q0 + MLA_Q_LORA
C_KRA0 = C_KV0 + MLA_KV_LORA
C_KRB0 = C_KRA0 + LANES
C_FQ0 = C_KRB0 + LANES
C_FK0 = C_FQ0 + FOX_W
C_FV0 = C_FK0 + FOX_W
C_END = C_FV0 + FOX_W

NEG_BIG = -1e30
LOG2E = 1.4426950408889634
ONES_ROWS = 16


def _rms(v):
    return v * lax.rsqrt(jnp.mean(v * v, axis=-1, keepdims=True) + EPS)


def _sigmoid(v):
    return 1.0 / (1.0 + jnp.exp(-v))


def _split3(v):
    hi = v.astype(BF16).astype(F32)
    r = v - hi
    mid = r.astype(BF16).astype(F32)
    lo = (r - mid).astype(BF16).astype(F32)
    return hi, mid, lo


def _adaln_kernel(c_ref, w_ref, b_ref, o_ref):
    c = c_ref[...]
    ca = (c * _sigmoid(c)).astype(BF16)
    o_ref[0] = jnp.dot(ca, w_ref[0].astype(BF16), preferred_element_type=F32) + b_ref[0]


def _adaln(c, ada_w, ada_b):
    L, D, N = ada_w.shape
    B = c.shape[0]
    tn = 1536
    return pl.pallas_call(
        _adaln_kernel,
        grid=(L, N // tn),
        in_specs=[
            pl.BlockSpec((B, D), lambda l, j: (0, 0)),
            pl.BlockSpec((1, D, tn), lambda l, j: (l, 0, j)),
            pl.BlockSpec((1, 1, tn), lambda l, j: (l, 0, j)),
        ],
        out_specs=pl.BlockSpec((1, B, tn), lambda l, j: (l, 0, j)),
        out_shape=jax.ShapeDtypeStruct((L, B, N), F32),
        compiler_params=pltpu.CompilerParams(
            dimension_semantics=("arbitrary", "arbitrary"), vmem_limit_bytes=VMEM_LIMIT),
    )(c, ada_w, ada_b.reshape(L, 1, N))


def _rope_tab_kernel(pos_ref, freq_ref, cos_ref, sin_ref):
    ang = freq_ref[...] * pos_ref[0]
    cos_ref[0] = jnp.cos(ang)
    sin_ref[0] = jnp.sin(ang)


def _rope_tables(positions):
    B, S = positions.shape
    half = HALF_ROPE
    inv_freq = ROPE_THETA ** (-jnp.arange(half, dtype=F32) / half)
    pos = positions.astype(F32).reshape(B, 1, S)
    cos_t, sin_t = pl.pallas_call(
        _rope_tab_kernel,
        grid=(B,),
        in_specs=[
            pl.BlockSpec((1, 1, S), lambda b: (b, 0, 0)),
            pl.BlockSpec((half, 1), lambda b: (0, 0)),
        ],
        out_specs=[pl.BlockSpec((1, half, S), lambda b: (b, 0, 0))] * 2,
        out_shape=[jax.ShapeDtypeStruct((B, half, S), F32)] * 2,
        compiler_params=pltpu.CompilerParams(dimension_semantics=("arbitrary",)),
    )(pos, inv_freq.reshape(half, 1))
    cos = jnp.transpose(cos_t, (0, 2, 1))
    sin = jnp.transpose(sin_t, (0, 2, 1))
    ones = jnp.ones((B, S, MLA_NOPE), F32)
    z_lo = jnp.zeros((B, S, MLA_NOPE), F32)
    z_hi = jnp.zeros((B, S, LANES - MLA_NOPE - MLA_ROPE), F32)
    ctab = jnp.concatenate([ones, cos, cos, z_hi], axis=-1)
    stab = jnp.concatenate([z_lo, -sin, sin, z_hi], axis=-1)
    return ctab, stab


def _take_cols(w, idx):
    pieces, i, n = [], 0, len(idx)
    while i < n:
        j = i + 1
        if idx[i] < 0:
            while j < n and idx[j] < 0:
                j += 1
            pieces.append(jnp.zeros((w.shape[0], j - i), w.dtype))
        else:
            while j < n and idx[j] == idx[j - 1] + 1:
                j += 1
            pieces.append(w[:, int(idx[i]):int(idx[i]) + (j - i)])
        i = j
    return jnp.concatenate(pieces, axis=1)


def _perm_w_in():
    cq = 0
    ckv = cq + MLA_Q_LORA
    kr = ckv + MLA_KV_LORA
    fq = kr + MLA_ROPE
    fk = fq + FOX_W
    fv = fk + FOX_W
    fl = fv + FOX_W
    idx = -np.ones((C_END,), np.int64)
    idx[C_Q0:C_Q0 + MLA_Q_LORA] = cq + np.arange(MLA_Q_LORA)
    idx[C_KV0:C_KV0 + MLA_KV_LORA] = ckv + np.arange(MLA_KV_LORA)
    h = HALF_ROPE
    idx[C_KRA0 + ROPE_LO:C_KRA0 + ROPE_LO + MLA_ROPE] = kr + np.arange(MLA_ROPE)
    idx[C_KRA0 + CTRL_LOGIT:C_KRA0 + CTRL_LOGIT + FOX_HEADS] = fl + np.arange(FOX_HEADS)
    idx[C_KRB0 + ROPE_LO:C_KRB0 + ROPE_LO + h] = kr + h + np.arange(h)
    idx[C_KRB0 + ROPE_LO + h:C_KRB0 + ROPE_LO + MLA_ROPE] = kr + np.arange(h)
    idx[C_FQ0:C_FQ0 + FOX_W] = fq + np.arange(FOX_W)
    idx[C_FK0:C_FK0 + FOX_W] = fk + np.arange(FOX_W)
    idx[C_FV0:C_FV0 + FOX_W] = fv + np.arange(FOX_W)
    return idx


def _perm_w_uq():
    per = MLA_NOPE + MLA_ROPE
    ia = -np.ones((MLA_HEADS * LANES,), np.int64)
    ib = -np.ones((MLA_HEADS * LANES,), np.int64)
    h2 = HALF_ROPE
    for h in range(MLA_HEADS):
        ia[h * LANES:h * LANES + per] = h * per + np.arange(per)
        ib[h * LANES + ROPE_LO:h * LANES + ROPE_LO + h2] = h * per + MLA_NOPE + h2 + np.arange(h2)
        ib[h * LANES + ROPE_LO + h2:h * LANES + ROPE_LO + MLA_ROPE] = h * per + MLA_NOPE + np.arange(h2)
    return ia, ib


def _perm_w_ukv():
    per = MLA_NOPE + MLA_V
    ik = -np.ones((MLA_HEADS * LANES,), np.int64)
    iv = np.zeros((MLA_HEADS * MLA_V,), np.int64)
    for h in range(MLA_HEADS):
        ik[h * LANES:h * LANES + MLA_NOPE] = h * per + np.arange(MLA_NOPE)
        iv[h * MLA_V:(h + 1) * MLA_V] = h * per + MLA_NOPE + np.arange(MLA_V)
    return np.concatenate([ik, iv])


def _aug_placement():
    e = np.zeros((LANES, 2 * FOX_HEADS * LANES), np.float32)
    koff = FOX_HEADS * LANES
    for h in range(FOX_HEADS):
        for p in range(3):
            src = CTRL_LOGIT + 8 * p + h
            e[src, h * LANES + AUG_LO + p] = 1.0
            e[CTRL_ONE, h * LANES + AUG_LO + 3 + p] = -1.0
            e[CTRL_ONE, koff + h * LANES + AUG_LO + p] = 1.0
            e[src, koff + h * LANES + AUG_LO + 3 + p] = 1.0
    return e


def _inproj_kernel(x_ref, mod_ref, g_ref, win_ref, qg_ref, kvg_ref, wuqa_ref, wuqb_ref, wukv_ref,
                   fb_ref, ct_ref, st_ref, tri_ref, eqk_ref,
                   qa_ref, ka_ref, vt_ref, carry_ref, *, ts):
    si = pl.program_id(1)

    @pl.when(si == 0)
    def _():
        carry_ref[...] = jnp.zeros_like(carry_ref)

    x = x_ref[0]
    shift = mod_ref[0, 0:1, :]
    scale = mod_ref[0, 1:2, :]
    h = ((_rms(x) * g_ref[...]) * (1.0 + scale) + shift).astype(BF16)
    proj = jnp.dot(h, win_ref[...], preferred_element_type=F32)

    ctab = ct_ref[0]
    stab = st_ref[0]
    lane = lax.broadcasted_iota(jnp.int32, (ts, LANES), 1)

    c_q = proj[:, C_Q0:C_Q0 + MLA_Q_LORA]
    cqn = (_rms(c_q) * qg_ref[...]).astype(BF16)
    qa = jnp.dot(cqn, wuqa_ref[...], preferred_element_type=F32)
    qb = jnp.dot(cqn, wuqb_ref[...], preferred_element_type=F32)
    mla_scale = (MLA_NOPE + MLA_ROPE) ** -0.5 * LOG2E
    for hh in range(MLA_HEADS):
        sl = slice(hh * LANES, (hh + 1) * LANES)
        qa_ref[0, hh] = ((qa[:, sl] * ctab + qb[:, sl] * stab) * mla_scale).astype(BF16)

    c_kv = proj[:, C_KV0:C_KV0 + MLA_KV_LORA]
    ckvn = (_rms(c_kv) * kvg_ref[...]).astype(BF16)
    kv = jnp.dot(ckvn, wukv_ref[...], preferred_element_type=F32)
    kra = proj[:, C_KRA0:C_KRA0 + LANES]
    krb = proj[:, C_KRB0:C_KRB0 + LANES]
    krope = kra * ctab + krb * stab
    for hh in range(MLA_HEADS):
        ka_ref[0, hh] = (kv[:, hh * LANES:(hh + 1) * LANES] + krope).astype(BF16)
    v_mla = kv[:, MLA_HEADS * LANES:]

    ctrl = (lane >= CTRL_LOGIT) & (lane < CTRL_LOGIT + FOX_HEADS)
    fl = kra + fb_ref[...]
    lsig = jnp.minimum(fl, 0.0) - jnp.log1p(jnp.exp(-jnp.abs(fl)))
    lf = jnp.where(ctrl, lsig, 0.0)
    hi, mid, lo = _split3(lf)
    p1 = (hi + pltpu.roll(mid, 8, 1) + pltpu.roll(lo, 16, 1)).astype(BF16)
    cs = jnp.dot(tri_ref[...], p1, preferred_element_type=F32)
    cs = cs + pltpu.roll(cs, LANES - 8, 1) + pltpu.roll(cs, LANES - 16, 1)
    fcum = jnp.where(ctrl, cs, 0.0) + carry_ref[...]
    carry_ref[...] = fcum[ts - 1:ts, :]

    hi, mid, lo = _split3(fcum * LOG2E)
    p2 = hi + pltpu.roll(mid, 8, 1) + pltpu.roll(lo, 16, 1)
    p2 = jnp.where(lane == CTRL_ONE, 1.0, p2).astype(BF16)
    aug = jnp.dot(p2, eqk_ref[...], preferred_element_type=F32)

    fq = proj[:, C_FQ0:C_FQ0 + FOX_W]
    fk = proj[:, C_FK0:C_FK0 + FOX_W]
    low = lane < FOX_HEAD_DIM
    koff = FOX_HEADS * LANES
    for j in range(FOX_HEADS // 2):
        sl = slice(j * LANES, (j + 1) * LANES)
        for src, off, dst in ((fq, 0, qa_ref), (fk, koff, ka_ref)):
            blk = src[:, sl] * LOG2E if dst is qa_ref else src[:, sl]
            blk_r = pltpu.roll(blk, FOX_HEAD_DIM, 1)
            h0 = 2 * j
            a0 = aug[:, off + h0 * LANES:off + (h0 + 1) * LANES]
            a1 = aug[:, off + (h0 + 1) * LANES:off + (h0 + 2) * LANES]
            dst[0, MLA_HEADS + h0] = jnp.where(low, blk, a0).astype(BF16)
            dst[0, MLA_HEADS + h0 + 1] = jnp.where(low, blk_r, a1).astype(BF16)

    fv = proj[:, C_FV0:C_FV0 + FOX_W]
    vt_ref[0, 0, 0:MLA_HEADS * MLA_V, :] = v_mla.T.astype(BF16)
    vt_ref[0, 0, MLA_HEADS * MLA_V:, :] = fv.T.astype(BF16)


def _inproj(x, mod_l, g, win, qg, kvg, wuqa, wuqb, wukv, fb_row, ctab, stab, tri, eqk, *, ts):
    B, S, D = x.shape
    nst = S // ts
    const2 = lambda b, s: (0, 0)
    kern = functools.partial(_inproj_kernel, ts=ts)
    return pl.pallas_call(
        kern,
        grid=(B, nst),
        in_specs=[
            pl.BlockSpec((1, ts, D), lambda b, s: (b, s, 0)),
            pl.BlockSpec((1, 6, D), lambda b, s: (b, 0, 0)),
            pl.BlockSpec((1, D), const2),
            pl.BlockSpec(win.shape, const2),
            pl.BlockSpec(qg.shape, const2),
            pl.BlockSpec(kvg.shape, const2),
            pl.BlockSpec(wuqa.shape, const2),
            pl.BlockSpec(wuqb.shape, const2),
            pl.BlockSpec(wukv.shape, const2),
            pl.BlockSpec(fb_row.shape, const2),
            pl.BlockSpec((1, ts, LANES), lambda b, s: (b, s, 0)),
            pl.BlockSpec((1, ts, LANES), lambda b, s: (b, s, 0)),
            pl.BlockSpec(tri.shape, const2),
            pl.BlockSpec(eqk.shape, const2),
        ],
        out_specs=[
            pl.BlockSpec((1, N_HEADS, ts, LANES), lambda b, s: (b, 0, s, 0)),
            pl.BlockSpec((1, N_HEADS, ts, LANES), lambda b, s: (b, 0, s, 0)),
            pl.BlockSpec((1, 1, N_HEADS * MLA_V, ts), lambda b, s: (b, s, 0, 0)),
        ],
        out_shape=[
            jax.ShapeDtypeStruct((B, N_HEADS, S, LANES), BF16),
            jax.ShapeDtypeStruct((B, N_HEADS, S, LANES), BF16),
            jax.ShapeDtypeStruct((B, nst, N_HEADS * MLA_V, ts), BF16),
        ],
        scratch_shapes=[pltpu.VMEM((1, LANES), F32)],
        compiler_params=pltpu.CompilerParams(
            dimension_semantics=("arbitrary", "arbitrary"), vmem_limit_bytes=VMEM_LIMIT),
    )(x, mod_l, g, win, qg, kvg, wuqa, wuqb, wukv, fb_row, ctab, stab, tri, eqk)


def _colmax8(st, groups):
    tk, tq = st.shape
    v = st.reshape(groups, tk // (8 * groups), 8, tq)
    return jnp.max(jnp.max(v, axis=1), axis=0)


def _attn_kernel(q_ref, k_ref, vt_ref, o_ref, s_scr, tmax_scr, m_scr, acc_scr,
                 *, tq, tk, hp, nq):
    dv = MLA_V
    groups = 4
    ones = jnp.ones((ONES_ROWS, tk), BF16)

    def qk(qi, j):
        q0 = pl.multiple_of(qi * tq, tq)
        return [lax.dot_general(k_ref[0, h, pl.ds(pl.multiple_of(j * tk, tk), tk), :],
                                q_ref[0, h, pl.ds(q0, tq), :], (((1,), (1,)), ((), ())),
                                preferred_element_type=F32) for h in range(hp)]

    def park(sts, qi, j, masked):
        for h in range(hp):
            st = sts[h]
            if masked:
                kpos = j * tk + lax.broadcasted_iota(jnp.int32, (tk, tq), 0)
                qpos = qi * tq + lax.broadcasted_iota(jnp.int32, (tk, tq), 1)
                st = jnp.where(kpos <= qpos, st, NEG_BIG)
            s_scr[h] = st
            tmax_scr[h] = _colmax8(st, groups)

    def update(j):
        for h in range(hp):
            m = m_scr[h]
            m_new = jnp.maximum(m, jnp.max(tmax_scr[h], axis=0, keepdims=True))
            alpha = jnp.exp2(m - m_new)
            p = jnp.exp2((s_scr[h] - m_new).astype(BF16))
            vt1 = jnp.concatenate([vt_ref[0, j, h * dv:(h + 1) * dv, :], ones], axis=0)
            acc_scr[h] = alpha * acc_scr[h] + jnp.dot(vt1, p, preferred_element_type=F32)
            m_scr[h] = m_new

    def reset_state():
        m_scr[...] = jnp.full(m_scr.shape, NEG_BIG, F32)
        acc_scr[...] = jnp.zeros(acc_scr.shape, F32)

    def finish(qi):
        q0 = pl.multiple_of(qi * tq, tq)
        for h in range(hp):
            acc = acc_scr[h]
            o_ref[0, h * dv:(h + 1) * dv, pl.ds(q0, tq)] = (
                acc[0:dv, :] / acc[dv:dv + 1, :]).astype(o_ref.dtype)
        reset_state()

    reset_state()
    park(qk(0, 0), 0, 0, True)

    def q_tile(qi, c):
        nf = (qi * tq) // tk

        def below_diagonal(t, c2):
            sts = qk(qi, t + 1)
            update(t)
            park(sts, qi, t + 1, False)
            return c2

        lax.fori_loop(0, nf - 1, below_diagonal, 0)

        @pl.when(nf > 0)
        def _():
            sts = qk(qi, nf)
            update(nf - 1)
            park(sts, qi, nf, True)

        nxt = qi + 1
        nxt_on_diagonal = (nxt * tq) // tk == 0

        @pl.when((nxt < nq) & nxt_on_diagonal)
        def _():
            sts = qk(nxt, 0)
            update(nf)
            finish(qi)
            park(sts, nxt, 0, True)

        @pl.when((nxt < nq) & jnp.logical_not(nxt_on_diagonal))
        def _():
            sts = qk(nxt, 0)
            update(nf)
            finish(qi)
            park(sts, nxt, 0, False)

        @pl.when(nxt == nq)
        def _():
            update(nf)
            finish(qi)

        return c

    lax.fori_loop(0, nq, q_tile, 0)


def _attention(qa, ka, vt, *, tq, tk, hp):
    B, H, S, _ = qa.shape
    nkt = vt.shape[1]
    assert S % tq == 0 and nkt * tk == S and H % hp == 0 and tk % tq == 0
    kern = functools.partial(_attn_kernel, tq=tq, tk=tk, hp=hp, nq=S // tq)
    return pl.pallas_call(
        kern,
        grid=(B, H // hp),
        in_specs=[
            pl.BlockSpec((1, hp, S, LANES), lambda b, g: (b, g, 0, 0)),
            pl.BlockSpec((1, hp, S, LANES), lambda b, g: (b, g, 0, 0)),
            pl.BlockSpec((1, nkt, hp * MLA_V, tk), lambda b, g: (b, 0, g, 0)),
        ],
        out_specs=pl.BlockSpec((1, hp * MLA_V, S), lambda b, g: (b, g, 0)),
        out_shape=jax.ShapeDtypeStruct((B, H * MLA_V, S), BF16),
        scratch_shapes=[
            pltpu.VMEM((hp, tk, tq), F32),
            pltpu.VMEM((hp, 8, tq), F32),
            pltpu.VMEM((hp, 1, tq), F32),
            pltpu.VMEM((hp, MLA_V + ONES_ROWS, tq), F32),
        ],
        compiler_params=pltpu.CompilerParams(
            dimension_semantics=("arbitrary", "arbitrary"), vmem_limit_bytes=VMEM_LIMIT),
    )(qa, ka, vt)


def _top2_gates(logits, lane):
    lg = jnp.where(lane < N_EXPERTS, logits, NEG_BIG)
    m1 = jnp.max(lg, axis=1, keepdims=True)
    i1 = jnp.min(jnp.where(lg == m1, lane, LANES), axis=1, keepdims=True)
    lg2 = jnp.where(lane == i1, NEG_BIG, lg)
    m2 = jnp.max(lg2, axis=1, keepdims=True)
    i2 = jnp.min(jnp.where(lg2 == m2, lane, LANES), axis=1, keepdims=True)
    e2 = jnp.exp(m2 - m1)
    den = 1.0 + e2
    return jnp.where(lane == i1, 1.0 / den, 0.0) + jnp.where(lane == i2, e2 / den, 0.0)


def _outproj_kernel(*refs, ts, with_router):
    if with_router:
        (ot_ref, x_ref, mod_ref, mg_ref, fg_ref, wo_ref, ng_ref, rw_ref,
         x1_ref, h2_ref, gates_ref) = refs
    else:
        ot_ref, x_ref, mod_ref, mg_ref, fg_ref, wo_ref, ng_ref, x1_ref, h2_ref = refs
    half = MLA_HEADS * MLA_V
    om = ot_ref[0, 0:half, :].astype(F32).T
    of = ot_ref[0, half:, :].astype(F32).T
    on = jnp.concatenate([_rms(om) * mg_ref[...], _rms(of) * fg_ref[...]], axis=1).astype(BF16)
    mix = jnp.dot(on, wo_ref[...], preferred_element_type=F32)
    x1 = x_ref[0] + mod_ref[0, 2:3, :] * mix
    x1_ref[0] = x1
    h2 = ((_rms(x1) * ng_ref[...]) * (1.0 + mod_ref[0, 4:5, :]) + mod_ref[0, 3:4, :]).astype(BF16)
    h2_ref[0] = h2
    if with_router:
        logits = jnp.dot(h2, rw_ref[...], preferred_element_type=F32)
        lane = lax.broadcasted_iota(jnp.int32, (ts, LANES), 1)
        gates_ref[0] = _top2_gates(logits, lane)


def _outproj(ot, x, mod_l, mg, fg, wo, ng, rw, *, ts):
    B, S, D = x.shape
    with_router = rw is not None
    const2 = lambda b, s: (0, 0)
    in_specs = [
        pl.BlockSpec((1, ot.shape[1], ts), lambda b, s: (b, 0, s)),
        pl.BlockSpec((1, ts, D), lambda b, s: (b, s, 0)),
        pl.BlockSpec((1, 6, D), lambda b, s: (b, 0, 0)),
        pl.BlockSpec(mg.shape, const2),
        pl.BlockSpec(fg.shape, const2),
        pl.BlockSpec(wo.shape, const2),
        pl.BlockSpec(ng.shape, const2),
    ]
    args = [ot, x, mod_l, mg, fg, wo, ng]
    out_specs = [pl.BlockSpec((1, ts, D), lambda b, s: (b, s, 0)),
                 pl.BlockSpec((1, ts, D), lambda b, s: (b, s, 0))]
    out_shape = [jax.ShapeDtypeStruct((B, S, D), F32), jax.ShapeDtypeStruct((B, S, D), BF16)]
    if with_router:
        in_specs.append(pl.BlockSpec(rw.shape, const2))
        args.append(rw)
        out_specs.append(pl.BlockSpec((1, ts, LANES), lambda b, s: (b, s, 0)))
        out_shape.append(jax.ShapeDtypeStruct((B, S, LANES), F32))
    kern = functools.partial(_outproj_kernel, ts=ts, with_router=with_router)
    return pl.pallas_call(
        kern,
        grid=(B, S // ts),
        in_specs=in_specs,
        out_specs=out_specs,
        out_shape=out_shape,
        compiler_params=pltpu.CompilerParams(
            dimension_semantics=("arbitrary", "arbitrary"), vmem_limit_bytes=VMEM_LIMIT),
    )(*args)


def _fuse_gate_up(wg, wu, tf):
    E, D, F = wg.shape
    w = jnp.concatenate([wg.reshape(E, D, F // tf, tf), wu.reshape(E, D, F // tf, tf)], axis=-1)
    return jnp.transpose(w, (0, 2, 1, 3)).astype(BF16)


def _swiglu_tile(x, wgu, tf):
    gu = jnp.dot(x, wgu, preferred_element_type=F32)
    g = gu[:, :tf]
    return g * _sigmoid(g) * gu[:, tf:]


def _ffn_kernel(h_ref, wgu_ref, wd_ref, x_ref, mod_ref, fin_ref, o_ref, acc_ref,
                *, tf, final_norm):
    f = pl.program_id(1)

    @pl.when(f == 0)
    def _():
        acc_ref[...] = jnp.zeros_like(acc_ref)

    a = _swiglu_tile(h_ref[...], wgu_ref[0, 0], tf)
    acc_ref[...] += jnp.dot(a.astype(BF16), wd_ref[...], preferred_element_type=F32)

    @pl.when(f == pl.num_programs(1) - 1)
    def _():
        x2 = x_ref[...] + mod_ref[0, 5:6, :] * acc_ref[...]
        if final_norm:
            x2 = _rms(x2) * fin_ref[...]
        o_ref[...] = x2


def _ffn(h2, wgu, wd, x1, mod_l, fin_g, *, tm, tf, seq, final_norm):
    T, D = h2.shape
    F = wd.shape[0]
    tiles_per_batch = seq // tm
    kern = functools.partial(_ffn_kernel, tf=tf, final_norm=final_norm)
    return pl.pallas_call(
        kern,
        grid=(T // tm, F // tf),
        in_specs=[
            pl.BlockSpec((tm, D), lambda i, f: (i, 0)),
            pl.BlockSpec((1, 1, D, 2 * tf), lambda i, f: (0, f, 0, 0)),
            pl.BlockSpec((tf, D), lambda i, f: (f, 0)),
            pl.BlockSpec((tm, D), lambda i, f: (i, 0)),
            pl.BlockSpec((1, 6, D), lambda i, f: (i // tiles_per_batch, 0, 0)),
            pl.BlockSpec((1, D), lambda i, f: (0, 0)),
        ],
        out_specs=pl.BlockSpec((tm, D), lambda i, f: (i, 0)),
        out_shape=jax.ShapeDtypeStruct((T, D), F32),
        scratch_shapes=[pltpu.VMEM((tm, D), F32)],
        compiler_params=pltpu.CompilerParams(
            dimension_semantics=("arbitrary", "arbitrary"), vmem_limit_bytes=VMEM_LIMIT),
    )(h2, wgu, wd, x1, mod_l, fin_g)


def _moe_kernel(h_ref, gates_ref, wg0_ref, wg1_ref, wu0_ref, wu1_ref, wd0_ref, wd1_ref,
                x_ref, mod_ref, fin_ref, o_ref,
                rank_scr, rankt_scr, gatest_scr, xc_scr, gc_scr, yc_scr, nch_ref,
                *, tm, tf, rc, final_norm):
    rcp = xc_scr.shape[1]
    dh = xc_scr.shape[2] // 2
    e = pl.program_id(1)
    f = pl.program_id(2)
    last_f = f == pl.num_programs(2) - 1

    @pl.when((e == 0) & (f == 0))
    def _():
        gates = gates_ref[...]
        sel = gates > 0.0
        selb = jnp.where(sel, 1.0, 0.0).astype(BF16)
        earlier = (lax.broadcasted_iota(jnp.int32, (tm, tm), 1)
                   < lax.broadcasted_iota(jnp.int32, (tm, tm), 0))
        rank = jnp.dot(jnp.where(earlier, 1.0, 0.0).astype(BF16), selb,
                       preferred_element_type=F32)
        rank = jnp.where(sel, rank, -1.0)
        rank_scr[...] = rank
        rankt_scr[...] = rank.T
        gatest_scr[...] = gates.T
        o_ref[...] = jnp.zeros_like(o_ref)

    @pl.when(f == 0)
    def _():
        lane = lax.broadcasted_iota(jnp.int32, (tm, LANES), 1)
        cnt = jnp.sum(jnp.where((lane == e) & (rank_scr[...] >= 0.0), 1.0, 0.0))
        nch = (cnt.astype(jnp.int32) + (rc - 1)) // rc
        nch_ref[0] = nch
        rrow = rankt_scr[pl.ds(e, 1), :].astype(jnp.int32)
        grow = gatest_scr[pl.ds(e, 1), :]

        def gather(k, c):
            row = lax.broadcasted_iota(jnp.int32, (rc, tm), 0) + k * rc
            hit = rrow == row
            onehot = jnp.where(hit, 1.0, 0.0).astype(BF16)
            xc_scr[k, 0:rc] = jnp.dot(onehot, h_ref[...], preferred_element_type=F32).astype(BF16)
            gcol = jnp.sum(jnp.where(hit, grow, 0.0), axis=1, keepdims=True)
            gc_scr[k, 0:rc] = jnp.broadcast_to(gcol, (rc, LANES))
            yc_scr[k] = jnp.zeros(yc_scr.shape[1:], F32)
            return c

        lax.fori_loop(0, nch, gather, 0)

    def expert(k, c):
        xa = xc_scr[k, 0:rc, 0:dh]
        xb = xc_scr[k, 0:rc, dh:]
        g = (jnp.dot(xa, wg0_ref[0, 0], preferred_element_type=F32)
             + jnp.dot(xb, wg1_ref[0, 0], preferred_element_type=F32))
        u = (jnp.dot(xa, wu0_ref[0, 0], preferred_element_type=F32)
             + jnp.dot(xb, wu1_ref[0, 0], preferred_element_type=F32))
        a = (g * _sigmoid(g) * u * gc_scr[k, 0:rc, 0:1]).astype(BF16)
        yc_scr[k, 0:rc, 0:dh] += jnp.dot(a, wd0_ref[0, 0, 0], preferred_element_type=F32)
        yc_scr[k, 0:rc, dh:] += jnp.dot(a, wd1_ref[0, 0, 0], preferred_element_type=F32)
        return c

    lax.fori_loop(0, nch_ref[0], expert, 0)

    @pl.when(last_f)
    def _():
        lane = lax.broadcasted_iota(jnp.int32, (tm, LANES), 1)
        rcol = jnp.sum(jnp.where(lane == e, rank_scr[...], 0.0), axis=1,
                       keepdims=True).astype(jnp.int32)

        def scatter(k, c):
            col = lax.broadcasted_iota(jnp.int32, (tm, rcp), 1)
            hit = (rcol == col + k * rc) & (col < rc)
            onehot_t = jnp.where(hit, 1.0, 0.0).astype(BF16)
            o_ref[...] += jnp.dot(onehot_t, yc_scr[k].astype(BF16), preferred_element_type=F32)
            return c

        lax.fori_loop(0, nch_ref[0], scatter, 0)

    @pl.when((e == pl.num_programs(1) - 1) & last_f)
    def _():
        x2 = x_ref[...] + mod_ref[0, 5:6, :] * o_ref[...]
        if final_norm:
            x2 = _rms(x2) * fin_ref[...]
        o_ref[...] = x2


def _tile_major(w, tf):
    E, D, F = w.shape
    return jnp.transpose(w.reshape(E, D, F // tf, tf), (0, 2, 1, 3)).astype(BF16)


def _col_halves(w, tf):
    E, F, D = w.shape
    return jnp.transpose(w.reshape(E, F // tf, tf, 2, D // 2), (0, 1, 3, 2, 4)).astype(BF16)


def _moe(h2, gates, wg, wu, wd, x1, mod_l, fin_g, *, tm, tf, rc, seq, final_norm):
    T, D = h2.shape
    E, NF = wd.shape[0], wd.shape[1]
    tiles_per_batch = seq // tm
    up_half = lambda p: pl.BlockSpec((1, 1, D // 2, tf), lambda i, e, f: (e, f, p, 0))
    down_half = lambda p: pl.BlockSpec((1, 1, 1, tf, D // 2), lambda i, e, f: (e, f, p, 0, 0))
    ncap = -(-tm // rc)
    rcp = -(-rc // LANES) * LANES
    kern = functools.partial(_moe_kernel, tm=tm, tf=tf, rc=rc, final_norm=final_norm)
    return pl.pallas_call(
        kern,
        grid=(T // tm, E, NF),
        in_specs=[
            pl.BlockSpec((tm, D), lambda i, e, f: (i, 0)),
            pl.BlockSpec((tm, LANES), lambda i, e, f: (i, 0)),
            up_half(0), up_half(1), up_half(0), up_half(1), down_half(0), down_half(1),
            pl.BlockSpec((tm, D), lambda i, e, f: (i, 0)),
            pl.BlockSpec((1, 6, D), lambda i, e, f: (i // tiles_per_batch, 0, 0)),
            pl.BlockSpec((1, D), lambda i, e, f: (0, 0)),
        ],
        out_specs=pl.BlockSpec((tm, D), lambda i, e, f: (i, 0)),
        out_shape=jax.ShapeDtypeStruct((T, D), F32),
        scratch_shapes=[
            pltpu.VMEM((tm, LANES), F32),
            pltpu.VMEM((LANES, tm), F32),
            pltpu.VMEM((LANES, tm), F32),
            pltpu.VMEM((ncap, rcp, D), BF16),
            pltpu.VMEM((ncap, rcp, LANES), F32),
            pltpu.VMEM((ncap, rcp, D), F32),
            pltpu.SMEM((1,), jnp.int32),
        ],
        compiler_params=pltpu.CompilerParams(
            dimension_semantics=("arbitrary", "arbitrary", "arbitrary"),
            vmem_limit_bytes=VMEM_LIMIT),
    )(h2, gates, wg, wg, wu, wu, wd, wd, x1, mod_l, fin_g)


GROUP = 4


def _dispatch_kernel(h_ref, gates_ref, xs_ref, gc_ref, rank_ref, rankt_scr, gatest_scr, *, tm, rc):
    e = pl.program_id(1)

    @pl.when(e == 0)
    def _():
        gates = gates_ref[...]
        sel = gates > 0.0
        selb = jnp.where(sel, 1.0, 0.0).astype(BF16)
        earlier = (lax.broadcasted_iota(jnp.int32, (tm, tm), 1)
                   < lax.broadcasted_iota(jnp.int32, (tm, tm), 0))
        rank = jnp.dot(jnp.where(earlier, 1.0, 0.0).astype(BF16), selb,
                       preferred_element_type=F32)
        rank = jnp.where(sel, rank, -1.0)
        rank_ref[...] = rank
        rankt_scr[...] = rank.T
        gatest_scr[...] = gates.T

    rrow = rankt_scr[pl.ds(e, 1), :].astype(jnp.int32)
    grow = gatest_scr[pl.ds(e, 1), :]
    hit = rrow == lax.broadcasted_iota(jnp.int32, (rc, tm), 0)
    onehot = jnp.where(hit, 1.0, 0.0).astype(BF16)
    xs_ref[...] = jnp.zeros_like(xs_ref)
    gc_ref[...] = jnp.zeros_like(gc_ref)
    xs_ref[0, 0, 0, 0:rc] = jnp.dot(onehot, h_ref[...], preferred_element_type=F32).astype(BF16)
    gcol = jnp.sum(jnp.where(hit, grow, 0.0), axis=1, keepdims=True)
    gc_ref[0, 0, 0, 0:rc] = jnp.broadcast_to(gcol, (rc, LANES))


def _experts_kernel(xs_ref, gc_ref, wgu_ref, wd_ref, ys_ref, acc_ref, *, tf, rc):
    f = pl.program_id(2)

    @pl.when(f == 0)
    def _():
        acc_ref[...] = jnp.zeros_like(acc_ref)

    for j in range(GROUP):
        a = _swiglu_tile(xs_ref[0, j, 0, 0:rc], wgu_ref[0, 0], tf) * gc_ref[0, j, 0, 0:rc, 0:1]
        acc_ref[j] += jnp.dot(a.astype(BF16), wd_ref[0], preferred_element_type=F32)

    @pl.when(f == pl.num_programs(2) - 1)
    def _():
        ys_ref[...] = jnp.zeros_like(ys_ref)
        for j in range(GROUP):
            ys_ref[0, j, 0, 0:rc] = acc_ref[j].astype(BF16)


def _combine_kernel(ys_ref, rank_ref, x_ref, mod_ref, fin_ref, o_ref, *, tm, rc, final_norm):
    e = pl.program_id(1)
    rcp = ys_ref.shape[3]

    @pl.when(e == 0)
    def _():
        o_ref[...] = jnp.zeros_like(o_ref)

    lane = lax.broadcasted_iota(jnp.int32, (tm, LANES), 1)
    rcol = jnp.sum(jnp.where(lane == e, rank_ref[...], 0.0), axis=1,
                   keepdims=True).astype(jnp.int32)
    col = lax.broadcasted_iota(jnp.int32, (tm, rcp), 1)
    onehot_t = jnp.where((rcol == col) & (col < rc), 1.0, 0.0).astype(BF16)
    o_ref[...] += jnp.dot(onehot_t, ys_ref[0, 0, 0], preferred_element_type=F32)

    @pl.when(e == pl.num_programs(1) - 1)
    def _():
        x2 = x_ref[...] + mod_ref[0, 5:6, :] * o_ref[...]
        if final_norm:
            x2 = _rms(x2) * fin_ref[...]
        o_ref[...] = x2


def _dispatch(h2, gates, *, tm, rc, rcp):
    T, D = h2.shape
    nt = T // tm
    E = N_EXPERTS
    kern = functools.partial(_dispatch_kernel, tm=tm, rc=rc)
    chunk = lambda i, e: (i // GROUP, i % GROUP, e, 0, 0)
    return pl.pallas_call(
        kern,
        grid=(nt, E),
        in_specs=[
            pl.BlockSpec((tm, D), lambda i, e: (i, 0)),
            pl.BlockSpec((tm, LANES), lambda i, e: (i, 0)),
        ],
        out_specs=[
            pl.BlockSpec((1, 1, 1, rcp, D), chunk),
            pl.BlockSpec((1, 1, 1, rcp, LANES), chunk),
            pl.BlockSpec((tm, LANES), lambda i, e: (i, 0)),
        ],
        out_shape=[
            jax.ShapeDtypeStruct((nt // GROUP, GROUP, E, rcp, D), BF16),
            jax.ShapeDtypeStruct((nt // GROUP, GROUP, E, rcp, LANES), F32),
            jax.ShapeDtypeStruct((T, LANES), F32),
        ],
        scratch_shapes=[pltpu.VMEM((LANES, tm), F32), pltpu.VMEM((LANES, tm), F32)],
        compiler_params=pltpu.CompilerParams(
            dimension_semantics=("arbitrary", "arbitrary"), vmem_limit_bytes=VMEM_LIMIT),
    )(h2, gates)


def _experts(xs, gc, wgu, wd, *, tf, rc):
    ng, _, E, rcp, D = xs.shape
    NF = wgu.shape[1]
    kern = functools.partial(_experts_kernel, tf=tf, rc=rc)
    chunks = lambda e, g, f: (g, 0, e, 0, 0)
    return pl.pallas_call(
        kern,
        grid=(E, ng, NF),
        in_specs=[
            pl.BlockSpec((1, GROUP, 1, rcp, D), chunks),
            pl.BlockSpec((1, GROUP, 1, rcp, LANES), chunks),
            pl.BlockSpec((1, 1, D, 2 * tf), lambda e, g, f: (e, f, 0, 0)),
            pl.BlockSpec((1, tf, D), lambda e, g, f: (e, f, 0)),
        ],
        out_specs=pl.BlockSpec((1, GROUP, 1, rcp, D), chunks),
        out_shape=jax.ShapeDtypeStruct(xs.shape, BF16),
        scratch_shapes=[pltpu.VMEM((GROUP, rc, D), F32)],
        compiler_params=pltpu.CompilerParams(
            dimension_semantics=("arbitrary", "arbitrary", "arbitrary"),
            vmem_limit_bytes=VMEM_LIMIT),
    )(xs, gc, wgu, wd)


def _combine(ys, rank, x1, mod_l, fin_g, *, tm, rc, seq, final_norm):
    T, D = x1.shape
    E, rcp = ys.shape[2], ys.shape[3]
    tiles_per_batch = seq // tm
    kern = functools.partial(_combine_kernel, tm=tm, rc=rc, final_norm=final_norm)
    return pl.pallas_call(
        kern,
        grid=(T // tm, E),
        in_specs=[
            pl.BlockSpec((1, 1, 1, rcp, D), lambda i, e: (i // GROUP, i % GROUP, e, 0, 0)),
            pl.BlockSpec((tm, LANES), lambda i, e: (i, 0)),
            pl.BlockSpec((tm, D), lambda i, e: (i, 0)),
            pl.BlockSpec((1, 6, D), lambda i, e: (i // tiles_per_batch, 0, 0)),
            pl.BlockSpec((1, D), lambda i, e: (0, 0)),
        ],
        out_specs=pl.BlockSpec((tm, D), lambda i, e: (i, 0)),
        out_shape=jax.ShapeDtypeStruct((T, D), F32),
        compiler_params=pltpu.CompilerParams(
            dimension_semantics=("arbitrary", "arbitrary"), vmem_limit_bytes=VMEM_LIMIT),
    )(ys, rank, x1, mod_l, fin_g)


def _moe_layer(h2, gates, w_gate, w_up, w_down, x1, mod_l, fin_g, *, tm, tf, rc, seq, final_norm):
    T = h2.shape[0]
    rcp = -(-rc // LANES) * LANES
    if (T // tm) % GROUP != 0:
        return _moe(h2, gates, _tile_major(w_gate, tf), _tile_major(w_up, tf),
                    _col_halves(w_down, tf), x1, mod_l, fin_g,
                    tm=tm, tf=tf, rc=rc, seq=seq, final_norm=final_norm)
    xs, gc, rank = _dispatch(h2, gates, tm=tm, rc=rc, rcp=rcp)

    def expert_major(_):
        ys = _experts(xs, gc, _fuse_gate_up(w_gate, w_up, tf), w_down.astype(BF16), tf=tf, rc=rc)
        return _combine(ys, rank, x1, mod_l, fin_g, tm=tm, rc=rc, seq=seq, final_norm=final_norm)

    def tile_major(_):
        return _moe(h2, gates, _tile_major(w_gate, tf), _tile_major(w_up, tf),
                    _col_halves(w_down, tf), x1, mod_l, fin_g,
                    tm=tm, tf=tf, rc=rc, seq=seq, final_norm=final_norm)

    return lax.cond(jnp.max(rank) < rc, expert_major, tile_major, 0)


def _tiles(S):
    ts = min(512, S)
    tq = min(512, S)
    tk = ts
    tm = min(1024, S)
    tf = 896
    rc = tm // 4 + 64
    assert rc % 16 == 0
    return ts, tq, tk, tm, tf, rc


def kernel(x, c, positions, ada_w, ada_b, attn_norm_g, w_in, q_norm_g, w_uq, kv_norm_g, w_ukv,
           fox_forget_b, mla_out_g, fox_out_g, w_o, ffn_norm_g, dense_w_gate, dense_w_up,
           dense_w_down, router_w, moe_w_gate, moe_w_up, moe_w_down, final_norm_g):
    B, S, D = x.shape
    L = ada_w.shape[0]
    ts, tq, tk, tm, tf, rc = _tiles(S)

    mod = _adaln(c, ada_w, ada_b).reshape(L, B, 6, D)
    ctab, stab = _rope_tables(positions)

    idx_in = _perm_w_in()
    idx_qa, idx_qb = _perm_w_uq()
    idx_kv = _perm_w_ukv()
    col_scale = np.ones((C_END,), np.float32)
    col_scale[C_FQ0:C_FQ0 + FOX_W] = FOX_HEAD_DIM ** -0.5
    tri = jnp.asarray(np.tril(np.ones((ts, ts), np.float32)), BF16)
    eqk = jnp.asarray(_aug_placement(), BF16)

    for l in range(L):
        win = (_take_cols(w_in[l], idx_in) * col_scale).astype(BF16)
        wuqa = _take_cols(w_uq[l], idx_qa).astype(BF16)
        wuqb = _take_cols(w_uq[l], idx_qb).astype(BF16)
        wukv = _take_cols(w_ukv[l], idx_kv).astype(BF16)
        fb_row = jnp.zeros((1, LANES), F32).at[0, CTRL_LOGIT:CTRL_LOGIT + FOX_HEADS].set(
            fox_forget_b[l].astype(F32))
        qa, ka, vt = _inproj(
            x, mod[l], attn_norm_g[l].reshape(1, D), win, q_norm_g[l].reshape(1, -1),
            kv_norm_g[l].reshape(1, -1), wuqa, wuqb, wukv, fb_row, ctab, stab, tri, eqk, ts=ts)
        ot = _attention(qa, ka, vt, tq=tq, tk=tk, hp=4)

        j = l // 2
        is_moe = (l % 2 == 1)
        rw = None
        if is_moe:
            rw = jnp.pad(router_w[j], ((0, 0), (0, LANES - N_EXPERTS))).astype(BF16)
        outs = _outproj(ot, x, mod[l], mla_out_g[l].reshape(1, -1), fox_out_g[l].reshape(1, -1),
                        w_o[l].astype(BF16), ffn_norm_g[l].reshape(1, D), rw, ts=ts)
        x1, h2 = outs[0].reshape(B * S, D), outs[1].reshape(B * S, D)
        fin_g = final_norm_g.reshape(1, D)
        last = l == L - 1
        if is_moe:
            x = _moe_layer(h2, outs[2].reshape(B * S, LANES), moe_w_gate[j], moe_w_up[j],
                           moe_w_down[j], x1, mod[l], fin_g,
                           tm=tm, tf=tf, rc=rc, seq=S, final_norm=last)
        else:
            x = _ffn(h2, _fuse_gate_up(dense_w_gate[j][None], dense_w_up[j][None], tf),
                     dense_w_down[j].astype(BF16), x1, mod[l], fin_g,
                     tm=tm, tf=tf, seq=S, final_norm=last)
        x = x.reshape(B, S, D)
    return x
```

```python
import functools

import numpy as np
import jax
import jax.numpy as jnp
from jax import lax
from jax.experimental import pallas as pl
from jax.experimental.pallas import tpu as pltpu

F32 = jnp.float32
BF16 = jnp.bfloat16

D_MODEL = 1024
MLA_HEADS = 8
MLA_V = 64
MLA_NOPE = 64
MLA_ROPE = 32
MLA_Q_LORA = 256
MLA_KV_LORA = 128
FOX_HEADS = 8
FOX_HEAD_DIM = 64
FOX_W = FOX_HEADS * FOX_HEAD_DIM
ROPE_THETA = 10000.0
D_FF = 3584
N_EXPERTS = 8
EPS = 1e-6
N_HEADS = MLA_HEADS + FOX_HEADS
HALF_ROPE = MLA_ROPE // 2

LANES = 128
VMEM_LIMIT = 56 * 1024 * 1024

ROPE_LO = MLA_NOPE
AUG_LO = FOX_HEAD_DIM
CTRL_LOGIT = 96
CTRL_ONE = 120

C_Q0 = 0
C_KV0 = C_Q0 + MLA_Q_LORA
C_KRA0 = C_KV0 + MLA_KV_LORA
C_KRB0 = C_KRA0 + LANES
C_FQ0 = C_KRB0 + LANES
C_FK0 = C_FQ0 + FOX_W
C_FV0 = C_FK0 + FOX_W
C_END = C_FV0 + FOX_W

NEG_BIG = -1e30
LOG2E = 1.4426950408889634
ONES_ROWS = 16


def _rms(v):
    return v * lax.rsqrt(jnp.mean(v * v, axis=-1, keepdims=True) + EPS)


def _sigmoid(v):
    return 1.0 / (1.0 + jnp.exp(-v))


def _split3(v):
    hi = v.astype(BF16).astype(F32)
    r = v - hi
    mid = r.astype(BF16).astype(F32)
    lo = (r - mid).astype(BF16).astype(F32)
    return hi, mid, lo


def _adaln_kernel(c_ref, w_ref, b_ref, o_ref):
    c = c_ref[...]
    ca = (c * _sigmoid(c)).astype(BF16)
    o_ref[0] = jnp.dot(ca, w_ref[0].astype(BF16), preferred_element_type=F32) + b_ref[0]


def _adaln(c, ada_w, ada_b):
    L, D, N = ada_w.shape
    B = c.shape[0]
    tn = 1536
    return pl.pallas_call(
        _adaln_kernel,
        grid=(L, N // tn),
        in_specs=[
            pl.BlockSpec((B, D), lambda l, j: (0, 0)),
            pl.BlockSpec((1, D, tn), lambda l, j: (l, 0, j)),
            pl.BlockSpec((1, 1, tn), lambda l, j: (l, 0, j)),
        ],
        out_specs=pl.BlockSpec((1, B, tn), lambda l, j: (l, 0, j)),
        out_shape=jax.ShapeDtypeStruct((L, B, N), F32),
        compiler_params=pltpu.CompilerParams(
            dimension_semantics=("arbitrary", "arbitrary"), vmem_limit_bytes=VMEM_LIMIT),
    )(c, ada_w, ada_b.reshape(L, 1, N))


def _rope_tab_kernel(pos_ref, freq_ref, cos_ref, sin_ref):
    ang = freq_ref[...] * pos_ref[0]
    cos_ref[0] = jnp.cos(ang)
    sin_ref[0] = jnp.sin(ang)


def _rope_tables(positions):
    B, S = positions.shape
    half = HALF_ROPE
    inv_freq = ROPE_THETA ** (-jnp.arange(half, dtype=F32) / half)
    pos = positions.astype(F32).reshape(B, 1, S)
    cos_t, sin_t = pl.pallas_call(
        _rope_tab_kernel,
        grid=(B,),
        in_specs=[
            pl.BlockSpec((1, 1, S), lambda b: (b, 0, 0)),
            pl.BlockSpec((half, 1), lambda b: (0, 0)),
        ],
        out_specs=[pl.BlockSpec((1, half, S), lambda b: (b, 0, 0))] * 2,
        out_shape=[jax.ShapeDtypeStruct((B, half, S), F32)] * 2,
        compiler_params=pltpu.CompilerParams(dimension_semantics=("arbitrary",)),
    )(pos, inv_freq.reshape(half, 1))
    cos = jnp.transpose(cos_t, (0, 2, 1))
    sin = jnp.transpose(sin_t, (0, 2, 1))
    ones = jnp.ones((B, S, MLA_NOPE), F32)
    z_lo = jnp.zeros((B, S, MLA_NOPE), F32)
    z_hi = jnp.zeros((B, S, LANES - MLA_NOPE - MLA_ROPE), F32)
    ctab = jnp.concatenate([ones, cos, cos, z_hi], axis=-1)
    stab = jnp.concatenate([z_lo, -sin, sin, z_hi], axis=-1)
    return ctab, stab


def _take_cols(w, idx):
    pieces, i, n = [], 0, len(idx)
    while i < n:
        j = i + 1
        if idx[i] < 0:
            while j < n and idx[j] < 0:
                j += 1
            pieces.append(jnp.zeros((w.shape[0], j - i), w.dtype))
        else:
            while j < n and idx[j] == idx[j - 1] + 1:
                j += 1
            pieces.append(w[:, int(idx[i]):int(idx[i]) + (j - i)])
        i = j
    return jnp.concatenate(pieces, axis=1)


def _perm_w_in():
    cq = 0
    ckv = cq + MLA_Q_LORA
    kr = ckv + MLA_KV_LORA
    fq = kr + MLA_ROPE
    fk = fq + FOX_W
    fv = fk + FOX_W
    fl = fv + FOX_W
    idx = -np.ones((C_END,), np.int64)
    idx[C_Q0:C_Q0 + MLA_Q_LORA] = cq + np.arange(MLA_Q_LORA)
    idx[C_KV0:C_KV0 + MLA_KV_LORA] = ckv + np.arange(MLA_KV_LORA)
    h = HALF_ROPE
    idx[C_KRA0 + ROPE_LO:C_KRA0 + ROPE_LO + MLA_ROPE] = kr + np.arange(MLA_ROPE)
    idx[C_KRA0 + CTRL_LOGIT:C_KRA0 + CTRL_LOGIT + FOX_HEADS] = fl + np.arange(FOX_HEADS)
    idx[C_KRB0 + ROPE_LO:C_KRB0 + ROPE_LO + h] = kr + h + np.arange(h)
    idx[C_KRB0 + ROPE_LO + h:C_KRB0 + ROPE_LO + MLA_ROPE] = kr + np.arange(h)
    idx[C_FQ0:C_FQ0 + FOX_W] = fq + np.arange(FOX_W)
    idx[C_FK0:C_FK0 + FOX_W] = fk + np.arange(FOX_W)
    idx[C_FV0:C_FV0 + FOX_W] = fv + np.arange(FOX_W)
    return idx


def _perm_w_uq():
    per = MLA_NOPE + MLA_ROPE
    ia = -np.ones((MLA_HEADS * LANES,), np.int64)
    ib = -np.ones((MLA_HEADS * LANES,), np.int64)
    h2 = HALF_ROPE
    for h in range(MLA_HEADS):
        ia[h * LANES:h * LANES + per] = h * per + np.arange(per)
        ib[h * LANES + ROPE_LO:h * LANES + ROPE_LO + h2] = h * per + MLA_NOPE + h2 + np.arange(h2)
        ib[h * LANES + ROPE_LO + h2:h * LANES + ROPE_LO + MLA_ROPE] = h * per + MLA_NOPE + np.arange(h2)
    return ia, ib


def _perm_w_ukv():
    per = MLA_NOPE + MLA_V
    ik = -np.ones((MLA_HEADS * LANES,), np.int64)
    iv = np.zeros((MLA_HEADS * MLA_V,), np.int64)
    for h in range(MLA_HEADS):
        ik[h * LANES:h * LANES + MLA_NOPE] = h * per + np.arange(MLA_NOPE)
        iv[h * MLA_V:(h + 1) * MLA_V] = h * per + MLA_NOPE + np.arange(MLA_V)
    return np.concatenate([ik, iv])


def _aug_placement():
    e = np.zeros((LANES, 2 * FOX_HEADS * LANES), np.float32)
    koff = FOX_HEADS * LANES
    for h in range(FOX_HEADS):
        for p in range(3):
            src = CTRL_LOGIT + 8 * p + h
            e[src, h * LANES + AUG_LO + p] = 1.0
            e[CTRL_ONE, h * LANES + AUG_LO + 3 + p] = -1.0
            e[CTRL_ONE, koff + h * LANES + AUG_LO + p] = 1.0
            e[src, koff + h * LANES + AUG_LO + 3 + p] = 1.0
    return e


def _inproj_kernel(x_ref, mod_ref, g_ref, win_ref, qg_ref, kvg_ref, wuqa_ref, wuqb_ref, wukv_ref,
                   fb_ref, ct_ref, st_ref, tri_ref, eqk_ref,
                   qa_ref, ka_ref, vt_ref, carry_ref, *, ts):
    si = pl.program_id(1)

    @pl.when(si == 0)
    def _():
        carry_ref[...] = jnp.zeros_like(carry_ref)

    x = x_ref[0]
    shift = mod_ref[0, 0:1, :]
    scale = mod_ref[0, 1:2, :]
    h = ((_rms(x) * g_ref[...]) * (1.0 + scale) + shift).astype(BF16)
    proj = jnp.dot(h, win_ref[...], preferred_element_type=F32)

    ctab = ct_ref[0]
    stab = st_ref[0]
    lane = lax.broadcasted_iota(jnp.int32, (ts, LANES), 1)

    c_q = proj[:, C_Q0:C_Q0 + MLA_Q_LORA]
    cqn = (_rms(c_q) * qg_ref[...]).astype(BF16)
    qa = jnp.dot(cqn, wuqa_ref[...], preferred_element_type=F32)
    qb = jnp.dot(cqn, wuqb_ref[...], preferred_element_type=F32)
    mla_scale = (MLA_NOPE + MLA_ROPE) ** -0.5 * LOG2E
    for hh in range(MLA_HEADS):
        sl = slice(hh * LANES, (hh + 1) * LANES)
        qa_ref[0, hh] = ((qa[:, sl] * ctab + qb[:, sl] * stab) * mla_scale).astype(BF16)

    c_kv = proj[:, C_KV0:C_KV0 + MLA_KV_LORA]
    ckvn = (_rms(c_kv) * kvg_ref[...]).astype(BF16)
    kv = jnp.dot(ckvn, wukv_ref[...], preferred_element_type=F32)
    kra = proj[:, C_KRA0:C_KRA0 + LANES]
    krb = proj[:, C_KRB0:C_KRB0 + LANES]
    krope = kra * ctab + krb * stab
    for hh in range(MLA_HEADS):
        ka_ref[0, hh] = (kv[:, hh * LANES:(hh + 1) * LANES] + krope).astype(BF16)
    v_mla = kv[:, MLA_HEADS * LANES:]

    ctrl = (lane >= CTRL_LOGIT) & (lane < CTRL_LOGIT + FOX_HEADS)
    fl = kra + fb_ref[...]
    lsig = jnp.minimum(fl, 0.0) - jnp.log1p(jnp.exp(-jnp.abs(fl)))
    lf = jnp.where(ctrl, lsig, 0.0)
    hi, mid, lo = _split3(lf)
    p1 = (hi + pltpu.roll(mid, 8, 1) + pltpu.roll(lo, 16, 1)).astype(BF16)
    cs = jnp.dot(tri_ref[...], p1, preferred_element_type=F32)
    cs = cs + pltpu.roll(cs, LANES - 8, 1) + pltpu.roll(cs, LANES - 16, 1)
    fcum = jnp.where(ctrl, cs, 0.0) + carry_ref[...]
    carry_ref[...] = fcum[ts - 1:ts, :]

    hi, mid, lo = _split3(fcum * LOG2E)
    p2 = hi + pltpu.roll(mid, 8, 1) + pltpu.roll(lo, 16, 1)
    p2 = jnp.where(lane == CTRL_ONE, 1.0, p2).astype(BF16)
    aug = jnp.dot(p2, eqk_ref[...], preferred_element_type=F32)

    fq = proj[:, C_FQ0:C_FQ0 + FOX_W]
    fk = proj[:, C_FK0:C_FK0 + FOX_W]
    low = lane < FOX_HEAD_DIM
    koff = FOX_HEADS * LANES
    for j in range(FOX_HEADS // 2):
        sl = slice(j * LANES, (j + 1) * LANES)
        for src, off, dst in ((fq, 0, qa_ref), (fk, koff, ka_ref)):
            blk = src[:, sl] * LOG2E if dst is qa_ref else src[:, sl]
            blk_r = pltpu.roll(blk, FOX_HEAD_DIM, 1)
            h0 = 2 * j
            a0 = aug[:, off + h0 * LANES:off + (h0 + 1) * LANES]
            a1 = aug[:, off + (h0 + 1) * LANES:off + (h0 + 2) * LANES]
            dst[0, MLA_HEADS + h0] = jnp.where(low, blk, a0).astype(BF16)
            dst[0, MLA_HEADS + h0 + 1] = jnp.where(low, blk_r, a1).astype(BF16)

    fv = proj[:, C_FV0:C_FV0 + FOX_W]
    vt_ref[0, 0, 0:MLA_HEADS * MLA_V, :] = v_mla.T.astype(BF16)
    vt_ref[0, 0, MLA_HEADS * MLA_V:, :] = fv.T.astype(BF16)


def _inproj(x, mod_l, g, win, qg, kvg, wuqa, wuqb, wukv, fb_row, ctab, stab, tri, eqk, *, ts):
    B, S, D = x.shape
    nst = S // ts
    const2 = lambda b, s: (0, 0)
    kern = functools.partial(_inproj_kernel, ts=ts)
    return pl.pallas_call(
        kern,
        grid=(B, nst),
        in_specs=[
            pl.BlockSpec((1, ts, D), lambda b, s: (b, s, 0)),
            pl.BlockSpec((1, 6, D), lambda b, s: (b, 0, 0)),
            pl.BlockSpec((1, D), const2),
            pl.BlockSpec(win.shape, const2),
            pl.BlockSpec(qg.shape, const2),
            pl.BlockSpec(kvg.shape, const2),
            pl.BlockSpec(wuqa.shape, const2),
            pl.BlockSpec(wuqb.shape, const2),
            pl.BlockSpec(wukv.shape, const2),
            pl.BlockSpec(fb_row.shape, const2),
            pl.BlockSpec((1, ts, LANES), lambda b, s: (b, s, 0)),
            pl.BlockSpec((1, ts, LANES), lambda b, s: (b, s, 0)),
            pl.BlockSpec(tri.shape, const2),
            pl.BlockSpec(eqk.shape, const2),
        ],
        out_specs=[
            pl.BlockSpec((1, N_HEADS, ts, LANES), lambda b, s: (b, 0, s, 0)),
            pl.BlockSpec((1, N_HEADS, ts, LANES), lambda b, s: (b, 0, s, 0)),
            pl.BlockSpec((1, 1, N_HEADS * MLA_V, ts), lambda b, s: (b, s, 0, 0)),
        ],
        out_shape=[
            jax.ShapeDtypeStruct((B, N_HEADS, S, LANES), BF16),
            jax.ShapeDtypeStruct((B, N_HEADS, S, LANES), BF16),
            jax.ShapeDtypeStruct((B, nst, N_HEADS * MLA_V, ts), BF16),
        ],
        scratch_shapes=[pltpu.VMEM((1, LANES), F32)],
        compiler_params=pltpu.CompilerParams(
            dimension_semantics=("arbitrary", "arbitrary"), vmem_limit_bytes=VMEM_LIMIT),
    )(x, mod_l, g, win, qg, kvg, wuqa, wuqb, wukv, fb_row, ctab, stab, tri, eqk)


def _colmax8(st, groups):
    tk, tq = st.shape
    v = st.reshape(groups, tk // (8 * groups), 8, tq)
    return jnp.max(jnp.max(v, axis=1), axis=0)


def _attn_kernel(q_ref, k_ref, vt_ref, o_ref, s_scr, tmax_scr, m_scr, acc_scr,
                 *, tq, tk, hp, nq):
    dv = MLA_V
    groups = 4
    ones = jnp.ones((ONES_ROWS, tk), BF16)

    def qk(qi, j):
        q0 = pl.multiple_of(qi * tq, tq)
        return [lax.dot_general(k_ref[0, h, pl.ds(pl.multiple_of(j * tk, tk), tk), :],
                                q_ref[0, h, pl.ds(q0, tq), :], (((1,), (1,)), ((), ())),
                                preferred_element_type=F32) for h in range(hp)]

    def park(sts, qi, j, masked):
        for h in range(hp):
            st = sts[h]
            if masked:
                kpos = j * tk + lax.broadcasted_iota(jnp.int32, (tk, tq), 0)
                qpos = qi * tq + lax.broadcasted_iota(jnp.int32, (tk, tq), 1)
                st = jnp.where(kpos <= qpos, st, NEG_BIG)
            s_scr[h] = st
            tmax_scr[h] = _colmax8(st, groups)

    def update(j):
        for h in range(hp):
            m = m_scr[h]
            m_new = jnp.maximum(m, jnp.max(tmax_scr[h], axis=0, keepdims=True))
            alpha = jnp.exp2(m - m_new)
            p = jnp.exp2((s_scr[h] - m_new).astype(BF16))
            vt1 = jnp.concatenate([vt_ref[0, j, h * dv:(h + 1) * dv, :], ones], axis=0)
            acc_scr[h] = alpha * acc_scr[h] + jnp.dot(vt1, p, preferred_element_type=F32)
            m_scr[h] = m_new

    def reset_state():
        m_scr[...] = jnp.full(m_scr.shape, NEG_BIG, F32)
        acc_scr[...] = jnp.zeros(acc_scr.shape, F32)

    def finish(qi):
        q0 = pl.multiple_of(qi * tq, tq)
        for h in range(hp):
            acc = acc_scr[h]
            o_ref[0, h * dv:(h + 1) * dv, pl.ds(q0, tq)] = (
                acc[0:dv, :] / acc[dv:dv + 1, :]).astype(o_ref.dtype)
        reset_state()

    reset_state()
    park(qk(0, 0), 0, 0, True)

    def q_tile(qi, c):
        nf = (qi * tq) // tk

        def below_diagonal(t, c2):
            sts = qk(qi, t + 1)
            update(t)
            park(sts, qi, t + 1, False)
            return c2

        lax.fori_loop(0, nf - 1, below_diagonal, 0)

        @pl.when(nf > 0)
        def _():
            sts = qk(qi, nf)
            update(nf - 1)
            park(sts, qi, nf, True)

        nxt = qi + 1
        nxt_on_diagonal = (nxt * tq) // tk == 0

        @pl.when((nxt < nq) & nxt_on_diagonal)
        def _():
            sts = qk(nxt, 0)
            update(nf)
            finish(qi)
            park(sts, nxt, 0, True)

        @pl.when((nxt < nq) & jnp.logical_not(nxt_on_diagonal))
        def _():
            sts = qk(nxt, 0)
            update(nf)
            finish(qi)
            park(sts, nxt, 0, False)

        @pl.when(nxt == nq)
        def _():
            update(nf)
            finish(qi)

        return c

    lax.fori_loop(0, nq, q_tile, 0)


def _attention(qa, ka, vt, *, tq, tk, hp):
    B, H, S, _ = qa.shape
    nkt = vt.shape[1]
    assert S % tq == 0 and nkt * tk == S and H % hp == 0 and tk % tq == 0
    kern = functools.partial(_attn_kernel, tq=tq, tk=tk, hp=hp, nq=S // tq)
    return pl.pallas_call(
        kern,
        grid=(B, H // hp),
        in_specs=[
            pl.BlockSpec((1, hp, S, LANES), lambda b, g: (b, g, 0, 0)),
            pl.BlockSpec((1, hp, S, LANES), lambda b, g: (b, g, 0, 0)),
            pl.BlockSpec((1, nkt, hp * MLA_V, tk), lambda b, g: (b, 0, g, 0)),
        ],
        out_specs=pl.BlockSpec((1, hp * MLA_V, S), lambda b, g: (b, g, 0)),
        out_shape=jax.ShapeDtypeStruct((B, H * MLA_V, S), BF16),
        scratch_shapes=[
            pltpu.VMEM((hp, tk, tq), F32),
            pltpu.VMEM((hp, 8, tq), F32),
            pltpu.VMEM((hp, 1, tq), F32),
            pltpu.VMEM((hp, MLA_V + ONES_ROWS, tq), F32),
        ],
        compiler_params=pltpu.CompilerParams(
            dimension_semantics=("arbitrary", "arbitrary"), vmem_limit_bytes=VMEM_LIMIT),
    )(qa, ka, vt)


def _top2_gates(logits, lane):
    lg = jnp.where(lane < N_EXPERTS, logits, NEG_BIG)
    m1 = jnp.max(lg, axis=1, keepdims=True)
    i1 = jnp.min(jnp.where(lg == m1, lane, LANES), axis=1, keepdims=True)
    lg2 = jnp.where(lane == i1, NEG_BIG, lg)
    m2 = jnp.max(lg2, axis=1, keepdims=True)
    i2 = jnp.min(jnp.where(lg2 == m2, lane, LANES), axis=1, keepdims=True)
    e2 = jnp.exp(m2 - m1)
    den = 1.0 + e2
    return jnp.where(lane == i1, 1.0 / den, 0.0) + jnp.where(lane == i2, e2 / den, 0.0)


def _outproj_kernel(*refs, ts, with_router):
    if with_router:
        (ot_ref, x_ref, mod_ref, mg_ref, fg_ref, wo_ref, ng_ref, rw_ref,
         x1_ref, h2_ref, gates_ref) = refs
    else:
        ot_ref, x_ref, mod_ref, mg_ref, fg_ref, wo_ref, ng_ref, x1_ref, h2_ref = refs
    half = MLA_HEADS * MLA_V
    om = ot_ref[0, 0:half, :].astype(F32).T
    of = ot_ref[0, half:, :].astype(F32).T
    on = jnp.concatenate([_rms(om) * mg_ref[...], _rms(of) * fg_ref[...]], axis=1).astype(BF16)
    mix = jnp.dot(on, wo_ref[...], preferred_element_type=F32)
    x1 = x_ref[0] + mod_ref[0, 2:3, :] * mix
    x1_ref[0] = x1
    h2 = ((_rms(x1) * ng_ref[...]) * (1.0 + mod_ref[0, 4:5, :]) + mod_ref[0, 3:4, :]).astype(BF16)
    h2_ref[0] = h2
    if with_router:
        logits = jnp.dot(h2, rw_ref[...], preferred_element_type=F32)
        lane = lax.broadcasted_iota(jnp.int32, (ts, LANES), 1)
        gates_ref[0] = _top2_gates(logits, lane)


def _outproj(ot, x, mod_l, mg, fg, wo, ng, rw, *, ts):
    B, S, D = x.shape
    with_router = rw is not None
    const2 = lambda b, s: (0, 0)
    in_specs = [
        pl.BlockSpec((1, ot.shape[1], ts), lambda b, s: (b, 0, s)),
        pl.BlockSpec((1, ts, D), lambda b, s: (b, s, 0)),
        pl.BlockSpec((1, 6, D), lambda b, s: (b, 0, 0)),
        pl.BlockSpec(mg.shape, const2),
        pl.BlockSpec(fg.shape, const2),
        pl.BlockSpec(wo.shape, const2),
        pl.BlockSpec(ng.shape, const2),
    ]
    args = [ot, x, mod_l, mg, fg, wo, ng]
    out_specs = [pl.BlockSpec((1, ts, D), lambda b, s: (b, s, 0)),
                 pl.BlockSpec((1, ts, D), lambda b, s: (b, s, 0))]
    out_shape = [jax.ShapeDtypeStruct((B, S, D), F32), jax.ShapeDtypeStruct((B, S, D), BF16)]
    if with_router:
        in_specs.append(pl.BlockSpec(rw.shape, const2))
        args.append(rw)
        out_specs.append(pl.BlockSpec((1, ts, LANES), lambda b, s: (b, s, 0)))
        out_shape.append(jax.ShapeDtypeStruct((B, S, LANES), F32))
    kern = functools.partial(_outproj_kernel, ts=ts, with_router=with_router)
    return pl.pallas_call(
        kern,
        grid=(B, S // ts),
        in_specs=in_specs,
        out_specs=out_specs,
        out_shape=out_shape,
        compiler_params=pltpu.CompilerParams(
            dimension_semantics=("arbitrary", "arbitrary"), vmem_limit_bytes=VMEM_LIMIT),
    )(*args)


def _fuse_gate_up(wg, wu, tf):
    E, D, F = wg.shape
    w = jnp.concatenate([wg.reshape(E, D, F // tf, tf), wu.reshape(E, D, F // tf, tf)], axis=-1)
    return jnp.transpose(w, (0, 2, 1, 3)).astype(BF16)


def _swiglu_tile(x, wgu, tf):
    gu = jnp.dot(x, wgu, preferred_element_type=F32)
    g = gu[:, :tf]
    return g * _sigmoid(g) * gu[:, tf:]


def _ffn_kernel(h_ref, wgu_ref, wd_ref, x_ref, mod_ref, fin_ref, o_ref, acc_ref,
                *, tf, final_norm):
    f = pl.program_id(1)

    @pl.when(f == 0)
    def _():
        acc_ref[...] = jnp.zeros_like(acc_ref)

    a = _swiglu_tile(h_ref[...], wgu_ref[0, 0], tf)
    acc_ref[...] += jnp.dot(a.astype(BF16), wd_ref[...], preferred_element_type=F32)

    @pl.when(f == pl.num_programs(1) - 1)
    def _():
        x2 = x_ref[...] + mod_ref[0, 5:6, :] * acc_ref[...]
        if final_norm:
            x2 = _rms(x2) * fin_ref[...]
        o_ref[...] = x2


def _ffn(h2, wgu, wd, x1, mod_l, fin_g, *, tm, tf, seq, final_norm):
    T, D = h2.shape
    F = wd.shape[0]
    tiles_per_batch = seq // tm
    kern = functools.partial(_ffn_kernel, tf=tf, final_norm=final_norm)
    return pl.pallas_call(
        kern,
        grid=(T // tm, F // tf),
        in_specs=[
            pl.BlockSpec((tm, D), lambda i, f: (i, 0)),
            pl.BlockSpec((1, 1, D, 2 * tf), lambda i, f: (0, f, 0, 0)),
            pl.BlockSpec((tf, D), lambda i, f: (f, 0)),
            pl.BlockSpec((tm, D), lambda i, f: (i, 0)),
            pl.BlockSpec((1, 6, D), lambda i, f: (i // tiles_per_batch, 0, 0)),
            pl.BlockSpec((1, D), lambda i, f: (0, 0)),
        ],
        out_specs=pl.BlockSpec((tm, D), lambda i, f: (i, 0)),
        out_shape=jax.ShapeDtypeStruct((T, D), F32),
        scratch_shapes=[pltpu.VMEM((tm, D), F32)],
        compiler_params=pltpu.CompilerParams(
            dimension_semantics=("arbitrary", "arbitrary"), vmem_limit_bytes=VMEM_LIMIT),
    )(h2, wgu, wd, x1, mod_l, fin_g)


def _moe_kernel(h_ref, gates_ref, wg0_ref, wg1_ref, wu0_ref, wu1_ref, wd0_ref, wd1_ref,
                x_ref, mod_ref, fin_ref, o_ref,
                rank_scr, rankt_scr, gatest_scr, xc_scr, gc_scr, yc_scr, nch_ref,
                *, tm, tf, rc, final_norm):
    rcp = xc_scr.shape[1]
    dh = xc_scr.shape[2] // 2
    e = pl.program_id(1)
    f = pl.program_id(2)
    last_f = f == pl.num_programs(2) - 1

    @pl.when((e == 0) & (f == 0))
    def _():
        gates = gates_ref[...]
        sel = gates > 0.0
        selb = jnp.where(sel, 1.0, 0.0).astype(BF16)
        earlier = (lax.broadcasted_iota(jnp.int32, (tm, tm), 1)
                   < lax.broadcasted_iota(jnp.int32, (tm, tm), 0))
        rank = jnp.dot(jnp.where(earlier, 1.0, 0.0).astype(BF16), selb,
                       preferred_element_type=F32)
        rank = jnp.where(sel, rank, -1.0)
        rank_scr[...] = rank
        rankt_scr[...] = rank.T
        gatest_scr[...] = gates.T
        o_ref[...] = jnp.zeros_like(o_ref)

    @pl.when(f == 0)
    def _():
        lane = lax.broadcasted_iota(jnp.int32, (tm, LANES), 1)
        cnt = jnp.sum(jnp.where((lane == e) & (rank_scr[...] >= 0.0), 1.0, 0.0))
        nch = (cnt.astype(jnp.int32) + (rc - 1)) // rc
        nch_ref[0] = nch
        rrow = rankt_scr[pl.ds(e, 1), :].astype(jnp.int32)
        grow = gatest_scr[pl.ds(e, 1), :]

        def gather(k, c):
            row = lax.broadcasted_iota(jnp.int32, (rc, tm), 0) + k * rc
            hit = rrow == row
            onehot = jnp.where(hit, 1.0, 0.0).astype(BF16)
            xc_scr[k, 0:rc] = jnp.dot(onehot, h_ref[...], preferred_element_type=F32).astype(BF16)
            gcol = jnp.sum(jnp.where(hit, grow, 0.0), axis=1, keepdims=True)
            gc_scr[k, 0:rc] = jnp.broadcast_to(gcol, (rc, LANES))
            yc_scr[k] = jnp.zeros(yc_scr.shape[1:], F32)
            return c

        lax.fori_loop(0, nch, gather, 0)

    def expert(k, c):
        xa = xc_scr[k, 0:rc, 0:dh]
        xb = xc_scr[k, 0:rc, dh:]
        g = (jnp.dot(xa, wg0_ref[0, 0], preferred_element_type=F32)
             + jnp.dot(xb, wg1_ref[0, 0], preferred_element_type=F32))
        u = (jnp.dot(xa, wu0_ref[0, 0], preferred_element_type=F32)
             + jnp.dot(xb, wu1_ref[0, 0], preferred_element_type=F32))
        a = (g * _sigmoid(g) * u * gc_scr[k, 0:rc, 0:1]).astype(BF16)
        yc_scr[k, 0:rc, 0:dh] += jnp.dot(a, wd0_ref[0, 0, 0], preferred_element_type=F32)
        yc_scr[k, 0:rc, dh:] += jnp.dot(a, wd1_ref[0, 0, 0], preferred_element_type=F32)
        return c

    lax.fori_loop(0, nch_ref[0], expert, 0)

    @pl.when(last_f)
    def _():
        lane = lax.broadcasted_iota(jnp.int32, (tm, LANES), 1)
        rcol = jnp.sum(jnp.where(lane == e, rank_scr[...], 0.0), axis=1,
                       keepdims=True).astype(jnp.int32)

        def scatter(k, c):
            col = lax.broadcasted_iota(jnp.int32, (tm, rcp), 1)
            hit = (rcol == col + k * rc) & (col < rc)
            onehot_t = jnp.where(hit, 1.0, 0.0).astype(BF16)
            o_ref[...] += jnp.dot(onehot_t, yc_scr[k].astype(BF16), preferred_element_type=F32)
            return c

        lax.fori_loop(0, nch_ref[0], scatter, 0)

    @pl.when((e == pl.num_programs(1) - 1) & last_f)
    def _():
        x2 = x_ref[...] + mod_ref[0, 5:6, :] * o_ref[...]
        if final_norm:
            x2 = _rms(x2) * fin_ref[...]
        o_ref[...] = x2


def _tile_major(w, tf):
    E, D, F = w.shape
    return jnp.transpose(w.reshape(E, D, F // tf, tf), (0, 2, 1, 3)).astype(BF16)


def _col_halves(w, tf):
    E, F, D = w.shape
    return jnp.transpose(w.reshape(E, F // tf, tf, 2, D // 2), (0, 1, 3, 2, 4)).astype(BF16)


def _moe(h2, gates, wg, wu, wd, x1, mod_l, fin_g, *, tm, tf, rc, seq, final_norm):
    T, D = h2.shape
    E, NF = wd.shape[0], wd.shape[1]
    tiles_per_batch = seq // tm
    up_half = lambda p: pl.BlockSpec((1, 1, D // 2, tf), lambda i, e, f: (e, f, p, 0))
    down_half = lambda p: pl.BlockSpec((1, 1, 1, tf, D // 2), lambda i, e, f: (e, f, p, 0, 0))
    ncap = -(-tm // rc)
    rcp = -(-rc // LANES) * LANES
    kern = functools.partial(_moe_kernel, tm=tm, tf=tf, rc=rc, final_norm=final_norm)
    return pl.pallas_call(
        kern,
        grid=(T // tm, E, NF),
        in_specs=[
            pl.BlockSpec((tm, D), lambda i, e, f: (i, 0)),
            pl.BlockSpec((tm, LANES), lambda i, e, f: (i, 0)),
            up_half(0), up_half(1), up_half(0), up_half(1), down_half(0), down_half(1),
            pl.BlockSpec((tm, D), lambda i, e, f: (i, 0)),
            pl.BlockSpec((1, 6, D), lambda i, e, f: (i // tiles_per_batch, 0, 0)),
            pl.BlockSpec((1, D), lambda i, e, f: (0, 0)),
        ],
        out_specs=pl.BlockSpec((tm, D), lambda i, e, f: (i, 0)),
        out_shape=jax.ShapeDtypeStruct((T, D), F32),
        scratch_shapes=[
            pltpu.VMEM((tm, LANES), F32),
            pltpu.VMEM((LANES, tm), F32),
            pltpu.VMEM((LANES, tm), F32),
            pltpu.VMEM((ncap, rcp, D), BF16),
            pltpu.VMEM((ncap, rcp, LANES), F32),
            pltpu.VMEM((ncap, rcp, D), F32),
            pltpu.SMEM((1,), jnp.int32),
        ],
        compiler_params=pltpu.CompilerParams(
            dimension_semantics=("arbitrary", "arbitrary", "arbitrary"),
            vmem_limit_bytes=VMEM_LIMIT),
    )(h2, gates, wg, wg, wu, wu, wd, wd, x1, mod_l, fin_g)


GROUP = 4


def _dispatch_kernel(h_ref, gates_ref, xs_ref, gc_ref, rank_ref, rankt_scr, gatest_scr, *, tm, rc):
    e = pl.program_id(1)

    @pl.when(e == 0)
    def _():
        gates = gates_ref[...]
        sel = gates > 0.0
        selb = jnp.where(sel, 1.0, 0.0).astype(BF16)
        earlier = (lax.broadcasted_iota(jnp.int32, (tm, tm), 1)
                   < lax.broadcasted_iota(jnp.int32, (tm, tm), 0))
        rank = jnp.dot(jnp.where(earlier, 1.0, 0.0).astype(BF16), selb,
                       preferred_element_type=F32)
        rank = jnp.where(sel, rank, -1.0)
        rank_ref[...] = rank
        rankt_scr[...] = rank.T
        gatest_scr[...] = gates.T

    rrow = rankt_scr[pl.ds(e, 1), :].astype(jnp.int32)
    grow = gatest_scr[pl.ds(e, 1), :]
    hit = rrow == lax.broadcasted_iota(jnp.int32, (rc, tm), 0)
    onehot = jnp.where(hit, 1.0, 0.0).astype(BF16)
    xs_ref[...] = jnp.zeros_like(xs_ref)
    gc_ref[...] = jnp.zeros_like(gc_ref)
    xs_ref[0, 0, 0, 0:rc] = jnp.dot(onehot, h_ref[...], preferred_element_type=F32).astype(BF16)
    gcol = jnp.sum(jnp.where(hit, grow, 0.0), axis=1, keepdims=True)
    gc_ref[0, 0, 0, 0:rc] = jnp.broadcast_to(gcol, (rc, LANES))


def _experts_kernel(xs_ref, gc_ref, wgu_ref, wd_ref, ys_ref, acc_ref, *, tf, rc):
    f = pl.program_id(2)

    @pl.when(f == 0)
    def _():
        acc_ref[...] = jnp.zeros_like(acc_ref)

    for j in range(GROUP):
        a = _swiglu_tile(xs_ref[0, j, 0, 0:rc], wgu_ref[0, 0], tf) * gc_ref[0, j, 0, 0:rc, 0:1]
        acc_ref[j] += jnp.dot(a.astype(BF16), wd_ref[0], preferred_element_type=F32)

    @pl.when(f == pl.num_programs(2) - 1)
    def _():
        ys_ref[...] = jnp.zeros_like(ys_ref)
        for j in range(GROUP):
            ys_ref[0, j, 0, 0:rc] = acc_ref[j].astype(BF16)


def _combine_kernel(ys_ref, rank_ref, x_ref, mod_ref, fin_ref, o_ref, *, tm, rc, final_norm):
    e = pl.program_id(1)
    rcp = ys_ref.shape[3]

    @pl.when(e == 0)
    def _():
        o_ref[...] = jnp.zeros_like(o_ref)

    lane = lax.broadcasted_iota(jnp.int32, (tm, LANES), 1)
    rcol = jnp.sum(jnp.where(lane == e, rank_ref[...], 0.0), axis=1,
                   keepdims=True).astype(jnp.int32)
    col = lax.broadcasted_iota(jnp.int32, (tm, rcp), 1)
    onehot_t = jnp.where((rcol == col) & (col < rc), 1.0, 0.0).astype(BF16)
    o_ref[...] += jnp.dot(onehot_t, ys_ref[0, 0, 0], preferred_element_type=F32)

    @pl.when(e == pl.num_programs(1) - 1)
    def _():
        x2 = x_ref[...] + mod_ref[0, 5:6, :] * o_ref[...]
        if final_norm:
            x2 = _rms(x2) * fin_ref[...]
        o_ref[...] = x2


def _dispatch(h2, gates, *, tm, rc, rcp):
    T, D = h2.shape
    nt = T // tm
    E = N_EXPERTS
    kern = functools.partial(_dispatch_kernel, tm=tm, rc=rc)
    chunk = lambda i, e: (i // GROUP, i % GROUP, e, 0, 0)
    return pl.pallas_call(
        kern,
        grid=(nt, E),
        in_specs=[
            pl.BlockSpec((tm, D), lambda i, e: (i, 0)),
            pl.BlockSpec((tm, LANES), lambda i, e: (i, 0)),
        ],
        out_specs=[
            pl.BlockSpec((1, 1, 1, rcp, D), chunk),
            pl.BlockSpec((1, 1, 1, rcp, LANES), chunk),
            pl.BlockSpec((tm, LANES), lambda i, e: (i, 0)),
        ],
        out_shape=[
            jax.ShapeDtypeStruct((nt // GROUP, GROUP, E, rcp, D), BF16),
            jax.ShapeDtypeStruct((nt // GROUP, GROUP, E, rcp, LANES), F32),
            jax.ShapeDtypeStruct((T, LANES), F32),
        ],
        scratch_shapes=[pltpu.VMEM((LANES, tm), F32), pltpu.VMEM((LANES, tm), F32)],
        compiler_params=pltpu.CompilerParams(
            dimension_semantics=("arbitrary", "arbitrary"), vmem_limit_bytes=VMEM_LIMIT),
    )(h2, gates)


def _experts(xs, gc, wgu, wd, *, tf, rc):
    ng, _, E, rcp, D = xs.shape
    NF = wgu.shape[1]
    kern = functools.partial(_experts_kernel, tf=tf, rc=rc)
    chunks = lambda e, g, f: (g, 0, e, 0, 0)
    return pl.pallas_call(
        kern,
        grid=(E, ng, NF),
        in_specs=[
            pl.BlockSpec((1, GROUP, 1, rcp, D), chunks),
            pl.BlockSpec((1, GROUP, 1, rcp, LANES), chunks),
            pl.BlockSpec((1, 1, D, 2 * tf), lambda e, g, f: (e, f, 0, 0)),
            pl.BlockSpec((1, tf, D), lambda e, g, f: (e, f, 0)),
        ],
        out_specs=pl.BlockSpec((1, GROUP, 1, rcp, D), chunks),
        out_shape=jax.ShapeDtypeStruct(xs.shape, BF16),
        scratch_shapes=[pltpu.VMEM((GROUP, rc, D), F32)],
        compiler_params=pltpu.CompilerParams(
            dimension_semantics=("arbitrary", "arbitrary", "arbitrary"),
            vmem_limit_bytes=VMEM_LIMIT),
    )(xs, gc, wgu, wd)


def _combine(ys, rank, x1, mod_l, fin_g, *, tm, rc, seq, final_norm):
    T, D = x1.shape
    E, rcp = ys.shape[2], ys.shape[3]
    tiles_per_batch = seq // tm
    kern = functools.partial(_combine_kernel, tm=tm, rc=rc, final_norm=final_norm)
    return pl.pallas_call(
        kern,
        grid=(T // tm, E),
        in_specs=[
            pl.BlockSpec((1, 1, 1, rcp, D), lambda i, e: (i // GROUP, i % GROUP, e, 0, 0)),
            pl.BlockSpec((tm, LANES), lambda i, e: (i, 0)),
            pl.BlockSpec((tm, D), lambda i, e: (i, 0)),
            pl.BlockSpec((1, 6, D), lambda i, e: (i // tiles_per_batch, 0, 0)),
            pl.BlockSpec((1, D), lambda i, e: (0, 0)),
        ],
        out_specs=pl.BlockSpec((tm, D), lambda i, e: (i, 0)),
        out_shape=jax.ShapeDtypeStruct((T, D), F32),
        compiler_params=pltpu.CompilerParams(
            dimension_semantics=("arbitrary", "arbitrary"), vmem_limit_bytes=VMEM_LIMIT),
    )(ys, rank, x1, mod_l, fin_g)


def _moe_layer(h2, gates, w_gate, w_up, w_down, x1, mod_l, fin_g, *, tm, tf, rc, seq, final_norm):
    T = h2.shape[0]
    rcp = -(-rc // LANES) * LANES
    if (T // tm) % GROUP != 0:
        return _moe(h2, gates, _tile_major(w_gate, tf), _tile_major(w_up, tf),
                    _col_halves(w_down, tf), x1, mod_l, fin_g,
                    tm=tm, tf=tf, rc=rc, seq=seq, final_norm=final_norm)
    xs, gc, rank = _dispatch(h2, gates, tm=tm, rc=rc, rcp=rcp)

    def expert_major(_):
        ys = _experts(xs, gc, _fuse_gate_up(w_gate, w_up, tf), w_down.astype(BF16), tf=tf, rc=rc)
        return _combine(ys, rank, x1, mod_l, fin_g, tm=tm, rc=rc, seq=seq, final_norm=final_norm)

    def tile_major(_):
        return _moe(h2, gates, _tile_major(w_gate, tf), _tile_major(w_up, tf),
                    _col_halves(w_down, tf), x1, mod_l, fin_g,
                    tm=tm, tf=tf, rc=rc, seq=seq, final_norm=final_norm)

    return lax.cond(jnp.max(rank) < rc, expert_major, tile_major, 0)


RC = 128
VT = 8


def _route_kernel(gates_ref, rank_ref, rankt_ref, gatest_ref, cnt_ref, *, tm):
    gates = gates_ref[...]
    sel = gates > 0.0
    self32 = jnp.where(sel, 1.0, 0.0)
    earlier = (lax.broadcasted_iota(jnp.int32, (tm, tm), 1)
               < lax.broadcasted_iota(jnp.int32, (tm, tm), 0))
    rank = jnp.dot(jnp.where(earlier, 1.0, 0.0).astype(BF16), self32.astype(BF16),
                   preferred_element_type=F32)
    rank = jnp.where(sel, rank, -1.0)
    rank_ref[...] = rank
    rankt_ref[0] = rank.T
    gatest_ref[0] = gates.T
    cnt_ref[0] = jnp.sum(self32, axis=0, keepdims=True)


def _route(gates, *, tm):
    T = gates.shape[0]
    nt = T // tm
    return pl.pallas_call(
        functools.partial(_route_kernel, tm=tm),
        grid=(nt,),
        in_specs=[pl.BlockSpec((tm, LANES), lambda i: (i, 0))],
        out_specs=[
            pl.BlockSpec((tm, LANES), lambda i: (i, 0)),
            pl.BlockSpec((1, LANES, tm), lambda i: (i, 0, 0)),
            pl.BlockSpec((1, LANES, tm), lambda i: (i, 0, 0)),
            pl.BlockSpec((1, 1, LANES), lambda i: (i, 0, 0)),
        ],
        out_shape=[
            jax.ShapeDtypeStruct((T, LANES), F32),
            jax.ShapeDtypeStruct((nt, LANES, tm), F32),
            jax.ShapeDtypeStruct((nt, LANES, tm), F32),
            jax.ShapeDtypeStruct((nt, 1, LANES), F32),
        ],
        compiler_params=pltpu.CompilerParams(
            dimension_semantics=("arbitrary",), vmem_limit_bytes=VMEM_LIMIT),
    )(gates)


def _chunk_tables(cnt, nt, n_list, n_vt):
    E = N_EXPERTS
    nch = (cnt + (RC - 1)) // RC
    nce = jnp.sum(nch, axis=0)
    padded = (nce + (VT - 1)) // VT * VT
    base = jnp.cumsum(padded) - padded
    cid0 = (base[None, :] + jnp.cumsum(nch, axis=0) - nch).reshape(-1)
    nch_f = nch.reshape(-1)
    cum = jnp.cumsum(nch_f)
    n_real = cum[-1]
    s = jnp.arange(n_list, dtype=jnp.int32)
    g = jnp.minimum(jnp.searchsorted(cum, s, side="right"), nt * E - 1).astype(jnp.int32)
    k = s - (cum[g] - nch_f[g])
    tile = g // E
    tile_first = (cum - nch_f).reshape(nt, E)[:, 0]
    tile_last = cum.reshape(nt, E)[:, E - 1] - 1
    real = s < n_real
    npad = padded - nce
    pcum = jnp.cumsum(npad)
    j = s - n_real
    pe = jnp.minimum(jnp.searchsorted(pcum, j, side="right"), E - 1).astype(jnp.int32)
    pcid = base[pe] + nce[pe] + (j - (pcum[pe] - npad[pe]))
    is_pad = jnp.logical_and(s >= n_real, s < n_real + pcum[-1])
    kind = jnp.where(real, 0, jnp.where(is_pad, 1, 2)).astype(jnp.int32)
    fields = dict(
        tile=jnp.where(real, tile, nt - 1),
        exp=jnp.where(real, g % E, 0),
        k=jnp.where(real, k, 0),
        cid=jnp.where(real, cid0[g] + k, pcid),
        first=jnp.logical_and(real, s == tile_first[tile]),
        last=jnp.logical_and(real, s == tile_last[tile]),
    )
    hold = jnp.minimum(s, n_real + pcum[-1] - 1)
    fields = {name: v[hold].astype(jnp.int32) for name, v in fields.items()}
    vcum = jnp.cumsum(padded // VT)
    nv = vcum[-1]
    v = jnp.minimum(jnp.arange(n_vt, dtype=jnp.int32), nv - 1)
    v_exp = jnp.minimum(jnp.searchsorted(vcum, v, side="right"), E - 1).astype(jnp.int32)
    return fields, kind, v_exp, nv.reshape(1).astype(jnp.int32)


def _scatter_kernel(tile_ref, exp_ref, k_ref, cid_ref, kind_ref,
                    h_ref, rankt_ref, gatest_ref, xs_ref, gc_ref, *, tm):
    s = pl.program_id(0)
    kind = kind_ref[s]

    @pl.when(kind == 0)
    def _():
        e = exp_ref[s]
        rrow = rankt_ref[0, pl.ds(e, 1), :].astype(jnp.int32)
        grow = gatest_ref[0, pl.ds(e, 1), :]
        row = lax.broadcasted_iota(jnp.int32, (RC, tm), 0) + k_ref[s] * RC
        hit = rrow == row
        onehot = jnp.where(hit, 1.0, 0.0).astype(BF16)
        xs_ref[...] = jnp.dot(onehot, h_ref[...], preferred_element_type=F32).astype(BF16)
        gcol = jnp.sum(jnp.where(hit, grow, 0.0), axis=1, keepdims=True)
        gc_ref[...] = jnp.broadcast_to(gcol, (RC, LANES))

    @pl.when(kind == 1)
    def _():
        xs_ref[...] = jnp.zeros_like(xs_ref)
        gc_ref[...] = jnp.zeros_like(gc_ref)


def _scatter_rows(tabs, kind, h2, rankt, gatest, *, tm, n_list, n_rows):
    T, D = h2.shape
    grid_spec = pltpu.PrefetchScalarGridSpec(
        num_scalar_prefetch=5,
        grid=(n_list,),
        in_specs=[
            pl.BlockSpec((tm, D), lambda s, tile, exp, k, cid, kind: (tile[s], 0)),
            pl.BlockSpec((1, LANES, tm), lambda s, tile, exp, k, cid, kind: (tile[s], 0, 0)),
            pl.BlockSpec((1, LANES, tm), lambda s, tile, exp, k, cid, kind: (tile[s], 0, 0)),
        ],
        out_specs=[
            pl.BlockSpec((RC, D), lambda s, tile, exp, k, cid, kind: (cid[s], 0)),
            pl.BlockSpec((RC, LANES), lambda s, tile, exp, k, cid, kind: (cid[s], 0)),
        ],
    )
    return pl.pallas_call(
        functools.partial(_scatter_kernel, tm=tm),
        grid_spec=grid_spec,
        out_shape=[jax.ShapeDtypeStruct((n_rows, D), BF16),
                   jax.ShapeDtypeStruct((n_rows, LANES), F32)],
        compiler_params=pltpu.CompilerParams(
            dimension_semantics=("arbitrary",), vmem_limit_bytes=VMEM_LIMIT),
    )(tabs["tile"], tabs["exp"], tabs["k"], tabs["cid"], kind, h2, rankt, gatest)


def _sorted_experts_kernel(vexp_ref, nv_ref, xs_ref, gc_ref, wgu_ref, wd_ref, ys_ref, acc_ref, *, tf):
    v = pl.program_id(0)
    f = pl.program_id(1)

    @pl.when(v < nv_ref[0])
    def _():
        @pl.when(f == 0)
        def _():
            acc_ref[...] = jnp.zeros_like(acc_ref)

        a = _swiglu_tile(xs_ref[...], wgu_ref[0, 0], tf) * gc_ref[:, 0:1]
        acc_ref[...] += jnp.dot(a.astype(BF16), wd_ref[0], preferred_element_type=F32)

        @pl.when(f == pl.num_programs(1) - 1)
        def _():
            ys_ref[...] = acc_ref[...].astype(BF16)


def _sorted_experts(v_exp, nv, xs, gc, wgu, wd, *, tf, n_vt):
    D = xs.shape[1]
    NF = wgu.shape[1]
    rows = VT * RC

    def live(v, nv):
        return jnp.minimum(v, nv[0] - 1)

    def hidden(v, f, nv):
        return jnp.where(v < nv[0], f, NF - 1)

    grid_spec = pltpu.PrefetchScalarGridSpec(
        num_scalar_prefetch=2,
        grid=(n_vt, NF),
        in_specs=[
            pl.BlockSpec((rows, D), lambda v, f, vexp, nv: (live(v, nv), 0)),
            pl.BlockSpec((rows, LANES), lambda v, f, vexp, nv: (live(v, nv), 0)),
            pl.BlockSpec((1, 1, D, 2 * tf), lambda v, f, vexp, nv: (vexp[v], hidden(v, f, nv), 0, 0)),
            pl.BlockSpec((1, tf, D), lambda v, f, vexp, nv: (vexp[v], hidden(v, f, nv), 0)),
        ],
        out_specs=pl.BlockSpec((rows, D), lambda v, f, vexp, nv: (live(v, nv), 0)),
        scratch_shapes=[pltpu.VMEM((rows, D), F32)],
    )
    return pl.pallas_call(
        functools.partial(_sorted_experts_kernel, tf=tf),
        grid_spec=grid_spec,
        out_shape=jax.ShapeDtypeStruct(xs.shape, BF16),
        compiler_params=pltpu.CompilerParams(
            dimension_semantics=("arbitrary", "arbitrary"), vmem_limit_bytes=VMEM_LIMIT),
    )(v_exp, nv, xs, gc, wgu, wd)


def _gather_kernel(tile_ref, exp_ref, k_ref, cid_ref, kind_ref, first_ref, last_ref,
                   ys_ref, rank_ref, x_ref, mod_ref, fin_ref, o_ref, *, tm, final_norm):
    s = pl.program_id(0)

    @pl.when(kind_ref[s] == 0)
    def _():
        @pl.when(first_ref[s] == 1)
        def _():
            o_ref[...] = jnp.zeros_like(o_ref)

        lane = lax.broadcasted_iota(jnp.int32, (tm, LANES), 1)
        rcol = jnp.sum(jnp.where(lane == exp_ref[s], rank_ref[...], 0.0), axis=1,
                       keepdims=True).astype(jnp.int32)
        col = lax.broadcasted_iota(jnp.int32, (tm, RC), 1) + k_ref[s] * RC
        onehot_t = jnp.where(rcol == col, 1.0, 0.0).astype(BF16)
        o_ref[...] += jnp.dot(onehot_t, ys_ref[...], preferred_element_type=F32)

        @pl.when(last_ref[s] == 1)
        def _():
            x2 = x_ref[...] + mod_ref[0, 5:6, :] * o_ref[...]
            if final_norm:
                x2 = _rms(x2) * fin_ref[...]
            o_ref[...] = x2


def _gather_rows(tabs, kind, ys, rank, x1, mod_l, fin_g, *, tm, seq, n_list, final_norm):
    T, D = x1.shape
    tpb = seq // tm
    ix = lambda fn: (lambda s, tile, exp, k, cid, kind, first, last: fn(s, tile, cid))
    grid_spec = pltpu.PrefetchScalarGridSpec(
        num_scalar_prefetch=7,
        grid=(n_list,),
        in_specs=[
            pl.BlockSpec((RC, D), ix(lambda s, tile, cid: (cid[s], 0))),
            pl.BlockSpec((tm, LANES), ix(lambda s, tile, cid: (tile[s], 0))),
            pl.BlockSpec((tm, D), ix(lambda s, tile, cid: (tile[s], 0))),
            pl.BlockSpec((1, 6, D), ix(lambda s, tile, cid: (tile[s] // tpb, 0, 0))),
            pl.BlockSpec((1, D), ix(lambda s, tile, cid: (0, 0))),
        ],
        out_specs=pl.BlockSpec((tm, D), ix(lambda s, tile, cid: (tile[s], 0))),
    )
    return pl.pallas_call(
        functools.partial(_gather_kernel, tm=tm, final_norm=final_norm),
        grid_spec=grid_spec,
        out_shape=jax.ShapeDtypeStruct((T, D), F32),
        compiler_params=pltpu.CompilerParams(
            dimension_semantics=("arbitrary",), vmem_limit_bytes=VMEM_LIMIT),
    )(tabs["tile"], tabs["exp"], tabs["k"], tabs["cid"], kind, tabs["first"], tabs["last"],
      ys, rank, x1, mod_l, fin_g)


def _routed_experts(h2, gates, w_gate, w_up, w_down, x1, mod_l, fin_g, *, tm, tf, seq, final_norm):
    T, D = h2.shape
    E = N_EXPERTS
    nt = T // tm
    n_real_max = 2 * T // RC + nt * E
    n_vt = -(-(n_real_max + E * (VT - 1)) // VT)
    n_list = n_vt * VT
    rank, rankt, gatest, cnt = _route(gates, tm=tm)
    cnt = cnt[:, 0, :E].astype(jnp.int32)
    tabs, kind, v_exp, nv = _chunk_tables(cnt, nt, n_list, n_vt)
    xs, gc = _scatter_rows(tabs, kind, h2, rankt, gatest, tm=tm, n_list=n_list, n_rows=n_list * RC)
    ys = _sorted_experts(v_exp, nv, xs, gc, _fuse_gate_up(w_gate, w_up, tf), w_down.astype(BF16),
                         tf=tf, n_vt=n_vt)
    return _gather_rows(tabs, kind, ys, rank, x1, mod_l, fin_g, tm=tm, seq=seq, n_list=n_list,
                        final_norm=final_norm)


def _tiles(S):
    ts = min(512, S)
    tq = min(512, S)
    tk = ts
    tm = min(1024, S)
    tf = 896
    rc = tm // 4 + 64
    assert rc % 16 == 0
    return ts, tq, tk, tm, tf, rc


def kernel(x, c, positions, ada_w, ada_b, attn_norm_g, w_in, q_norm_g, w_uq, kv_norm_g, w_ukv,
           fox_forget_b, mla_out_g, fox_out_g, w_o, ffn_norm_g, dense_w_gate, dense_w_up,
           dense_w_down, router_w, moe_w_gate, moe_w_up, moe_w_down, final_norm_g):
    B, S, D = x.shape
    L = ada_w.shape[0]
    ts, tq, tk, tm, tf, rc = _tiles(S)

    mod = _adaln(c, ada_w, ada_b).reshape(L, B, 6, D)
    ctab, stab = _rope_tables(positions)

    idx_in = _perm_w_in()
    idx_qa, idx_qb = _perm_w_uq()
    idx_kv = _perm_w_ukv()
    col_scale = np.ones((C_END,), np.float32)
    col_scale[C_FQ0:C_FQ0 + FOX_W] = FOX_HEAD_DIM ** -0.5
    tri = jnp.asarray(np.tril(np.ones((ts, ts), np.float32)), BF16)
    eqk = jnp.asarray(_aug_placement(), BF16)

    for l in range(L):
        win = (_take_cols(w_in[l], idx_in) * col_scale).astype(BF16)
        wuqa = _take_cols(w_uq[l], idx_qa).astype(BF16)
        wuqb = _take_cols(w_uq[l], idx_qb).astype(BF16)
        wukv = _take_cols(w_ukv[l], idx_kv).astype(BF16)
        fb_row = jnp.zeros((1, LANES), F32).at[0, CTRL_LOGIT:CTRL_LOGIT + FOX_HEADS].set(
            fox_forget_b[l].astype(F32))
        qa, ka, vt = _inproj(
            x, mod[l], attn_norm_g[l].reshape(1, D), win, q_norm_g[l].reshape(1, -1),
            kv_norm_g[l].reshape(1, -1), wuqa, wuqb, wukv, fb_row, ctab, stab, tri, eqk, ts=ts)
        ot = _attention(qa, ka, vt, tq=tq, tk=tk, hp=4)

        j = l // 2
        is_moe = (l % 2 == 1)
        rw = None
        if is_moe:
            rw = jnp.pad(router_w[j], ((0, 0), (0, LANES - N_EXPERTS))).astype(BF16)
        outs = _outproj(ot, x, mod[l], mla_out_g[l].reshape(1, -1), fox_out_g[l].reshape(1, -1),
                        w_o[l].astype(BF16), ffn_norm_g[l].reshape(1, D), rw, ts=ts)
        x1, h2 = outs[0].reshape(B * S, D), outs[1].reshape(B * S, D)
        fin_g = final_norm_g.reshape(1, D)
        last = l == L - 1
        if is_moe:
            x = _routed_experts(h2, outs[2].reshape(B * S, LANES), moe_w_gate[j], moe_w_up[j],
                                moe_w_down[j], x1, mod[l], fin_g,
                                tm=tm, tf=tf, seq=S, final_norm=last)
        else:
            x = _ffn(h2, _fuse_gate_up(dense_w_gate[j][None], dense_w_up[j][None], tf),
                     dense_w_down[j].astype(BF16), x1, mod[l], fin_g,
                     tm=tm, tf=tf, seq=S, final_norm=last)
        x = x.reshape(B, S, D)
    return x
```

```python
import functools

import numpy as np
import jax
import jax.numpy as jnp
from jax import lax
from jax.experimental import pallas as pl
from jax.experimental.pallas import tpu as pltpu

F32 = jnp.float32
BF16 = jnp.bfloat16

D_MODEL = 1024
MLA_HEADS = 8
MLA_V = 64
MLA_NOPE = 64
MLA_ROPE = 32
MLA_Q_LORA = 256
MLA_KV_LORA = 128
FOX_HEADS = 8
FOX_HEAD_DIM = 64
FOX_W = FOX_HEADS * FOX_HEAD_DIM
ROPE_THETA = 10000.0
D_FF = 3584
N_EXPERTS = 8
EPS = 1e-6
N_HEADS = MLA_HEADS + FOX_HEADS
HALF_ROPE = MLA_ROPE // 2

LANES = 128
VMEM_LIMIT = 56 * 1024 * 1024

ROPE_LO = MLA_NOPE
AUG_LO = FOX_HEAD_DIM
CTRL_LOGIT = 96
CTRL_ONE = 120

C_Q0 = 0
C_KV0 = C_Q0 + MLA_Q_LORA
C_KRA0 = C_KV0 + MLA_KV_LORA
C_KRB0 = C_KRA0 + LANES
C_FQ0 = C_KRB0 + LANES
C_FK0 = C_FQ0 + FOX_W
C_FV0 = C_FK0 + FOX_W
C_END = C_FV0 + FOX_W

NEG_BIG = -1e30
LOG2E = 1.4426950408889634
ONES_ROWS = 16


def _rms(v):
    return v * lax.rsqrt(jnp.mean(v * v, axis=-1, keepdims=True) + EPS)


def _sigmoid(v):
    return 1.0 / (1.0 + jnp.exp(-v))


def _split3(v):
    hi = v.astype(BF16).astype(F32)
    r = v - hi
    mid = r.astype(BF16).astype(F32)
    lo = (r - mid).astype(BF16).astype(F32)
    return hi, mid, lo


def _adaln_kernel(c_ref, w_ref, b_ref, o_ref):
    c = c_ref[...]
    ca = (c * _sigmoid(c)).astype(BF16)
    o_ref[0] = jnp.dot(ca, w_ref[0].astype(BF16), preferred_element_type=F32) + b_ref[0]


def _adaln(c, ada_w, ada_b):
    L, D, N = ada_w.shape
    B = c.shape[0]
    tn = 1536
    return pl.pallas_call(
        _adaln_kernel,
        grid=(L, N // tn),
        in_specs=[
            pl.BlockSpec((B, D), lambda l, j: (0, 0)),
            pl.BlockSpec((1, D, tn), lambda l, j: (l, 0, j)),
            pl.BlockSpec((1, 1, tn), lambda l, j: (l, 0, j)),
        ],
        out_specs=pl.BlockSpec((1, B, tn), lambda l, j: (l, 0, j)),
        out_shape=jax.ShapeDtypeStruct((L, B, N), F32),
        compiler_params=pltpu.CompilerParams(
            dimension_semantics=("arbitrary", "arbitrary"), vmem_limit_bytes=VMEM_LIMIT),
    )(c, ada_w, ada_b.reshape(L, 1, N))


def _rope_tab_kernel(pos_ref, freq_ref, cos_ref, sin_ref):
    ang = freq_ref[...] * pos_ref[0]
    cos_ref[0] = jnp.cos(ang)
    sin_ref[0] = jnp.sin(ang)


def _rope_tables(positions):
    B, S = positions.shape
    half = HALF_ROPE
    inv_freq = ROPE_THETA ** (-jnp.arange(half, dtype=F32) / half)
    pos = positions.astype(F32).reshape(B, 1, S)
    cos_t, sin_t = pl.pallas_call(
        _rope_tab_kernel,
        grid=(B,),
        in_specs=[
            pl.BlockSpec((1, 1, S), lambda b: (b, 0, 0)),
            pl.BlockSpec((half, 1), lambda b: (0, 0)),
        ],
        out_specs=[pl.BlockSpec((1, half, S), lambda b: (b, 0, 0))] * 2,
        out_shape=[jax.ShapeDtypeStruct((B, half, S), F32)] * 2,
        compiler_params=pltpu.CompilerParams(dimension_semantics=("arbitrary",)),
    )(pos, inv_freq.reshape(half, 1))
    cos = jnp.transpose(cos_t, (0, 2, 1))
    sin = jnp.transpose(sin_t, (0, 2, 1))
    ones = jnp.ones((B, S, MLA_NOPE), F32)
    z_lo = jnp.zeros((B, S, MLA_NOPE), F32)
    z_hi = jnp.zeros((B, S, LANES - MLA_NOPE - MLA_ROPE), F32)
    ctab = jnp.concatenate([ones, cos, cos, z_hi], axis=-1)
    stab = jnp.concatenate([z_lo, -sin, sin, z_hi], axis=-1)
    return ctab, stab


def _take_cols(w, idx):
    pieces, i, n = [], 0, len(idx)
    while i < n:
        j = i + 1
        if idx[i] < 0:
            while j < n and idx[j] < 0:
                j += 1
            pieces.append(jnp.zeros((w.shape[0], j - i), w.dtype))
        else:
            while j < n and idx[j] == idx[j - 1] + 1:
                j += 1
            pieces.append(w[:, int(idx[i]):int(idx[i]) + (j - i)])
        i = j
    return jnp.concatenate(pieces, axis=1)


def _perm_w_in():
    cq = 0
    ckv = cq + MLA_Q_LORA
    kr = ckv + MLA_KV_LORA
    fq = kr + MLA_ROPE
    fk = fq + FOX_W
    fv = fk + FOX_W
    fl = fv + FOX_W
    idx = -np.ones((C_END,), np.int64)
    idx[C_Q0:C_Q0 + MLA_Q_LORA] = cq + np.arange(MLA_Q_LORA)
    idx[C_KV0:C_KV0 + MLA_KV_LORA] = ckv + np.arange(MLA_KV_LORA)
    h = HALF_ROPE
    idx[C_KRA0 + ROPE_LO:C_KRA0 + ROPE_LO + MLA_ROPE] = kr + np.arange(MLA_ROPE)
    idx[C_KRA0 + CTRL_LOGIT:C_KRA0 + CTRL_LOGIT + FOX_HEADS] = fl + np.arange(FOX_HEADS)
    idx[C_KRB0 + ROPE_LO:C_KRB0 + ROPE_LO + h] = kr + h + np.arange(h)
    idx[C_KRB0 + ROPE_LO + h:C_KRB0 + ROPE_LO + MLA_ROPE] = kr + np.arange(h)
    idx[C_FQ0:C_FQ0 + FOX_W] = fq + np.arange(FOX_W)
    idx[C_FK0:C_FK0 + FOX_W] = fk + np.arange(FOX_W)
    idx[C_FV0:C_FV0 + FOX_W] = fv + np.arange(FOX_W)
    return idx


def _perm_w_uq():
    per = MLA_NOPE + MLA_ROPE
    ia = -np.ones((MLA_HEADS * LANES,), np.int64)
    ib = -np.ones((MLA_HEADS * LANES,), np.int64)
    h2 = HALF_ROPE
    for h in range(MLA_HEADS):
        ia[h * LANES:h * LANES + per] = h * per + np.arange(per)
        ib[h * LANES + ROPE_LO:h * LANES + ROPE_LO + h2] = h * per + MLA_NOPE + h2 + np.arange(h2)
        ib[h * LANES + ROPE_LO + h2:h * LANES + ROPE_LO + MLA_ROPE] = h * per + MLA_NOPE + np.arange(h2)
    return ia, ib


def _perm_w_ukv():
    per = MLA_NOPE + MLA_V
    ik = -np.ones((MLA_HEADS * LANES,), np.int64)
    iv = np.zeros((MLA_HEADS * MLA_V,), np.int64)
    for h in range(MLA_HEADS):
        ik[h * LANES:h * LANES + MLA_NOPE] = h * per + np.arange(MLA_NOPE)
        iv[h * MLA_V:(h + 1) * MLA_V] = h * per + MLA_NOPE + np.arange(MLA_V)
    return np.concatenate([ik, iv])


def _aug_placement():
    e = np.zeros((LANES, 2 * FOX_HEADS * LANES), np.float32)
    koff = FOX_HEADS * LANES
    for h in range(FOX_HEADS):
        for p in range(3):
            src = CTRL_LOGIT + 8 * p + h
            e[src, h * LANES + AUG_LO + p] = 1.0
            e[CTRL_ONE, h * LANES + AUG_LO + 3 + p] = -1.0
            e[CTRL_ONE, koff + h * LANES + AUG_LO + p] = 1.0
            e[src, koff + h * LANES + AUG_LO + 3 + p] = 1.0
    return e


def _inproj_kernel(x_ref, mod_ref, g_ref, win_ref, qg_ref, kvg_ref, wuqa_ref, wuqb_ref, wukv_ref,
                   fb_ref, ct_ref, st_ref, tri_ref, eqk_ref,
                   qa_ref, ka_ref, vt_ref, carry_ref, *, ts):
    si = pl.program_id(1)

    @pl.when(si == 0)
    def _():
        carry_ref[...] = jnp.zeros_like(carry_ref)

    x = x_ref[0]
    shift = mod_ref[0, 0:1, :]
    scale = mod_ref[0, 1:2, :]
    h = ((_rms(x) * g_ref[...]) * (1.0 + scale) + shift).astype(BF16)
    proj = jnp.dot(h, win_ref[...], preferred_element_type=F32)

    ctab = ct_ref[0]
    stab = st_ref[0]
    lane = lax.broadcasted_iota(jnp.int32, (ts, LANES), 1)

    c_q = proj[:, C_Q0:C_Q0 + MLA_Q_LORA]
    cqn = (_rms(c_q) * qg_ref[...]).astype(BF16)
    qa = jnp.dot(cqn, wuqa_ref[...], preferred_element_type=F32)
    qb = jnp.dot(cqn, wuqb_ref[...], preferred_element_type=F32)
    mla_scale = (MLA_NOPE + MLA_ROPE) ** -0.5 * LOG2E
    for hh in range(MLA_HEADS):
        sl = slice(hh * LANES, (hh + 1) * LANES)
        qa_ref[0, hh] = ((qa[:, sl] * ctab + qb[:, sl] * stab) * mla_scale).astype(BF16)

    c_kv = proj[:, C_KV0:C_KV0 + MLA_KV_LORA]
    ckvn = (_rms(c_kv) * kvg_ref[...]).astype(BF16)
    kv = jnp.dot(ckvn, wukv_ref[...], preferred_element_type=F32)
    kra = proj[:, C_KRA0:C_KRA0 + LANES]
    krb = proj[:, C_KRB0:C_KRB0 + LANES]
    krope = kra * ctab + krb * stab
    for hh in range(MLA_HEADS):
        ka_ref[0, hh] = (kv[:, hh * LANES:(hh + 1) * LANES] + krope).astype(BF16)
    v_mla = kv[:, MLA_HEADS * LANES:]

    ctrl = (lane >= CTRL_LOGIT) & (lane < CTRL_LOGIT + FOX_HEADS)
    fl = kra + fb_ref[...]
    lsig = jnp.minimum(fl, 0.0) - jnp.log1p(jnp.exp(-jnp.abs(fl)))
    lf = jnp.where(ctrl, lsig, 0.0)
    hi, mid, lo = _split3(lf)
    p1 = (hi + pltpu.roll(mid, 8, 1) + pltpu.roll(lo, 16, 1)).astype(BF16)
    cs = jnp.dot(tri_ref[...], p1, preferred_element_type=F32)
    cs = cs + pltpu.roll(cs, LANES - 8, 1) + pltpu.roll(cs, LANES - 16, 1)
    fcum = jnp.where(ctrl, cs, 0.0) + carry_ref[...]
    carry_ref[...] = fcum[ts - 1:ts, :]

    hi, mid, lo = _split3(fcum * LOG2E)
    p2 = hi + pltpu.roll(mid, 8, 1) + pltpu.roll(lo, 16, 1)
    p2 = jnp.where(lane == CTRL_ONE, 1.0, p2).astype(BF16)
    aug = jnp.dot(p2, eqk_ref[...], preferred_element_type=F32)

    fq = proj[:, C_FQ0:C_FQ0 + FOX_W]
    fk = proj[:, C_FK0:C_FK0 + FOX_W]
    low = lane < FOX_HEAD_DIM
    koff = FOX_HEADS * LANES
    for j in range(FOX_HEADS // 2):
        sl = slice(j * LANES, (j + 1) * LANES)
        for src, off, dst in ((fq, 0, qa_ref), (fk, koff, ka_ref)):
            blk = src[:, sl] * LOG2E if dst is qa_ref else src[:, sl]
            blk_r = pltpu.roll(blk, FOX_HEAD_DIM, 1)
            h0 = 2 * j
            a0 = aug[:, off + h0 * LANES:off + (h0 + 1) * LANES]
            a1 = aug[:, off + (h0 + 1) * LANES:off + (h0 + 2) * LANES]
            dst[0, MLA_HEADS + h0] = jnp.where(low, blk, a0).astype(BF16)
            dst[0, MLA_HEADS + h0 + 1] = jnp.where(low, blk_r, a1).astype(BF16)

    fv = proj[:, C_FV0:C_FV0 + FOX_W]
    vt_ref[0, 0, 0:MLA_HEADS * MLA_V, :] = v_mla.T.astype(BF16)
    vt_ref[0, 0, MLA_HEADS * MLA_V:, :] = fv.T.astype(BF16)


def _inproj(x, mod_l, g, win, qg, kvg, wuqa, wuqb, wukv, fb_row, ctab, stab, tri, eqk, *, ts):
    B, S, D = x.shape
    nst = S // ts
    const2 = lambda b, s: (0, 0)
    kern = functools.partial(_inproj_kernel, ts=ts)
    return pl.pallas_call(
        kern,
        grid=(B, nst),
        in_specs=[
            pl.BlockSpec((1, ts, D), lambda b, s: (b, s, 0)),
            pl.BlockSpec((1, 6, D), lambda b, s: (b, 0, 0)),
            pl.BlockSpec((1, D), const2),
            pl.BlockSpec(win.shape, const2),
            pl.BlockSpec(qg.shape, const2),
            pl.BlockSpec(kvg.shape, const2),
            pl.BlockSpec(wuqa.shape, const2),
            pl.BlockSpec(wuqb.shape, const2),
            pl.BlockSpec(wukv.shape, const2),
            pl.BlockSpec(fb_row.shape, const2),
            pl.BlockSpec((1, ts, LANES), lambda b, s: (b, s, 0)),
            pl.BlockSpec((1, ts, LANES), lambda b, s: (b, s, 0)),
            pl.BlockSpec(tri.shape, const2),
            pl.BlockSpec(eqk.shape, const2),
        ],
        out_specs=[
            pl.BlockSpec((1, N_HEADS, ts, LANES), lambda b, s: (b, 0, s, 0)),
            pl.BlockSpec((1, N_HEADS, ts, LANES), lambda b, s: (b, 0, s, 0)),
            pl.BlockSpec((1, 1, N_HEADS * MLA_V, ts), lambda b, s: (b, s, 0, 0)),
        ],
        out_shape=[
            jax.ShapeDtypeStruct((B, N_HEADS, S, LANES), BF16),
            jax.ShapeDtypeStruct((B, N_HEADS, S, LANES), BF16),
            jax.ShapeDtypeStruct((B, nst, N_HEADS * MLA_V, ts), BF16),
        ],
        scratch_shapes=[pltpu.VMEM((1, LANES), F32)],
        compiler_params=pltpu.CompilerParams(
            dimension_semantics=("arbitrary", "arbitrary"), vmem_limit_bytes=VMEM_LIMIT),
    )(x, mod_l, g, win, qg, kvg, wuqa, wuqb, wukv, fb_row, ctab, stab, tri, eqk)


def _colmax8(st, groups):
    tk, tq = st.shape
    v = st.reshape(groups, tk // (8 * groups), 8, tq)
    return jnp.max(jnp.max(v, axis=1), axis=0)


def _attn_kernel(q_ref, k_ref, vt_ref, o_ref, s_scr, tmax_scr, m_scr, acc_scr,
                 *, tq, tk, hp, nq):
    dv = MLA_V
    groups = 4
    ones = jnp.ones((ONES_ROWS, tk), BF16)

    def qk(qi, j):
        q0 = pl.multiple_of(qi * tq, tq)
        return [lax.dot_general(k_ref[0, h, pl.ds(pl.multiple_of(j * tk, tk), tk), :],
                                q_ref[0, h, pl.ds(q0, tq), :], (((1,), (1,)), ((), ())),
                                preferred_element_type=F32) for h in range(hp)]

    def park(sts, qi, j, masked):
        for h in range(hp):
            st = sts[h]
            if masked:
                kpos = j * tk + lax.broadcasted_iota(jnp.int32, (tk, tq), 0)
                qpos = qi * tq + lax.broadcasted_iota(jnp.int32, (tk, tq), 1)
                st = jnp.where(kpos <= qpos, st, NEG_BIG)
            s_scr[h] = st
            tmax_scr[h] = _colmax8(st, groups)

    def update(j):
        for h in range(hp):
            m = m_scr[h]
            m_new = jnp.maximum(m, jnp.max(tmax_scr[h], axis=0, keepdims=True))
            alpha = jnp.exp2(m - m_new)
            p = jnp.exp2((s_scr[h] - m_new).astype(BF16))
            vt1 = jnp.concatenate([vt_ref[0, j, h * dv:(h + 1) * dv, :], ones], axis=0)
            acc_scr[h] = alpha * acc_scr[h] + jnp.dot(vt1, p, preferred_element_type=F32)
            m_scr[h] = m_new

    def reset_state():
        m_scr[...] = jnp.full(m_scr.shape, NEG_BIG, F32)
        acc_scr[...] = jnp.zeros(acc_scr.shape, F32)

    def finish(qi):
        q0 = pl.multiple_of(qi * tq, tq)
        for h in range(hp):
            acc = acc_scr[h]
            o_ref[0, h * dv:(h + 1) * dv, pl.ds(q0, tq)] = (
                acc[0:dv, :] / acc[dv:dv + 1, :]).astype(o_ref.dtype)
        reset_state()

    reset_state()
    park(qk(0, 0), 0, 0, True)

    def q_tile(qi, c):
        nf = (qi * tq) // tk

        def below_diagonal(t, c2):
            sts = qk(qi, t + 1)
            update(t)
            park(sts, qi, t + 1, False)
            return c2

        lax.fori_loop(0, nf - 1, below_diagonal, 0)

        @pl.when(nf > 0)
        def _():
            sts = qk(qi, nf)
            update(nf - 1)
            park(sts, qi, nf, True)

        nxt = qi + 1
        nxt_on_diagonal = (nxt * tq) // tk == 0

        @pl.when((nxt < nq) & nxt_on_diagonal)
        def _():
            sts = qk(nxt, 0)
            update(nf)
            finish(qi)
            park(sts, nxt, 0, True)

        @pl.when((nxt < nq) & jnp.logical_not(nxt_on_diagonal))
        def _():
            sts = qk(nxt, 0)
            update(nf)
            finish(qi)
            park(sts, nxt, 0, False)

        @pl.when(nxt == nq)
        def _():
            update(nf)
            finish(qi)

        return c

    lax.fori_loop(0, nq, q_tile, 0)


def _attention(qa, ka, vt, *, tq, tk, hp):
    B, H, S, _ = qa.shape
    nkt = vt.shape[1]
    assert S % tq == 0 and nkt * tk == S and H % hp == 0 and tk % tq == 0
    kern = functools.partial(_attn_kernel, tq=tq, tk=tk, hp=hp, nq=S // tq)
    return pl.pallas_call(
        kern,
        grid=(B, H // hp),
        in_specs=[
            pl.BlockSpec((1, hp, S, LANES), lambda b, g: (b, g, 0, 0)),
            pl.BlockSpec((1, hp, S, LANES), lambda b, g: (b, g, 0, 0)),
            pl.BlockSpec((1, nkt, hp * MLA_V, tk), lambda b, g: (b, 0, g, 0)),
        ],
        out_specs=pl.BlockSpec((1, hp * MLA_V, S), lambda b, g: (b, g, 0)),
        out_shape=jax.ShapeDtypeStruct((B, H * MLA_V, S), BF16),
        scratch_shapes=[
            pltpu.VMEM((hp, tk, tq), F32),
            pltpu.VMEM((hp, 8, tq), F32),
            pltpu.VMEM((hp, 1, tq), F32),
            pltpu.VMEM((hp, MLA_V + ONES_ROWS, tq), F32),
        ],
        compiler_params=pltpu.CompilerParams(
            dimension_semantics=("arbitrary", "arbitrary"), vmem_limit_bytes=VMEM_LIMIT),
    )(qa, ka, vt)


def _top2_gates(logits, lane):
    lg = jnp.where(lane < N_EXPERTS, logits, NEG_BIG)
    m1 = jnp.max(lg, axis=1, keepdims=True)
    i1 = jnp.min(jnp.where(lg == m1, lane, LANES), axis=1, keepdims=True)
    lg2 = jnp.where(lane == i1, NEG_BIG, lg)
    m2 = jnp.max(lg2, axis=1, keepdims=True)
    i2 = jnp.min(jnp.where(lg2 == m2, lane, LANES), axis=1, keepdims=True)
    e2 = jnp.exp(m2 - m1)
    den = 1.0 + e2
    return jnp.where(lane == i1, 1.0 / den, 0.0) + jnp.where(lane == i2, e2 / den, 0.0)


def _outproj_kernel(*refs, ts, with_router):
    if with_router:
        (ot_ref, x_ref, mod_ref, mg_ref, fg_ref, wo_ref, ng_ref, rw_ref,
         x1_ref, h2_ref, gates_ref) = refs
    else:
        ot_ref, x_ref, mod_ref, mg_ref, fg_ref, wo_ref, ng_ref, x1_ref, h2_ref = refs
    half = MLA_HEADS * MLA_V
    om = ot_ref[0, 0:half, :].astype(F32).T
    of = ot_ref[0, half:, :].astype(F32).T
    on = jnp.concatenate([_rms(om) * mg_ref[...], _rms(of) * fg_ref[...]], axis=1).astype(BF16)
    mix = jnp.dot(on, wo_ref[...], preferred_element_type=F32)
    x1 = x_ref[0] + mod_ref[0, 2:3, :] * mix
    x1_ref[0] = x1
    h2 = ((_rms(x1) * ng_ref[...]) * (1.0 + mod_ref[0, 4:5, :]) + mod_ref[0, 3:4, :]).astype(BF16)
    h2_ref[0] = h2
    if with_router:
        logits = jnp.dot(h2, rw_ref[...], preferred_element_type=F32)
        lane = lax.broadcasted_iota(jnp.int32, (ts, LANES), 1)
        gates_ref[0] = _top2_gates(logits, lane)


def _outproj(ot, x, mod_l, mg, fg, wo, ng, rw, *, ts):
    B, S, D = x.shape
    with_router = rw is not None
    const2 = lambda b, s: (0, 0)
    in_specs = [
        pl.BlockSpec((1, ot.shape[1], ts), lambda b, s: (b, 0, s)),
        pl.BlockSpec((1, ts, D), lambda b, s: (b, s, 0)),
        pl.BlockSpec((1, 6, D), lambda b, s: (b, 0, 0)),
        pl.BlockSpec(mg.shape, const2),
        pl.BlockSpec(fg.shape, const2),
        pl.BlockSpec(wo.shape, const2),
        pl.BlockSpec(ng.shape, const2),
    ]
    args = [ot, x, mod_l, mg, fg, wo, ng]
    out_specs = [pl.BlockSpec((1, ts, D), lambda b, s: (b, s, 0)),
                 pl.BlockSpec((1, ts, D), lambda b, s: (b, s, 0))]
    out_shape = [jax.ShapeDtypeStruct((B, S, D), F32), jax.ShapeDtypeStruct((B, S, D), BF16)]
    if with_router:
        in_specs.append(pl.BlockSpec(rw.shape, const2))
        args.append(rw)
        out_specs.append(pl.BlockSpec((1, ts, LANES), lambda b, s: (b, s, 0)))
        out_shape.append(jax.ShapeDtypeStruct((B, S, LANES), F32))
    kern = functools.partial(_outproj_kernel, ts=ts, with_router=with_router)
    return pl.pallas_call(
        kern,
        grid=(B, S // ts),
        in_specs=in_specs,
        out_specs=out_specs,
        out_shape=out_shape,
        compiler_params=pltpu.CompilerParams(
            dimension_semantics=("arbitrary", "arbitrary"), vmem_limit_bytes=VMEM_LIMIT),
    )(*args)


def _fuse_gate_up(wg, wu, tf):
    E, D, F = wg.shape
    w = jnp.concatenate([wg.reshape(E, D, F // tf, tf), wu.reshape(E, D, F // tf, tf)], axis=-1)
    return jnp.transpose(w, (0, 2, 1, 3)).astype(BF16)


def _swiglu_tile(x, wgu, tf):
    gu = jnp.dot(x, wgu, preferred_element_type=F32)
    g = gu[:, :tf]
    return g * _sigmoid(g) * gu[:, tf:]


def _ffn_kernel(h_ref, wgu_ref, wd_ref, x_ref, mod_ref, fin_ref, o_ref, acc_ref,
                *, tf, final_norm):
    f = pl.program_id(1)

    @pl.when(f == 0)
    def _():
        acc_ref[...] = jnp.zeros_like(acc_ref)

    a = _swiglu_tile(h_ref[...], wgu_ref[0, 0], tf)
    acc_ref[...] += jnp.dot(a.astype(BF16), wd_ref[...], preferred_element_type=F32)

    @pl.when(f == pl.num_programs(1) - 1)
    def _():
        x2 = x_ref[...] + mod_ref[0, 5:6, :] * acc_ref[...]
        if final_norm:
            x2 = _rms(x2) * fin_ref[...]
        o_ref[...] = x2


def _ffn(h2, wgu, wd, x1, mod_l, fin_g, *, tm, tf, seq, final_norm):
    T, D = h2.shape
    F = wd.shape[0]
    tiles_per_batch = seq // tm
    kern = functools.partial(_ffn_kernel, tf=tf, final_norm=final_norm)
    return pl.pallas_call(
        kern,
        grid=(T // tm, F // tf),
        in_specs=[
            pl.BlockSpec((tm, D), lambda i, f: (i, 0)),
            pl.BlockSpec((1, 1, D, 2 * tf), lambda i, f: (0, f, 0, 0)),
            pl.BlockSpec((tf, D), lambda i, f: (f, 0)),
            pl.BlockSpec((tm, D), lambda i, f: (i, 0)),
            pl.BlockSpec((1, 6, D), lambda i, f: (i // tiles_per_batch, 0, 0)),
            pl.BlockSpec((1, D), lambda i, f: (0, 0)),
        ],
        out_specs=pl.BlockSpec((tm, D), lambda i, f: (i, 0)),
        out_shape=jax.ShapeDtypeStruct((T, D), F32),
        scratch_shapes=[pltpu.VMEM((tm, D), F32)],
        compiler_params=pltpu.CompilerParams(
            dimension_semantics=("arbitrary", "arbitrary"), vmem_limit_bytes=VMEM_LIMIT),
    )(h2, wgu, wd, x1, mod_l, fin_g)


def _moe_kernel(h_ref, gates_ref, wg0_ref, wg1_ref, wu0_ref, wu1_ref, wd0_ref, wd1_ref,
                x_ref, mod_ref, fin_ref, o_ref,
                rank_scr, rankt_scr, gatest_scr, xc_scr, gc_scr, yc_scr, nch_ref,
                *, tm, tf, rc, final_norm):
    rcp = xc_scr.shape[1]
    dh = xc_scr.shape[2] // 2
    e = pl.program_id(1)
    f = pl.program_id(2)
    last_f = f == pl.num_programs(2) - 1

    @pl.when((e == 0) & (f == 0))
    def _():
        gates = gates_ref[...]
        sel = gates > 0.0
        selb = jnp.where(sel, 1.0, 0.0).astype(BF16)
        earlier = (lax.broadcasted_iota(jnp.int32, (tm, tm), 1)
                   < lax.broadcasted_iota(jnp.int32, (tm, tm), 0))
        rank = jnp.dot(jnp.where(earlier, 1.0, 0.0).astype(BF16), selb,
                       preferred_element_type=F32)
        rank = jnp.where(sel, rank, -1.0)
        rank_scr[...] = rank
        rankt_scr[...] = rank.T
        gatest_scr[...] = gates.T
        o_ref[...] = jnp.zeros_like(o_ref)

    @pl.when(f == 0)
    def _():
        lane = lax.broadcasted_iota(jnp.int32, (tm, LANES), 1)
        cnt = jnp.sum(jnp.where((lane == e) & (rank_scr[...] >= 0.0), 1.0, 0.0))
        nch = (cnt.astype(jnp.int32) + (rc - 1)) // rc
        nch_ref[0] = nch
        rrow = rankt_scr[pl.ds(e, 1), :].astype(jnp.int32)
        grow = gatest_scr[pl.ds(e, 1), :]

        def gather(k, c):
            row = lax.broadcasted_iota(jnp.int32, (rc, tm), 0) + k * rc
            hit = rrow == row
            onehot = jnp.where(hit, 1.0, 0.0).astype(BF16)
            xc_scr[k, 0:rc] = jnp.dot(onehot, h_ref[...], preferred_element_type=F32).astype(BF16)
            gcol = jnp.sum(jnp.where(hit, grow, 0.0), axis=1, keepdims=True)
            gc_scr[k, 0:rc] = jnp.broadcast_to(gcol, (rc, LANES))
            yc_scr[k] = jnp.zeros(yc_scr.shape[1:], F32)
            return c

        lax.fori_loop(0, nch, gather, 0)

    def expert(k, c):
        xa = xc_scr[k, 0:rc, 0:dh]
        xb = xc_scr[k, 0:rc, dh:]
        g = (jnp.dot(xa, wg0_ref[0, 0], preferred_element_type=F32)
             + jnp.dot(xb, wg1_ref[0, 0], preferred_element_type=F32))
        u = (jnp.dot(xa, wu0_ref[0, 0], preferred_element_type=F32)
             + jnp.dot(xb, wu1_ref[0, 0], preferred_element_type=F32))
        a = (g * _sigmoid(g) * u * gc_scr[k, 0:rc, 0:1]).astype(BF16)
        yc_scr[k, 0:rc, 0:dh] += jnp.dot(a, wd0_ref[0, 0, 0], preferred_element_type=F32)
        yc_scr[k, 0:rc, dh:] += jnp.dot(a, wd1_ref[0, 0, 0], preferred_element_type=F32)
        return c

    lax.fori_loop(0, nch_ref[0], expert, 0)

    @pl.when(last_f)
    def _():
        lane = lax.broadcasted_iota(jnp.int32, (tm, LANES), 1)
        rcol = jnp.sum(jnp.where(lane == e, rank_scr[...], 0.0), axis=1,
                       keepdims=True).astype(jnp.int32)

        def scatter(k, c):
            col = lax.broadcasted_iota(jnp.int32, (tm, rcp), 1)
            hit = (rcol == col + k * rc) & (col < rc)
            onehot_t = jnp.where(hit, 1.0, 0.0).astype(BF16)
            o_ref[...] += jnp.dot(onehot_t, yc_scr[k].astype(BF16), preferred_element_type=F32)
            return c

        lax.fori_loop(0, nch_ref[0], scatter, 0)

    @pl.when((e == pl.num_programs(1) - 1) & last_f)
    def _():
        x2 = x_ref[...] + mod_ref[0, 5:6, :] * o_ref[...]
        if final_norm:
            x2 = _rms(x2) * fin_ref[...]
        o_ref[...] = x2


def _tile_major(w, tf):
    E, D, F = w.shape
    return jnp.transpose(w.reshape(E, D, F // tf, tf), (0, 2, 1, 3)).astype(BF16)


def _col_halves(w, tf):
    E, F, D = w.shape
    return jnp.transpose(w.reshape(E, F // tf, tf, 2, D // 2), (0, 1, 3, 2, 4)).astype(BF16)


def _moe(h2, gates, wg, wu, wd, x1, mod_l, fin_g, *, tm, tf, rc, seq, final_norm):
    T, D = h2.shape
    E, NF = wd.shape[0], wd.shape[1]
    tiles_per_batch = seq // tm
    up_half = lambda p: pl.BlockSpec((1, 1, D // 2, tf), lambda i, e, f: (e, f, p, 0))
    down_half = lambda p: pl.BlockSpec((1, 1, 1, tf, D // 2), lambda i, e, f: (e, f, p, 0, 0))
    ncap = -(-tm // rc)
    rcp = -(-rc // LANES) * LANES
    kern = functools.partial(_moe_kernel, tm=tm, tf=tf, rc=rc, final_norm=final_norm)
    return pl.pallas_call(
        kern,
        grid=(T // tm, E, NF),
        in_specs=[
            pl.BlockSpec((tm, D), lambda i, e, f: (i, 0)),
            pl.BlockSpec((tm, LANES), lambda i, e, f: (i, 0)),
            up_half(0), up_half(1), up_half(0), up_half(1), down_half(0), down_half(1),
            pl.BlockSpec((tm, D), lambda i, e, f: (i, 0)),
            pl.BlockSpec((1, 6, D), lambda i, e, f: (i // tiles_per_batch, 0, 0)),
            pl.BlockSpec((1, D), lambda i, e, f: (0, 0)),
        ],
        out_specs=pl.BlockSpec((tm, D), lambda i, e, f: (i, 0)),
        out_shape=jax.ShapeDtypeStruct((T, D), F32),
        scratch_shapes=[
            pltpu.VMEM((tm, LANES), F32),
            pltpu.VMEM((LANES, tm), F32),
            pltpu.VMEM((LANES, tm), F32),
            pltpu.VMEM((ncap, rcp, D), BF16),
            pltpu.VMEM((ncap, rcp, LANES), F32),
            pltpu.VMEM((ncap, rcp, D), F32),
            pltpu.SMEM((1,), jnp.int32),
        ],
        compiler_params=pltpu.CompilerParams(
            dimension_semantics=("arbitrary", "arbitrary", "arbitrary"),
            vmem_limit_bytes=VMEM_LIMIT),
    )(h2, gates, wg, wg, wu, wu, wd, wd, x1, mod_l, fin_g)


GROUP = 4


def _dispatch_kernel(h_ref, gates_ref, xs_ref, gc_ref, rank_ref, rankt_scr, gatest_scr, *, tm, rc):
    e = pl.program_id(1)

    @pl.when(e == 0)
    def _():
        gates = gates_ref[...]
        sel = gates > 0.0
        selb = jnp.where(sel, 1.0, 0.0).astype(BF16)
        earlier = (lax.broadcasted_iota(jnp.int32, (tm, tm), 1)
                   < lax.broadcasted_iota(jnp.int32, (tm, tm), 0))
        rank = jnp.dot(jnp.where(earlier, 1.0, 0.0).astype(BF16), selb,
                       preferred_element_type=F32)
        rank = jnp.where(sel, rank, -1.0)
        rank_ref[...] = rank
        rankt_scr[...] = rank.T
        gatest_scr[...] = gates.T

    rrow = rankt_scr[pl.ds(e, 1), :].astype(jnp.int32)
    grow = gatest_scr[pl.ds(e, 1), :]
    hit = rrow == lax.broadcasted_iota(jnp.int32, (rc, tm), 0)
    onehot = jnp.where(hit, 1.0, 0.0).astype(BF16)
    xs_ref[...] = jnp.zeros_like(xs_ref)
    gc_ref[...] = jnp.zeros_like(gc_ref)
    xs_ref[0, 0, 0, 0:rc] = jnp.dot(onehot, h_ref[...], preferred_element_type=F32).astype(BF16)
    gcol = jnp.sum(jnp.where(hit, grow, 0.0), axis=1, keepdims=True)
    gc_ref[0, 0, 0, 0:rc] = jnp.broadcast_to(gcol, (rc, LANES))


def _experts_kernel(xs_ref, gc_ref, wgu_ref, wd_ref, ys_ref, acc_ref, *, tf, rc):
    f = pl.program_id(2)

    @pl.when(f == 0)
    def _():
        acc_ref[...] = jnp.zeros_like(acc_ref)

    for j in range(GROUP):
        a = _swiglu_tile(xs_ref[0, j, 0, 0:rc], wgu_ref[0, 0], tf) * gc_ref[0, j, 0, 0:rc, 0:1]
        acc_ref[j] += jnp.dot(a.astype(BF16), wd_ref[0], preferred_element_type=F32)

    @pl.when(f == pl.num_programs(2) - 1)
    def _():
        ys_ref[...] = jnp.zeros_like(ys_ref)
        for j in range(GROUP):
            ys_ref[0, j, 0, 0:rc] = acc_ref[j].astype(BF16)


def _combine_kernel(ys_ref, rank_ref, x_ref, mod_ref, fin_ref, o_ref, *, tm, rc, final_norm):
    e = pl.program_id(1)
    rcp = ys_ref.shape[3]

    @pl.when(e == 0)
    def _():
        o_ref[...] = jnp.zeros_like(o_ref)

    lane = lax.broadcasted_iota(jnp.int32, (tm, LANES), 1)
    rcol = jnp.sum(jnp.where(lane == e, rank_ref[...], 0.0), axis=1,
                   keepdims=True).astype(jnp.int32)
    col = lax.broadcasted_iota(jnp.int32, (tm, rcp), 1)
    onehot_t = jnp.where((rcol == col) & (col < rc), 1.0, 0.0).astype(BF16)
    o_ref[...] += jnp.dot(onehot_t, ys_ref[0, 0, 0], preferred_element_type=F32)

    @pl.when(e == pl.num_programs(1) - 1)
    def _():
        x2 = x_ref[...] + mod_ref[0, 5:6, :] * o_ref[...]
        if final_norm:
            x2 = _rms(x2) * fin_ref[...]
        o_ref[...] = x2


def _dispatch(h2, gates, *, tm, rc, rcp):
    T, D = h2.shape
    nt = T // tm
    E = N_EXPERTS
    kern = functools.partial(_dispatch_kernel, tm=tm, rc=rc)
    chunk = lambda i, e: (i // GROUP, i % GROUP, e, 0, 0)
    return pl.pallas_call(
        kern,
        grid=(nt, E),
        in_specs=[
            pl.BlockSpec((tm, D), lambda i, e: (i, 0)),
            pl.BlockSpec((tm, LANES), lambda i, e: (i, 0)),
        ],
        out_specs=[
            pl.BlockSpec((1, 1, 1, rcp, D), chunk),
            pl.BlockSpec((1, 1, 1, rcp, LANES), chunk),
            pl.BlockSpec((tm, LANES), lambda i, e: (i, 0)),
        ],
        out_shape=[
            jax.ShapeDtypeStruct((nt // GROUP, GROUP, E, rcp, D), BF16),
            jax.ShapeDtypeStruct((nt // GROUP, GROUP, E, rcp, LANES), F32),
            jax.ShapeDtypeStruct((T, LANES), F32),
        ],
        scratch_shapes=[pltpu.VMEM((LANES, tm), F32), pltpu.VMEM((LANES, tm), F32)],
        compiler_params=pltpu.CompilerParams(
            dimension_semantics=("arbitrary", "arbitrary"), vmem_limit_bytes=VMEM_LIMIT),
    )(h2, gates)


def _experts(xs, gc, wgu, wd, *, tf, rc):
    ng, _, E, rcp, D = xs.shape
    NF = wgu.shape[1]
    kern = functools.partial(_experts_kernel, tf=tf, rc=rc)
    chunks = lambda e, g, f: (g, 0, e, 0, 0)
    return pl.pallas_call(
        kern,
        grid=(E, ng, NF),
        in_specs=[
            pl.BlockSpec((1, GROUP, 1, rcp, D), chunks),
            pl.BlockSpec((1, GROUP, 1, rcp, LANES), chunks),
            pl.BlockSpec((1, 1, D, 2 * tf), lambda e, g, f: (e, f, 0, 0)),
            pl.BlockSpec((1, tf, D), lambda e, g, f: (e, f, 0)),
        ],
        out_specs=pl.BlockSpec((1, GROUP, 1, rcp, D), chunks),
        out_shape=jax.ShapeDtypeStruct(xs.shape, BF16),
        scratch_shapes=[pltpu.VMEM((GROUP, rc, D), F32)],
        compiler_params=pltpu.CompilerParams(
            dimension_semantics=("arbitrary", "arbitrary", "arbitrary"),
            vmem_limit_bytes=VMEM_LIMIT),
    )(xs, gc, wgu, wd)


def _combine(ys, rank, x1, mod_l, fin_g, *, tm, rc, seq, final_norm):
    T, D = x1.shape
    E, rcp = ys.shape[2], ys.shape[3]
    tiles_per_batch = seq // tm
    kern = functools.partial(_combine_kernel, tm=tm, rc=rc, final_norm=final_norm)
    return pl.pallas_call(
        kern,
        grid=(T // tm, E),
        in_specs=[
            pl.BlockSpec((1, 1, 1, rcp, D), lambda i, e: (i // GROUP, i % GROUP, e, 0, 0)),
            pl.BlockSpec((tm, LANES), lambda i, e: (i, 0)),
            pl.BlockSpec((tm, D), lambda i, e: (i, 0)),
            pl.BlockSpec((1, 6, D), lambda i, e: (i // tiles_per_batch, 0, 0)),
            pl.BlockSpec((1, D), lambda i, e: (0, 0)),
        ],
        out_specs=pl.BlockSpec((tm, D), lambda i, e: (i, 0)),
        out_shape=jax.ShapeDtypeStruct((T, D), F32),
        compiler_params=pltpu.CompilerParams(
            dimension_semantics=("arbitrary", "arbitrary"), vmem_limit_bytes=VMEM_LIMIT),
    )(ys, rank, x1, mod_l, fin_g)


def _moe_layer(h2, gates, w_gate, w_up, w_down, x1, mod_l, fin_g, *, tm, tf, rc, seq, final_norm):
    T = h2.shape[0]
    rcp = -(-rc // LANES) * LANES
    if (T // tm) % GROUP != 0:
        return _moe(h2, gates, _tile_major(w_gate, tf), _tile_major(w_up, tf),
                    _col_halves(w_down, tf), x1, mod_l, fin_g,
                    tm=tm, tf=tf, rc=rc, seq=seq, final_norm=final_norm)
    xs, gc, rank = _dispatch(h2, gates, tm=tm, rc=rc, rcp=rcp)

    def expert_major(_):
        ys = _experts(xs, gc, _fuse_gate_up(w_gate, w_up, tf), w_down.astype(BF16), tf=tf, rc=rc)
        return _combine(ys, rank, x1, mod_l, fin_g, tm=tm, rc=rc, seq=seq, final_norm=final_norm)

    def tile_major(_):
        return _moe(h2, gates, _tile_major(w_gate, tf), _tile_major(w_up, tf),
                    _col_halves(w_down, tf), x1, mod_l, fin_g,
                    tm=tm, tf=tf, rc=rc, seq=seq, final_norm=final_norm)

    return lax.cond(jnp.max(rank) < rc, expert_major, tile_major, 0)


RC = 128
VT = 8


def _route_kernel(gates_ref, rank_ref, rankt_ref, gatest_ref, cnt_ref, *, tm):
    gates = gates_ref[...]
    sel = gates > 0.0
    self32 = jnp.where(sel, 1.0, 0.0)
    earlier = (lax.broadcasted_iota(jnp.int32, (tm, tm), 1)
               < lax.broadcasted_iota(jnp.int32, (tm, tm), 0))
    rank = jnp.dot(jnp.where(earlier, 1.0, 0.0).astype(BF16), self32.astype(BF16),
                   preferred_element_type=F32)
    rank = jnp.where(sel, rank, -1.0)
    rank_ref[...] = rank
    rankt_ref[0] = rank.T
    gatest_ref[0] = gates.T
    cnt_ref[0] = jnp.sum(self32, axis=0, keepdims=True)


def _route(gates, *, tm):
    T = gates.shape[0]
    nt = T // tm
    return pl.pallas_call(
        functools.partial(_route_kernel, tm=tm),
        grid=(nt,),
        in_specs=[pl.BlockSpec((tm, LANES), lambda i: (i, 0))],
        out_specs=[
            pl.BlockSpec((tm, LANES), lambda i: (i, 0)),
            pl.BlockSpec((1, LANES, tm), lambda i: (i, 0, 0)),
            pl.BlockSpec((1, LANES, tm), lambda i: (i, 0, 0)),
            pl.BlockSpec((1, 1, LANES), lambda i: (i, 0, 0)),
        ],
        out_shape=[
            jax.ShapeDtypeStruct((T, LANES), F32),
            jax.ShapeDtypeStruct((nt, LANES, tm), F32),
            jax.ShapeDtypeStruct((nt, LANES, tm), F32),
            jax.ShapeDtypeStruct((nt, 1, LANES), F32),
        ],
        compiler_params=pltpu.CompilerParams(
            dimension_semantics=("arbitrary",), vmem_limit_bytes=VMEM_LIMIT),
    )(gates)


def _chunk_tables(cnt, nt, n_list, n_vt):
    E = N_EXPERTS
    nch = (cnt + (RC - 1)) // RC
    nce = jnp.sum(nch, axis=0)
    padded = (nce + (VT - 1)) // VT * VT
    base = jnp.cumsum(padded) - padded
    cid0 = (base[None, :] + jnp.cumsum(nch, axis=0) - nch).reshape(-1)
    nch_f = nch.reshape(-1)
    cum = jnp.cumsum(nch_f)
    n_real = cum[-1]
    s = jnp.arange(n_list, dtype=jnp.int32)
    g = jnp.minimum(jnp.searchsorted(cum, s, side="right"), nt * E - 1).astype(jnp.int32)
    k = s - (cum[g] - nch_f[g])
    tile = g // E
    tile_first = (cum - nch_f).reshape(nt, E)[:, 0]
    tile_last = cum.reshape(nt, E)[:, E - 1] - 1
    real = s < n_real
    npad = padded - nce
    pcum = jnp.cumsum(npad)
    j = s - n_real
    pe = jnp.minimum(jnp.searchsorted(pcum, j, side="right"), E - 1).astype(jnp.int32)
    pcid = base[pe] + nce[pe] + (j - (pcum[pe] - npad[pe]))
    is_pad = jnp.logical_and(s >= n_real, s < n_real + pcum[-1])
    kind = jnp.where(real, 0, jnp.where(is_pad, 1, 2)).astype(jnp.int32)
    fields = dict(
        tile=jnp.where(real, tile, nt - 1),
        exp=jnp.where(real, g % E, 0),
        k=jnp.where(real, k, 0),
        cid=jnp.where(real, cid0[g] + k, pcid),
        first=jnp.logical_and(real, s == tile_first[tile]),
        last=jnp.logical_and(real, s == tile_last[tile]),
    )
    hold = jnp.minimum(s, n_real + pcum[-1] - 1)
    fields = {name: v[hold].astype(jnp.int32) for name, v in fields.items()}
    vcum = jnp.cumsum(padded // VT)
    nv = vcum[-1]
    v = jnp.minimum(jnp.arange(n_vt, dtype=jnp.int32), nv - 1)
    v_exp = jnp.minimum(jnp.searchsorted(vcum, v, side="right"), E - 1).astype(jnp.int32)
    return fields, kind, v_exp, nv.reshape(1).astype(jnp.int32)


def _scatter_kernel(tile_ref, exp_ref, k_ref, cid_ref, kind_ref,
                    h_ref, rankt_ref, gatest_ref, xs_ref, gc_ref, *, tm):
    s = pl.program_id(0)
    kind = kind_ref[s]

    @pl.when(kind == 0)
    def _():
        e = exp_ref[s]
        rrow = rankt_ref[0, pl.ds(e, 1), :].astype(jnp.int32)
        grow = gatest_ref[0, pl.ds(e, 1), :]
        row = lax.broadcasted_iota(jnp.int32, (RC, tm), 0) + k_ref[s] * RC
        hit = rrow == row
        onehot = jnp.where(hit, 1.0, 0.0).astype(BF16)
        xs_ref[...] = jnp.dot(onehot, h_ref[...], preferred_element_type=F32).astype(BF16)
        gcol = jnp.sum(jnp.where(hit, grow, 0.0), axis=1, keepdims=True)
        gc_ref[...] = jnp.broadcast_to(gcol, (RC, LANES))

    @pl.when(kind == 1)
    def _():
        xs_ref[...] = jnp.zeros_like(xs_ref)
        gc_ref[...] = jnp.zeros_like(gc_ref)


def _scatter_rows(tabs, kind, h2, rankt, gatest, *, tm, n_list, n_rows):
    T, D = h2.shape
    grid_spec = pltpu.PrefetchScalarGridSpec(
        num_scalar_prefetch=5,
        grid=(n_list,),
        in_specs=[
            pl.BlockSpec((tm, D), lambda s, tile, exp, k, cid, kind: (tile[s], 0)),
            pl.BlockSpec((1, LANES, tm), lambda s, tile, exp, k, cid, kind: (tile[s], 0, 0)),
            pl.BlockSpec((1, LANES, tm), lambda s, tile, exp, k, cid, kind: (tile[s], 0, 0)),
        ],
        out_specs=[
            pl.BlockSpec((RC, D), lambda s, tile, exp, k, cid, kind: (cid[s], 0)),
            pl.BlockSpec((RC, LANES), lambda s, tile, exp, k, cid, kind: (cid[s], 0)),
        ],
    )
    return pl.pallas_call(
        functools.partial(_scatter_kernel, tm=tm),
        grid_spec=grid_spec,
        out_shape=[jax.ShapeDtypeStruct((n_rows, D), BF16),
                   jax.ShapeDtypeStruct((n_rows, LANES), F32)],
        compiler_params=pltpu.CompilerParams(
            dimension_semantics=("arbitrary",), vmem_limit_bytes=VMEM_LIMIT),
    )(tabs["tile"], tabs["exp"], tabs["k"], tabs["cid"], kind, h2, rankt, gatest)


def _sorted_experts_kernel(vexp_ref, nv_ref, xs_ref, gc_ref, wgu_ref, wd_ref, ys_ref, acc_ref, *, tf):
    v = pl.program_id(0)
    f = pl.program_id(1)

    @pl.when(v < nv_ref[0])
    def _():
        @pl.when(f == 0)
        def _():
            acc_ref[...] = jnp.zeros_like(acc_ref)

        a = _swiglu_tile(xs_ref[...], wgu_ref[0, 0], tf) * gc_ref[:, 0:1]
        acc_ref[...] += jnp.dot(a.astype(BF16), wd_ref[0], preferred_element_type=F32)

        @pl.when(f == pl.num_programs(1) - 1)
        def _():
            ys_ref[...] = acc_ref[...].astype(BF16)


def _sorted_experts(v_exp, nv, xs, gc, wgu, wd, *, tf, n_vt):
    D = xs.shape[1]
    NF = wgu.shape[1]
    rows = VT * RC

    def live(v, nv):
        return jnp.minimum(v, nv[0] - 1)

    def hidden(v, f, nv):
        return jnp.where(v < nv[0], f, NF - 1)

    grid_spec = pltpu.PrefetchScalarGridSpec(
        num_scalar_prefetch=2,
        grid=(n_vt, NF),
        in_specs=[
            pl.BlockSpec((rows, D), lambda v, f, vexp, nv: (live(v, nv), 0)),
            pl.BlockSpec((rows, LANES), lambda v, f, vexp, nv: (live(v, nv), 0)),
            pl.BlockSpec((1, 1, D, 2 * tf), lambda v, f, vexp, nv: (vexp[v], hidden(v, f, nv), 0, 0)),
            pl.BlockSpec((1, tf, D), lambda v, f, vexp, nv: (vexp[v], hidden(v, f, nv), 0)),
        ],
        out_specs=pl.BlockSpec((rows, D), lambda v, f, vexp, nv: (live(v, nv), 0)),
        scratch_shapes=[pltpu.VMEM((rows, D), F32)],
    )
    return pl.pallas_call(
        functools.partial(_sorted_experts_kernel, tf=tf),
        grid_spec=grid_spec,
        out_shape=jax.ShapeDtypeStruct(xs.shape, BF16),
        compiler_params=pltpu.CompilerParams(
            dimension_semantics=("arbitrary", "arbitrary"), vmem_limit_bytes=VMEM_LIMIT),
    )(v_exp, nv, xs, gc, wgu, wd)


def _gather_kernel(tile_ref, exp_ref, k_ref, cid_ref, kind_ref, first_ref, last_ref,
                   ys_ref, rank_ref, x_ref, mod_ref, fin_ref, o_ref, *, tm, final_norm):
    s = pl.program_id(0)

    @pl.when(kind_ref[s] == 0)
    def _():
        @pl.when(first_ref[s] == 1)
        def _():
            o_ref[...] = jnp.zeros_like(o_ref)

        lane = lax.broadcasted_iota(jnp.int32, (tm, LANES), 1)
        rcol = jnp.sum(jnp.where(lane == exp_ref[s], rank_ref[...], 0.0), axis=1,
                       keepdims=True).astype(jnp.int32)
        col = lax.broadcasted_iota(jnp.int32, (tm, RC), 1) + k_ref[s] * RC
        onehot_t = jnp.where(rcol == col, 1.0, 0.0).astype(BF16)
        o_ref[...] += jnp.dot(onehot_t, ys_ref[...], preferred_element_type=F32)

        @pl.when(last_ref[s] == 1)
        def _():
            x2 = x_ref[...] + mod_ref[0, 5:6, :] * o_ref[...]
            if final_norm:
                x2 = _rms(x2) * fin_ref[...]
            o_ref[...] = x2


def _gather_rows(tabs, kind, ys, rank, x1, mod_l, fin_g, *, tm, seq, n_list, final_norm):
    T, D = x1.shape
    tpb = seq // tm
    ix = lambda fn: (lambda s, tile, exp, k, cid, kind, first, last: fn(s, tile, cid))
    grid_spec = pltpu.PrefetchScalarGridSpec(
        num_scalar_prefetch=7,
        grid=(n_list,),
        in_specs=[
            pl.BlockSpec((RC, D), ix(lambda s, tile, cid: (cid[s], 0))),
            pl.BlockSpec((tm, LANES), ix(lambda s, tile, cid: (tile[s], 0))),
            pl.BlockSpec((tm, D), ix(lambda s, tile, cid: (tile[s], 0))),
            pl.BlockSpec((1, 6, D), ix(lambda s, tile, cid: (tile[s] // tpb, 0, 0))),
            pl.BlockSpec((1, D), ix(lambda s, tile, cid: (0, 0))),
        ],
        out_specs=pl.BlockSpec((tm, D), ix(lambda s, tile, cid: (tile[s], 0))),
    )
    return pl.pallas_call(
        functools.partial(_gather_kernel, tm=tm, final_norm=final_norm),
        grid_spec=grid_spec,
        out_shape=jax.ShapeDtypeStruct((T, D), F32),
        compiler_params=pltpu.CompilerParams(
            dimension_semantics=("arbitrary",), vmem_limit_bytes=VMEM_LIMIT),
    )(tabs["tile"], tabs["exp"], tabs["k"], tabs["cid"], kind, tabs["first"], tabs["last"],
      ys, rank, x1, mod_l, fin_g)


def _routed_experts(h2, gates, w_gate, w_up, w_down, x1, mod_l, fin_g, *, tm, tf, seq, final_norm):
    T, D = h2.shape
    E = N_EXPERTS
    nt = T // tm
    n_real_max = 2 * T // RC + nt * E
    n_vt = -(-(n_real_max + E * (VT - 1)) // VT)
    n_list = n_vt * VT
    rank, rankt, gatest, cnt = _route(gates, tm=tm)
    cnt = cnt[:, 0, :E].astype(jnp.int32)
    tabs, kind, v_exp, nv = _chunk_tables(cnt, nt, n_list, n_vt)
    xs, gc = _scatter_rows(tabs, kind, h2, rankt, gatest, tm=tm, n_list=n_list, n_rows=n_list * RC)
    ys = _sorted_experts(v_exp, nv, xs, gc, _fuse_gate_up(w_gate, w_up, tf), w_down.astype(BF16),
                         tf=tf, n_vt=n_vt)
    return _gather_rows(tabs, kind, ys, rank, x1, mod_l, fin_g, tm=tm, seq=seq, n_list=n_list,
                        final_norm=final_norm)


def _slot_tables(cnt, nt, ns, n_vt):
    E = N_EXPERTS
    nch = (cnt + (RC - 1)) // RC
    nct = jnp.sum(nch, axis=1)
    ccum = jnp.cumsum(nch, axis=1)
    j = jnp.arange(ns, dtype=jnp.int32)
    slot_e = jnp.sum((ccum[:, None, :] <= j[None, :, None]).astype(jnp.int32), axis=2)
    slot_e = jnp.minimum(slot_e, E - 1)
    first = jnp.take_along_axis(ccum - nch, slot_e, axis=1)
    valid = j[None, :] < nct[:, None]
    slot_k = jnp.where(valid, j[None, :] - first, 0)
    nce = jnp.sum(nch, axis=0)
    padded = (nce + (VT - 1)) // VT * VT
    base = jnp.cumsum(padded) - padded
    cid0 = base[None, :] + jnp.cumsum(nch, axis=0) - nch
    cid = jnp.take_along_axis(cid0, slot_e, axis=1) + slot_k
    cid = jnp.where(valid, cid, cid[:, 0:1])
    n_chunks = n_vt * VT
    slot_id = jnp.arange(nt, dtype=jnp.int32)[:, None] * ns + j[None, :]
    pos = jnp.full((n_chunks,), -1, jnp.int32).at[
        jnp.where(valid, cid, n_chunks).reshape(-1)].set(slot_id.reshape(-1), mode="drop")
    vcum = jnp.cumsum(padded // VT)
    nv = vcum[-1]
    v = jnp.minimum(jnp.arange(n_vt, dtype=jnp.int32), nv - 1)
    v_exp = jnp.minimum(jnp.searchsorted(vcum, v, side="right"), E - 1).astype(jnp.int32)
    i32 = lambda a: a.reshape(-1).astype(jnp.int32)
    return dict(slot_e=i32(slot_e), slot_k=i32(slot_k), nct=i32(nct), cid=i32(cid),
                pos=jnp.maximum(pos, 0), real=(pos >= 0).astype(jnp.int32),
                v_exp=v_exp, nv=i32(nv))


def _compact_kernel(se_ref, sk_ref, nct_ref, h_ref, rankt_ref, gatest_ref, xs_ref, gc_ref, oh_scr,
                    *, tm, ns):
    t = pl.program_id(0)
    n = nct_ref[t]

    def build(j, c):
        e = se_ref[t * ns + j]
        rrow = rankt_ref[0, pl.ds(e, 1), :].astype(jnp.int32)
        grow = gatest_ref[0, pl.ds(e, 1), :]
        row = lax.broadcasted_iota(jnp.int32, (RC, tm), 0) + sk_ref[t * ns + j] * RC
        hit = (rrow == row) & (j < n)
        oh_scr[j] = jnp.where(hit, 1.0, 0.0).astype(BF16)
        gcol = jnp.sum(jnp.where(hit, grow, 0.0), axis=1, keepdims=True)
        gc_ref[pl.ds(pl.multiple_of(j * RC, RC), RC), :] = jnp.broadcast_to(gcol, (RC, LANES))
        return c

    lax.fori_loop(0, ns, build, 0)
    rows = VT * RC
    for b in range(ns // VT):
        oh = oh_scr[b * VT:(b + 1) * VT].reshape(rows, tm)
        xs_ref[b * rows:(b + 1) * rows, :] = jnp.dot(
            oh, h_ref[...], preferred_element_type=F32).astype(BF16)


def _compact(tabs, h2, rankt, gatest, *, tm, ns):
    T, D = h2.shape
    nt = T // tm
    grid_spec = pltpu.PrefetchScalarGridSpec(
        num_scalar_prefetch=3,
        grid=(nt,),
        in_specs=[
            pl.BlockSpec((tm, D), lambda t, *_: (t, 0)),
            pl.BlockSpec((1, LANES, tm), lambda t, *_: (t, 0, 0)),
            pl.BlockSpec((1, LANES, tm), lambda t, *_: (t, 0, 0)),
        ],
        out_specs=[
            pl.BlockSpec((ns * RC, D), lambda t, *_: (t, 0)),
            pl.BlockSpec((ns * RC, LANES), lambda t, *_: (t, 0)),
        ],
        scratch_shapes=[pltpu.VMEM((ns, RC, tm), BF16)],
    )
    return pl.pallas_call(
        functools.partial(_compact_kernel, tm=tm, ns=ns),
        grid_spec=grid_spec,
        out_shape=[jax.ShapeDtypeStruct((nt * ns * RC, D), BF16),
                   jax.ShapeDtypeStruct((nt * ns * RC, LANES), F32)],
        compiler_params=pltpu.CompilerParams(
            dimension_semantics=("arbitrary",), vmem_limit_bytes=VMEM_LIMIT),
    )(tabs["slot_e"], tabs["slot_k"], tabs["nct"], h2, rankt, gatest)


def _slot_experts_kernel(vexp_ref, nv_ref, pos_ref, real_ref, *refs, tf):
    xs_refs = refs[0:VT]
    gc_refs = refs[VT:2 * VT]
    wgu_ref, wd_ref, ys_ref, x_scr, g_scr, acc_ref = refs[2 * VT:]
    v = pl.program_id(0)
    f = pl.program_id(1)

    @pl.when(v < nv_ref[0])
    def _():
        @pl.when(f == 0)
        def _():
            for j in range(VT):
                keep = real_ref[v * VT + j] == 1
                x_scr[j * RC:(j + 1) * RC, :] = jnp.where(keep, xs_refs[j][...], jnp.zeros_like(xs_refs[j]))
                g_scr[j * RC:(j + 1) * RC, :] = jnp.where(keep, gc_refs[j][...], 0.0)
            acc_ref[...] = jnp.zeros_like(acc_ref)

        a = _swiglu_tile(x_scr[...], wgu_ref[0, 0], tf) * g_scr[:, 0:1]
        acc_ref[...] += jnp.dot(a.astype(BF16), wd_ref[0], preferred_element_type=F32)

        @pl.when(f == pl.num_programs(1) - 1)
        def _():
            ys_ref[...] = acc_ref[...].astype(BF16)


def _slot_experts(tabs, xs, gc, wgu, wd, *, tf, n_vt):
    D = xs.shape[1]
    NF = wgu.shape[1]
    rows = VT * RC

    def live(v, nv):
        return jnp.minimum(v, nv[0] - 1)

    def hidden(v, f, nv):
        return jnp.where(v < nv[0], f, NF - 1)

    def chunk(j):
        return lambda v, f, vexp, nv, pos, real: (pos[live(v, nv) * VT + j], 0)

    grid_spec = pltpu.PrefetchScalarGridSpec(
        num_scalar_prefetch=4,
        grid=(n_vt, NF),
        in_specs=(
            [pl.BlockSpec((RC, D), chunk(j)) for j in range(VT)]
            + [pl.BlockSpec((RC, LANES), chunk(j)) for j in range(VT)]
            + [pl.BlockSpec((1, 1, D, 2 * tf),
                            lambda v, f, vexp, nv, pos, real: (vexp[v], hidden(v, f, nv), 0, 0)),
               pl.BlockSpec((1, tf, D),
                            lambda v, f, vexp, nv, pos, real: (vexp[v], hidden(v, f, nv), 0))]),
        out_specs=pl.BlockSpec((rows, D), lambda v, f, vexp, nv, pos, real: (live(v, nv), 0)),
        scratch_shapes=[pltpu.VMEM((rows, D), BF16), pltpu.VMEM((rows, LANES), F32),
                        pltpu.VMEM((rows, D), F32)],
    )
    return pl.pallas_call(
        functools.partial(_slot_experts_kernel, tf=tf),
        grid_spec=grid_spec,
        out_shape=jax.ShapeDtypeStruct((n_vt * rows, D), BF16),
        compiler_params=pltpu.CompilerParams(
            dimension_semantics=("arbitrary", "arbitrary"), vmem_limit_bytes=VMEM_LIMIT),
    )(tabs["v_exp"], tabs["nv"], tabs["pos"], tabs["real"], *([xs] * VT), *([gc] * VT), wgu, wd)


def _uncompact_kernel(se_ref, sk_ref, nct_ref, cid_ref, *refs, tm, ns, final_norm):
    ys_refs = refs[0:ns]
    rank_ref, x_ref, mod_ref, fin_ref, o_ref, oh_scr = refs[ns:]
    t = pl.program_id(0)
    n = nct_ref[t]
    lane = lax.broadcasted_iota(jnp.int32, (tm, LANES), 1)
    col = lax.broadcasted_iota(jnp.int32, (tm, RC), 1)
    rank = rank_ref[...]
    for j in range(ns):
        rcol = jnp.sum(jnp.where(lane == se_ref[t * ns + j], rank, 0.0), axis=1,
                       keepdims=True).astype(jnp.int32)
        hit = (rcol == col + sk_ref[t * ns + j] * RC) & (j < n)
        oh_scr[:, j * RC:(j + 1) * RC] = jnp.where(hit, 1.0, 0.0).astype(BF16)
    y = None
    for b in range(ns // VT):
        ys = jnp.concatenate([ys_refs[b * VT + j][...] for j in range(VT)], axis=0)
        part = jnp.dot(oh_scr[:, b * VT * RC:(b + 1) * VT * RC], ys, preferred_element_type=F32)
        y = part if y is None else y + part
    x2 = x_ref[...] + mod_ref[0, 5:6, :] * y
    if final_norm:
        x2 = _rms(x2) * fin_ref[...]
    o_ref[...] = x2


def _uncompact(tabs, ys, rank, x1, mod_l, fin_g, *, tm, ns, seq, final_norm):
    T, D = x1.shape
    tpb = seq // tm

    def chunk(j):
        return lambda t, se, sk, nct, cid: (cid[t * ns + j], 0)

    grid_spec = pltpu.PrefetchScalarGridSpec(
        num_scalar_prefetch=4,
        grid=(T // tm,),
        in_specs=(
            [pl.BlockSpec((RC, D), chunk(j)) for j in range(ns)]
            + [pl.BlockSpec((tm, LANES), lambda t, *_: (t, 0)),
               pl.BlockSpec((tm, D), lambda t, *_: (t, 0)),
               pl.BlockSpec((1, 6, D), lambda t, *_: (t // tpb, 0, 0)),
               pl.BlockSpec((1, D), lambda t, *_: (0, 0))]),
        out_specs=pl.BlockSpec((tm, D), lambda t, *_: (t, 0)),
        scratch_shapes=[pltpu.VMEM((tm, ns * RC), BF16)],
    )
    return pl.pallas_call(
        functools.partial(_uncompact_kernel, tm=tm, ns=ns, final_norm=final_norm),
        grid_spec=grid_spec,
        out_shape=jax.ShapeDtypeStruct((T, D), F32),
        compiler_params=pltpu.CompilerParams(
            dimension_semantics=("arbitrary",), vmem_limit_bytes=VMEM_LIMIT),
    )(tabs["slot_e"], tabs["slot_k"], tabs["nct"], tabs["cid"], *([ys] * ns),
      rank, x1, mod_l, fin_g)


def _routed_experts_slots(h2, gates, w_gate, w_up, w_down, x1, mod_l, fin_g,
                          *, tm, tf, seq, final_norm):
    T, D = h2.shape
    E = N_EXPERTS
    nt = T // tm
    ns = 2 * tm // RC + E
    assert ns % VT == 0
    n_vt = -(-(nt * ns + E * (VT - 1)) // VT)
    rank, rankt, gatest, cnt = _route(gates, tm=tm)
    tabs = _slot_tables(cnt[:, 0, :E].astype(jnp.int32), nt, ns, n_vt)
    xs, gc = _compact(tabs, h2, rankt, gatest, tm=tm, ns=ns)
    ys = _slot_experts(tabs, xs, gc, _fuse_gate_up(w_gate, w_up, tf), w_down.astype(BF16),
                       tf=tf, n_vt=n_vt)
    return _uncompact(tabs, ys, rank, x1, mod_l, fin_g, tm=tm, ns=ns, seq=seq,
                      final_norm=final_norm)


def _tiles(S):
    ts = min(512, S)
    tq = min(512, S)
    tk = ts
    tm = min(1024, S)
    tf = 896
    rc = tm // 4 + 64
    assert rc % 16 == 0
    return ts, tq, tk, tm, tf, rc


def kernel(x, c, positions, ada_w, ada_b, attn_norm_g, w_in, q_norm_g, w_uq, kv_norm_g, w_ukv,
           fox_forget_b, mla_out_g, fox_out_g, w_o, ffn_norm_g, dense_w_gate, dense_w_up,
           dense_w_down, router_w, moe_w_gate, moe_w_up, moe_w_down, final_norm_g):
    B, S, D = x.shape
    L = ada_w.shape[0]
    ts, tq, tk, tm, tf, rc = _tiles(S)

    mod = _adaln(c, ada_w, ada_b).reshape(L, B, 6, D)
    ctab, stab = _rope_tables(positions)

    idx_in = _perm_w_in()
    idx_qa, idx_qb = _perm_w_uq()
    idx_kv = _perm_w_ukv()
    col_scale = np.ones((C_END,), np.float32)
    col_scale[C_FQ0:C_FQ0 + FOX_W] = FOX_HEAD_DIM ** -0.5
    tri = jnp.asarray(np.tril(np.ones((ts, ts), np.float32)), BF16)
    eqk = jnp.asarray(_aug_placement(), BF16)

    for l in range(L):
        win = (_take_cols(w_in[l], idx_in) * col_scale).astype(BF16)
        wuqa = _take_cols(w_uq[l], idx_qa).astype(BF16)
        wuqb = _take_cols(w_uq[l], idx_qb).astype(BF16)
        wukv = _take_cols(w_ukv[l], idx_kv).astype(BF16)
        fb_row = jnp.zeros((1, LANES), F32).at[0, CTRL_LOGIT:CTRL_LOGIT + FOX_HEADS].set(
            fox_forget_b[l].astype(F32))
        qa, ka, vt = _inproj(
            x, mod[l], attn_norm_g[l].reshape(1, D), win, q_norm_g[l].reshape(1, -1),
            kv_norm_g[l].reshape(1, -1), wuqa, wuqb, wukv, fb_row, ctab, stab, tri, eqk, ts=ts)
        ot = _attention(qa, ka, vt, tq=tq, tk=tk, hp=4)

        j = l // 2
        is_moe = (l % 2 == 1)
        rw = None
        if is_moe:
            rw = jnp.pad(router_w[j], ((0, 0), (0, LANES - N_EXPERTS))).astype(BF16)
        outs = _outproj(ot, x, mod[l], mla_out_g[l].reshape(1, -1), fox_out_g[l].reshape(1, -1),
                        w_o[l].astype(BF16), ffn_norm_g[l].reshape(1, D), rw, ts=ts)
        x1, h2 = outs[0].reshape(B * S, D), outs[1].reshape(B * S, D)
        fin_g = final_norm_g.reshape(1, D)
        last = l == L - 1
        if is_moe:
            x = _routed_experts_slots(h2, outs[2].reshape(B * S, LANES), moe_w_gate[j],
                                      moe_w_up[j], moe_w_down[j], x1, mod[l], fin_g,
                                      tm=tm, tf=tf, seq=S, final_norm=last)
        else:
            x = _ffn(h2, _fuse_gate_up(dense_w_gate[j][None], dense_w_up[j][None], tf),
                     dense_w_down[j].astype(BF16), x1, mod[l], fin_g,
                     tm=tm, tf=tf, seq=S, final_norm=last)
        x = x.reshape(B, S, D)
    return x
```

```python
import functools

import numpy as np
import jax
import jax.numpy as jnp
from jax import lax
from jax.experimental import pallas as pl
from jax.experimental.pallas import tpu as pltpu

F32 = jnp.float32
BF16 = jnp.bfloat16

D_MODEL = 1024
MLA_HEADS = 8
MLA_V = 64
MLA_NOPE = 64
MLA_ROPE = 32
MLA_Q_LORA = 256
MLA_KV_LORA = 128
FOX_HEADS = 8
FOX_HEAD_DIM = 64
FOX_W = FOX_HEADS * FOX_HEAD_DIM
ROPE_THETA = 10000.0
D_FF = 3584
N_EXPERTS = 8
EPS = 1e-6
N_HEADS = MLA_HEADS + FOX_HEADS
HALF_ROPE = MLA_ROPE // 2

LANES = 128
VMEM_LIMIT = 56 * 1024 * 1024

ROPE_LO = MLA_NOPE
AUG_LO = FOX_HEAD_DIM
CTRL_LOGIT = 96
CTRL_ONE = 120

C_Q0 = 0
C_KV0 = C_Q0 + MLA_Q_LORA
C_KRA0 = C_KV0 + MLA_KV_LORA
C_KRB0 = C_KRA0 + LANES
C_FQ0 = C_KRB0 + LANES
C_FK0 = C_FQ0 + FOX_W
C_FV0 = C_FK0 + FOX_W
C_END = C_FV0 + FOX_W

NEG_BIG = -1e30
LOG2E = 1.4426950408889634
ONES_ROWS = 16


def _rms(v):
    return v * lax.rsqrt(jnp.mean(v * v, axis=-1, keepdims=True) + EPS)


def _sigmoid(v):
    return 1.0 / (1.0 + jnp.exp(-v))


def _split3(v):
    hi = v.astype(BF16).astype(F32)
    r = v - hi
    mid = r.astype(BF16).astype(F32)
    lo = (r - mid).astype(BF16).astype(F32)
    return hi, mid, lo


def _adaln_kernel(c_ref, w_ref, b_ref, o_ref):
    c = c_ref[...]
    ca = (c * _sigmoid(c)).astype(BF16)
    o_ref[0] = jnp.dot(ca, w_ref[0].astype(BF16), preferred_element_type=F32) + b_ref[0]


def _adaln(c, ada_w, ada_b):
    L, D, N = ada_w.shape
    B = c.shape[0]
    tn = 1536
    return pl.pallas_call(
        _adaln_kernel,
        grid=(L, N // tn),
        in_specs=[
            pl.BlockSpec((B, D), lambda l, j: (0, 0)),
            pl.BlockSpec((1, D, tn), lambda l, j: (l, 0, j)),
            pl.BlockSpec((1, 1, tn), lambda l, j: (l, 0, j)),
        ],
        out_specs=pl.BlockSpec((1, B, tn), lambda l, j: (l, 0, j)),
        out_shape=jax.ShapeDtypeStruct((L, B, N), F32),
        compiler_params=pltpu.CompilerParams(
            dimension_semantics=("arbitrary", "arbitrary"), vmem_limit_bytes=VMEM_LIMIT),
    )(c, ada_w, ada_b.reshape(L, 1, N))


def _rope_tab_kernel(pos_ref, freq_ref, cos_ref, sin_ref):
    ang = freq_ref[...] * pos_ref[0]
    cos_ref[0] = jnp.cos(ang)
    sin_ref[0] = jnp.sin(ang)


def _rope_tables(positions):
    B, S = positions.shape
    half = HALF_ROPE
    inv_freq = ROPE_THETA ** (-jnp.arange(half, dtype=F32) / half)
    pos = positions.astype(F32).reshape(B, 1, S)
    cos_t, sin_t = pl.pallas_call(
        _rope_tab_kernel,
        grid=(B,),
        in_specs=[
            pl.BlockSpec((1, 1, S), lambda b: (b, 0, 0)),
            pl.BlockSpec((half, 1), lambda b: (0, 0)),
        ],
        out_specs=[pl.BlockSpec((1, half, S), lambda b: (b, 0, 0))] * 2,
        out_shape=[jax.ShapeDtypeStruct((B, half, S), F32)] * 2,
        compiler_params=pltpu.CompilerParams(dimension_semantics=("arbitrary",)),
    )(pos, inv_freq.reshape(half, 1))
    cos = jnp.transpose(cos_t, (0, 2, 1))
    sin = jnp.transpose(sin_t, (0, 2, 1))
    ones = jnp.ones((B, S, MLA_NOPE), F32)
    z_lo = jnp.zeros((B, S, MLA_NOPE), F32)
    z_hi = jnp.zeros((B, S, LANES - MLA_NOPE - MLA_ROPE), F32)
    ctab = jnp.concatenate([ones, cos, cos, z_hi], axis=-1)
    stab = jnp.concatenate([z_lo, -sin, sin, z_hi], axis=-1)
    return ctab, stab


def _take_cols(w, idx):
    pieces, i, n = [], 0, len(idx)
    while i < n:
        j = i + 1
        if idx[i] < 0:
            while j < n and idx[j] < 0:
                j += 1
            pieces.append(jnp.zeros((w.shape[0], j - i), w.dtype))
        else:
            while j < n and idx[j] == idx[j - 1] + 1:
                j += 1
            pieces.append(w[:, int(idx[i]):int(idx[i]) + (j - i)])
        i = j
    return jnp.concatenate(pieces, axis=1)


def _perm_w_in():
    cq = 0
    ckv = cq + MLA_Q_LORA
    kr = ckv + MLA_KV_LORA
    fq = kr + MLA_ROPE
    fk = fq + FOX_W
    fv = fk + FOX_W
    fl = fv + FOX_W
    idx = -np.ones((C_END,), np.int64)
    idx[C_Q0:C_Q0 + MLA_Q_LORA] = cq + np.arange(MLA_Q_LORA)
    idx[C_KV0:C_KV0 + MLA_KV_LORA] = ckv + np.arange(MLA_KV_LORA)
    h = HALF_ROPE
    idx[C_KRA0 + ROPE_LO:C_KRA0 + ROPE_LO + MLA_ROPE] = kr + np.arange(MLA_ROPE)
    idx[C_KRA0 + CTRL_LOGIT:C_KRA0 + CTRL_LOGIT + FOX_HEADS] = fl + np.arange(FOX_HEADS)
    idx[C_KRB0 + ROPE_LO:C_KRB0 + ROPE_LO + h] = kr + h + np.arange(h)
    idx[C_KRB0 + ROPE_LO + h:C_KRB0 + ROPE_LO + MLA_ROPE] = kr + np.arange(h)
    idx[C_FQ0:C_FQ0 + FOX_W] = fq + np.arange(FOX_W)
    idx[C_FK0:C_FK0 + FOX_W] = fk + np.arange(FOX_W)
    idx[C_FV0:C_FV0 + FOX_W] = fv + np.arange(FOX_W)
    return idx


def _perm_w_uq():
    per = MLA_NOPE + MLA_ROPE
    ia = -np.ones((MLA_HEADS * LANES,), np.int64)
    ib = -np.ones((MLA_HEADS * LANES,), np.int64)
    h2 = HALF_ROPE
    for h in range(MLA_HEADS):
        ia[h * LANES:h * LANES + per] = h * per + np.arange(per)
        ib[h * LANES + ROPE_LO:h * LANES + ROPE_LO + h2] = h * per + MLA_NOPE + h2 + np.arange(h2)
        ib[h * LANES + ROPE_LO + h2:h * LANES + ROPE_LO + MLA_ROPE] = h * per + MLA_NOPE + np.arange(h2)
    return ia, ib


def _perm_w_ukv():
    per = MLA_NOPE + MLA_V
    ik = -np.ones((MLA_HEADS * LANES,), np.int64)
    iv = np.zeros((MLA_HEADS * MLA_V,), np.int64)
    for h in range(MLA_HEADS):
        ik[h * LANES:h * LANES + MLA_NOPE] = h * per + np.arange(MLA_NOPE)
        iv[h * MLA_V:(h + 1) * MLA_V] = h * per + MLA_NOPE + np.arange(MLA_V)
    return np.concatenate([ik, iv])


def _aug_placement():
    e = np.zeros((LANES, 2 * FOX_HEADS * LANES), np.float32)
    koff = FOX_HEADS * LANES
    for h in range(FOX_HEADS):
        for p in range(3):
            src = CTRL_LOGIT + 8 * p + h
            e[src, h * LANES + AUG_LO + p] = 1.0
            e[CTRL_ONE, h * LANES + AUG_LO + 3 + p] = -1.0
            e[CTRL_ONE, koff + h * LANES + AUG_LO + p] = 1.0
            e[src, koff + h * LANES + AUG_LO + 3 + p] = 1.0
    return e


def _inproj_kernel(x_ref, mod_ref, g_ref, win_ref, qg_ref, kvg_ref, wuqa_ref, wuqb_ref, wukv_ref,
                   fb_ref, ct_ref, st_ref, tri_ref, eqk_ref,
                   qa_ref, ka_ref, vt_ref, carry_ref, *, ts):
    si = pl.program_id(1)

    @pl.when(si == 0)
    def _():
        carry_ref[...] = jnp.zeros_like(carry_ref)

    x = x_ref[0]
    shift = mod_ref[0, 0:1, :]
    scale = mod_ref[0, 1:2, :]
    h = ((_rms(x) * g_ref[...]) * (1.0 + scale) + shift).astype(BF16)
    proj = jnp.dot(h, win_ref[...], preferred_element_type=F32)

    ctab = ct_ref[0]
    stab = st_ref[0]
    lane = lax.broadcasted_iota(jnp.int32, (ts, LANES), 1)

    c_q = proj[:, C_Q0:C_Q0 + MLA_Q_LORA]
    cqn = (_rms(c_q) * qg_ref[...]).astype(BF16)
    qa = jnp.dot(cqn, wuqa_ref[...], preferred_element_type=F32)
    qb = jnp.dot(cqn, wuqb_ref[...], preferred_element_type=F32)
    mla_scale = (MLA_NOPE + MLA_ROPE) ** -0.5 * LOG2E
    for hh in range(MLA_HEADS):
        sl = slice(hh * LANES, (hh + 1) * LANES)
        qa_ref[0, hh] = ((qa[:, sl] * ctab + qb[:, sl] * stab) * mla_scale).astype(BF16)

    c_kv = proj[:, C_KV0:C_KV0 + MLA_KV_LORA]
    ckvn = (_rms(c_kv) * kvg_ref[...]).astype(BF16)
    kv = jnp.dot(ckvn, wukv_ref[...], preferred_element_type=F32)
    kra = proj[:, C_KRA0:C_KRA0 + LANES]
    krb = proj[:, C_KRB0:C_KRB0 + LANES]
    krope = kra * ctab + krb * stab
    for hh in range(MLA_HEADS):
        ka_ref[0, hh] = (kv[:, hh * LANES:(hh + 1) * LANES] + krope).astype(BF16)
    v_mla = kv[:, MLA_HEADS * LANES:]

    ctrl = (lane >= CTRL_LOGIT) & (lane < CTRL_LOGIT + FOX_HEADS)
    fl = kra + fb_ref[...]
    lsig = jnp.minimum(fl, 0.0) - jnp.log1p(jnp.exp(-jnp.abs(fl)))
    lf = jnp.where(ctrl, lsig, 0.0)
    hi, mid, lo = _split3(lf)
    p1 = (hi + pltpu.roll(mid, 8, 1) + pltpu.roll(lo, 16, 1)).astype(BF16)
    cs = jnp.dot(tri_ref[...], p1, preferred_element_type=F32)
    cs = cs + pltpu.roll(cs, LANES - 8, 1) + pltpu.roll(cs, LANES - 16, 1)
    fcum = jnp.where(ctrl, cs, 0.0) + carry_ref[...]
    carry_ref[...] = fcum[ts - 1:ts, :]

    hi, mid, lo = _split3(fcum * LOG2E)
    p2 = hi + pltpu.roll(mid, 8, 1) + pltpu.roll(lo, 16, 1)
    p2 = jnp.where(lane == CTRL_ONE, 1.0, p2).astype(BF16)
    aug = jnp.dot(p2, eqk_ref[...], preferred_element_type=F32)

    fq = proj[:, C_FQ0:C_FQ0 + FOX_W]
    fk = proj[:, C_FK0:C_FK0 + FOX_W]
    low = lane < FOX_HEAD_DIM
    koff = FOX_HEADS * LANES
    for j in range(FOX_HEADS // 2):
        sl = slice(j * LANES, (j + 1) * LANES)
        for src, off, dst in ((fq, 0, qa_ref), (fk, koff, ka_ref)):
            blk = src[:, sl] * LOG2E if dst is qa_ref else src[:, sl]
            blk_r = pltpu.roll(blk, FOX_HEAD_DIM, 1)
            h0 = 2 * j
            a0 = aug[:, off + h0 * LANES:off + (h0 + 1) * LANES]
            a1 = aug[:, off + (h0 + 1) * LANES:off + (h0 + 2) * LANES]
            dst[0, MLA_HEADS + h0] = jnp.where(low, blk, a0).astype(BF16)
            dst[0, MLA_HEADS + h0 + 1] = jnp.where(low, blk_r, a1).astype(BF16)

    fv = proj[:, C_FV0:C_FV0 + FOX_W]
    vt_ref[0, 0, 0:MLA_HEADS * MLA_V, :] = v_mla.T.astype(BF16)
    vt_ref[0, 0, MLA_HEADS * MLA_V:, :] = fv.T.astype(BF16)


def _inproj(x, mod_l, g, win, qg, kvg, wuqa, wuqb, wukv, fb_row, ctab, stab, tri, eqk, *, ts):
    B, S, D = x.shape
    nst = S // ts
    const2 = lambda b, s: (0, 0)
    kern = functools.partial(_inproj_kernel, ts=ts)
    return pl.pallas_call(
        kern,
        grid=(B, nst),
        in_specs=[
            pl.BlockSpec((1, ts, D), lambda b, s: (b, s, 0)),
            pl.BlockSpec((1, 6, D), lambda b, s: (b, 0, 0)),
            pl.BlockSpec((1, D), const2),
            pl.BlockSpec(win.shape, const2),
            pl.BlockSpec(qg.shape, const2),
            pl.BlockSpec(kvg.shape, const2),
            pl.BlockSpec(wuqa.shape, const2),
            pl.BlockSpec(wuqb.shape, const2),
            pl.BlockSpec(wukv.shape, const2),
            pl.BlockSpec(fb_row.shape, const2),
            pl.BlockSpec((1, ts, LANES), lambda b, s: (b, s, 0)),
            pl.BlockSpec((1, ts, LANES), lambda b, s: (b, s, 0)),
            pl.BlockSpec(tri.shape, const2),
            pl.BlockSpec(eqk.shape, const2),
        ],
        out_specs=[
            pl.BlockSpec((1, N_HEADS, ts, LANES), lambda b, s: (b, 0, s, 0)),
            pl.BlockSpec((1, N_HEADS, ts, LANES), lambda b, s: (b, 0, s, 0)),
            pl.BlockSpec((1, 1, N_HEADS * MLA_V, ts), lambda b, s: (b, s, 0, 0)),
        ],
        out_shape=[
            jax.ShapeDtypeStruct((B, N_HEADS, S, LANES), BF16),
            jax.ShapeDtypeStruct((B, N_HEADS, S, LANES), BF16),
            jax.ShapeDtypeStruct((B, nst, N_HEADS * MLA_V, ts), BF16),
        ],
        scratch_shapes=[pltpu.VMEM((1, LANES), F32)],
        compiler_params=pltpu.CompilerParams(
            dimension_semantics=("arbitrary", "arbitrary"), vmem_limit_bytes=VMEM_LIMIT),
    )(x, mod_l, g, win, qg, kvg, wuqa, wuqb, wukv, fb_row, ctab, stab, tri, eqk)


def _colmax8(st, groups):
    tk, tq = st.shape
    v = st.reshape(groups, tk // (8 * groups), 8, tq)
    return jnp.max(jnp.max(v, axis=1), axis=0)


def _attn_kernel(q_ref, k_ref, vt_ref, o_ref, s_scr, tmax_scr, m_scr, acc_scr,
                 *, tq, tk, hp, nq):
    dv = MLA_V
    groups = 4
    ones = jnp.ones((ONES_ROWS, tk), BF16)

    def qk(qi, j):
        q0 = pl.multiple_of(qi * tq, tq)
        return [lax.dot_general(k_ref[0, h, pl.ds(pl.multiple_of(j * tk, tk), tk), :],
                                q_ref[0, h, pl.ds(q0, tq), :], (((1,), (1,)), ((), ())),
                                preferred_element_type=F32) for h in range(hp)]

    def park(sts, qi, j, masked):
        for h in range(hp):
            st = sts[h]
            if masked:
                kpos = j * tk + lax.broadcasted_iota(jnp.int32, (tk, tq), 0)
                qpos = qi * tq + lax.broadcasted_iota(jnp.int32, (tk, tq), 1)
                st = jnp.where(kpos <= qpos, st, NEG_BIG)
            s_scr[h] = st
            tmax_scr[h] = _colmax8(st, groups)

    def update(j):
        for h in range(hp):
            m = m_scr[h]
            m_new = jnp.maximum(m, jnp.max(tmax_scr[h], axis=0, keepdims=True))
            alpha = jnp.exp2(m - m_new)
            p = jnp.exp2((s_scr[h] - m_new).astype(BF16))
            vt1 = jnp.concatenate([vt_ref[0, j, h * dv:(h + 1) * dv, :], ones], axis=0)
            acc_scr[h] = alpha * acc_scr[h] + jnp.dot(vt1, p, preferred_element_type=F32)
            m_scr[h] = m_new

    def reset_state():
        m_scr[...] = jnp.full(m_scr.shape, NEG_BIG, F32)
        acc_scr[...] = jnp.zeros(acc_scr.shape, F32)

    def finish(qi):
        q0 = pl.multiple_of(qi * tq, tq)
        for h in range(hp):
            acc = acc_scr[h]
            o_ref[0, h * dv:(h + 1) * dv, pl.ds(q0, tq)] = (
                acc[0:dv, :] / acc[dv:dv + 1, :]).astype(o_ref.dtype)
        reset_state()

    reset_state()
    park(qk(0, 0), 0, 0, True)

    def q_tile(qi, c):
        nf = (qi * tq) // tk

        def below_diagonal(t, c2):
            sts = qk(qi, t + 1)
            update(t)
            park(sts, qi, t + 1, False)
            return c2

        lax.fori_loop(0, nf - 1, below_diagonal, 0)

        @pl.when(nf > 0)
        def _():
            sts = qk(qi, nf)
            update(nf - 1)
            park(sts, qi, nf, True)

        nxt = qi + 1
        nxt_on_diagonal = (nxt * tq) // tk == 0

        @pl.when((nxt < nq) & nxt_on_diagonal)
        def _():
            sts = qk(nxt, 0)
            update(nf)
            finish(qi)
            park(sts, nxt, 0, True)

        @pl.when((nxt < nq) & jnp.logical_not(nxt_on_diagonal))
        def _():
            sts = qk(nxt, 0)
            update(nf)
            finish(qi)
            park(sts, nxt, 0, False)

        @pl.when(nxt == nq)
        def _():
            update(nf)
            finish(qi)

        return c

    lax.fori_loop(0, nq, q_tile, 0)


def _attention(qa, ka, vt, *, tq, tk, hp):
    B, H, S, _ = qa.shape
    nkt = vt.shape[1]
    assert S % tq == 0 and nkt * tk == S and H % hp == 0 and tk % tq == 0
    kern = functools.partial(_attn_kernel, tq=tq, tk=tk, hp=hp, nq=S // tq)
    return pl.pallas_call(
        kern,
        grid=(B, H // hp),
        in_specs=[
            pl.BlockSpec((1, hp, S, LANES), lambda b, g: (b, g, 0, 0)),
            pl.BlockSpec((1, hp, S, LANES), lambda b, g: (b, g, 0, 0)),
            pl.BlockSpec((1, nkt, hp * MLA_V, tk), lambda b, g: (b, 0, g, 0)),
        ],
        out_specs=pl.BlockSpec((1, hp * MLA_V, S), lambda b, g: (b, g, 0)),
        out_shape=jax.ShapeDtypeStruct((B, H * MLA_V, S), BF16),
        scratch_shapes=[
            pltpu.VMEM((hp, tk, tq), F32),
            pltpu.VMEM((hp, 8, tq), F32),
            pltpu.VMEM((hp, 1, tq), F32),
            pltpu.VMEM((hp, MLA_V + ONES_ROWS, tq), F32),
        ],
        compiler_params=pltpu.CompilerParams(
            dimension_semantics=("arbitrary", "arbitrary"), vmem_limit_bytes=VMEM_LIMIT),
    )(qa, ka, vt)


def _top2_gates(logits, lane):
    lg = jnp.where(lane < N_EXPERTS, logits, NEG_BIG)
    m1 = jnp.max(lg, axis=1, keepdims=True)
    i1 = jnp.min(jnp.where(lg == m1, lane, LANES), axis=1, keepdims=True)
    lg2 = jnp.where(lane == i1, NEG_BIG, lg)
    m2 = jnp.max(lg2, axis=1, keepdims=True)
    i2 = jnp.min(jnp.where(lg2 == m2, lane, LANES), axis=1, keepdims=True)
    e2 = jnp.exp(m2 - m1)
    den = 1.0 + e2
    return jnp.where(lane == i1, 1.0 / den, 0.0) + jnp.where(lane == i2, e2 / den, 0.0)


def _outproj_kernel(*refs, ts, with_router):
    if with_router:
        (ot_ref, x_ref, mod_ref, mg_ref, fg_ref, wo_ref, ng_ref, rw_ref,
         x1_ref, h2_ref, gates_ref) = refs
    else:
        ot_ref, x_ref, mod_ref, mg_ref, fg_ref, wo_ref, ng_ref, x1_ref, h2_ref = refs
    half = MLA_HEADS * MLA_V
    om = ot_ref[0, 0:half, :].astype(F32).T
    of = ot_ref[0, half:, :].astype(F32).T
    on = jnp.concatenate([_rms(om) * mg_ref[...], _rms(of) * fg_ref[...]], axis=1).astype(BF16)
    mix = jnp.dot(on, wo_ref[...], preferred_element_type=F32)
    x1 = x_ref[0] + mod_ref[0, 2:3, :] * mix
    x1_ref[0] = x1
    h2 = ((_rms(x1) * ng_ref[...]) * (1.0 + mod_ref[0, 4:5, :]) + mod_ref[0, 3:4, :]).astype(BF16)
    h2_ref[0] = h2
    if with_router:
        logits = jnp.dot(h2, rw_ref[...], preferred_element_type=F32)
        lane = lax.broadcasted_iota(jnp.int32, (ts, LANES), 1)
        gates_ref[0] = _top2_gates(logits, lane)


def _outproj(ot, x, mod_l, mg, fg, wo, ng, rw, *, ts):
    B, S, D = x.shape
    with_router = rw is not None
    const2 = lambda b, s: (0, 0)
    in_specs = [
        pl.BlockSpec((1, ot.shape[1], ts), lambda b, s: (b, 0, s)),
        pl.BlockSpec((1, ts, D), lambda b, s: (b, s, 0)),
        pl.BlockSpec((1, 6, D), lambda b, s: (b, 0, 0)),
        pl.BlockSpec(mg.shape, const2),
        pl.BlockSpec(fg.shape, const2),
        pl.BlockSpec(wo.shape, const2),
        pl.BlockSpec(ng.shape, const2),
    ]
    args = [ot, x, mod_l, mg, fg, wo, ng]
    out_specs = [pl.BlockSpec((1, ts, D), lambda b, s: (b, s, 0)),
                 pl.BlockSpec((1, ts, D), lambda b, s: (b, s, 0))]
    out_shape = [jax.ShapeDtypeStruct((B, S, D), F32), jax.ShapeDtypeStruct((B, S, D), BF16)]
    if with_router:
        in_specs.append(pl.BlockSpec(rw.shape, const2))
        args.append(rw)
        out_specs.append(pl.BlockSpec((1, ts, LANES), lambda b, s: (b, s, 0)))
        out_shape.append(jax.ShapeDtypeStruct((B, S, LANES), F32))
    kern = functools.partial(_outproj_kernel, ts=ts, with_router=with_router)
    return pl.pallas_call(
        kern,
        grid=(B, S // ts),
        in_specs=in_specs,
        out_specs=out_specs,
        out_shape=out_shape,
        compiler_params=pltpu.CompilerParams(
            dimension_semantics=("arbitrary", "arbitrary"), vmem_limit_bytes=VMEM_LIMIT),
    )(*args)


def _swiglu_tile(x, wg, wu):
    g = jnp.dot(x, wg, preferred_element_type=F32)
    u = jnp.dot(x, wu, preferred_element_type=F32)
    return g * _sigmoid(g) * u


def _ffn_kernel(h_ref, wg_ref, wu_ref, wd_ref, x_ref, mod_ref, fin_ref, o_ref, acc_ref,
                *, final_norm):
    f = pl.program_id(1)

    @pl.when(f == 0)
    def _():
        acc_ref[...] = jnp.zeros_like(acc_ref)

    a = _swiglu_tile(h_ref[...], wg_ref[...], wu_ref[...])
    acc_ref[...] += jnp.dot(a.astype(BF16), wd_ref[...], preferred_element_type=F32)

    @pl.when(f == pl.num_programs(1) - 1)
    def _():
        x2 = x_ref[...] + mod_ref[0, 5:6, :] * acc_ref[...]
        if final_norm:
            x2 = _rms(x2) * fin_ref[...]
        o_ref[...] = x2


def _ffn(h2, wg, wu, wd, x1, mod_l, fin_g, *, tm, tf, seq, final_norm):
    T, D = h2.shape
    F = wd.shape[0]
    tiles_per_batch = seq // tm
    kern = functools.partial(_ffn_kernel, final_norm=final_norm)
    return pl.pallas_call(
        kern,
        grid=(T // tm, F // tf),
        in_specs=[
            pl.BlockSpec((tm, D), lambda i, f: (i, 0)),
            pl.BlockSpec((D, tf), lambda i, f: (0, f)),
            pl.BlockSpec((D, tf), lambda i, f: (0, f)),
            pl.BlockSpec((tf, D), lambda i, f: (f, 0)),
            pl.BlockSpec((tm, D), lambda i, f: (i, 0)),
            pl.BlockSpec((1, 6, D), lambda i, f: (i // tiles_per_batch, 0, 0)),
            pl.BlockSpec((1, D), lambda i, f: (0, 0)),
        ],
        out_specs=pl.BlockSpec((tm, D), lambda i, f: (i, 0)),
        out_shape=jax.ShapeDtypeStruct((T, D), F32),
        scratch_shapes=[pltpu.VMEM((tm, D), F32)],
        compiler_params=pltpu.CompilerParams(
            dimension_semantics=("arbitrary", "arbitrary"), vmem_limit_bytes=VMEM_LIMIT),
    )(h2, wg, wu, wd, x1, mod_l, fin_g)


RC = 128
VT = 8


def _route_kernel(gates_ref, rank_ref, rankt_ref, gatest_ref, cnt_ref, *, tm):
    gates = gates_ref[...]
    sel = gates > 0.0
    self32 = jnp.where(sel, 1.0, 0.0)
    earlier = (lax.broadcasted_iota(jnp.int32, (tm, tm), 1)
               < lax.broadcasted_iota(jnp.int32, (tm, tm), 0))
    rank = jnp.dot(jnp.where(earlier, 1.0, 0.0).astype(BF16), self32.astype(BF16),
                   preferred_element_type=F32)
    rank = jnp.where(sel, rank, -1.0)
    rank_ref[...] = rank
    rankt_ref[0] = rank.T
    gatest_ref[0] = gates.T
    cnt_ref[0] = jnp.sum(self32, axis=0, keepdims=True)


def _route(gates, *, tm):
    T = gates.shape[0]
    nt = T // tm
    return pl.pallas_call(
        functools.partial(_route_kernel, tm=tm),
        grid=(nt,),
        in_specs=[pl.BlockSpec((tm, LANES), lambda i: (i, 0))],
        out_specs=[
            pl.BlockSpec((tm, LANES), lambda i: (i, 0)),
            pl.BlockSpec((1, LANES, tm), lambda i: (i, 0, 0)),
            pl.BlockSpec((1, LANES, tm), lambda i: (i, 0, 0)),
            pl.BlockSpec((1, 1, LANES), lambda i: (i, 0, 0)),
        ],
        out_shape=[
            jax.ShapeDtypeStruct((T, LANES), F32),
            jax.ShapeDtypeStruct((nt, LANES, tm), F32),
            jax.ShapeDtypeStruct((nt, LANES, tm), F32),
            jax.ShapeDtypeStruct((nt, 1, LANES), F32),
        ],
        compiler_params=pltpu.CompilerParams(
            dimension_semantics=("arbitrary",), vmem_limit_bytes=VMEM_LIMIT),
    )(gates)


def _slot_tables(cnt, nt, ns, n_vt):
    E = N_EXPERTS
    nch = (cnt + (RC - 1)) // RC
    nct = jnp.sum(nch, axis=1)
    ccum = jnp.cumsum(nch, axis=1)
    j = jnp.arange(ns, dtype=jnp.int32)
    slot_e = jnp.sum((ccum[:, None, :] <= j[None, :, None]).astype(jnp.int32), axis=2)
    slot_e = jnp.minimum(slot_e, E - 1)
    first = jnp.take_along_axis(ccum - nch, slot_e, axis=1)
    valid = j[None, :] < nct[:, None]
    slot_k = jnp.where(valid, j[None, :] - first, 0)
    nce = jnp.sum(nch, axis=0)
    padded = (nce + (VT - 1)) // VT * VT
    base = jnp.cumsum(padded) - padded
    cid0 = base[None, :] + jnp.cumsum(nch, axis=0) - nch
    cid = jnp.take_along_axis(cid0, slot_e, axis=1) + slot_k
    cid = jnp.where(valid, cid, cid[:, 0:1])
    n_chunks = n_vt * VT
    slot_id = jnp.arange(nt, dtype=jnp.int32)[:, None] * ns + j[None, :]
    pos = jnp.full((n_chunks,), -1, jnp.int32).at[
        jnp.where(valid, cid, n_chunks).reshape(-1)].set(slot_id.reshape(-1), mode="drop")
    vcum = jnp.cumsum(padded // VT)
    nv = vcum[-1]
    v = jnp.minimum(jnp.arange(n_vt, dtype=jnp.int32), nv - 1)
    v_exp = jnp.minimum(jnp.searchsorted(vcum, v, side="right"), E - 1).astype(jnp.int32)
    i32 = lambda a: a.reshape(-1).astype(jnp.int32)
    return dict(slot_e=i32(slot_e), slot_k=i32(slot_k), nct=i32(nct), cid=i32(cid),
                pos=jnp.maximum(pos, 0), real=(pos >= 0).astype(jnp.int32),
                v_exp=v_exp, nv=i32(nv))


def _compact_kernel(se_ref, sk_ref, nct_ref, h_ref, rankt_ref, gatest_ref, xs_ref, gc_ref, oh_scr,
                    *, tm, ns):
    t = pl.program_id(0)
    n = nct_ref[t]

    def build(j, c):
        e = se_ref[t * ns + j]
        rrow = rankt_ref[0, pl.ds(e, 1), :].astype(jnp.int32)
        grow = gatest_ref[0, pl.ds(e, 1), :]
        row = lax.broadcasted_iota(jnp.int32, (RC, tm), 0) + sk_ref[t * ns + j] * RC
        hit = (rrow == row) & (j < n)
        oh_scr[j] = jnp.where(hit, 1.0, 0.0).astype(BF16)
        gcol = jnp.sum(jnp.where(hit, grow, 0.0), axis=1, keepdims=True)
        gc_ref[pl.ds(pl.multiple_of(j * RC, RC), RC), :] = jnp.broadcast_to(gcol, (RC, LANES))
        return c

    lax.fori_loop(0, ns, build, 0)
    rows = VT * RC
    for b in range(ns // VT):
        oh = oh_scr[b * VT:(b + 1) * VT].reshape(rows, tm)
        xs_ref[b * rows:(b + 1) * rows, :] = jnp.dot(
            oh, h_ref[...], preferred_element_type=F32).astype(BF16)


def _compact(tabs, h2, rankt, gatest, *, tm, ns):
    T, D = h2.shape
    nt = T // tm
    grid_spec = pltpu.PrefetchScalarGridSpec(
        num_scalar_prefetch=3,
        grid=(nt,),
        in_specs=[
            pl.BlockSpec((tm, D), lambda t, *_: (t, 0)),
            pl.BlockSpec((1, LANES, tm), lambda t, *_: (t, 0, 0)),
            pl.BlockSpec((1, LANES, tm), lambda t, *_: (t, 0, 0)),
        ],
        out_specs=[
            pl.BlockSpec((ns * RC, D), lambda t, *_: (t, 0)),
            pl.BlockSpec((ns * RC, LANES), lambda t, *_: (t, 0)),
        ],
        scratch_shapes=[pltpu.VMEM((ns, RC, tm), BF16)],
    )
    return pl.pallas_call(
        functools.partial(_compact_kernel, tm=tm, ns=ns),
        grid_spec=grid_spec,
        out_shape=[jax.ShapeDtypeStruct((nt * ns * RC, D), BF16),
                   jax.ShapeDtypeStruct((nt * ns * RC, LANES), F32)],
        compiler_params=pltpu.CompilerParams(
            dimension_semantics=("arbitrary",), vmem_limit_bytes=VMEM_LIMIT),
    )(tabs["slot_e"], tabs["slot_k"], tabs["nct"], h2, rankt, gatest)


def _slot_experts_kernel(vexp_ref, nv_ref, pos_ref, real_ref, *refs):
    xs_refs = refs[0:VT]
    gc_refs = refs[VT:2 * VT]
    wg_ref, wu_ref, wd_ref, ys_ref, x_scr, g_scr, acc_ref = refs[2 * VT:]
    v = pl.program_id(0)
    f = pl.program_id(1)

    @pl.when(v < nv_ref[0])
    def _():
        @pl.when(f == 0)
        def _():
            for j in range(VT):
                keep = real_ref[v * VT + j] == 1
                x_scr[j * RC:(j + 1) * RC, :] = jnp.where(keep, xs_refs[j][...], jnp.zeros_like(xs_refs[j]))
                g_scr[j * RC:(j + 1) * RC, :] = jnp.where(keep, gc_refs[j][...], 0.0)
            acc_ref[...] = jnp.zeros_like(acc_ref)

        a = _swiglu_tile(x_scr[...], wg_ref[0], wu_ref[0]) * g_scr[:, 0:1]
        acc_ref[...] += jnp.dot(a.astype(BF16), wd_ref[0], preferred_element_type=F32)

        @pl.when(f == pl.num_programs(1) - 1)
        def _():
            ys_ref[...] = acc_ref[...].astype(BF16)


def _slot_experts(tabs, xs, gc, wg, wu, wd, *, tf, n_vt):
    D = xs.shape[1]
    NF = wd.shape[1] // tf
    rows = VT * RC

    def live(v, nv):
        return jnp.minimum(v, nv[0] - 1)

    def hidden(v, f, nv):
        return jnp.where(v < nv[0], f, NF - 1)

    def chunk(j):
        return lambda v, f, vexp, nv, pos, real: (pos[live(v, nv) * VT + j], 0)

    grid_spec = pltpu.PrefetchScalarGridSpec(
        num_scalar_prefetch=4,
        grid=(n_vt, NF),
        in_specs=(
            [pl.BlockSpec((RC, D), chunk(j)) for j in range(VT)]
            + [pl.BlockSpec((RC, LANES), chunk(j)) for j in range(VT)]
            + [pl.BlockSpec((1, D, tf),
                            lambda v, f, vexp, nv, pos, real: (vexp[v], 0, hidden(v, f, nv)))] * 2
            + [pl.BlockSpec((1, tf, D),
                            lambda v, f, vexp, nv, pos, real: (vexp[v], hidden(v, f, nv), 0))]),
        out_specs=pl.BlockSpec((rows, D), lambda v, f, vexp, nv, pos, real: (live(v, nv), 0)),
        scratch_shapes=[pltpu.VMEM((rows, D), BF16), pltpu.VMEM((rows, LANES), F32),
                        pltpu.VMEM((rows, D), F32)],
    )
    return pl.pallas_call(
        _slot_experts_kernel,
        grid_spec=grid_spec,
        out_shape=jax.ShapeDtypeStruct((n_vt * rows, D), BF16),
        compiler_params=pltpu.CompilerParams(
            dimension_semantics=("arbitrary", "arbitrary"), vmem_limit_bytes=VMEM_LIMIT),
    )(tabs["v_exp"], tabs["nv"], tabs["pos"], tabs["real"], *([xs] * VT), *([gc] * VT), wg, wu, wd)


def _uncompact_kernel(se_ref, sk_ref, nct_ref, cid_ref, *refs, tm, ns, final_norm):
    ys_refs = refs[0:ns]
    rank_ref, x_ref, mod_ref, fin_ref, o_ref, oh_scr = refs[ns:]
    t = pl.program_id(0)
    n = nct_ref[t]
    lane = lax.broadcasted_iota(jnp.int32, (tm, LANES), 1)
    col = lax.broadcasted_iota(jnp.int32, (tm, RC), 1)
    rank = rank_ref[...]
    for j in range(ns):
        rcol = jnp.sum(jnp.where(lane == se_ref[t * ns + j], rank, 0.0), axis=1,
                       keepdims=True).astype(jnp.int32)
        hit = (rcol == col + sk_ref[t * ns + j] * RC) & (j < n)
        oh_scr[:, j * RC:(j + 1) * RC] = jnp.where(hit, 1.0, 0.0).astype(BF16)
    y = None
    for b in range(ns // VT):
        ys = jnp.concatenate([ys_refs[b * VT + j][...] for j in range(VT)], axis=0)
        part = jnp.dot(oh_scr[:, b * VT * RC:(b + 1) * VT * RC], ys, preferred_element_type=F32)
        y = part if y is None else y + part
    x2 = x_ref[...] + mod_ref[0, 5:6, :] * y
    if final_norm:
        x2 = _rms(x2) * fin_ref[...]
    o_ref[...] = x2


def _uncompact(tabs, ys, rank, x1, mod_l, fin_g, *, tm, ns, seq, final_norm):
    T, D = x1.shape
    tpb = seq // tm

    def chunk(j):
        return lambda t, se, sk, nct, cid: (cid[t * ns + j], 0)

    grid_spec = pltpu.PrefetchScalarGridSpec(
        num_scalar_prefetch=4,
        grid=(T // tm,),
        in_specs=(
            [pl.BlockSpec((RC, D), chunk(j)) for j in range(ns)]
            + [pl.BlockSpec((tm, LANES), lambda t, *_: (t, 0)),
               pl.BlockSpec((tm, D), lambda t, *_: (t, 0)),
               pl.BlockSpec((1, 6, D), lambda t, *_: (t // tpb, 0, 0)),
               pl.BlockSpec((1, D), lambda t, *_: (0, 0))]),
        out_specs=pl.BlockSpec((tm, D), lambda t, *_: (t, 0)),
        scratch_shapes=[pltpu.VMEM((tm, ns * RC), BF16)],
    )
    return pl.pallas_call(
        functools.partial(_uncompact_kernel, tm=tm, ns=ns, final_norm=final_norm),
        grid_spec=grid_spec,
        out_shape=jax.ShapeDtypeStruct((T, D), F32),
        compiler_params=pltpu.CompilerParams(
            dimension_semantics=("arbitrary",), vmem_limit_bytes=VMEM_LIMIT),
    )(tabs["slot_e"], tabs["slot_k"], tabs["nct"], tabs["cid"], *([ys] * ns),
      rank, x1, mod_l, fin_g)


def _routed_experts(h2, gates, w_gate, w_up, w_down, x1, mod_l, fin_g, *, tm, tf, seq, final_norm):
    T, D = h2.shape
    E = N_EXPERTS
    nt = T // tm
    ns = 2 * tm // RC + E
    assert ns % VT == 0
    n_vt = -(-(nt * ns + E * (VT - 1)) // VT)
    rank, rankt, gatest, cnt = _route(gates, tm=tm)
    tabs = _slot_tables(cnt[:, 0, :E].astype(jnp.int32), nt, ns, n_vt)
    xs, gc = _compact(tabs, h2, rankt, gatest, tm=tm, ns=ns)
    ys = _slot_experts(tabs, xs, gc, w_gate.astype(BF16), w_up.astype(BF16), w_down.astype(BF16),
                       tf=tf, n_vt=n_vt)
    return _uncompact(tabs, ys, rank, x1, mod_l, fin_g, tm=tm, ns=ns, seq=seq,
                      final_norm=final_norm)


def _tiles(S):
    ts = min(512, S)
    tq = min(512, S)
    tk = ts
    tm = min(1024, S)
    tf = 512
    return ts, tq, tk, tm, tf


def kernel(x, c, positions, ada_w, ada_b, attn_norm_g, w_in, q_norm_g, w_uq, kv_norm_g, w_ukv,
           fox_forget_b, mla_out_g, fox_out_g, w_o, ffn_norm_g, dense_w_gate, dense_w_up,
           dense_w_down, router_w, moe_w_gate, moe_w_up, moe_w_down, final_norm_g):
    B, S, D = x.shape
    L = ada_w.shape[0]
    ts, tq, tk, tm, tf = _tiles(S)

    mod = _adaln(c, ada_w, ada_b).reshape(L, B, 6, D)
    ctab, stab = _rope_tables(positions)

    idx_in = _perm_w_in()
    idx_qa, idx_qb = _perm_w_uq()
    idx_kv = _perm_w_ukv()
    col_scale = np.ones((C_END,), np.float32)
    col_scale[C_FQ0:C_FQ0 + FOX_W] = FOX_HEAD_DIM ** -0.5
    tri = jnp.asarray(np.tril(np.ones((ts, ts), np.float32)), BF16)
    eqk = jnp.asarray(_aug_placement(), BF16)

    for l in range(L):
        win = (_take_cols(w_in[l], idx_in) * col_scale).astype(BF16)
        wuqa = _take_cols(w_uq[l], idx_qa).astype(BF16)
        wuqb = _take_cols(w_uq[l], idx_qb).astype(BF16)
        wukv = _take_cols(w_ukv[l], idx_kv).astype(BF16)
        fb_row = jnp.zeros((1, LANES), F32).at[0, CTRL_LOGIT:CTRL_LOGIT + FOX_HEADS].set(
            fox_forget_b[l].astype(F32))
        qa, ka, vt = _inproj(
            x, mod[l], attn_norm_g[l].reshape(1, D), win, q_norm_g[l].reshape(1, -1),
            kv_norm_g[l].reshape(1, -1), wuqa, wuqb, wukv, fb_row, ctab, stab, tri, eqk, ts=ts)
        ot = _attention(qa, ka, vt, tq=tq, tk=tk, hp=4)

        j = l // 2
        is_moe = (l % 2 == 1)
        rw = None
        if is_moe:
            rw = jnp.pad(router_w[j], ((0, 0), (0, LANES - N_EXPERTS))).astype(BF16)
        outs = _outproj(ot, x, mod[l], mla_out_g[l].reshape(1, -1), fox_out_g[l].reshape(1, -1),
                        w_o[l].astype(BF16), ffn_norm_g[l].reshape(1, D), rw, ts=ts)
        x1, h2 = outs[0].reshape(B * S, D), outs[1].reshape(B * S, D)
        fin_g = final_norm_g.reshape(1, D)
        last = l == L - 1
        if is_moe:
            x = _routed_experts(h2, outs[2].reshape(B * S, LANES), moe_w_gate[j], moe_w_up[j],
                                moe_w_down[j], x1, mod[l], fin_g,
                                tm=tm, tf=tf, seq=S, final_norm=last)
        else:
            x = _ffn(h2, dense_w_gate[j].astype(BF16), dense_w_up[j].astype(BF16),
                     dense_w_down[j].astype(BF16), x1, mod[l], fin_g,
                     tm=tm, tf=tf, seq=S, final_norm=last)
        x = x.reshape(B, S, D)
    return x
```

```python
import functools

import numpy as np
import jax
import jax.numpy as jnp
from jax import lax
from jax.experimental import pallas as pl
from jax.experimental.pallas import tpu as pltpu

F32 = jnp.float32
BF16 = jnp.bfloat16

D_MODEL = 1024
MLA_HEADS = 8
MLA_V = 64
MLA_NOPE = 64
MLA_ROPE = 32
MLA_Q_LORA = 256
MLA_KV_LORA = 128
FOX_HEADS = 8
FOX_HEAD_DIM = 64
FOX_W = FOX_HEADS * FOX_HEAD_DIM
ROPE_THETA = 10000.0
D_FF = 3584
N_EXPERTS = 8
EPS = 1e-6
N_HEADS = MLA_HEADS + FOX_HEADS
HALF_ROPE = MLA_ROPE // 2

LANES = 128
VMEM_LIMIT = 56 * 1024 * 1024

ROPE_LO = MLA_NOPE
AUG_LO = FOX_HEAD_DIM
CTRL_LOGIT = 96
CTRL_ONE = 120

C_Q0 = 0
C_KV0 = C_Q0 + MLA_Q_LORA
C_KRA0 = C_KV0 + MLA_KV_LORA
C_KRB0 = C_KRA0 + LANES
C_FQ0 = C_KRB0 + LANES
C_FK0 = C_FQ0 + FOX_W
C_FV0 = C_FK0 + FOX_W
C_END = C_FV0 + FOX_W

NEG_BIG = -1e30
LOG2E = 1.4426950408889634
ONES_ROWS = 16


def _rms(v):
    return v * lax.rsqrt(jnp.mean(v * v, axis=-1, keepdims=True) + EPS)


def _sigmoid(v):
    return 1.0 / (1.0 + jnp.exp(-v))


def _split3(v):
    hi = v.astype(BF16).astype(F32)
    r = v - hi
    mid = r.astype(BF16).astype(F32)
    lo = (r - mid).astype(BF16).astype(F32)
    return hi, mid, lo


def _adaln_kernel(c_ref, w_ref, b_ref, o_ref):
    c = c_ref[...]
    ca = (c * _sigmoid(c)).astype(BF16)
    o_ref[0] = jnp.dot(ca, w_ref[0].astype(BF16), preferred_element_type=F32) + b_ref[0]


def _adaln(c, ada_w, ada_b):
    L, D, N = ada_w.shape
    B = c.shape[0]
    tn = 1536
    return pl.pallas_call(
        _adaln_kernel,
        grid=(L, N // tn),
        in_specs=[
            pl.BlockSpec((B, D), lambda l, j: (0, 0)),
            pl.BlockSpec((1, D, tn), lambda l, j: (l, 0, j)),
            pl.BlockSpec((1, 1, tn), lambda l, j: (l, 0, j)),
        ],
        out_specs=pl.BlockSpec((1, B, tn), lambda l, j: (l, 0, j)),
        out_shape=jax.ShapeDtypeStruct((L, B, N), F32),
        compiler_params=pltpu.CompilerParams(
            dimension_semantics=("arbitrary", "arbitrary"), vmem_limit_bytes=VMEM_LIMIT),
    )(c, ada_w, ada_b.reshape(L, 1, N))


def _rope_tab_kernel(pos_ref, freq_ref, cos_ref, sin_ref):
    ang = freq_ref[...] * pos_ref[0]
    cos_ref[0] = jnp.cos(ang)
    sin_ref[0] = jnp.sin(ang)


def _rope_tables(positions):
    B, S = positions.shape
    half = HALF_ROPE
    inv_freq = ROPE_THETA ** (-jnp.arange(half, dtype=F32) / half)
    pos = positions.astype(F32).reshape(B, 1, S)
    cos_t, sin_t = pl.pallas_call(
        _rope_tab_kernel,
        grid=(B,),
        in_specs=[
            pl.BlockSpec((1, 1, S), lambda b: (b, 0, 0)),
            pl.BlockSpec((half, 1), lambda b: (0, 0)),
        ],
        out_specs=[pl.BlockSpec((1, half, S), lambda b: (b, 0, 0))] * 2,
        out_shape=[jax.ShapeDtypeStruct((B, half, S), F32)] * 2,
        compiler_params=pltpu.CompilerParams(dimension_semantics=("arbitrary",)),
    )(pos, inv_freq.reshape(half, 1))
    cos = jnp.transpose(cos_t, (0, 2, 1))
    sin = jnp.transpose(sin_t, (0, 2, 1))
    ones = jnp.ones((B, S, MLA_NOPE), F32)
    z_lo = jnp.zeros((B, S, MLA_NOPE), F32)
    z_hi = jnp.zeros((B, S, LANES - MLA_NOPE - MLA_ROPE), F32)
    ctab = jnp.concatenate([ones, cos, cos, z_hi], axis=-1)
    stab = jnp.concatenate([z_lo, -sin, sin, z_hi], axis=-1)
    return ctab, stab


def _take_cols(w, idx):
    pieces, i, n = [], 0, len(idx)
    while i < n:
        j = i + 1
        if idx[i] < 0:
            while j < n and idx[j] < 0:
                j += 1
            pieces.append(jnp.zeros((w.shape[0], j - i), w.dtype))
        else:
            while j < n and idx[j] == idx[j - 1] + 1:
                j += 1
            pieces.append(w[:, int(idx[i]):int(idx[i]) + (j - i)])
        i = j
    return jnp.concatenate(pieces, axis=1)


def _perm_w_in():
    cq = 0
    ckv = cq + MLA_Q_LORA
    kr = ckv + MLA_KV_LORA
    fq = kr + MLA_ROPE
    fk = fq + FOX_W
    fv = fk + FOX_W
    fl = fv + FOX_W
    idx = -np.ones((C_END,), np.int64)
    idx[C_Q0:C_Q0 + MLA_Q_LORA] = cq + np.arange(MLA_Q_LORA)
    idx[C_KV0:C_KV0 + MLA_KV_LORA] = ckv + np.arange(MLA_KV_LORA)
    h = HALF_ROPE
    idx[C_KRA0 + ROPE_LO:C_KRA0 + ROPE_LO + MLA_ROPE] = kr + np.arange(MLA_ROPE)
    idx[C_KRA0 + CTRL_LOGIT:C_KRA0 + CTRL_LOGIT + FOX_HEADS] = fl + np.arange(FOX_HEADS)
    idx[C_KRB0 + ROPE_LO:C_KRB0 + ROPE_LO + h] = kr + h + np.arange(h)
    idx[C_KRB0 + ROPE_LO + h:C_KRB0 + ROPE_LO + MLA_ROPE] = kr + np.arange(h)
    idx[C_FQ0:C_FQ0 + FOX_W] = fq + np.arange(FOX_W)
    idx[C_FK0:C_FK0 + FOX_W] = fk + np.arange(FOX_W)
    idx[C_FV0:C_FV0 + FOX_W] = fv + np.arange(FOX_W)
    return idx


def _perm_w_uq():
    per = MLA_NOPE + MLA_ROPE
    ia = -np.ones((MLA_HEADS * LANES,), np.int64)
    ib = -np.ones((MLA_HEADS * LANES,), np.int64)
    h2 = HALF_ROPE
    for h in range(MLA_HEADS):
        ia[h * LANES:h * LANES + per] = h * per + np.arange(per)
        ib[h * LANES + ROPE_LO:h * LANES + ROPE_LO + h2] = h * per + MLA_NOPE + h2 + np.arange(h2)
        ib[h * LANES + ROPE_LO + h2:h * LANES + ROPE_LO + MLA_ROPE] = h * per + MLA_NOPE + np.arange(h2)
    return ia, ib


def _perm_w_ukv():
    per = MLA_NOPE + MLA_V
    ik = -np.ones((MLA_HEADS * LANES,), np.int64)
    iv = np.zeros((MLA_HEADS * MLA_V,), np.int64)
    for h in range(MLA_HEADS):
        ik[h * LANES:h * LANES + MLA_NOPE] = h * per + np.arange(MLA_NOPE)
        iv[h * MLA_V:(h + 1) * MLA_V] = h * per + MLA_NOPE + np.arange(MLA_V)
    return np.concatenate([ik, iv])


def _aug_placement():
    e = np.zeros((LANES, 2 * FOX_HEADS * LANES), np.float32)
    koff = FOX_HEADS * LANES
    for h in range(FOX_HEADS):
        for p in range(3):
            src = CTRL_LOGIT + 8 * p + h
            e[src, h * LANES + AUG_LO + p] = 1.0
            e[CTRL_ONE, h * LANES + AUG_LO + 3 + p] = -1.0
            e[CTRL_ONE, koff + h * LANES + AUG_LO + p] = 1.0
            e[src, koff + h * LANES + AUG_LO + 3 + p] = 1.0
    return e


def _inproj_kernel(x_ref, mod_ref, g_ref, win_ref, qg_ref, kvg_ref, wuqa_ref, wuqb_ref, wukv_ref,
                   fb_ref, ct_ref, st_ref, tri_ref, eqk_ref,
                   qa_ref, ka_ref, vt_ref, carry_ref, *, ts):
    si = pl.program_id(1)

    @pl.when(si == 0)
    def _():
        carry_ref[...] = jnp.zeros_like(carry_ref)

    x = x_ref[0]
    shift = mod_ref[0, 0:1, :]
    scale = mod_ref[0, 1:2, :]
    h = ((_rms(x) * g_ref[...]) * (1.0 + scale) + shift).astype(BF16)
    proj = jnp.dot(h, win_ref[...], preferred_element_type=F32)

    ctab = ct_ref[0]
    stab = st_ref[0]
    lane = lax.broadcasted_iota(jnp.int32, (ts, LANES), 1)

    c_q = proj[:, C_Q0:C_Q0 + MLA_Q_LORA]
    cqn = (_rms(c_q) * qg_ref[...]).astype(BF16)
    qa = jnp.dot(cqn, wuqa_ref[...], preferred_element_type=F32)
    qb = jnp.dot(cqn, wuqb_ref[...], preferred_element_type=F32)
    mla_scale = (MLA_NOPE + MLA_ROPE) ** -0.5 * LOG2E
    for hh in range(MLA_HEADS):
        sl = slice(hh * LANES, (hh + 1) * LANES)
        qa_ref[0, hh] = ((qa[:, sl] * ctab + qb[:, sl] * stab) * mla_scale).astype(BF16)

    c_kv = proj[:, C_KV0:C_KV0 + MLA_KV_LORA]
    ckvn = (_rms(c_kv) * kvg_ref[...]).astype(BF16)
    kv = jnp.dot(ckvn, wukv_ref[...], preferred_element_type=F32)
    kra = proj[:, C_KRA0:C_KRA0 + LANES]
    krb = proj[:, C_KRB0:C_KRB0 + LANES]
    krope = kra * ctab + krb * stab
    for hh in range(MLA_HEADS):
        ka_ref[0, hh] = (kv[:, hh * LANES:(hh + 1) * LANES] + krope).astype(BF16)
    v_mla = kv[:, MLA_HEADS * LANES:]

    ctrl = (lane >= CTRL_LOGIT) & (lane < CTRL_LOGIT + FOX_HEADS)
    fl = kra + fb_ref[...]
    lsig = jnp.minimum(fl, 0.0) - jnp.log1p(jnp.exp(-jnp.abs(fl)))
    lf = jnp.where(ctrl, lsig, 0.0)
    hi, mid, lo = _split3(lf)
    p1 = (hi + pltpu.roll(mid, 8, 1) + pltpu.roll(lo, 16, 1)).astype(BF16)
    cs = jnp.dot(tri_ref[...], p1, preferred_element_type=F32)
    cs = cs + pltpu.roll(cs, LANES - 8, 1) + pltpu.roll(cs, LANES - 16, 1)
    fcum = jnp.where(ctrl, cs, 0.0) + carry_ref[...]
    carry_ref[...] = fcum[ts - 1:ts, :]

    hi, mid, lo = _split3(fcum * LOG2E)
    p2 = hi + pltpu.roll(mid, 8, 1) + pltpu.roll(lo, 16, 1)
    p2 = jnp.where(lane == CTRL_ONE, 1.0, p2).astype(BF16)
    aug = jnp.dot(p2, eqk_ref[...], preferred_element_type=F32)

    fq = proj[:, C_FQ0:C_FQ0 + FOX_W]
    fk = proj[:, C_FK0:C_FK0 + FOX_W]
    low = lane < FOX_HEAD_DIM
    koff = FOX_HEADS * LANES
    for j in range(FOX_HEADS // 2):
        sl = slice(j * LANES, (j + 1) * LANES)
        for src, off, dst in ((fq, 0, qa_ref), (fk, koff, ka_ref)):
            blk = src[:, sl] * LOG2E if dst is qa_ref else src[:, sl]
            blk_r = pltpu.roll(blk, FOX_HEAD_DIM, 1)
            h0 = 2 * j
            a0 = aug[:, off + h0 * LANES:off + (h0 + 1) * LANES]
            a1 = aug[:, off + (h0 + 1) * LANES:off + (h0 + 2) * LANES]
            dst[0, MLA_HEADS + h0] = jnp.where(low, blk, a0).astype(BF16)
            dst[0, MLA_HEADS + h0 + 1] = jnp.where(low, blk_r, a1).astype(BF16)

    fv = proj[:, C_FV0:C_FV0 + FOX_W]
    vt_ref[0, 0, 0:MLA_HEADS * MLA_V, :] = v_mla.T.astype(BF16)
    vt_ref[0, 0, MLA_HEADS * MLA_V:, :] = fv.T.astype(BF16)


def _inproj(x, mod_l, g, win, qg, kvg, wuqa, wuqb, wukv, fb_row, ctab, stab, tri, eqk, *, ts):
    B, S, D = x.shape
    nst = S // ts
    const2 = lambda b, s: (0, 0)
    kern = functools.partial(_inproj_kernel, ts=ts)
    return pl.pallas_call(
        kern,
        grid=(B, nst),
        in_specs=[
            pl.BlockSpec((1, ts, D), lambda b, s: (b, s, 0)),
            pl.BlockSpec((1, 6, D), lambda b, s: (b, 0, 0)),
            pl.BlockSpec((1, D), const2),
            pl.BlockSpec(win.shape, const2),
            pl.BlockSpec(qg.shape, const2),
            pl.BlockSpec(kvg.shape, const2),
            pl.BlockSpec(wuqa.shape, const2),
            pl.BlockSpec(wuqb.shape, const2),
            pl.BlockSpec(wukv.shape, const2),
            pl.BlockSpec(fb_row.shape, const2),
            pl.BlockSpec((1, ts, LANES), lambda b, s: (b, s, 0)),
            pl.BlockSpec((1, ts, LANES), lambda b, s: (b, s, 0)),
            pl.BlockSpec(tri.shape, const2),
            pl.BlockSpec(eqk.shape, const2),
        ],
        out_specs=[
            pl.BlockSpec((1, N_HEADS, ts, LANES), lambda b, s: (b, 0, s, 0)),
            pl.BlockSpec((1, N_HEADS, ts, LANES), lambda b, s: (b, 0, s, 0)),
            pl.BlockSpec((1, 1, N_HEADS * MLA_V, ts), lambda b, s: (b, s, 0, 0)),
        ],
        out_shape=[
            jax.ShapeDtypeStruct((B, N_HEADS, S, LANES), BF16),
            jax.ShapeDtypeStruct((B, N_HEADS, S, LANES), BF16),
            jax.ShapeDtypeStruct((B, nst, N_HEADS * MLA_V, ts), BF16),
        ],
        scratch_shapes=[pltpu.VMEM((1, LANES), F32)],
        compiler_params=pltpu.CompilerParams(
            dimension_semantics=("arbitrary", "arbitrary"), vmem_limit_bytes=VMEM_LIMIT),
    )(x, mod_l, g, win, qg, kvg, wuqa, wuqb, wukv, fb_row, ctab, stab, tri, eqk)


def _colmax8(st, groups):
    tk, tq = st.shape
    v = st.reshape(groups, tk // (8 * groups), 8, tq)
    return jnp.max(jnp.max(v, axis=1), axis=0)


def _attn_kernel(q_ref, k_ref, vt_ref, o_ref, s_scr, tmax_scr, m_scr, acc_scr,
                 *, tq, tk, hp, nq):
    dv = MLA_V
    groups = 4
    ones = jnp.ones((ONES_ROWS, tk), BF16)

    def qk(qi, j):
        q0 = pl.multiple_of(qi * tq, tq)
        return [lax.dot_general(k_ref[0, h, pl.ds(pl.multiple_of(j * tk, tk), tk), :],
                                q_ref[0, h, pl.ds(q0, tq), :], (((1,), (1,)), ((), ())),
                                preferred_element_type=F32) for h in range(hp)]

    def park(sts, qi, j, masked):
        for h in range(hp):
            st = sts[h]
            if masked:
                kpos = j * tk + lax.broadcasted_iota(jnp.int32, (tk, tq), 0)
                qpos = qi * tq + lax.broadcasted_iota(jnp.int32, (tk, tq), 1)
                st = jnp.where(kpos <= qpos, st, NEG_BIG)
            s_scr[h] = st
            tmax_scr[h] = _colmax8(st, groups)

    def update(j):
        for h in range(hp):
            m = m_scr[h]
            m_new = jnp.maximum(m, jnp.max(tmax_scr[h], axis=0, keepdims=True))
            alpha = jnp.exp2(m - m_new)
            p = jnp.exp2((s_scr[h] - m_new).astype(BF16))
            vt1 = jnp.concatenate([vt_ref[0, j, h * dv:(h + 1) * dv, :], ones], axis=0)
            acc_scr[h] = alpha * acc_scr[h] + jnp.dot(vt1, p, preferred_element_type=F32)
            m_scr[h] = m_new

    def reset_state():
        m_scr[...] = jnp.full(m_scr.shape, NEG_BIG, F32)
        acc_scr[...] = jnp.zeros(acc_scr.shape, F32)

    def finish(qi):
        q0 = pl.multiple_of(qi * tq, tq)
        for h in range(hp):
            acc = acc_scr[h]
            o_ref[0, h * dv:(h + 1) * dv, pl.ds(q0, tq)] = (
                acc[0:dv, :] / acc[dv:dv + 1, :]).astype(o_ref.dtype)
        reset_state()

    reset_state()
    park(qk(0, 0), 0, 0, True)

    def q_tile(qi, c):
        nf = (qi * tq) // tk

        def below_diagonal(t):
            sts = qk(qi, t + 1)
            update(t)
            park(sts, qi, t + 1, False)

        def two_below(u, c2):
            below_diagonal(2 * u)
            below_diagonal(2 * u + 1)
            return c2

        lax.fori_loop(0, (nf - 1) // 2, two_below, 0)

        @pl.when((nf > 1) & ((nf - 1) % 2 == 1))
        def _():
            below_diagonal(nf - 2)

        @pl.when(nf > 0)
        def _():
            sts = qk(qi, nf)
            update(nf - 1)
            park(sts, qi, nf, True)

        nxt = qi + 1
        nxt_on_diagonal = (nxt * tq) // tk == 0

        @pl.when((nxt < nq) & nxt_on_diagonal)
        def _():
            sts = qk(nxt, 0)
            update(nf)
            finish(qi)
            park(sts, nxt, 0, True)

        @pl.when((nxt < nq) & jnp.logical_not(nxt_on_diagonal))
        def _():
            sts = qk(nxt, 0)
            update(nf)
            finish(qi)
            park(sts, nxt, 0, False)

        @pl.when(nxt == nq)
        def _():
            update(nf)
            finish(qi)

        return c

    lax.fori_loop(0, nq, q_tile, 0)


def _attention(qa, ka, vt, *, tq, tk, hp):
    B, H, S, _ = qa.shape
    nkt = vt.shape[1]
    assert S % tq == 0 and nkt * tk == S and H % hp == 0 and tk % tq == 0
    kern = functools.partial(_attn_kernel, tq=tq, tk=tk, hp=hp, nq=S // tq)
    return pl.pallas_call(
        kern,
        grid=(B, H // hp),
        in_specs=[
            pl.BlockSpec((1, hp, S, LANES), lambda b, g: (b, g, 0, 0)),
            pl.BlockSpec((1, hp, S, LANES), lambda b, g: (b, g, 0, 0)),
            pl.BlockSpec((1, nkt, hp * MLA_V, tk), lambda b, g: (b, 0, g, 0)),
        ],
        out_specs=pl.BlockSpec((1, hp * MLA_V, S), lambda b, g: (b, g, 0)),
        out_shape=jax.ShapeDtypeStruct((B, H * MLA_V, S), BF16),
        scratch_shapes=[
            pltpu.VMEM((hp, tk, tq), F32),
            pltpu.VMEM((hp, 8, tq), F32),
            pltpu.VMEM((hp, 1, tq), F32),
            pltpu.VMEM((hp, MLA_V + ONES_ROWS, tq), F32),
        ],
        compiler_params=pltpu.CompilerParams(
            dimension_semantics=("arbitrary", "arbitrary"), vmem_limit_bytes=VMEM_LIMIT),
    )(qa, ka, vt)


def _top2_gates(logits, lane):
    lg = jnp.where(lane < N_EXPERTS, logits, NEG_BIG)
    m1 = jnp.max(lg, axis=1, keepdims=True)
    i1 = jnp.min(jnp.where(lg == m1, lane, LANES), axis=1, keepdims=True)
    lg2 = jnp.where(lane == i1, NEG_BIG, lg)
    m2 = jnp.max(lg2, axis=1, keepdims=True)
    i2 = jnp.min(jnp.where(lg2 == m2, lane, LANES), axis=1, keepdims=True)
    e2 = jnp.exp(m2 - m1)
    den = 1.0 + e2
    return jnp.where(lane == i1, 1.0 / den, 0.0) + jnp.where(lane == i2, e2 / den, 0.0)


def _outproj_kernel(*refs, ts, with_router):
    if with_router:
        (ot_ref, x_ref, mod_ref, mg_ref, fg_ref, wo_ref, ng_ref, rw_ref,
         x1_ref, h2_ref, gates_ref) = refs
    else:
        ot_ref, x_ref, mod_ref, mg_ref, fg_ref, wo_ref, ng_ref, x1_ref, h2_ref = refs
    half = MLA_HEADS * MLA_V
    om = ot_ref[0, 0:half, :].astype(F32).T
    of = ot_ref[0, half:, :].astype(F32).T
    on = jnp.concatenate([_rms(om) * mg_ref[...], _rms(of) * fg_ref[...]], axis=1).astype(BF16)
    mix = jnp.dot(on, wo_ref[...], preferred_element_type=F32)
    x1 = x_ref[0] + mod_ref[0, 2:3, :] * mix
    x1_ref[0] = x1
    h2 = ((_rms(x1) * ng_ref[...]) * (1.0 + mod_ref[0, 4:5, :]) + mod_ref[0, 3:4, :]).astype(BF16)
    h2_ref[0] = h2
    if with_router:
        logits = jnp.dot(h2, rw_ref[...], preferred_element_type=F32)
        lane = lax.broadcasted_iota(jnp.int32, (ts, LANES), 1)
        gates_ref[0] = _top2_gates(logits, lane)


def _outproj(ot, x, mod_l, mg, fg, wo, ng, rw, *, ts):
    B, S, D = x.shape
    with_router = rw is not None
    const2 = lambda b, s: (0, 0)
    in_specs = [
        pl.BlockSpec((1, ot.shape[1], ts), lambda b, s: (b, 0, s)),
        pl.BlockSpec((1, ts, D), lambda b, s: (b, s, 0)),
        pl.BlockSpec((1, 6, D), lambda b, s: (b, 0, 0)),
        pl.BlockSpec(mg.shape, const2),
        pl.BlockSpec(fg.shape, const2),
        pl.BlockSpec(wo.shape, const2),
        pl.BlockSpec(ng.shape, const2),
    ]
    args = [ot, x, mod_l, mg, fg, wo, ng]
    out_specs = [pl.BlockSpec((1, ts, D), lambda b, s: (b, s, 0)),
                 pl.BlockSpec((1, ts, D), lambda b, s: (b, s, 0))]
    out_shape = [jax.ShapeDtypeStruct((B, S, D), F32), jax.ShapeDtypeStruct((B, S, D), BF16)]
    if with_router:
        in_specs.append(pl.BlockSpec(rw.shape, const2))
        args.append(rw)
        out_specs.append(pl.BlockSpec((1, ts, LANES), lambda b, s: (b, s, 0)))
        out_shape.append(jax.ShapeDtypeStruct((B, S, LANES), F32))
    kern = functools.partial(_outproj_kernel, ts=ts, with_router=with_router)
    return pl.pallas_call(
        kern,
        grid=(B, S // ts),
        in_specs=in_specs,
        out_specs=out_specs,
        out_shape=out_shape,
        compiler_params=pltpu.CompilerParams(
            dimension_semantics=("arbitrary", "arbitrary"), vmem_limit_bytes=VMEM_LIMIT),
    )(*args)


def _swiglu_tile(x, wg, wu):
    g = jnp.dot(x, wg, preferred_element_type=F32)
    u = jnp.dot(x, wu, preferred_element_type=F32)
    return g * _sigmoid(g) * u


def _ffn_kernel(h_ref, wg_ref, wu_ref, wd_ref, x_ref, mod_ref, fin_ref, o_ref, acc_ref,
                *, final_norm):
    f = pl.program_id(1)

    @pl.when(f == 0)
    def _():
        acc_ref[...] = jnp.zeros_like(acc_ref)

    a = _swiglu_tile(h_ref[...], wg_ref[...], wu_ref[...])
    acc_ref[...] += jnp.dot(a.astype(BF16), wd_ref[...], preferred_element_type=F32)

    @pl.when(f == pl.num_programs(1) - 1)
    def _():
        x2 = x_ref[...] + mod_ref[0, 5:6, :] * acc_ref[...]
        if final_norm:
            x2 = _rms(x2) * fin_ref[...]
        o_ref[...] = x2


def _ffn(h2, wg, wu, wd, x1, mod_l, fin_g, *, tm, tf, seq, final_norm):
    T, D = h2.shape
    F = wd.shape[0]
    tiles_per_batch = seq // tm
    kern = functools.partial(_ffn_kernel, final_norm=final_norm)
    return pl.pallas_call(
        kern,
        grid=(T // tm, F // tf),
        in_specs=[
            pl.BlockSpec((tm, D), lambda i, f: (i, 0)),
            pl.BlockSpec((D, tf), lambda i, f: (0, f)),
            pl.BlockSpec((D, tf), lambda i, f: (0, f)),
            pl.BlockSpec((tf, D), lambda i, f: (f, 0)),
            pl.BlockSpec((tm, D), lambda i, f: (i, 0)),
            pl.BlockSpec((1, 6, D), lambda i, f: (i // tiles_per_batch, 0, 0)),
            pl.BlockSpec((1, D), lambda i, f: (0, 0)),
        ],
        out_specs=pl.BlockSpec((tm, D), lambda i, f: (i, 0)),
        out_shape=jax.ShapeDtypeStruct((T, D), F32),
        scratch_shapes=[pltpu.VMEM((tm, D), F32)],
        compiler_params=pltpu.CompilerParams(
            dimension_semantics=("arbitrary", "arbitrary"), vmem_limit_bytes=VMEM_LIMIT),
    )(h2, wg, wu, wd, x1, mod_l, fin_g)


RC = 128
VT = 8


def _route_kernel(gates_ref, rank_ref, rankt_ref, gatest_ref, cnt_ref, *, tm):
    gates = gates_ref[...]
    sel = gates > 0.0
    self32 = jnp.where(sel, 1.0, 0.0)
    earlier = (lax.broadcasted_iota(jnp.int32, (tm, tm), 1)
               < lax.broadcasted_iota(jnp.int32, (tm, tm), 0))
    rank = jnp.dot(jnp.where(earlier, 1.0, 0.0).astype(BF16), self32.astype(BF16),
                   preferred_element_type=F32)
    rank = jnp.where(sel, rank, -1.0)
    rank_ref[...] = rank
    rankt_ref[0] = rank.T
    gatest_ref[0] = gates.T
    cnt_ref[0] = jnp.sum(self32, axis=0, keepdims=True)


def _route(gates, *, tm):
    T = gates.shape[0]
    nt = T // tm
    return pl.pallas_call(
        functools.partial(_route_kernel, tm=tm),
        grid=(nt,),
        in_specs=[pl.BlockSpec((tm, LANES), lambda i: (i, 0))],
        out_specs=[
            pl.BlockSpec((tm, LANES), lambda i: (i, 0)),
            pl.BlockSpec((1, LANES, tm), lambda i: (i, 0, 0)),
            pl.BlockSpec((1, LANES, tm), lambda i: (i, 0, 0)),
            pl.BlockSpec((1, 1, LANES), lambda i: (i, 0, 0)),
        ],
        out_shape=[
            jax.ShapeDtypeStruct((T, LANES), F32),
            jax.ShapeDtypeStruct((nt, LANES, tm), F32),
            jax.ShapeDtypeStruct((nt, LANES, tm), F32),
            jax.ShapeDtypeStruct((nt, 1, LANES), F32),
        ],
        compiler_params=pltpu.CompilerParams(
            dimension_semantics=("arbitrary",), vmem_limit_bytes=VMEM_LIMIT),
    )(gates)


def _slot_tables(cnt, nt, ns, n_vt):
    E = N_EXPERTS
    nch = (cnt + (RC - 1)) // RC
    nct = jnp.sum(nch, axis=1)
    ccum = jnp.cumsum(nch, axis=1)
    j = jnp.arange(ns, dtype=jnp.int32)
    slot_e = jnp.sum((ccum[:, None, :] <= j[None, :, None]).astype(jnp.int32), axis=2)
    slot_e = jnp.minimum(slot_e, E - 1)
    first = jnp.take_along_axis(ccum - nch, slot_e, axis=1)
    valid = j[None, :] < nct[:, None]
    slot_k = jnp.where(valid, j[None, :] - first, 0)
    nce = jnp.sum(nch, axis=0)
    padded = (nce + (VT - 1)) // VT * VT
    base = jnp.cumsum(padded) - padded
    cid0 = base[None, :] + jnp.cumsum(nch, axis=0) - nch
    cid = jnp.take_along_axis(cid0, slot_e, axis=1) + slot_k
    cid = jnp.where(valid, cid, cid[:, 0:1])
    n_chunks = n_vt * VT
    slot_id = jnp.arange(nt, dtype=jnp.int32)[:, None] * ns + j[None, :]
    pos = jnp.full((n_chunks,), -1, jnp.int32).at[
        jnp.where(valid, cid, n_chunks).reshape(-1)].set(slot_id.reshape(-1), mode="drop")
    vcum = jnp.cumsum(padded // VT)
    nv = vcum[-1]
    v = jnp.minimum(jnp.arange(n_vt, dtype=jnp.int32), nv - 1)
    v_exp = jnp.minimum(jnp.searchsorted(vcum, v, side="right"), E - 1).astype(jnp.int32)
    i32 = lambda a: a.reshape(-1).astype(jnp.int32)
    return dict(slot_e=i32(slot_e), slot_k=i32(slot_k), nct=i32(nct), cid=i32(cid),
                pos=jnp.maximum(pos, 0), real=(pos >= 0).astype(jnp.int32),
                v_exp=v_exp, nv=i32(nv))


def _compact_kernel(se_ref, sk_ref, nct_ref, h_ref, rankt_ref, gatest_ref, xs_ref, gc_ref, oh_scr,
                    *, tm, ns):
    t = pl.program_id(0)
    n = nct_ref[t]

    def build(j, c):
        e = se_ref[t * ns + j]
        rrow = rankt_ref[0, pl.ds(e, 1), :].astype(jnp.int32)
        grow = gatest_ref[0, pl.ds(e, 1), :]
        row = lax.broadcasted_iota(jnp.int32, (RC, tm), 0) + sk_ref[t * ns + j] * RC
        hit = (rrow == row) & (j < n)
        oh_scr[j] = jnp.where(hit, 1.0, 0.0).astype(BF16)
        gcol = jnp.sum(jnp.where(hit, grow, 0.0), axis=1, keepdims=True)
        gc_ref[pl.ds(pl.multiple_of(j * RC, RC), RC), :] = jnp.broadcast_to(gcol, (RC, LANES))
        return c

    lax.fori_loop(0, ns, build, 0)
    rows = VT * RC
    for b in range(ns // VT):
        oh = oh_scr[b * VT:(b + 1) * VT].reshape(rows, tm)
        xs_ref[b * rows:(b + 1) * rows, :] = jnp.dot(
            oh, h_ref[...], preferred_element_type=F32).astype(BF16)


def _compact(tabs, h2, rankt, gatest, *, tm, ns):
    T, D = h2.shape
    nt = T // tm
    grid_spec = pltpu.PrefetchScalarGridSpec(
        num_scalar_prefetch=3,
        grid=(nt,),
        in_specs=[
            pl.BlockSpec((tm, D), lambda t, *_: (t, 0)),
            pl.BlockSpec((1, LANES, tm), lambda t, *_: (t, 0, 0)),
            pl.BlockSpec((1, LANES, tm), lambda t, *_: (t, 0, 0)),
        ],
        out_specs=[
            pl.BlockSpec((ns * RC, D), lambda t, *_: (t, 0)),
            pl.BlockSpec((ns * RC, LANES), lambda t, *_: (t, 0)),
        ],
        scratch_shapes=[pltpu.VMEM((ns, RC, tm), BF16)],
    )
    return pl.pallas_call(
        functools.partial(_compact_kernel, tm=tm, ns=ns),
        grid_spec=grid_spec,
        out_shape=[jax.ShapeDtypeStruct((nt * ns * RC, D), BF16),
                   jax.ShapeDtypeStruct((nt * ns * RC, LANES), F32)],
        compiler_params=pltpu.CompilerParams(
            dimension_semantics=("arbitrary",), vmem_limit_bytes=VMEM_LIMIT),
    )(tabs["slot_e"], tabs["slot_k"], tabs["nct"], h2, rankt, gatest)


def _slot_experts_kernel(vexp_ref, nv_ref, pos_ref, real_ref, *refs):
    xs_refs = refs[0:VT]
    gc_refs = refs[VT:2 * VT]
    wg_ref, wu_ref, wd_ref, ys_ref, x_scr, g_scr, acc_ref = refs[2 * VT:]
    v = pl.program_id(0)
    f = pl.program_id(1)

    @pl.when(v < nv_ref[0])
    def _():
        @pl.when(f == 0)
        def _():
            for j in range(VT):
                keep = real_ref[v * VT + j] == 1
                x_scr[j * RC:(j + 1) * RC, :] = jnp.where(keep, xs_refs[j][...], jnp.zeros_like(xs_refs[j]))
                g_scr[j * RC:(j + 1) * RC, :] = jnp.where(keep, gc_refs[j][...], 0.0)
            acc_ref[...] = jnp.zeros_like(acc_ref)

        a = _swiglu_tile(x_scr[...], wg_ref[0], wu_ref[0]) * g_scr[:, 0:1]
        acc_ref[...] += jnp.dot(a.astype(BF16), wd_ref[0], preferred_element_type=F32)

        @pl.when(f == pl.num_programs(1) - 1)
        def _():
            ys_ref[...] = acc_ref[...].astype(BF16)


def _slot_experts(tabs, xs, gc, wg, wu, wd, *, tf, n_vt):
    D = xs.shape[1]
    NF = wd.shape[1] // tf
    rows = VT * RC

    def live(v, nv):
        return jnp.minimum(v, nv[0] - 1)

    def hidden(v, f, nv):
        return jnp.where(v < nv[0], f, NF - 1)

    def chunk(j):
        return lambda v, f, vexp, nv, pos, real: (pos[live(v, nv) * VT + j], 0)

    grid_spec = pltpu.PrefetchScalarGridSpec(
        num_scalar_prefetch=4,
        grid=(n_vt, NF),
        in_specs=(
            [pl.BlockSpec((RC, D), chunk(j)) for j in range(VT)]
            + [pl.BlockSpec((RC, LANES), chunk(j)) for j in range(VT)]
            + [pl.BlockSpec((1, D, tf),
                            lambda v, f, vexp, nv, pos, real: (vexp[v], 0, hidden(v, f, nv)))] * 2
            + [pl.BlockSpec((1, tf, D),
                            lambda v, f, vexp, nv, pos, real: (vexp[v], hidden(v, f, nv), 0))]),
        out_specs=pl.BlockSpec((rows, D), lambda v, f, vexp, nv, pos, real: (live(v, nv), 0)),
        scratch_shapes=[pltpu.VMEM((rows, D), BF16), pltpu.VMEM((rows, LANES), F32),
                        pltpu.VMEM((rows, D), F32)],
    )
    return pl.pallas_call(
        _slot_experts_kernel,
        grid_spec=grid_spec,
        out_shape=jax.ShapeDtypeStruct((n_vt * rows, D), BF16),
        compiler_params=pltpu.CompilerParams(
            dimension_semantics=("arbitrary", "arbitrary"), vmem_limit_bytes=VMEM_LIMIT),
    )(tabs["v_exp"], tabs["nv"], tabs["pos"], tabs["real"], *([xs] * VT), *([gc] * VT), wg, wu, wd)


def _uncompact_kernel(se_ref, sk_ref, nct_ref, cid_ref, *refs, tm, ns, final_norm):
    ys_refs = refs[0:ns]
    rank_ref, x_ref, mod_ref, fin_ref, o_ref, oh_scr = refs[ns:]
    t = pl.program_id(0)
    n = nct_ref[t]
    lane = lax.broadcasted_iota(jnp.int32, (tm, LANES), 1)
    col = lax.broadcasted_iota(jnp.int32, (tm, RC), 1)
    rank = rank_ref[...]
    for j in range(ns):
        rcol = jnp.sum(jnp.where(lane == se_ref[t * ns + j], rank, 0.0), axis=1,
                       keepdims=True).astype(jnp.int32)
        hit = (rcol == col + sk_ref[t * ns + j] * RC) & (j < n)
        oh_scr[:, j * RC:(j + 1) * RC] = jnp.where(hit, 1.0, 0.0).astype(BF16)
    y = None
    for b in range(ns // VT):
        ys = jnp.concatenate([ys_refs[b * VT + j][...] for j in range(VT)], axis=0)
        part = jnp.dot(oh_scr[:, b * VT * RC:(b + 1) * VT * RC], ys, preferred_element_type=F32)
        y = part if y is None else y + part
    x2 = x_ref[...] + mod_ref[0, 5:6, :] * y
    if final_norm:
        x2 = _rms(x2) * fin_ref[...]
    o_ref[...] = x2


def _uncompact(tabs, ys, rank, x1, mod_l, fin_g, *, tm, ns, seq, final_norm):
    T, D = x1.shape
    tpb = seq // tm

    def chunk(j):
        return lambda t, se, sk, nct, cid: (cid[t * ns + j], 0)

    grid_spec = pltpu.PrefetchScalarGridSpec(
        num_scalar_prefetch=4,
        grid=(T // tm,),
        in_specs=(
            [pl.BlockSpec((RC, D), chunk(j)) for j in range(ns)]
            + [pl.BlockSpec((tm, LANES), lambda t, *_: (t, 0)),
               pl.BlockSpec((tm, D), lambda t, *_: (t, 0)),
               pl.BlockSpec((1, 6, D), lambda t, *_: (t // tpb, 0, 0)),
               pl.BlockSpec((1, D), lambda t, *_: (0, 0))]),
        out_specs=pl.BlockSpec((tm, D), lambda t, *_: (t, 0)),
        scratch_shapes=[pltpu.VMEM((tm, ns * RC), BF16)],
    )
    return pl.pallas_call(
        functools.partial(_uncompact_kernel, tm=tm, ns=ns, final_norm=final_norm),
        grid_spec=grid_spec,
        out_shape=jax.ShapeDtypeStruct((T, D), F32),
        compiler_params=pltpu.CompilerParams(
            dimension_semantics=("arbitrary",), vmem_limit_bytes=VMEM_LIMIT),
    )(tabs["slot_e"], tabs["slot_k"], tabs["nct"], tabs["cid"], *([ys] * ns),
      rank, x1, mod_l, fin_g)


def _routed_experts(h2, gates, w_gate, w_up, w_down, x1, mod_l, fin_g, *, tm, tf, seq, final_norm):
    T, D = h2.shape
    E = N_EXPERTS
    nt = T // tm
    ns = 2 * tm // RC + E
    assert ns % VT == 0
    n_vt = -(-(nt * ns + E * (VT - 1)) // VT)
    rank, rankt, gatest, cnt = _route(gates, tm=tm)
    tabs = _slot_tables(cnt[:, 0, :E].astype(jnp.int32), nt, ns, n_vt)
    xs, gc = _compact(tabs, h2, rankt, gatest, tm=tm, ns=ns)
    ys = _slot_experts(tabs, xs, gc, w_gate.astype(BF16), w_up.astype(BF16), w_down.astype(BF16),
                       tf=tf, n_vt=n_vt)
    return _uncompact(tabs, ys, rank, x1, mod_l, fin_g, tm=tm, ns=ns, seq=seq,
                      final_norm=final_norm)


def _tiles(S):
    ts = min(512, S)
    tq = min(512, S)
    tk = ts
    tm = min(1024, S)
    tf = D_FF // 2
    return ts, tq, tk, tm, tf


def kernel(x, c, positions, ada_w, ada_b, attn_norm_g, w_in, q_norm_g, w_uq, kv_norm_g, w_ukv,
           fox_forget_b, mla_out_g, fox_out_g, w_o, ffn_norm_g, dense_w_gate, dense_w_up,
           dense_w_down, router_w, moe_w_gate, moe_w_up, moe_w_down, final_norm_g):
    B, S, D = x.shape
    L = ada_w.shape[0]
    ts, tq, tk, tm, tf = _tiles(S)

    mod = _adaln(c, ada_w, ada_b).reshape(L, B, 6, D)
    ctab, stab = _rope_tables(positions)

    idx_in = _perm_w_in()
    idx_qa, idx_qb = _perm_w_uq()
    idx_kv = _perm_w_ukv()
    col_scale = np.ones((C_END,), np.float32)
    col_scale[C_FQ0:C_FQ0 + FOX_W] = FOX_HEAD_DIM ** -0.5
    tri = jnp.asarray(np.tril(np.ones((ts, ts), np.float32)), BF16)
    eqk = jnp.asarray(_aug_placement(), BF16)

    for l in range(L):
        win = (_take_cols(w_in[l], idx_in) * col_scale).astype(BF16)
        wuqa = _take_cols(w_uq[l], idx_qa).astype(BF16)
        wuqb = _take_cols(w_uq[l], idx_qb).astype(BF16)
        wukv = _take_cols(w_ukv[l], idx_kv).astype(BF16)
        fb_row = jnp.zeros((1, LANES), F32).at[0, CTRL_LOGIT:CTRL_LOGIT + FOX_HEADS].set(
            fox_forget_b[l].astype(F32))
        qa, ka, vt = _inproj(
            x, mod[l], attn_norm_g[l].reshape(1, D), win, q_norm_g[l].reshape(1, -1),
            kv_norm_g[l].reshape(1, -1), wuqa, wuqb, wukv, fb_row, ctab, stab, tri, eqk, ts=ts)
        ot = _attention(qa, ka, vt, tq=tq, tk=tk, hp=4)

        j = l // 2
        is_moe = (l % 2 == 1)
        rw = None
        if is_moe:
            rw = jnp.pad(router_w[j], ((0, 0), (0, LANES - N_EXPERTS))).astype(BF16)
        outs = _outproj(ot, x, mod[l], mla_out_g[l].reshape(1, -1), fox_out_g[l].reshape(1, -1),
                        w_o[l].astype(BF16), ffn_norm_g[l].reshape(1, D), rw, ts=ts)
        x1, h2 = outs[0].reshape(B * S, D), outs[1].reshape(B * S, D)
        fin_g = final_norm_g.reshape(1, D)
        last = l == L - 1
        if is_moe:
            x = _routed_experts(h2, outs[2].reshape(B * S, LANES), moe_w_gate[j], moe_w_up[j],
                                moe_w_down[j], x1, mod[l], fin_g,
                                tm=tm, tf=tf, seq=S, final_norm=last)
        else:
            x = _ffn(h2, dense_w_gate[j].astype(BF16), dense_w_up[j].astype(BF16),
                     dense_w_down[j].astype(BF16), x1, mod[l], fin_g,
                     tm=tm, tf=tf, seq=S, final_norm=last)
        x = x.reshape(B, S, D)
    return x
```

```python
import functools

import numpy as np
import jax
import jax.numpy as jnp
from jax import lax
from jax.experimental import pallas as pl
from jax.experimental.pallas import tpu as pltpu

F32 = jnp.float32
BF16 = jnp.bfloat16

D_MODEL = 1024
MLA_HEADS = 8
MLA_V = 64
MLA_NOPE = 64
MLA_ROPE = 32
MLA_Q_LORA = 256
MLA_KV_LORA = 128
FOX_HEADS = 8
FOX_HEAD_DIM = 64
FOX_W = FOX_HEADS * FOX_HEAD_DIM
ROPE_THETA = 10000.0
D_FF = 3584
N_EXPERTS = 8
EPS = 1e-6
N_HEADS = MLA_HEADS + FOX_HEADS
HALF_ROPE = MLA_ROPE // 2

LANES = 128
VMEM_LIMIT = 56 * 1024 * 1024

ROPE_LO = MLA_NOPE
AUG_LO = FOX_HEAD_DIM
CTRL_LOGIT = 96
CTRL_ONE = 120

C_Q0 = 0
C_KV0 = C_Q0 + MLA_Q_LORA
C_KRA0 = C_KV0 + MLA_KV_LORA
C_KRB0 = C_KRA0 + LANES
C_FQ0 = C_KRB0 + LANES
C_FK0 = C_FQ0 + FOX_W
C_FV0 = C_FK0 + FOX_W
C_END = C_FV0 + FOX_W

NEG_BIG = -1e30
LOG2E = 1.4426950408889634
ONES_ROWS = 16


def _rms(v):
    return v * lax.rsqrt(jnp.mean(v * v, axis=-1, keepdims=True) + EPS)


def _sigmoid(v):
    return 1.0 / (1.0 + jnp.exp(-v))


def _split3(v):
    hi = v.astype(BF16).astype(F32)
    r = v - hi
    mid = r.astype(BF16).astype(F32)
    lo = (r - mid).astype(BF16).astype(F32)
    return hi, mid, lo


def _adaln_kernel(c_ref, w_ref, b_ref, o_ref):
    c = c_ref[...]
    ca = (c * _sigmoid(c)).astype(BF16)
    o_ref[0] = jnp.dot(ca, w_ref[0].astype(BF16), preferred_element_type=F32) + b_ref[0]


def _adaln(c, ada_w, ada_b):
    L, D, N = ada_w.shape
    B = c.shape[0]
    tn = 1536
    return pl.pallas_call(
        _adaln_kernel,
        grid=(L, N // tn),
        in_specs=[
            pl.BlockSpec((B, D), lambda l, j: (0, 0)),
            pl.BlockSpec((1, D, tn), lambda l, j: (l, 0, j)),
            pl.BlockSpec((1, 1, tn), lambda l, j: (l, 0, j)),
        ],
        out_specs=pl.BlockSpec((1, B, tn), lambda l, j: (l, 0, j)),
        out_shape=jax.ShapeDtypeStruct((L, B, N), F32),
        compiler_params=pltpu.CompilerParams(
            dimension_semantics=("arbitrary", "arbitrary"), vmem_limit_bytes=VMEM_LIMIT),
    )(c, ada_w, ada_b.reshape(L, 1, N))


def _rope_tab_kernel(pos_ref, freq_ref, cos_ref, sin_ref):
    ang = freq_ref[...] * pos_ref[0]
    cos_ref[0] = jnp.cos(ang)
    sin_ref[0] = jnp.sin(ang)


def _rope_tables(positions):
    B, S = positions.shape
    half = HALF_ROPE
    inv_freq = ROPE_THETA ** (-jnp.arange(half, dtype=F32) / half)
    pos = positions.astype(F32).reshape(B, 1, S)
    cos_t, sin_t = pl.pallas_call(
        _rope_tab_kernel,
        grid=(B,),
        in_specs=[
            pl.BlockSpec((1, 1, S), lambda b: (b, 0, 0)),
            pl.BlockSpec((half, 1), lambda b: (0, 0)),
        ],
        out_specs=[pl.BlockSpec((1, half, S), lambda b: (b, 0, 0))] * 2,
        out_shape=[jax.ShapeDtypeStruct((B, half, S), F32)] * 2,
        compiler_params=pltpu.CompilerParams(dimension_semantics=("arbitrary",)),
    )(pos, inv_freq.reshape(half, 1))
    cos = jnp.transpose(cos_t, (0, 2, 1))
    sin = jnp.transpose(sin_t, (0, 2, 1))
    ones = jnp.ones((B, S, MLA_NOPE), F32)
    z_lo = jnp.zeros((B, S, MLA_NOPE), F32)
    z_hi = jnp.zeros((B, S, LANES - MLA_NOPE - MLA_ROPE), F32)
    ctab = jnp.concatenate([ones, cos, cos, z_hi], axis=-1)
    stab = jnp.concatenate([z_lo, -sin, sin, z_hi], axis=-1)
    return ctab, stab


def _take_cols(w, idx):
    pieces, i, n = [], 0, len(idx)
    while i < n:
        j = i + 1
        if idx[i] < 0:
            while j < n and idx[j] < 0:
                j += 1
            pieces.append(jnp.zeros((w.shape[0], j - i), w.dtype))
        else:
            while j < n and idx[j] == idx[j - 1] + 1:
                j += 1
            pieces.append(w[:, int(idx[i]):int(idx[i]) + (j - i)])
        i = j
    return jnp.concatenate(pieces, axis=1)


def _perm_w_in():
    cq = 0
    ckv = cq + MLA_Q_LORA
    kr = ckv + MLA_KV_LORA
    fq = kr + MLA_ROPE
    fk = fq + FOX_W
    fv = fk + FOX_W
    fl = fv + FOX_W
    idx = -np.ones((C_END,), np.int64)
    idx[C_Q0:C_Q0 + MLA_Q_LORA] = cq + np.arange(MLA_Q_LORA)
    idx[C_KV0:C_KV0 + MLA_KV_LORA] = ckv + np.arange(MLA_KV_LORA)
    h = HALF_ROPE
    idx[C_KRA0 + ROPE_LO:C_KRA0 + ROPE_LO + MLA_ROPE] = kr + np.arange(MLA_ROPE)
    idx[C_KRA0 + CTRL_LOGIT:C_KRA0 + CTRL_LOGIT + FOX_HEADS] = fl + np.arange(FOX_HEADS)
    idx[C_KRB0 + ROPE_LO:C_KRB0 + ROPE_LO + h] = kr + h + np.arange(h)
    idx[C_KRB0 + ROPE_LO + h:C_KRB0 + ROPE_LO + MLA_ROPE] = kr + np.arange(h)
    idx[C_FQ0:C_FQ0 + FOX_W] = fq + np.arange(FOX_W)
    idx[C_FK0:C_FK0 + FOX_W] = fk + np.arange(FOX_W)
    idx[C_FV0:C_FV0 + FOX_W] = fv + np.arange(FOX_W)
    return idx


def _perm_w_uq():
    per = MLA_NOPE + MLA_ROPE
    ia = -np.ones((MLA_HEADS * LANES,), np.int64)
    ib = -np.ones((MLA_HEADS * LANES,), np.int64)
    h2 = HALF_ROPE
    for h in range(MLA_HEADS):
        ia[h * LANES:h * LANES + per] = h * per + np.arange(per)
        ib[h * LANES + ROPE_LO:h * LANES + ROPE_LO + h2] = h * per + MLA_NOPE + h2 + np.arange(h2)
        ib[h * LANES + ROPE_LO + h2:h * LANES + ROPE_LO + MLA_ROPE] = h * per + MLA_NOPE + np.arange(h2)
    return ia, ib


def _perm_w_ukv():
    per = MLA_NOPE + MLA_V
    ik = -np.ones((MLA_HEADS * LANES,), np.int64)
    iv = np.zeros((MLA_HEADS * MLA_V,), np.int64)
    for h in range(MLA_HEADS):
        ik[h * LANES:h * LANES + MLA_NOPE] = h * per + np.arange(MLA_NOPE)
        iv[h * MLA_V:(h + 1) * MLA_V] = h * per + MLA_NOPE + np.arange(MLA_V)
    return np.concatenate([ik, iv])


def _aug_placement():
    e = np.zeros((LANES, 2 * FOX_HEADS * LANES), np.float32)
    koff = FOX_HEADS * LANES
    for h in range(FOX_HEADS):
        for p in range(3):
            src = CTRL_LOGIT + 8 * p + h
            e[src, h * LANES + AUG_LO + p] = 1.0
            e[CTRL_ONE, h * LANES + AUG_LO + 3 + p] = -1.0
            e[CTRL_ONE, koff + h * LANES + AUG_LO + p] = 1.0
            e[src, koff + h * LANES + AUG_LO + 3 + p] = 1.0
    return e


def _inproj_kernel(x_ref, mod_ref, g_ref, win_ref, qg_ref, kvg_ref, wuqa_ref, wuqb_ref, wukv_ref,
                   fb_ref, ct_ref, st_ref, tri_ref, eqk_ref,
                   qa_ref, ka_ref, vt_ref, carry_ref, *, ts):
    si = pl.program_id(1)

    @pl.when(si == 0)
    def _():
        carry_ref[...] = jnp.zeros_like(carry_ref)

    x = x_ref[0]
    shift = mod_ref[0, 0:1, :]
    scale = mod_ref[0, 1:2, :]
    h = ((_rms(x) * g_ref[...]) * (1.0 + scale) + shift).astype(BF16)
    proj = jnp.dot(h, win_ref[...], preferred_element_type=F32)

    ctab = ct_ref[0]
    stab = st_ref[0]
    lane = lax.broadcasted_iota(jnp.int32, (ts, LANES), 1)

    c_q = proj[:, C_Q0:C_Q0 + MLA_Q_LORA]
    cqn = (_rms(c_q) * qg_ref[...]).astype(BF16)
    qa = jnp.dot(cqn, wuqa_ref[...], preferred_element_type=F32)
    qb = jnp.dot(cqn, wuqb_ref[...], preferred_element_type=F32)
    mla_scale = (MLA_NOPE + MLA_ROPE) ** -0.5 * LOG2E
    for hh in range(MLA_HEADS):
        sl = slice(hh * LANES, (hh + 1) * LANES)
        qa_ref[0, hh] = ((qa[:, sl] * ctab + qb[:, sl] * stab) * mla_scale).astype(BF16)

    c_kv = proj[:, C_KV0:C_KV0 + MLA_KV_LORA]
    ckvn = (_rms(c_kv) * kvg_ref[...]).astype(BF16)
    kv = jnp.dot(ckvn, wukv_ref[...], preferred_element_type=F32)
    kra = proj[:, C_KRA0:C_KRA0 + LANES]
    krb = proj[:, C_KRB0:C_KRB0 + LANES]
    krope = kra * ctab + krb * stab
    for hh in range(MLA_HEADS):
        ka_ref[0, hh] = (kv[:, hh * LANES:(hh + 1) * LANES] + krope).astype(BF16)
    v_mla = kv[:, MLA_HEADS * LANES:]

    ctrl = (lane >= CTRL_LOGIT) & (lane < CTRL_LOGIT + FOX_HEADS)
    fl = kra + fb_ref[...]
    lsig = jnp.minimum(fl, 0.0) - jnp.log1p(jnp.exp(-jnp.abs(fl)))
    lf = jnp.where(ctrl, lsig, 0.0)
    hi, mid, lo = _split3(lf)
    p1 = (hi + pltpu.roll(mid, 8, 1) + pltpu.roll(lo, 16, 1)).astype(BF16)
    cs = jnp.dot(tri_ref[...], p1, preferred_element_type=F32)
    cs = cs + pltpu.roll(cs, LANES - 8, 1) + pltpu.roll(cs, LANES - 16, 1)
    fcum = jnp.where(ctrl, cs, 0.0) + carry_ref[...]
    carry_ref[...] = fcum[ts - 1:ts, :]

    hi, mid, lo = _split3(fcum * LOG2E)
    p2 = hi + pltpu.roll(mid, 8, 1) + pltpu.roll(lo, 16, 1)
    p2 = jnp.where(lane == CTRL_ONE, 1.0, p2).astype(BF16)
    aug = jnp.dot(p2, eqk_ref[...], preferred_element_type=F32)

    fq = proj[:, C_FQ0:C_FQ0 + FOX_W]
    fk = proj[:, C_FK0:C_FK0 + FOX_W]
    low = lane < FOX_HEAD_DIM
    koff = FOX_HEADS * LANES
    for j in range(FOX_HEADS // 2):
        sl = slice(j * LANES, (j + 1) * LANES)
        for src, off, dst in ((fq, 0, qa_ref), (fk, koff, ka_ref)):
            blk = src[:, sl] * LOG2E if dst is qa_ref else src[:, sl]
            blk_r = pltpu.roll(blk, FOX_HEAD_DIM, 1)
            h0 = 2 * j
            a0 = aug[:, off + h0 * LANES:off + (h0 + 1) * LANES]
            a1 = aug[:, off + (h0 + 1) * LANES:off + (h0 + 2) * LANES]
            dst[0, MLA_HEADS + h0] = jnp.where(low, blk, a0).astype(BF16)
            dst[0, MLA_HEADS + h0 + 1] = jnp.where(low, blk_r, a1).astype(BF16)

    fv = proj[:, C_FV0:C_FV0 + FOX_W]
    vt_ref[0, 0, 0:MLA_HEADS * MLA_V, :] = v_mla.T.astype(BF16)
    vt_ref[0, 0, MLA_HEADS * MLA_V:, :] = fv.T.astype(BF16)


def _inproj(x, mod_l, g, win, qg, kvg, wuqa, wuqb, wukv, fb_row, ctab, stab, tri, eqk, *, ts):
    B, S, D = x.shape
    nst = S // ts
    const2 = lambda b, s: (0, 0)
    kern = functools.partial(_inproj_kernel, ts=ts)
    return pl.pallas_call(
        kern,
        grid=(B, nst),
        in_specs=[
            pl.BlockSpec((1, ts, D), lambda b, s: (b, s, 0)),
            pl.BlockSpec((1, 6, D), lambda b, s: (b, 0, 0)),
            pl.BlockSpec((1, D), const2),
            pl.BlockSpec(win.shape, const2),
            pl.BlockSpec(qg.shape, const2),
            pl.BlockSpec(kvg.shape, const2),
            pl.BlockSpec(wuqa.shape, const2),
            pl.BlockSpec(wuqb.shape, const2),
            pl.BlockSpec(wukv.shape, const2),
            pl.BlockSpec(fb_row.shape, const2),
            pl.BlockSpec((1, ts, LANES), lambda b, s: (b, s, 0)),
            pl.BlockSpec((1, ts, LANES), lambda b, s: (b, s, 0)),
            pl.BlockSpec(tri.shape, const2),
            pl.BlockSpec(eqk.shape, const2),
        ],
        out_specs=[
            pl.BlockSpec((1, N_HEADS, ts, LANES), lambda b, s: (b, 0, s, 0)),
            pl.BlockSpec((1, N_HEADS, ts, LANES), lambda b, s: (b, 0, s, 0)),
            pl.BlockSpec((1, 1, N_HEADS * MLA_V, ts), lambda b, s: (b, s, 0, 0)),
        ],
        out_shape=[
            jax.ShapeDtypeStruct((B, N_HEADS, S, LANES), BF16),
            jax.ShapeDtypeStruct((B, N_HEADS, S, LANES), BF16),
            jax.ShapeDtypeStruct((B, nst, N_HEADS * MLA_V, ts), BF16),
        ],
        scratch_shapes=[pltpu.VMEM((1, LANES), F32)],
        compiler_params=pltpu.CompilerParams(
            dimension_semantics=("arbitrary", "arbitrary"), vmem_limit_bytes=VMEM_LIMIT),
    )(x, mod_l, g, win, qg, kvg, wuqa, wuqb, wukv, fb_row, ctab, stab, tri, eqk)


def _colmax8(st, groups):
    tk, tq = st.shape
    v = st.reshape(groups, tk // (8 * groups), 8, tq)
    return jnp.max(jnp.max(v, axis=1), axis=0)


def _attn_kernel(q_ref, k_ref, vt_ref, o_ref, s_scr, tmax_scr, m_scr, acc_scr,
                 *, tq, tk, hp, nq):
    dv = MLA_V
    groups = 4
    ones = jnp.ones((ONES_ROWS, tk), BF16)

    def qk(qi, j):
        q0 = pl.multiple_of(qi * tq, tq)
        return [lax.dot_general(k_ref[0, h, pl.ds(pl.multiple_of(j * tk, tk), tk), :],
                                q_ref[0, h, pl.ds(q0, tq), :], (((1,), (1,)), ((), ())),
                                preferred_element_type=F32) for h in range(hp)]

    def park(sts, qi, j, masked):
        for h in range(hp):
            st = sts[h]
            if masked:
                kpos = j * tk + lax.broadcasted_iota(jnp.int32, (tk, tq), 0)
                qpos = qi * tq + lax.broadcasted_iota(jnp.int32, (tk, tq), 1)
                st = jnp.where(kpos <= qpos, st, NEG_BIG)
            s_scr[h] = st
            tmax_scr[h] = _colmax8(st, groups)

    def update(j):
        for h in range(hp):
            m = m_scr[h]
            m_new = jnp.maximum(m, jnp.max(tmax_scr[h], axis=0, keepdims=True))
            alpha = jnp.exp2(m - m_new)
            p = jnp.exp2((s_scr[h] - m_new).astype(BF16))
            vt1 = jnp.concatenate([vt_ref[0, j, h * dv:(h + 1) * dv, :], ones], axis=0)
            acc_scr[h] = alpha * acc_scr[h] + jnp.dot(vt1, p, preferred_element_type=F32)
            m_scr[h] = m_new

    def reset_state():
        m_scr[...] = jnp.full(m_scr.shape, NEG_BIG, F32)
        acc_scr[...] = jnp.zeros(acc_scr.shape, F32)

    def finish(qi):
        q0 = pl.multiple_of(qi * tq, tq)
        for h in range(hp):
            acc = acc_scr[h]
            o_ref[0, h * dv:(h + 1) * dv, pl.ds(q0, tq)] = (
                acc[0:dv, :] / acc[dv:dv + 1, :]).astype(o_ref.dtype)
        reset_state()

    reset_state()
    park(qk(0, 0), 0, 0, True)

    def q_tile(qi, c):
        nf = (qi * tq) // tk

        def below_diagonal(t):
            sts = qk(qi, t + 1)
            update(t)
            park(sts, qi, t + 1, False)

        def two_below(u, c2):
            below_diagonal(2 * u)
            below_diagonal(2 * u + 1)
            return c2

        lax.fori_loop(0, (nf - 1) // 2, two_below, 0)

        @pl.when((nf > 1) & ((nf - 1) % 2 == 1))
        def _():
            below_diagonal(nf - 2)

        @pl.when(nf > 0)
        def _():
            sts = qk(qi, nf)
            update(nf - 1)
            park(sts, qi, nf, True)

        nxt = qi + 1
        nxt_on_diagonal = (nxt * tq) // tk == 0

        @pl.when((nxt < nq) & nxt_on_diagonal)
        def _():
            sts = qk(nxt, 0)
            update(nf)
            finish(qi)
            park(sts, nxt, 0, True)

        @pl.when((nxt < nq) & jnp.logical_not(nxt_on_diagonal))
        def _():
            sts = qk(nxt, 0)
            update(nf)
            finish(qi)
            park(sts, nxt, 0, False)

        @pl.when(nxt == nq)
        def _():
            update(nf)
            finish(qi)

        return c

    lax.fori_loop(0, nq, q_tile, 0)


def _attention(qa, ka, vt, *, tq, tk, hp):
    B, H, S, _ = qa.shape
    nkt = vt.shape[1]
    assert S % tq == 0 and nkt * tk == S and H % hp == 0 and tk % tq == 0
    kern = functools.partial(_attn_kernel, tq=tq, tk=tk, hp=hp, nq=S // tq)
    return pl.pallas_call(
        kern,
        grid=(B, H // hp),
        in_specs=[
            pl.BlockSpec((1, hp, S, LANES), lambda b, g: (b, g, 0, 0)),
            pl.BlockSpec((1, hp, S, LANES), lambda b, g: (b, g, 0, 0)),
            pl.BlockSpec((1, nkt, hp * MLA_V, tk), lambda b, g: (b, 0, g, 0)),
        ],
        out_specs=pl.BlockSpec((1, hp * MLA_V, S), lambda b, g: (b, g, 0)),
        out_shape=jax.ShapeDtypeStruct((B, H * MLA_V, S), BF16),
        scratch_shapes=[
            pltpu.VMEM((hp, tk, tq), F32),
            pltpu.VMEM((hp, 8, tq), F32),
            pltpu.VMEM((hp, 1, tq), F32),
            pltpu.VMEM((hp, MLA_V + ONES_ROWS, tq), F32),
        ],
        compiler_params=pltpu.CompilerParams(
            dimension_semantics=("arbitrary", "arbitrary"), vmem_limit_bytes=VMEM_LIMIT),
    )(qa, ka, vt)


def _top2_gates(logits, lane):
    lg = jnp.where(lane < N_EXPERTS, logits, NEG_BIG)
    m1 = jnp.max(lg, axis=1, keepdims=True)
    i1 = jnp.min(jnp.where(lg == m1, lane, LANES), axis=1, keepdims=True)
    lg2 = jnp.where(lane == i1, NEG_BIG, lg)
    m2 = jnp.max(lg2, axis=1, keepdims=True)
    i2 = jnp.min(jnp.where(lg2 == m2, lane, LANES), axis=1, keepdims=True)
    e2 = jnp.exp(m2 - m1)
    den = 1.0 + e2
    return jnp.where(lane == i1, 1.0 / den, 0.0) + jnp.where(lane == i2, e2 / den, 0.0)


def _outproj_kernel(*refs, ts, with_router):
    if with_router:
        (ot_ref, x_ref, mod_ref, mg_ref, fg_ref, wo_ref, ng_ref, rw_ref,
         x1_ref, h2_ref, gates_ref) = refs
    else:
        ot_ref, x_ref, mod_ref, mg_ref, fg_ref, wo_ref, ng_ref, x1_ref, h2_ref = refs
    half = MLA_HEADS * MLA_V
    om = ot_ref[0, 0:half, :].astype(F32).T
    of = ot_ref[0, half:, :].astype(F32).T
    on = jnp.concatenate([_rms(om) * mg_ref[...], _rms(of) * fg_ref[...]], axis=1).astype(BF16)
    mix = jnp.dot(on, wo_ref[...], preferred_element_type=F32)
    x1 = x_ref[0] + mod_ref[0, 2:3, :] * mix
    x1_ref[0] = x1
    h2 = ((_rms(x1) * ng_ref[...]) * (1.0 + mod_ref[0, 4:5, :]) + mod_ref[0, 3:4, :]).astype(BF16)
    h2_ref[0] = h2
    if with_router:
        logits = jnp.dot(h2, rw_ref[...], preferred_element_type=F32)
        lane = lax.broadcasted_iota(jnp.int32, (ts, LANES), 1)
        gates_ref[0] = _top2_gates(logits, lane)


def _outproj(ot, x, mod_l, mg, fg, wo, ng, rw, *, ts):
    B, S, D = x.shape
    with_router = rw is not None
    const2 = lambda b, s: (0, 0)
    in_specs = [
        pl.BlockSpec((1, ot.shape[1], ts), lambda b, s: (b, 0, s)),
        pl.BlockSpec((1, ts, D), lambda b, s: (b, s, 0)),
        pl.BlockSpec((1, 6, D), lambda b, s: (b, 0, 0)),
        pl.BlockSpec(mg.shape, const2),
        pl.BlockSpec(fg.shape, const2),
        pl.BlockSpec(wo.shape, const2),
        pl.BlockSpec(ng.shape, const2),
    ]
    args = [ot, x, mod_l, mg, fg, wo, ng]
    out_specs = [pl.BlockSpec((1, ts, D), lambda b, s: (b, s, 0)),
                 pl.BlockSpec((1, ts, D), lambda b, s: (b, s, 0))]
    out_shape = [jax.ShapeDtypeStruct((B, S, D), F32), jax.ShapeDtypeStruct((B, S, D), BF16)]
    if with_router:
        in_specs.append(pl.BlockSpec(rw.shape, const2))
        args.append(rw)
        out_specs.append(pl.BlockSpec((1, ts, LANES), lambda b, s: (b, s, 0)))
        out_shape.append(jax.ShapeDtypeStruct((B, S, LANES), F32))
    kern = functools.partial(_outproj_kernel, ts=ts, with_router=with_router)
    return pl.pallas_call(
        kern,
        grid=(B, S // ts),
        in_specs=in_specs,
        out_specs=out_specs,
        out_shape=out_shape,
        compiler_params=pltpu.CompilerParams(
            dimension_semantics=("arbitrary", "arbitrary"), vmem_limit_bytes=VMEM_LIMIT),
    )(*args)


def _swiglu_tile(x, wg, wu):
    g = jnp.dot(x, wg, preferred_element_type=F32)
    u = jnp.dot(x, wu, preferred_element_type=F32)
    return g * _sigmoid(g) * u


def _ffn_kernel(h_ref, wg_ref, wu_ref, wd_ref, x_ref, mod_ref, fin_ref, o_ref, acc_ref,
                *, final_norm):
    f = pl.program_id(1)

    @pl.when(f == 0)
    def _():
        acc_ref[...] = jnp.zeros_like(acc_ref)

    a = _swiglu_tile(h_ref[...], wg_ref[...], wu_ref[...])
    acc_ref[...] += jnp.dot(a.astype(BF16), wd_ref[...], preferred_element_type=F32)

    @pl.when(f == pl.num_programs(1) - 1)
    def _():
        x2 = x_ref[...] + mod_ref[0, 5:6, :] * acc_ref[...]
        if final_norm:
            x2 = _rms(x2) * fin_ref[...]
        o_ref[...] = x2


def _ffn(h2, wg, wu, wd, x1, mod_l, fin_g, *, tm, tf, seq, final_norm):
    T, D = h2.shape
    F = wd.shape[0]
    tiles_per_batch = seq // tm
    kern = functools.partial(_ffn_kernel, final_norm=final_norm)
    return pl.pallas_call(
        kern,
        grid=(T // tm, F // tf),
        in_specs=[
            pl.BlockSpec((tm, D), lambda i, f: (i, 0)),
            pl.BlockSpec((D, tf), lambda i, f: (0, f)),
            pl.BlockSpec((D, tf), lambda i, f: (0, f)),
            pl.BlockSpec((tf, D), lambda i, f: (f, 0)),
            pl.BlockSpec((tm, D), lambda i, f: (i, 0)),
            pl.BlockSpec((1, 6, D), lambda i, f: (i // tiles_per_batch, 0, 0)),
            pl.BlockSpec((1, D), lambda i, f: (0, 0)),
        ],
        out_specs=pl.BlockSpec((tm, D), lambda i, f: (i, 0)),
        out_shape=jax.ShapeDtypeStruct((T, D), F32),
        scratch_shapes=[pltpu.VMEM((tm, D), F32)],
        compiler_params=pltpu.CompilerParams(
            dimension_semantics=("arbitrary", "arbitrary"), vmem_limit_bytes=VMEM_LIMIT),
    )(h2, wg, wu, wd, x1, mod_l, fin_g)


RC = 64
VT = 16
SPB = 8


def _route_kernel(gates_ref, rank_ref, rankt_ref, gatest_ref, cnt_ref, *, tm):
    gates = gates_ref[...]
    sel = gates > 0.0
    self32 = jnp.where(sel, 1.0, 0.0)
    earlier = (lax.broadcasted_iota(jnp.int32, (tm, tm), 1)
               < lax.broadcasted_iota(jnp.int32, (tm, tm), 0))
    rank = jnp.dot(jnp.where(earlier, 1.0, 0.0).astype(BF16), self32.astype(BF16),
                   preferred_element_type=F32)
    rank = jnp.where(sel, rank, -1.0)
    rank_ref[...] = rank
    rankt_ref[0] = rank.T
    gatest_ref[0] = gates.T
    cnt_ref[0] = jnp.sum(self32, axis=0, keepdims=True)


def _route(gates, *, tm):
    T = gates.shape[0]
    nt = T // tm
    return pl.pallas_call(
        functools.partial(_route_kernel, tm=tm),
        grid=(nt,),
        in_specs=[pl.BlockSpec((tm, LANES), lambda i: (i, 0))],
        out_specs=[
            pl.BlockSpec((tm, LANES), lambda i: (i, 0)),
            pl.BlockSpec((1, LANES, tm), lambda i: (i, 0, 0)),
            pl.BlockSpec((1, LANES, tm), lambda i: (i, 0, 0)),
            pl.BlockSpec((1, 1, LANES), lambda i: (i, 0, 0)),
        ],
        out_shape=[
            jax.ShapeDtypeStruct((T, LANES), F32),
            jax.ShapeDtypeStruct((nt, LANES, tm), F32),
            jax.ShapeDtypeStruct((nt, LANES, tm), F32),
            jax.ShapeDtypeStruct((nt, 1, LANES), F32),
        ],
        compiler_params=pltpu.CompilerParams(
            dimension_semantics=("arbitrary",), vmem_limit_bytes=VMEM_LIMIT),
    )(gates)


def _slot_tables(cnt, nt, ns, n_vt):
    E = N_EXPERTS
    nch = (cnt + (RC - 1)) // RC
    nct = jnp.sum(nch, axis=1)
    ccum = jnp.cumsum(nch, axis=1)
    j = jnp.arange(ns, dtype=jnp.int32)
    slot_e = jnp.sum((ccum[:, None, :] <= j[None, :, None]).astype(jnp.int32), axis=2)
    slot_e = jnp.minimum(slot_e, E - 1)
    first = jnp.take_along_axis(ccum - nch, slot_e, axis=1)
    valid = j[None, :] < nct[:, None]
    slot_k = jnp.where(valid, j[None, :] - first, 0)
    nce = jnp.sum(nch, axis=0)
    padded = (nce + (VT - 1)) // VT * VT
    base = jnp.cumsum(padded) - padded
    cid0 = base[None, :] + jnp.cumsum(nch, axis=0) - nch
    cid = jnp.take_along_axis(cid0, slot_e, axis=1) + slot_k
    cid = jnp.where(valid, cid, cid[:, 0:1])
    n_chunks = n_vt * VT
    slot_id = jnp.arange(nt, dtype=jnp.int32)[:, None] * ns + j[None, :]
    pos = jnp.full((n_chunks,), -1, jnp.int32).at[
        jnp.where(valid, cid, n_chunks).reshape(-1)].set(slot_id.reshape(-1), mode="drop")
    vcum = jnp.cumsum(padded // VT)
    nv = vcum[-1]
    v = jnp.minimum(jnp.arange(n_vt, dtype=jnp.int32), nv - 1)
    v_exp = jnp.minimum(jnp.searchsorted(vcum, v, side="right"), E - 1).astype(jnp.int32)
    i32 = lambda a: a.reshape(-1).astype(jnp.int32)
    return dict(slot_e=i32(slot_e), slot_k=i32(slot_k), nct=i32(nct), cid=i32(cid),
                pos=jnp.maximum(pos, 0), real=(pos >= 0).astype(jnp.int32),
                v_exp=v_exp, nv=i32(nv))


def _compact_kernel(se_ref, sk_ref, nct_ref, h_ref, rankt_ref, gatest_ref, xs_ref, gc_ref, oh_scr,
                    *, tm, ns):
    t = pl.program_id(0)
    n = nct_ref[t]

    def build(j, c):
        e = se_ref[t * ns + j]
        rrow = rankt_ref[0, pl.ds(e, 1), :].astype(jnp.int32)
        grow = gatest_ref[0, pl.ds(e, 1), :]
        row = lax.broadcasted_iota(jnp.int32, (RC, tm), 0) + sk_ref[t * ns + j] * RC
        hit = (rrow == row) & (j < n)
        oh_scr[j] = jnp.where(hit, 1.0, 0.0).astype(BF16)
        gcol = jnp.sum(jnp.where(hit, grow, 0.0), axis=1, keepdims=True)
        gc_ref[pl.ds(pl.multiple_of(j * RC, RC), RC), :] = jnp.broadcast_to(gcol, (RC, LANES))
        return c

    lax.fori_loop(0, ns, build, 0)
    rows = SPB * RC
    for b in range(ns // SPB):
        oh = oh_scr[b * SPB:(b + 1) * SPB].reshape(rows, tm)
        xs_ref[b * rows:(b + 1) * rows, :] = jnp.dot(
            oh, h_ref[...], preferred_element_type=F32).astype(BF16)


def _compact(tabs, h2, rankt, gatest, *, tm, ns):
    T, D = h2.shape
    nt = T // tm
    grid_spec = pltpu.PrefetchScalarGridSpec(
        num_scalar_prefetch=3,
        grid=(nt,),
        in_specs=[
            pl.BlockSpec((tm, D), lambda t, *_: (t, 0)),
            pl.BlockSpec((1, LANES, tm), lambda t, *_: (t, 0, 0)),
            pl.BlockSpec((1, LANES, tm), lambda t, *_: (t, 0, 0)),
        ],
        out_specs=[
            pl.BlockSpec((ns * RC, D), lambda t, *_: (t, 0)),
            pl.BlockSpec((ns * RC, LANES), lambda t, *_: (t, 0)),
        ],
        scratch_shapes=[pltpu.VMEM((ns, RC, tm), BF16)],
    )
    return pl.pallas_call(
        functools.partial(_compact_kernel, tm=tm, ns=ns),
        grid_spec=grid_spec,
        out_shape=[jax.ShapeDtypeStruct((nt * ns * RC, D), BF16),
                   jax.ShapeDtypeStruct((nt * ns * RC, LANES), F32)],
        compiler_params=pltpu.CompilerParams(
            dimension_semantics=("arbitrary",), vmem_limit_bytes=VMEM_LIMIT),
    )(tabs["slot_e"], tabs["slot_k"], tabs["nct"], h2, rankt, gatest)


def _slot_experts_kernel(vexp_ref, nv_ref, pos_ref, real_ref, *refs):
    xs_refs = refs[0:VT]
    gc_refs = refs[VT:2 * VT]
    wg_ref, wu_ref, wd_ref, ys_ref, x_scr, g_scr, acc_ref = refs[2 * VT:]
    v = pl.program_id(0)
    f = pl.program_id(1)

    @pl.when(v < nv_ref[0])
    def _():
        @pl.when(f == 0)
        def _():
            for j in range(VT):
                keep = real_ref[v * VT + j] == 1
                x_scr[j * RC:(j + 1) * RC, :] = jnp.where(keep, xs_refs[j][...], jnp.zeros_like(xs_refs[j]))
                g_scr[j * RC:(j + 1) * RC, :] = jnp.where(keep, gc_refs[j][...], 0.0)
            acc_ref[...] = jnp.zeros_like(acc_ref)

        a = _swiglu_tile(x_scr[...], wg_ref[0], wu_ref[0]) * g_scr[:, 0:1]
        acc_ref[...] += jnp.dot(a.astype(BF16), wd_ref[0], preferred_element_type=F32)

        @pl.when(f == pl.num_programs(1) - 1)
        def _():
            ys_ref[...] = acc_ref[...].astype(BF16)


def _slot_experts(tabs, xs, gc, wg, wu, wd, *, tf, n_vt):
    D = xs.shape[1]
    NF = wd.shape[1] // tf
    rows = VT * RC

    def live(v, nv):
        return jnp.minimum(v, nv[0] - 1)

    def hidden(v, f, nv):
        return jnp.where(v < nv[0], f, NF - 1)

    def chunk(j):
        return lambda v, f, vexp, nv, pos, real: (pos[live(v, nv) * VT + j], 0)

    grid_spec = pltpu.PrefetchScalarGridSpec(
        num_scalar_prefetch=4,
        grid=(n_vt, NF),
        in_specs=(
            [pl.BlockSpec((RC, D), chunk(j)) for j in range(VT)]
            + [pl.BlockSpec((RC, LANES), chunk(j)) for j in range(VT)]
            + [pl.BlockSpec((1, D, tf),
                            lambda v, f, vexp, nv, pos, real: (vexp[v], 0, hidden(v, f, nv)))] * 2
            + [pl.BlockSpec((1, tf, D),
                            lambda v, f, vexp, nv, pos, real: (vexp[v], hidden(v, f, nv), 0))]),
        out_specs=pl.BlockSpec((rows, D), lambda v, f, vexp, nv, pos, real: (live(v, nv), 0)),
        scratch_shapes=[pltpu.VMEM((rows, D), BF16), pltpu.VMEM((rows, LANES), F32),
                        pltpu.VMEM((rows, D), F32)],
    )
    return pl.pallas_call(
        _slot_experts_kernel,
        grid_spec=grid_spec,
        out_shape=jax.ShapeDtypeStruct((n_vt * rows, D), BF16),
        compiler_params=pltpu.CompilerParams(
            dimension_semantics=("arbitrary", "arbitrary"), vmem_limit_bytes=VMEM_LIMIT),
    )(tabs["v_exp"], tabs["nv"], tabs["pos"], tabs["real"], *([xs] * VT), *([gc] * VT), wg, wu, wd)


def _uncompact_kernel(se_ref, sk_ref, nct_ref, cid_ref, *refs, tm, ns, final_norm):
    ys_refs = refs[0:ns]
    rank_ref, x_ref, mod_ref, fin_ref, o_ref, oh_scr = refs[ns:]
    t = pl.program_id(0)
    n = nct_ref[t]
    lane = lax.broadcasted_iota(jnp.int32, (tm, LANES), 1)
    right = lane >= RC
    col = jnp.where(right, lane - RC, lane)
    rank = rank_ref[...]

    def slot_rank(j):
        return jnp.sum(jnp.where(lane == se_ref[t * ns + j], rank, 0.0), axis=1,
                       keepdims=True).astype(jnp.int32)

    for jp in range(ns // 2):
        ja, jb = 2 * jp, 2 * jp + 1
        rcol = jnp.where(right, slot_rank(jb), slot_rank(ja))
        want = col + jnp.where(right, sk_ref[t * ns + jb], sk_ref[t * ns + ja]) * RC
        used = jnp.where(right, jb, ja) < n
        oh_scr[:, jp * LANES:(jp + 1) * LANES] = jnp.where(
            (rcol == want) & used, 1.0, 0.0).astype(BF16)
    y = None
    rows = SPB * RC
    for b in range(ns // SPB):
        ys = jnp.concatenate([ys_refs[b * SPB + j][...] for j in range(SPB)], axis=0)
        part = jnp.dot(oh_scr[:, b * rows:(b + 1) * rows], ys, preferred_element_type=F32)
        y = part if y is None else y + part
    x2 = x_ref[...] + mod_ref[0, 5:6, :] * y
    if final_norm:
        x2 = _rms(x2) * fin_ref[...]
    o_ref[...] = x2


def _uncompact(tabs, ys, rank, x1, mod_l, fin_g, *, tm, ns, seq, final_norm):
    T, D = x1.shape
    tpb = seq // tm

    def chunk(j):
        return lambda t, se, sk, nct, cid: (cid[t * ns + j], 0)

    grid_spec = pltpu.PrefetchScalarGridSpec(
        num_scalar_prefetch=4,
        grid=(T // tm,),
        in_specs=(
            [pl.BlockSpec((RC, D), chunk(j)) for j in range(ns)]
            + [pl.BlockSpec((tm, LANES), lambda t, *_: (t, 0)),
               pl.BlockSpec((tm, D), lambda t, *_: (t, 0)),
               pl.BlockSpec((1, 6, D), lambda t, *_: (t // tpb, 0, 0)),
               pl.BlockSpec((1, D), lambda t, *_: (0, 0))]),
        out_specs=pl.BlockSpec((tm, D), lambda t, *_: (t, 0)),
        scratch_shapes=[pltpu.VMEM((tm, ns * RC), BF16)],
    )
    return pl.pallas_call(
        functools.partial(_uncompact_kernel, tm=tm, ns=ns, final_norm=final_norm),
        grid_spec=grid_spec,
        out_shape=jax.ShapeDtypeStruct((T, D), F32),
        compiler_params=pltpu.CompilerParams(
            dimension_semantics=("arbitrary",), vmem_limit_bytes=VMEM_LIMIT),
    )(tabs["slot_e"], tabs["slot_k"], tabs["nct"], tabs["cid"], *([ys] * ns),
      rank, x1, mod_l, fin_g)


def _routed_experts(h2, gates, w_gate, w_up, w_down, x1, mod_l, fin_g, *, tm, tf, seq, final_norm):
    T, D = h2.shape
    E = N_EXPERTS
    nt = T // tm
    ns = 2 * tm // RC + E
    assert ns % SPB == 0 and 2 * RC == LANES
    n_vt = -(-(nt * ns + E * (VT - 1)) // VT)
    rank, rankt, gatest, cnt = _route(gates, tm=tm)
    tabs = _slot_tables(cnt[:, 0, :E].astype(jnp.int32), nt, ns, n_vt)
    xs, gc = _compact(tabs, h2, rankt, gatest, tm=tm, ns=ns)
    ys = _slot_experts(tabs, xs, gc, w_gate.astype(BF16), w_up.astype(BF16), w_down.astype(BF16),
                       tf=tf, n_vt=n_vt)
    return _uncompact(tabs, ys, rank, x1, mod_l, fin_g, tm=tm, ns=ns, seq=seq,
                      final_norm=final_norm)


def _tiles(S):
    ts = min(512, S)
    tq = min(512, S)
    tk = ts
    tm = min(1024, S)
    tf = D_FF // 2
    return ts, tq, tk, tm, tf


def kernel(x, c, positions, ada_w, ada_b, attn_norm_g, w_in, q_norm_g, w_uq, kv_norm_g, w_ukv,
           fox_forget_b, mla_out_g, fox_out_g, w_o, ffn_norm_g, dense_w_gate, dense_w_up,
           dense_w_down, router_w, moe_w_gate, moe_w_up, moe_w_down, final_norm_g):
    B, S, D = x.shape
    L = ada_w.shape[0]
    ts, tq, tk, tm, tf = _tiles(S)

    mod = _adaln(c, ada_w, ada_b).reshape(L, B, 6, D)
    ctab, stab = _rope_tables(positions)

    idx_in = _perm_w_in()
    idx_qa, idx_qb = _perm_w_uq()
    idx_kv = _perm_w_ukv()
    col_scale = np.ones((C_END,), np.float32)
    col_scale[C_FQ0:C_FQ0 + FOX_W] = FOX_HEAD_DIM ** -0.5
    tri = jnp.asarray(np.tril(np.ones((ts, ts), np.float32)), BF16)
    eqk = jnp.asarray(_aug_placement(), BF16)

    for l in range(L):
        win = (_take_cols(w_in[l], idx_in) * col_scale).astype(BF16)
        wuqa = _take_cols(w_uq[l], idx_qa).astype(BF16)
        wuqb = _take_cols(w_uq[l], idx_qb).astype(BF16)
        wukv = _take_cols(w_ukv[l], idx_kv).astype(BF16)
        fb_row = jnp.zeros((1, LANES), F32).at[0, CTRL_LOGIT:CTRL_LOGIT + FOX_HEADS].set(
            fox_forget_b[l].astype(F32))
        qa, ka, vt = _inproj(
            x, mod[l], attn_norm_g[l].reshape(1, D), win, q_norm_g[l].reshape(1, -1),
            kv_norm_g[l].reshape(1, -1), wuqa, wuqb, wukv, fb_row, ctab, stab, tri, eqk, ts=ts)
        ot = _attention(qa, ka, vt, tq=tq, tk=tk, hp=4)

        j = l // 2
        is_moe = (l % 2 == 1)
        rw = None
        if is_moe:
            rw = jnp.pad(router_w[j], ((0, 0), (0, LANES - N_EXPERTS))).astype(BF16)
        outs = _outproj(ot, x, mod[l], mla_out_g[l].reshape(1, -1), fox_out_g[l].reshape(1, -1),
                        w_o[l].astype(BF16), ffn_norm_g[l].reshape(1, D), rw, ts=ts)
        x1, h2 = outs[0].reshape(B * S, D), outs[1].reshape(B * S, D)
        fin_g = final_norm_g.reshape(1, D)
        last = l == L - 1
        if is_moe:
            x = _routed_experts(h2, outs[2].reshape(B * S, LANES), moe_w_gate[j], moe_w_up[j],
                                moe_w_down[j], x1, mod[l], fin_g,
                                tm=tm, tf=tf, seq=S, final_norm=last)
        else:
            x = _ffn(h2, dense_w_gate[j].astype(BF16), dense_w_up[j].astype(BF16),
                     dense_w_down[j].astype(BF16), x1, mod[l], fin_g,
                     tm=tm, tf=tf, seq=S, final_norm=last)
        x = x.reshape(B, S, D)
    return x
```

```python
import functools

import numpy as np
import jax
import jax.numpy as jnp
from jax import lax
from jax.experimental import pallas as pl
from jax.experimental.pallas import tpu as pltpu

F32 = jnp.float32
BF16 = jnp.bfloat16

D_MODEL = 1024
MLA_HEADS = 8
MLA_V = 64
MLA_NOPE = 64
MLA_ROPE = 32
MLA_Q_LORA = 256
MLA_KV_LORA = 128
FOX_HEADS = 8
FOX_HEAD_DIM = 64
FOX_W = FOX_HEADS * FOX_HEAD_DIM
ROPE_THETA = 10000.0
D_FF = 3584
N_EXPERTS = 8
EPS = 1e-6
N_HEADS = MLA_HEADS + FOX_HEADS
HALF_ROPE = MLA_ROPE // 2

LANES = 128
VMEM_LIMIT = 56 * 1024 * 1024

ROPE_LO = MLA_NOPE
AUG_LO = FOX_HEAD_DIM
CTRL_LOGIT = 96
CTRL_ONE = 120

C_Q0 = 0
C_KV0 = C_Q0 + MLA_Q_LORA
C_KRA0 = C_KV0 + MLA_KV_LORA
C_KRB0 = C_KRA0 + LANES
C_FQ0 = C_KRB0 + LANES
C_FK0 = C_FQ0 + FOX_W
C_FV0 = C_FK0 + FOX_W
C_END = C_FV0 + FOX_W

NEG_BIG = -1e30
LOG2E = 1.4426950408889634
ONES_ROWS = 16


def _rms(v):
    return v * lax.rsqrt(jnp.mean(v * v, axis=-1, keepdims=True) + EPS)


def _sigmoid(v):
    return 1.0 / (1.0 + jnp.exp(-v))


def _split3(v):
    hi = v.astype(BF16).astype(F32)
    r = v - hi
    mid = r.astype(BF16).astype(F32)
    lo = (r - mid).astype(BF16).astype(F32)
    return hi, mid, lo


def _adaln_kernel(c_ref, w_ref, b_ref, o_ref):
    c = c_ref[...]
    ca = (c * _sigmoid(c)).astype(BF16)
    o_ref[0] = jnp.dot(ca, w_ref[0].astype(BF16), preferred_element_type=F32) + b_ref[0]


def _adaln(c, ada_w, ada_b):
    L, D, N = ada_w.shape
    B = c.shape[0]
    tn = 1536
    return pl.pallas_call(
        _adaln_kernel,
        grid=(L, N // tn),
        in_specs=[
            pl.BlockSpec((B, D), lambda l, j: (0, 0)),
            pl.BlockSpec((1, D, tn), lambda l, j: (l, 0, j)),
            pl.BlockSpec((1, 1, tn), lambda l, j: (l, 0, j)),
        ],
        out_specs=pl.BlockSpec((1, B, tn), lambda l, j: (l, 0, j)),
        out_shape=jax.ShapeDtypeStruct((L, B, N), F32),
        compiler_params=pltpu.CompilerParams(
            dimension_semantics=("arbitrary", "arbitrary"), vmem_limit_bytes=VMEM_LIMIT),
    )(c, ada_w, ada_b.reshape(L, 1, N))


def _rope_tab_kernel(pos_ref, freq_ref, cos_ref, sin_ref):
    ang = freq_ref[...] * pos_ref[0]
    cos_ref[0] = jnp.cos(ang)
    sin_ref[0] = jnp.sin(ang)


def _rope_tables(positions):
    B, S = positions.shape
    half = HALF_ROPE
    inv_freq = ROPE_THETA ** (-jnp.arange(half, dtype=F32) / half)
    pos = positions.astype(F32).reshape(B, 1, S)
    cos_t, sin_t = pl.pallas_call(
        _rope_tab_kernel,
        grid=(B,),
        in_specs=[
            pl.BlockSpec((1, 1, S), lambda b: (b, 0, 0)),
            pl.BlockSpec((half, 1), lambda b: (0, 0)),
        ],
        out_specs=[pl.BlockSpec((1, half, S), lambda b: (b, 0, 0))] * 2,
        out_shape=[jax.ShapeDtypeStruct((B, half, S), F32)] * 2,
        compiler_params=pltpu.CompilerParams(dimension_semantics=("arbitrary",)),
    )(pos, inv_freq.reshape(half, 1))
    cos = jnp.transpose(cos_t, (0, 2, 1))
    sin = jnp.transpose(sin_t, (0, 2, 1))
    ones = jnp.ones((B, S, MLA_NOPE), F32)
    z_lo = jnp.zeros((B, S, MLA_NOPE), F32)
    z_hi = jnp.zeros((B, S, LANES - MLA_NOPE - MLA_ROPE), F32)
    ctab = jnp.concatenate([ones, cos, cos, z_hi], axis=-1)
    stab = jnp.concatenate([z_lo, -sin, sin, z_hi], axis=-1)
    return ctab, stab


def _take_cols(w, idx):
    pieces, i, n = [], 0, len(idx)
    while i < n:
        j = i + 1
        if idx[i] < 0:
            while j < n and idx[j] < 0:
                j += 1
            pieces.append(jnp.zeros((w.shape[0], j - i), w.dtype))
        else:
            while j < n and idx[j] == idx[j - 1] + 1:
                j += 1
            pieces.append(w[:, int(idx[i]):int(idx[i]) + (j - i)])
        i = j
    return jnp.concatenate(pieces, axis=1)


def _perm_w_in():
    cq = 0
    ckv = cq + MLA_Q_LORA
    kr = ckv + MLA_KV_LORA
    fq = kr + MLA_ROPE
    fk = fq + FOX_W
    fv = fk + FOX_W
    fl = fv + FOX_W
    idx = -np.ones((C_END,), np.int64)
    idx[C_Q0:C_Q0 + MLA_Q_LORA] = cq + np.arange(MLA_Q_LORA)
    idx[C_KV0:C_KV0 + MLA_KV_LORA] = ckv + np.arange(MLA_KV_LORA)
    h = HALF_ROPE
    idx[C_KRA0 + ROPE_LO:C_KRA0 + ROPE_LO + MLA_ROPE] = kr + np.arange(MLA_ROPE)
    idx[C_KRA0 + CTRL_LOGIT:C_KRA0 + CTRL_LOGIT + FOX_HEADS] = fl + np.arange(FOX_HEADS)
    idx[C_KRB0 + ROPE_LO:C_KRB0 + ROPE_LO + h] = kr + h + np.arange(h)
    idx[C_KRB0 + ROPE_LO + h:C_KRB0 + ROPE_LO + MLA_ROPE] = kr + np.arange(h)
    idx[C_FQ0:C_FQ0 + FOX_W] = fq + np.arange(FOX_W)
    idx[C_FK0:C_FK0 + FOX_W] = fk + np.arange(FOX_W)
    idx[C_FV0:C_FV0 + FOX_W] = fv + np.arange(FOX_W)
    return idx


def _perm_w_uq():
    per = MLA_NOPE + MLA_ROPE
    ia = -np.ones((MLA_HEADS * LANES,), np.int64)
    ib = -np.ones((MLA_HEADS * LANES,), np.int64)
    h2 = HALF_ROPE
    for h in range(MLA_HEADS):
        ia[h * LANES:h * LANES + per] = h * per + np.arange(per)
        ib[h * LANES + ROPE_LO:h * LANES + ROPE_LO + h2] = h * per + MLA_NOPE + h2 + np.arange(h2)
        ib[h * LANES + ROPE_LO + h2:h * LANES + ROPE_LO + MLA_ROPE] = h * per + MLA_NOPE + np.arange(h2)
    return ia, ib


def _perm_w_ukv():
    per = MLA_NOPE + MLA_V
    ik = -np.ones((MLA_HEADS * LANES,), np.int64)
    iv = np.zeros((MLA_HEADS * MLA_V,), np.int64)
    for h in range(MLA_HEADS):
        ik[h * LANES:h * LANES + MLA_NOPE] = h * per + np.arange(MLA_NOPE)
        iv[h * MLA_V:(h + 1) * MLA_V] = h * per + MLA_NOPE + np.arange(MLA_V)
    return np.concatenate([ik, iv])


def _aug_placement():
    e = np.zeros((LANES, 2 * FOX_HEADS * LANES), np.float32)
    koff = FOX_HEADS * LANES
    for h in range(FOX_HEADS):
        for p in range(3):
            src = CTRL_LOGIT + 8 * p + h
            e[src, h * LANES + AUG_LO + p] = 1.0
            e[CTRL_ONE, h * LANES + AUG_LO + 3 + p] = -1.0
            e[CTRL_ONE, koff + h * LANES + AUG_LO + p] = 1.0
            e[src, koff + h * LANES + AUG_LO + 3 + p] = 1.0
    return e


def _inproj_kernel(x_ref, mod_ref, g_ref, win_ref, qg_ref, kvg_ref, wuqa_ref, wuqb_ref, wukv_ref,
                   fb_ref, ct_ref, st_ref, tri_ref, eqk_ref,
                   qa_ref, ka_ref, vt_ref, carry_ref, *, ts):
    si = pl.program_id(1)

    @pl.when(si == 0)
    def _():
        carry_ref[...] = jnp.zeros_like(carry_ref)

    x = x_ref[0]
    shift = mod_ref[0, 0:1, :]
    scale = mod_ref[0, 1:2, :]
    h = ((_rms(x) * g_ref[...]) * (1.0 + scale) + shift).astype(BF16)
    proj = jnp.dot(h, win_ref[...], preferred_element_type=F32)

    ctab = ct_ref[0]
    stab = st_ref[0]
    lane = lax.broadcasted_iota(jnp.int32, (ts, LANES), 1)

    c_q = proj[:, C_Q0:C_Q0 + MLA_Q_LORA]
    cqn = (_rms(c_q) * qg_ref[...]).astype(BF16)
    qa = jnp.dot(cqn, wuqa_ref[...], preferred_element_type=F32)
    qb = jnp.dot(cqn, wuqb_ref[...], preferred_element_type=F32)
    mla_scale = (MLA_NOPE + MLA_ROPE) ** -0.5 * LOG2E
    for hh in range(MLA_HEADS):
        sl = slice(hh * LANES, (hh + 1) * LANES)
        qa_ref[0, hh] = ((qa[:, sl] * ctab + qb[:, sl] * stab) * mla_scale).astype(BF16)

    c_kv = proj[:, C_KV0:C_KV0 + MLA_KV_LORA]
    ckvn = (_rms(c_kv) * kvg_ref[...]).astype(BF16)
    kv = jnp.dot(ckvn, wukv_ref[...], preferred_element_type=F32)
    kra = proj[:, C_KRA0:C_KRA0 + LANES]
    krb = proj[:, C_KRB0:C_KRB0 + LANES]
    krope = kra * ctab + krb * stab
    for hh in range(MLA_HEADS):
        ka_ref[0, hh] = (kv[:, hh * LANES:(hh + 1) * LANES] + krope).astype(BF16)
    v_mla = kv[:, MLA_HEADS * LANES:]

    ctrl = (lane >= CTRL_LOGIT) & (lane < CTRL_LOGIT + FOX_HEADS)
    fl = kra + fb_ref[...]
    lsig = jnp.minimum(fl, 0.0) - jnp.log1p(jnp.exp(-jnp.abs(fl)))
    lf = jnp.where(ctrl, lsig, 0.0)
    hi, mid, lo = _split3(lf)
    p1 = (hi + pltpu.roll(mid, 8, 1) + pltpu.roll(lo, 16, 1)).astype(BF16)
    cs = jnp.dot(tri_ref[...], p1, preferred_element_type=F32)
    cs = cs + pltpu.roll(cs, LANES - 8, 1) + pltpu.roll(cs, LANES - 16, 1)
    fcum = jnp.where(ctrl, cs, 0.0) + carry_ref[...]
    carry_ref[...] = fcum[ts - 1:ts, :]

    hi, mid, lo = _split3(fcum * LOG2E)
    p2 = hi + pltpu.roll(mid, 8, 1) + pltpu.roll(lo, 16, 1)
    p2 = jnp.where(lane == CTRL_ONE, 1.0, p2).astype(BF16)
    aug = jnp.dot(p2, eqk_ref[...], preferred_element_type=F32)

    fq = proj[:, C_FQ0:C_FQ0 + FOX_W]
    fk = proj[:, C_FK0:C_FK0 + FOX_W]
    low = lane < FOX_HEAD_DIM
    koff = FOX_HEADS * LANES
    for j in range(FOX_HEADS // 2):
        sl = slice(j * LANES, (j + 1) * LANES)
        for src, off, dst in ((fq, 0, qa_ref), (fk, koff, ka_ref)):
            blk = src[:, sl] * LOG2E if dst is qa_ref else src[:, sl]
            blk_r = pltpu.roll(blk, FOX_HEAD_DIM, 1)
            h0 = 2 * j
            a0 = aug[:, off + h0 * LANES:off + (h0 + 1) * LANES]
            a1 = aug[:, off + (h0 + 1) * LANES:off + (h0 + 2) * LANES]
            dst[0, MLA_HEADS + h0] = jnp.where(low, blk, a0).astype(BF16)
            dst[0, MLA_HEADS + h0 + 1] = jnp.where(low, blk_r, a1).astype(BF16)

    fv = proj[:, C_FV0:C_FV0 + FOX_W]
    vt_ref[0, 0, 0:MLA_HEADS * MLA_V, :] = v_mla.T.astype(BF16)
    vt_ref[0, 0, MLA_HEADS * MLA_V:, :] = fv.T.astype(BF16)


def _inproj(x, mod_l, g, win, qg, kvg, wuqa, wuqb, wukv, fb_row, ctab, stab, tri, eqk, *, ts):
    B, S, D = x.shape
    nst = S // ts
    const2 = lambda b, s: (0, 0)
    kern = functools.partial(_inproj_kernel, ts=ts)
    return pl.pallas_call(
        kern,
        grid=(B, nst),
        in_specs=[
            pl.BlockSpec((1, ts, D), lambda b, s: (b, s, 0)),
            pl.BlockSpec((1, 6, D), lambda b, s: (b, 0, 0)),
            pl.BlockSpec((1, D), const2),
            pl.BlockSpec(win.shape, const2),
            pl.BlockSpec(qg.shape, const2),
            pl.BlockSpec(kvg.shape, const2),
            pl.BlockSpec(wuqa.shape, const2),
            pl.BlockSpec(wuqb.shape, const2),
            pl.BlockSpec(wukv.shape, const2),
            pl.BlockSpec(fb_row.shape, const2),
            pl.BlockSpec((1, ts, LANES), lambda b, s: (b, s, 0)),
            pl.BlockSpec((1, ts, LANES), lambda b, s: (b, s, 0)),
            pl.BlockSpec(tri.shape, const2),
            pl.BlockSpec(eqk.shape, const2),
        ],
        out_specs=[
            pl.BlockSpec((1, N_HEADS, ts, LANES), lambda b, s: (b, 0, s, 0)),
            pl.BlockSpec((1, N_HEADS, ts, LANES), lambda b, s: (b, 0, s, 0)),
            pl.BlockSpec((1, 1, N_HEADS * MLA_V, ts), lambda b, s: (b, s, 0, 0)),
        ],
        out_shape=[
            jax.ShapeDtypeStruct((B, N_HEADS, S, LANES), BF16),
            jax.ShapeDtypeStruct((B, N_HEADS, S, LANES), BF16),
            jax.ShapeDtypeStruct((B, nst, N_HEADS * MLA_V, ts), BF16),
        ],
        scratch_shapes=[pltpu.VMEM((1, LANES), F32)],
        compiler_params=pltpu.CompilerParams(
            dimension_semantics=("arbitrary", "arbitrary"), vmem_limit_bytes=VMEM_LIMIT),
    )(x, mod_l, g, win, qg, kvg, wuqa, wuqb, wukv, fb_row, ctab, stab, tri, eqk)


def _colmax8(st, groups):
    tk, tq = st.shape
    v = st.reshape(groups, tk // (8 * groups), 8, tq)
    return jnp.max(jnp.max(v, axis=1), axis=0)


def _attn_kernel(q_ref, k_ref, vt_ref, o_ref, s_scr, tmax_scr, m_scr, acc_scr, *, tq, hp, nq):
    tk = tq
    half = tq // 2
    dv = MLA_V
    groups = 4
    ones = jnp.ones((ONES_ROWS, tk), BF16)
    nt_dims = (((1,), (1,)), ((), ()))

    def qk(qi, j):
        q0 = pl.multiple_of(qi * tq, tq)
        return [lax.dot_general(k_ref[0, h, pl.ds(pl.multiple_of(j * tk, tk), tk), :],
                                q_ref[0, h, pl.ds(q0, tq), :], nt_dims,
                                preferred_element_type=F32) for h in range(hp)]

    def park(sts):
        for h in range(hp):
            s_scr[h] = sts[h]
            tmax_scr[h] = _colmax8(sts[h], groups)

    def qk_diag(qi):
        q0 = pl.multiple_of(qi * tq, tq)
        q1 = pl.multiple_of(qi * tq + half, half)
        out = []
        for h in range(hp):
            top = lax.dot_general(k_ref[0, h, pl.ds(q0, half), :], q_ref[0, h, pl.ds(q0, tq), :],
                                  nt_dims, preferred_element_type=F32)
            bot = lax.dot_general(k_ref[0, h, pl.ds(q1, half), :], q_ref[0, h, pl.ds(q1, half), :],
                                  nt_dims, preferred_element_type=F32)
            out.append((top, bot))
        return out

    def park_diag(sts):
        causal = (lax.broadcasted_iota(jnp.int32, (half, half), 0)
                  <= lax.broadcasted_iota(jnp.int32, (half, half), 1))
        for h in range(hp):
            top, bot = sts[h]
            top_l = jnp.where(causal, top[:, :half], NEG_BIG)
            bot_r = jnp.where(causal, bot, NEG_BIG)
            s_scr[h, 0:half, 0:half] = top_l
            s_scr[h, 0:half, half:] = top[:, half:]
            s_scr[h, half:, half:] = bot_r
            tmax_scr[h, :, 0:half] = _colmax8(top_l, groups)
            tmax_scr[h, :, half:] = jnp.maximum(_colmax8(top[:, half:], groups),
                                                _colmax8(bot_r, groups))

    def update(j):
        for h in range(hp):
            m = m_scr[h]
            m_new = jnp.maximum(m, jnp.max(tmax_scr[h], axis=0, keepdims=True))
            alpha = jnp.exp2(m - m_new)
            p = jnp.exp2((s_scr[h] - m_new).astype(BF16))
            vt1 = jnp.concatenate([vt_ref[0, j, h * dv:(h + 1) * dv, :], ones], axis=0)
            acc_scr[h] = alpha * acc_scr[h] + jnp.dot(vt1, p, preferred_element_type=F32)
            m_scr[h] = m_new

    def update_diag(j):
        for h in range(hp):
            m = m_scr[h]
            m_new = jnp.maximum(m, jnp.max(tmax_scr[h], axis=0, keepdims=True))
            alpha = jnp.exp2(m - m_new)
            p_top = jnp.exp2((s_scr[h, 0:half, :] - m_new).astype(BF16))
            p_bot = jnp.exp2((s_scr[h, half:, half:] - m_new[:, half:]).astype(BF16))
            vt1 = jnp.concatenate([vt_ref[0, j, h * dv:(h + 1) * dv, :], ones], axis=0)
            acc = alpha * acc_scr[h] + jnp.dot(vt1[:, 0:half], p_top, preferred_element_type=F32)
            acc_scr[h, :, 0:half] = acc[:, 0:half]
            acc_scr[h, :, half:] = acc[:, half:] + jnp.dot(vt1[:, half:], p_bot,
                                                           preferred_element_type=F32)
            m_scr[h] = m_new

    def reset_state():
        m_scr[...] = jnp.full(m_scr.shape, NEG_BIG, F32)
        acc_scr[...] = jnp.zeros(acc_scr.shape, F32)

    def finish(qi):
        q0 = pl.multiple_of(qi * tq, tq)
        for h in range(hp):
            acc = acc_scr[h]
            o_ref[0, h * dv:(h + 1) * dv, pl.ds(q0, tq)] = (
                acc[0:dv, :] / acc[dv:dv + 1, :]).astype(o_ref.dtype)
        reset_state()

    reset_state()
    park_diag(qk_diag(0))

    def q_tile(qi, c):
        def below_diagonal(t):
            sts = qk(qi, t + 1)
            update(t)
            park(sts)

        def two_below(u, c2):
            below_diagonal(2 * u)
            below_diagonal(2 * u + 1)
            return c2

        lax.fori_loop(0, (qi - 1) // 2, two_below, 0)

        @pl.when((qi > 1) & ((qi - 1) % 2 == 1))
        def _():
            below_diagonal(qi - 2)

        @pl.when(qi > 0)
        def _():
            sts = qk_diag(qi)
            update(qi - 1)
            park_diag(sts)

        @pl.when(qi + 1 < nq)
        def _():
            sts = qk(qi + 1, 0)
            update_diag(qi)
            finish(qi)
            park(sts)

        @pl.when(qi + 1 == nq)
        def _():
            update_diag(qi)
            finish(qi)

        return c

    lax.fori_loop(0, nq, q_tile, 0)


def _attention(qa, ka, vt, *, tq, tk, hp):
    B, H, S, _ = qa.shape
    nkt = vt.shape[1]
    assert S % tq == 0 and nkt * tk == S and H % hp == 0 and tk == tq
    kern = functools.partial(_attn_kernel, tq=tq, hp=hp, nq=S // tq)
    return pl.pallas_call(
        kern,
        grid=(B, H // hp),
        in_specs=[
            pl.BlockSpec((1, hp, S, LANES), lambda b, g: (b, g, 0, 0)),
            pl.BlockSpec((1, hp, S, LANES), lambda b, g: (b, g, 0, 0)),
            pl.BlockSpec((1, nkt, hp * MLA_V, tk), lambda b, g: (b, 0, g, 0)),
        ],
        out_specs=pl.BlockSpec((1, hp * MLA_V, S), lambda b, g: (b, g, 0)),
        out_shape=jax.ShapeDtypeStruct((B, H * MLA_V, S), BF16),
        scratch_shapes=[
            pltpu.VMEM((hp, tk, tq), F32),
            pltpu.VMEM((hp, 8, tq), F32),
            pltpu.VMEM((hp, 1, tq), F32),
            pltpu.VMEM((hp, MLA_V + ONES_ROWS, tq), F32),
        ],
        compiler_params=pltpu.CompilerParams(
            dimension_semantics=("arbitrary", "arbitrary"), vmem_limit_bytes=VMEM_LIMIT),
    )(qa, ka, vt)


def _top2_gates(logits, lane):
    lg = jnp.where(lane < N_EXPERTS, logits, NEG_BIG)
    m1 = jnp.max(lg, axis=1, keepdims=True)
    i1 = jnp.min(jnp.where(lg == m1, lane, LANES), axis=1, keepdims=True)
    lg2 = jnp.where(lane == i1, NEG_BIG, lg)
    m2 = jnp.max(lg2, axis=1, keepdims=True)
    i2 = jnp.min(jnp.where(lg2 == m2, lane, LANES), axis=1, keepdims=True)
    e2 = jnp.exp(m2 - m1)
    den = 1.0 + e2
    return jnp.where(lane == i1, 1.0 / den, 0.0) + jnp.where(lane == i2, e2 / den, 0.0)


def _outproj_kernel(*refs, ts, with_router):
    if with_router:
        (ot_ref, x_ref, mod_ref, mg_ref, fg_ref, wo_ref, ng_ref, rw_ref,
         x1_ref, h2_ref, gates_ref) = refs
    else:
        ot_ref, x_ref, mod_ref, mg_ref, fg_ref, wo_ref, ng_ref, x1_ref, h2_ref = refs
    half = MLA_HEADS * MLA_V
    om = ot_ref[0, 0:half, :].astype(F32).T
    of = ot_ref[0, half:, :].astype(F32).T
    on = jnp.concatenate([_rms(om) * mg_ref[...], _rms(of) * fg_ref[...]], axis=1).astype(BF16)
    mix = jnp.dot(on, wo_ref[...], preferred_element_type=F32)
    x1 = x_ref[0] + mod_ref[0, 2:3, :] * mix
    x1_ref[0] = x1
    h2 = ((_rms(x1) * ng_ref[...]) * (1.0 + mod_ref[0, 4:5, :]) + mod_ref[0, 3:4, :]).astype(BF16)
    h2_ref[0] = h2
    if with_router:
        logits = jnp.dot(h2, rw_ref[...], preferred_element_type=F32)
        lane = lax.broadcasted_iota(jnp.int32, (ts, LANES), 1)
        gates_ref[0] = _top2_gates(logits, lane)


def _outproj(ot, x, mod_l, mg, fg, wo, ng, rw, *, ts):
    B, S, D = x.shape
    with_router = rw is not None
    const2 = lambda b, s: (0, 0)
    in_specs = [
        pl.BlockSpec((1, ot.shape[1], ts), lambda b, s: (b, 0, s)),
        pl.BlockSpec((1, ts, D), lambda b, s: (b, s, 0)),
        pl.BlockSpec((1, 6, D), lambda b, s: (b, 0, 0)),
        pl.BlockSpec(mg.shape, const2),
        pl.BlockSpec(fg.shape, const2),
        pl.BlockSpec(wo.shape, const2),
        pl.BlockSpec(ng.shape, const2),
    ]
    args = [ot, x, mod_l, mg, fg, wo, ng]
    out_specs = [pl.BlockSpec((1, ts, D), lambda b, s: (b, s, 0)),
                 pl.BlockSpec((1, ts, D), lambda b, s: (b, s, 0))]
    out_shape = [jax.ShapeDtypeStruct((B, S, D), F32), jax.ShapeDtypeStruct((B, S, D), BF16)]
    if with_router:
        in_specs.append(pl.BlockSpec(rw.shape, const2))
        args.append(rw)
        out_specs.append(pl.BlockSpec((1, ts, LANES), lambda b, s: (b, s, 0)))
        out_shape.append(jax.ShapeDtypeStruct((B, S, LANES), F32))
    kern = functools.partial(_outproj_kernel, ts=ts, with_router=with_router)
    return pl.pallas_call(
        kern,
        grid=(B, S // ts),
        in_specs=in_specs,
        out_specs=out_specs,
        out_shape=out_shape,
        compiler_params=pltpu.CompilerParams(
            dimension_semantics=("arbitrary", "arbitrary"), vmem_limit_bytes=VMEM_LIMIT),
    )(*args)


def _swiglu_tile(x, wg, wu):
    g = jnp.dot(x, wg, preferred_element_type=F32)
    u = jnp.dot(x, wu, preferred_element_type=F32)
    return g * _sigmoid(g) * u


def _ffn_kernel(h_ref, wg_ref, wu_ref, wd_ref, x_ref, mod_ref, fin_ref, o_ref, acc_ref,
                *, final_norm):
    f = pl.program_id(1)

    @pl.when(f == 0)
    def _():
        acc_ref[...] = jnp.zeros_like(acc_ref)

    a = _swiglu_tile(h_ref[...], wg_ref[...], wu_ref[...])
    acc_ref[...] += jnp.dot(a.astype(BF16), wd_ref[...], preferred_element_type=F32)

    @pl.when(f == pl.num_programs(1) - 1)
    def _():
        x2 = x_ref[...] + mod_ref[0, 5:6, :] * acc_ref[...]
        if final_norm:
            x2 = _rms(x2) * fin_ref[...]
        o_ref[...] = x2


def _ffn(h2, wg, wu, wd, x1, mod_l, fin_g, *, tm, tf, seq, final_norm):
    T, D = h2.shape
    F = wd.shape[0]
    tiles_per_batch = seq // tm
    kern = functools.partial(_ffn_kernel, final_norm=final_norm)
    return pl.pallas_call(
        kern,
        grid=(T // tm, F // tf),
        in_specs=[
            pl.BlockSpec((tm, D), lambda i, f: (i, 0)),
            pl.BlockSpec((D, tf), lambda i, f: (0, f)),
            pl.BlockSpec((D, tf), lambda i, f: (0, f)),
            pl.BlockSpec((tf, D), lambda i, f: (f, 0)),
            pl.BlockSpec((tm, D), lambda i, f: (i, 0)),
            pl.BlockSpec((1, 6, D), lambda i, f: (i // tiles_per_batch, 0, 0)),
            pl.BlockSpec((1, D), lambda i, f: (0, 0)),
        ],
        out_specs=pl.BlockSpec((tm, D), lambda i, f: (i, 0)),
        out_shape=jax.ShapeDtypeStruct((T, D), F32),
        scratch_shapes=[pltpu.VMEM((tm, D), F32)],
        compiler_params=pltpu.CompilerParams(
            dimension_semantics=("arbitrary", "arbitrary"), vmem_limit_bytes=VMEM_LIMIT),
    )(h2, wg, wu, wd, x1, mod_l, fin_g)


RC = 64
VT = 16
SPB = 8


def _route_kernel(gates_ref, rank_ref, rankt_ref, gatest_ref, cnt_ref, *, tm):
    gates = gates_ref[...]
    sel = gates > 0.0
    self32 = jnp.where(sel, 1.0, 0.0)
    earlier = (lax.broadcasted_iota(jnp.int32, (tm, tm), 1)
               < lax.broadcasted_iota(jnp.int32, (tm, tm), 0))
    rank = jnp.dot(jnp.where(earlier, 1.0, 0.0).astype(BF16), self32.astype(BF16),
                   preferred_element_type=F32)
    rank = jnp.where(sel, rank, -1.0)
    rank_ref[...] = rank
    rankt_ref[0] = rank.T
    gatest_ref[0] = gates.T
    cnt_ref[0] = jnp.sum(self32, axis=0, keepdims=True)


def _route(gates, *, tm):
    T = gates.shape[0]
    nt = T // tm
    return pl.pallas_call(
        functools.partial(_route_kernel, tm=tm),
        grid=(nt,),
        in_specs=[pl.BlockSpec((tm, LANES), lambda i: (i, 0))],
        out_specs=[
            pl.BlockSpec((tm, LANES), lambda i: (i, 0)),
            pl.BlockSpec((1, LANES, tm), lambda i: (i, 0, 0)),
            pl.BlockSpec((1, LANES, tm), lambda i: (i, 0, 0)),
            pl.BlockSpec((1, 1, LANES), lambda i: (i, 0, 0)),
        ],
        out_shape=[
            jax.ShapeDtypeStruct((T, LANES), F32),
            jax.ShapeDtypeStruct((nt, LANES, tm), F32),
            jax.ShapeDtypeStruct((nt, LANES, tm), F32),
            jax.ShapeDtypeStruct((nt, 1, LANES), F32),
        ],
        compiler_params=pltpu.CompilerParams(
            dimension_semantics=("arbitrary",), vmem_limit_bytes=VMEM_LIMIT),
    )(gates)


def _slot_tables(cnt, nt, ns, n_vt):
    E = N_EXPERTS
    nch = (cnt + (RC - 1)) // RC
    nct = jnp.sum(nch, axis=1)
    ccum = jnp.cumsum(nch, axis=1)
    j = jnp.arange(ns, dtype=jnp.int32)
    slot_e = jnp.sum((ccum[:, None, :] <= j[None, :, None]).astype(jnp.int32), axis=2)
    slot_e = jnp.minimum(slot_e, E - 1)
    first = jnp.take_along_axis(ccum - nch, slot_e, axis=1)
    valid = j[None, :] < nct[:, None]
    slot_k = jnp.where(valid, j[None, :] - first, 0)
    nce = jnp.sum(nch, axis=0)
    padded = (nce + (VT - 1)) // VT * VT
    base = jnp.cumsum(padded) - padded
    cid0 = base[None, :] + jnp.cumsum(nch, axis=0) - nch
    cid = jnp.take_along_axis(cid0, slot_e, axis=1) + slot_k
    cid = jnp.where(valid, cid, cid[:, 0:1])
    n_chunks = n_vt * VT
    slot_id = jnp.arange(nt, dtype=jnp.int32)[:, None] * ns + j[None, :]
    pos = jnp.full((n_chunks,), -1, jnp.int32).at[
        jnp.where(valid, cid, n_chunks).reshape(-1)].set(slot_id.reshape(-1), mode="drop")
    vcum = jnp.cumsum(padded // VT)
    nv = vcum[-1]
    v = jnp.minimum(jnp.arange(n_vt, dtype=jnp.int32), nv - 1)
    v_exp = jnp.minimum(jnp.searchsorted(vcum, v, side="right"), E - 1).astype(jnp.int32)
    i32 = lambda a: a.reshape(-1).astype(jnp.int32)
    return dict(slot_e=i32(slot_e), slot_k=i32(slot_k), nct=i32(nct), cid=i32(cid),
                pos=jnp.maximum(pos, 0), real=(pos >= 0).astype(jnp.int32),
                v_exp=v_exp, nv=i32(nv))


def _compact_kernel(se_ref, sk_ref, nct_ref, h_ref, rankt_ref, gatest_ref, xs_ref, gc_ref, oh_scr,
                    *, tm, ns):
    t = pl.program_id(0)
    n = nct_ref[t]

    def build(j, c):
        e = se_ref[t * ns + j]
        rrow = rankt_ref[0, pl.ds(e, 1), :].astype(jnp.int32)
        grow = gatest_ref[0, pl.ds(e, 1), :]
        row = lax.broadcasted_iota(jnp.int32, (RC, tm), 0) + sk_ref[t * ns + j] * RC
        hit = (rrow == row) & (j < n)
        oh_scr[j] = jnp.where(hit, 1.0, 0.0).astype(BF16)
        gcol = jnp.sum(jnp.where(hit, grow, 0.0), axis=1, keepdims=True)
        gc_ref[pl.ds(pl.multiple_of(j * RC, RC), RC), :] = jnp.broadcast_to(gcol, (RC, LANES))
        return c

    lax.fori_loop(0, ns, build, 0)
    rows = SPB * RC
    for b in range(ns // SPB):
        oh = oh_scr[b * SPB:(b + 1) * SPB].reshape(rows, tm)
        xs_ref[b * rows:(b + 1) * rows, :] = jnp.dot(
            oh, h_ref[...], preferred_element_type=F32).astype(BF16)


def _compact(tabs, h2, rankt, gatest, *, tm, ns):
    T, D = h2.shape
    nt = T // tm
    grid_spec = pltpu.PrefetchScalarGridSpec(
        num_scalar_prefetch=3,
        grid=(nt,),
        in_specs=[
            pl.BlockSpec((tm, D), lambda t, *_: (t, 0)),
            pl.BlockSpec((1, LANES, tm), lambda t, *_: (t, 0, 0)),
            pl.BlockSpec((1, LANES, tm), lambda t, *_: (t, 0, 0)),
        ],
        out_specs=[
            pl.BlockSpec((ns * RC, D), lambda t, *_: (t, 0)),
            pl.BlockSpec((ns * RC, LANES), lambda t, *_: (t, 0)),
        ],
        scratch_shapes=[pltpu.VMEM((ns, RC, tm), BF16)],
    )
    return pl.pallas_call(
        functools.partial(_compact_kernel, tm=tm, ns=ns),
        grid_spec=grid_spec,
        out_shape=[jax.ShapeDtypeStruct((nt * ns * RC, D), BF16),
                   jax.ShapeDtypeStruct((nt * ns * RC, LANES), F32)],
        compiler_params=pltpu.CompilerParams(
            dimension_semantics=("arbitrary",), vmem_limit_bytes=VMEM_LIMIT),
    )(tabs["slot_e"], tabs["slot_k"], tabs["nct"], h2, rankt, gatest)


def _slot_experts_kernel(vexp_ref, nv_ref, pos_ref, real_ref, *refs):
    xs_refs = refs[0:VT]
    gc_refs = refs[VT:2 * VT]
    wg_ref, wu_ref, wd_ref, ys_ref, x_scr, g_scr, acc_ref = refs[2 * VT:]
    v = pl.program_id(0)
    f = pl.program_id(1)

    @pl.when(v < nv_ref[0])
    def _():
        @pl.when(f == 0)
        def _():
            for j in range(VT):
                keep = real_ref[v * VT + j] == 1
                x_scr[j * RC:(j + 1) * RC, :] = jnp.where(keep, xs_refs[j][...], jnp.zeros_like(xs_refs[j]))
                g_scr[j * RC:(j + 1) * RC, :] = jnp.where(keep, gc_refs[j][...], 0.0)
            acc_ref[...] = jnp.zeros_like(acc_ref)

        a = _swiglu_tile(x_scr[...], wg_ref[0], wu_ref[0]) * g_scr[:, 0:1]
        acc_ref[...] += jnp.dot(a.astype(BF16), wd_ref[0], preferred_element_type=F32)

        @pl.when(f == pl.num_programs(1) - 1)
        def _():
            ys_ref[...] = acc_ref[...].astype(BF16)


def _slot_experts(tabs, xs, gc, wg, wu, wd, *, tf, n_vt):
    D = xs.shape[1]
    NF = wd.shape[1] // tf
    rows = VT * RC

    def live(v, nv):
        return jnp.minimum(v, nv[0] - 1)

    def hidden(v, f, nv):
        return jnp.where(v < nv[0], f, NF - 1)

    def chunk(j):
        return lambda v, f, vexp, nv, pos, real: (pos[live(v, nv) * VT + j], 0)

    grid_spec = pltpu.PrefetchScalarGridSpec(
        num_scalar_prefetch=4,
        grid=(n_vt, NF),
        in_specs=(
            [pl.BlockSpec((RC, D), chunk(j)) for j in range(VT)]
            + [pl.BlockSpec((RC, LANES), chunk(j)) for j in range(VT)]
            + [pl.BlockSpec((1, D, tf),
                            lambda v, f, vexp, nv, pos, real: (vexp[v], 0, hidden(v, f, nv)))] * 2
            + [pl.BlockSpec((1, tf, D),
                            lambda v, f, vexp, nv, pos, real: (vexp[v], hidden(v, f, nv), 0))]),
        out_specs=pl.BlockSpec((rows, D), lambda v, f, vexp, nv, pos, real: (live(v, nv), 0)),
        scratch_shapes=[pltpu.VMEM((rows, D), BF16), pltpu.VMEM((rows, LANES), F32),
                        pltpu.VMEM((rows, D), F32)],
    )
    return pl.pallas_call(
        _slot_experts_kernel,
        grid_spec=grid_spec,
        out_shape=jax.ShapeDtypeStruct((n_vt * rows, D), BF16),
        compiler_params=pltpu.CompilerParams(
            dimension_semantics=("arbitrary", "arbitrary"), vmem_limit_bytes=VMEM_LIMIT),
    )(tabs["v_exp"], tabs["nv"], tabs["pos"], tabs["real"], *([xs] * VT), *([gc] * VT), wg, wu, wd)


def _uncompact_kernel(se_ref, sk_ref, nct_ref, cid_ref, *refs, tm, ns, final_norm):
    ys_refs = refs[0:ns]
    rank_ref, x_ref, mod_ref, fin_ref, o_ref, oh_scr = refs[ns:]
    t = pl.program_id(0)
    n = nct_ref[t]
    lane = lax.broadcasted_iota(jnp.int32, (tm, LANES), 1)
    right = lane >= RC
    col = jnp.where(right, lane - RC, lane)
    rank = rank_ref[...]

    def slot_rank(j):
        return jnp.sum(jnp.where(lane == se_ref[t * ns + j], rank, 0.0), axis=1,
                       keepdims=True).astype(jnp.int32)

    for jp in range(ns // 2):
        ja, jb = 2 * jp, 2 * jp + 1
        rcol = jnp.where(right, slot_rank(jb), slot_rank(ja))
        want = col + jnp.where(right, sk_ref[t * ns + jb], sk_ref[t * ns + ja]) * RC
        used = jnp.where(right, jb, ja) < n
        oh_scr[:, jp * LANES:(jp + 1) * LANES] = jnp.where(
            (rcol == want) & used, 1.0, 0.0).astype(BF16)
    y = None
    rows = SPB * RC
    for b in range(ns // SPB):
        ys = jnp.concatenate([ys_refs[b * SPB + j][...] for j in range(SPB)], axis=0)
        part = jnp.dot(oh_scr[:, b * rows:(b + 1) * rows], ys, preferred_element_type=F32)
        y = part if y is None else y + part
    x2 = x_ref[...] + mod_ref[0, 5:6, :] * y
    if final_norm:
        x2 = _rms(x2) * fin_ref[...]
    o_ref[...] = x2


def _uncompact(tabs, ys, rank, x1, mod_l, fin_g, *, tm, ns, seq, final_norm):
    T, D = x1.shape
    tpb = seq // tm

    def chunk(j):
        return lambda t, se, sk, nct, cid: (cid[t * ns + j], 0)

    grid_spec = pltpu.PrefetchScalarGridSpec(
        num_scalar_prefetch=4,
        grid=(T // tm,),
        in_specs=(
            [pl.BlockSpec((RC, D), chunk(j)) for j in range(ns)]
            + [pl.BlockSpec((tm, LANES), lambda t, *_: (t, 0)),
               pl.BlockSpec((tm, D), lambda t, *_: (t, 0)),
               pl.BlockSpec((1, 6, D), lambda t, *_: (t // tpb, 0, 0)),
               pl.BlockSpec((1, D), lambda t, *_: (0, 0))]),
        out_specs=pl.BlockSpec((tm, D), lambda t, *_: (t, 0)),
        scratch_shapes=[pltpu.VMEM((tm, ns * RC), BF16)],
    )
    return pl.pallas_call(
        functools.partial(_uncompact_kernel, tm=tm, ns=ns, final_norm=final_norm),
        grid_spec=grid_spec,
        out_shape=jax.ShapeDtypeStruct((T, D), F32),
        compiler_params=pltpu.CompilerParams(
            dimension_semantics=("arbitrary",), vmem_limit_bytes=VMEM_LIMIT),
    )(tabs["slot_e"], tabs["slot_k"], tabs["nct"], tabs["cid"], *([ys] * ns),
      rank, x1, mod_l, fin_g)


def _routed_experts(h2, gates, w_gate, w_up, w_down, x1, mod_l, fin_g, *, tm, tf, seq, final_norm):
    T, D = h2.shape
    E = N_EXPERTS
    nt = T // tm
    ns = 2 * tm // RC + E
    assert ns % SPB == 0 and 2 * RC == LANES
    n_vt = -(-(nt * ns + E * (VT - 1)) // VT)
    rank, rankt, gatest, cnt = _route(gates, tm=tm)
    tabs = _slot_tables(cnt[:, 0, :E].astype(jnp.int32), nt, ns, n_vt)
    xs, gc = _compact(tabs, h2, rankt, gatest, tm=tm, ns=ns)
    ys = _slot_experts(tabs, xs, gc, w_gate.astype(BF16), w_up.astype(BF16), w_down.astype(BF16),
                       tf=tf, n_vt=n_vt)
    return _uncompact(tabs, ys, rank, x1, mod_l, fin_g, tm=tm, ns=ns, seq=seq,
                      final_norm=final_norm)


def _tiles(S):
    ts = min(512, S)
    tq = min(512, S)
    tk = ts
    tm = min(1024, S)
    tf = D_FF // 2
    return ts, tq, tk, tm, tf


def kernel(x, c, positions, ada_w, ada_b, attn_norm_g, w_in, q_norm_g, w_uq, kv_norm_g, w_ukv,
           fox_forget_b, mla_out_g, fox_out_g, w_o, ffn_norm_g, dense_w_gate, dense_w_up,
           dense_w_down, router_w, moe_w_gate, moe_w_up, moe_w_down, final_norm_g):
    B, S, D = x.shape
    L = ada_w.shape[0]
    ts, tq, tk, tm, tf = _tiles(S)

    mod = _adaln(c, ada_w, ada_b).reshape(L, B, 6, D)
    ctab, stab = _rope_tables(positions)

    idx_in = _perm_w_in()
    idx_qa, idx_qb = _perm_w_uq()
    idx_kv = _perm_w_ukv()
    col_scale = np.ones((C_END,), np.float32)
    col_scale[C_FQ0:C_FQ0 + FOX_W] = FOX_HEAD_DIM ** -0.5
    tri = jnp.asarray(np.tril(np.ones((ts, ts), np.float32)), BF16)
    eqk = jnp.asarray(_aug_placement(), BF16)

    for l in range(L):
        win = (_take_cols(w_in[l], idx_in) * col_scale).astype(BF16)
        wuqa = _take_cols(w_uq[l], idx_qa).astype(BF16)
        wuqb = _take_cols(w_uq[l], idx_qb).astype(BF16)
        wukv = _take_cols(w_ukv[l], idx_kv).astype(BF16)
        fb_row = jnp.zeros((1, LANES), F32).at[0, CTRL_LOGIT:CTRL_LOGIT + FOX_HEADS].set(
            fox_forget_b[l].astype(F32))
        qa, ka, vt = _inproj(
            x, mod[l], attn_norm_g[l].reshape(1, D), win, q_norm_g[l].reshape(1, -1),
            kv_norm_g[l].reshape(1, -1), wuqa, wuqb, wukv, fb_row, ctab, stab, tri, eqk, ts=ts)
        ot = _attention(qa, ka, vt, tq=tq, tk=tk, hp=4)

        j = l // 2
        is_moe = (l % 2 == 1)
        rw = None
        if is_moe:
            rw = jnp.pad(router_w[j], ((0, 0), (0, LANES - N_EXPERTS))).astype(BF16)
        outs = _outproj(ot, x, mod[l], mla_out_g[l].reshape(1, -1), fox_out_g[l].reshape(1, -1),
                        w_o[l].astype(BF16), ffn_norm_g[l].reshape(1, D), rw, ts=ts)
        x1, h2 = outs[0].reshape(B * S, D), outs[1].reshape(B * S, D)
        fin_g = final_norm_g.reshape(1, D)
        last = l == L - 1
        if is_moe:
            x = _routed_experts(h2, outs[2].reshape(B * S, LANES), moe_w_gate[j], moe_w_up[j],
                                moe_w_down[j], x1, mod[l], fin_g,
                                tm=tm, tf=tf, seq=S, final_norm=last)
        else:
            x = _ffn(h2, dense_w_gate[j].astype(BF16), dense_w_up[j].astype(BF16),
                     dense_w_down[j].astype(BF16), x1, mod[l], fin_g,
                     tm=tm, tf=tf, seq=S, final_norm=last)
        x = x.reshape(B, S, D)
    return x
```

```python
import functools

import numpy as np
import jax
import jax.numpy as jnp
from jax import lax
from jax.experimental import pallas as pl
from jax.experimental.pallas import tpu as pltpu

F32 = jnp.float32
BF16 = jnp.bfloat16

D_MODEL = 1024
MLA_HEADS = 8
MLA_V = 64
MLA_NOPE = 64
MLA_ROPE = 32
MLA_Q_LORA = 256
MLA_KV_LORA = 128
FOX_HEADS = 8
FOX_HEAD_DIM = 64
FOX_W = FOX_HEADS * FOX_HEAD_DIM
ROPE_THETA = 10000.0
D_FF = 3584
N_EXPERTS = 8
EPS = 1e-6
N_HEADS = MLA_HEADS + FOX_HEADS
HALF_ROPE = MLA_ROPE // 2

LANES = 128
VMEM_LIMIT = 56 * 1024 * 1024

ROPE_LO = MLA_NOPE
AUG_LO = FOX_HEAD_DIM
AUG_W = 6
CTRL_LOGIT = 96
CTRL_ONE = 120

C_Q0 = 0
C_KV0 = C_Q0 + MLA_Q_LORA
C_KRA0 = C_KV0 + MLA_KV_LORA
C_FQ0 = C_KRA0 + LANES
C_FK0 = C_FQ0 + FOX_W
C_FV0 = C_FK0 + FOX_W
C_END = C_FV0 + FOX_W

NEG_BIG = -1e30
LOG2E = 1.4426950408889634
ONES_ROWS = 16


def _rms(v):
    return v * lax.rsqrt(jnp.mean(v * v, axis=-1, keepdims=True) + EPS)


def _sigmoid(v):
    return 1.0 / (1.0 + jnp.exp(-v))


def _split3(v):
    hi = v.astype(BF16).astype(F32)
    r = v - hi
    mid = r.astype(BF16).astype(F32)
    lo = (r - mid).astype(BF16).astype(F32)
    return hi, mid, lo


def _adaln_kernel(c_ref, w_ref, b_ref, o_ref):
    c = c_ref[...]
    ca = (c * _sigmoid(c)).astype(BF16)
    o_ref[0] = jnp.dot(ca, w_ref[0].astype(BF16), preferred_element_type=F32) + b_ref[0]


def _adaln(c, ada_w, ada_b):
    L, D, N = ada_w.shape
    B = c.shape[0]
    tn = 1536
    return pl.pallas_call(
        _adaln_kernel,
        grid=(L, N // tn),
        in_specs=[
            pl.BlockSpec((B, D), lambda l, j: (0, 0)),
            pl.BlockSpec((1, D, tn), lambda l, j: (l, 0, j)),
            pl.BlockSpec((1, 1, tn), lambda l, j: (l, 0, j)),
        ],
        out_specs=pl.BlockSpec((1, B, tn), lambda l, j: (l, 0, j)),
        out_shape=jax.ShapeDtypeStruct((L, B, N), F32),
        compiler_params=pltpu.CompilerParams(
            dimension_semantics=("arbitrary", "arbitrary"), vmem_limit_bytes=VMEM_LIMIT),
    )(c, ada_w, ada_b.reshape(L, 1, N))


def _rope_tab_kernel(pos_ref, freq_ref, cos_ref, sin_ref):
    ang = freq_ref[...] * pos_ref[0]
    cos_ref[0] = jnp.cos(ang)
    sin_ref[0] = jnp.sin(ang)


def _rope_tables(positions):
    B, S = positions.shape
    half = HALF_ROPE
    inv_freq = ROPE_THETA ** (-jnp.arange(half, dtype=F32) / half)
    pos = positions.astype(F32).reshape(B, 1, S)
    cos_t, sin_t = pl.pallas_call(
        _rope_tab_kernel,
        grid=(B,),
        in_specs=[
            pl.BlockSpec((1, 1, S), lambda b: (b, 0, 0)),
            pl.BlockSpec((half, 1), lambda b: (0, 0)),
        ],
        out_specs=[pl.BlockSpec((1, half, S), lambda b: (b, 0, 0))] * 2,
        out_shape=[jax.ShapeDtypeStruct((B, half, S), F32)] * 2,
        compiler_params=pltpu.CompilerParams(dimension_semantics=("arbitrary",)),
    )(pos, inv_freq.reshape(half, 1))
    cos = jnp.transpose(cos_t, (0, 2, 1))
    sin = jnp.transpose(sin_t, (0, 2, 1))
    ones = jnp.ones((B, S, MLA_NOPE), F32)
    z_lo = jnp.zeros((B, S, MLA_NOPE), F32)
    z_hi = jnp.zeros((B, S, LANES - MLA_NOPE - MLA_ROPE), F32)
    ctab = jnp.concatenate([ones, cos, cos, z_hi], axis=-1)
    stab = jnp.concatenate([z_lo, -sin, sin, z_hi], axis=-1)
    return ctab, stab


def _take_cols(w, idx):
    pieces, i, n = [], 0, len(idx)
    while i < n:
        j = i + 1
        if idx[i] < 0:
            while j < n and idx[j] < 0:
                j += 1
            pieces.append(jnp.zeros((w.shape[0], j - i), w.dtype))
        else:
            while j < n and idx[j] == idx[j - 1] + 1:
                j += 1
            pieces.append(w[:, int(idx[i]):int(idx[i]) + (j - i)])
        i = j
    return jnp.concatenate(pieces, axis=1)


def _perm_w_in():
    cq = 0
    ckv = cq + MLA_Q_LORA
    kr = ckv + MLA_KV_LORA
    fq = kr + MLA_ROPE
    fk = fq + FOX_W
    fv = fk + FOX_W
    fl = fv + FOX_W
    idx = -np.ones((C_END,), np.int64)
    idx[C_Q0:C_Q0 + MLA_Q_LORA] = cq + np.arange(MLA_Q_LORA)
    idx[C_KV0:C_KV0 + MLA_KV_LORA] = ckv + np.arange(MLA_KV_LORA)
    idx[C_KRA0 + ROPE_LO:C_KRA0 + ROPE_LO + MLA_ROPE] = kr + np.arange(MLA_ROPE)
    idx[C_KRA0 + CTRL_LOGIT:C_KRA0 + CTRL_LOGIT + FOX_HEADS] = fl + np.arange(FOX_HEADS)
    idx[C_FQ0:C_FQ0 + FOX_W] = fq + np.arange(FOX_W)
    idx[C_FK0:C_FK0 + FOX_W] = fk + np.arange(FOX_W)
    idx[C_FV0:C_FV0 + FOX_W] = fv + np.arange(FOX_W)
    return idx


def _perm_w_uq():
    per = MLA_NOPE + MLA_ROPE
    ia = -np.ones((MLA_HEADS * LANES,), np.int64)
    for h in range(MLA_HEADS):
        ia[h * LANES:h * LANES + per] = h * per + np.arange(per)
    return ia


def _perm_w_ukv():
    per = MLA_NOPE + MLA_V
    ik = -np.ones((MLA_HEADS * LANES,), np.int64)
    iv = np.zeros((MLA_HEADS * MLA_V,), np.int64)
    for h in range(MLA_HEADS):
        ik[h * LANES:h * LANES + MLA_NOPE] = h * per + np.arange(MLA_NOPE)
        iv[h * MLA_V:(h + 1) * MLA_V] = h * per + MLA_NOPE + np.arange(MLA_V)
    return np.concatenate([ik, iv])


def _aug_placement():
    e = np.zeros((LANES, 2 * LANES), np.float32)
    for h in range(FOX_HEADS):
        lo = AUG_LO + AUG_W * h
        for p in range(3):
            src = CTRL_LOGIT + 8 * p + h
            e[src, lo + p] = 1.0
            e[CTRL_ONE, lo + 3 + p] = -1.0
            e[CTRL_ONE, LANES + lo + p] = 1.0
            e[src, LANES + lo + 3 + p] = 1.0
    return e


def _inproj_kernel(x_ref, mod_ref, g_ref, win_ref, qg_ref, kvg_ref, wuqa_ref, wukv_ref,
                   fb_ref, ct_ref, st_ref, tri_ref, eqk_ref,
                   qa_ref, ka_ref, vt_ref, carry_ref, *, ts):
    si = pl.program_id(1)

    @pl.when(si == 0)
    def _():
        carry_ref[...] = jnp.zeros_like(carry_ref)

    x = x_ref[0]
    shift = mod_ref[0, 0:1, :]
    scale = mod_ref[0, 1:2, :]
    h = ((_rms(x) * g_ref[...]) * (1.0 + scale) + shift).astype(BF16)
    proj = jnp.dot(h, win_ref[...], preferred_element_type=F32)

    ctab = ct_ref[0]
    stab = st_ref[0]
    lane = lax.broadcasted_iota(jnp.int32, (ts, LANES), 1)

    c_q = proj[:, C_Q0:C_Q0 + MLA_Q_LORA]
    cqn = (_rms(c_q) * qg_ref[...]).astype(BF16)
    qa = jnp.dot(cqn, wuqa_ref[...], preferred_element_type=F32)
    mla_scale = (MLA_NOPE + MLA_ROPE) ** -0.5 * LOG2E
    first_half = lane < ROPE_LO + HALF_ROPE

    def rope(blk):
        swapped = jnp.where(first_half, pltpu.roll(blk, LANES - HALF_ROPE, 1),
                            pltpu.roll(blk, HALF_ROPE, 1))
        return blk * ctab + swapped * stab

    for hh in range(MLA_HEADS):
        qa_ref[0, hh] = (rope(qa[:, hh * LANES:(hh + 1) * LANES]) * mla_scale).astype(BF16)

    c_kv = proj[:, C_KV0:C_KV0 + MLA_KV_LORA]
    ckvn = (_rms(c_kv) * kvg_ref[...]).astype(BF16)
    kv = jnp.dot(ckvn, wukv_ref[...], preferred_element_type=F32)
    kra = proj[:, C_KRA0:C_KRA0 + LANES]
    krope = rope(kra)
    for hh in range(MLA_HEADS):
        ka_ref[0, hh] = (kv[:, hh * LANES:(hh + 1) * LANES] + krope).astype(BF16)
    v_mla = kv[:, MLA_HEADS * LANES:]

    ctrl = (lane >= CTRL_LOGIT) & (lane < CTRL_LOGIT + FOX_HEADS)
    fl = kra + fb_ref[...]
    lsig = jnp.minimum(fl, 0.0) - jnp.log1p(jnp.exp(-jnp.abs(fl)))
    lf = jnp.where(ctrl, lsig, 0.0)
    hi, mid, lo = _split3(lf)
    p1 = (hi + pltpu.roll(mid, 8, 1) + pltpu.roll(lo, 16, 1)).astype(BF16)
    cs = jnp.dot(tri_ref[...], p1, preferred_element_type=F32)
    cs = cs + pltpu.roll(cs, LANES - 8, 1) + pltpu.roll(cs, LANES - 16, 1)
    fcum = jnp.where(ctrl, cs, 0.0) + carry_ref[...]
    carry_ref[...] = fcum[ts - 1:ts, :]

    hi, mid, lo = _split3(fcum * LOG2E)
    p2 = hi + pltpu.roll(mid, 8, 1) + pltpu.roll(lo, 16, 1)
    p2 = jnp.where(lane == CTRL_ONE, 1.0, p2).astype(BF16)
    aug = jnp.dot(p2, eqk_ref[...], preferred_element_type=F32)
    aug_q = aug[:, :LANES]
    aug_k = aug[:, LANES:]

    fq = proj[:, C_FQ0:C_FQ0 + FOX_W]
    fk = proj[:, C_FK0:C_FK0 + FOX_W]
    low = lane < FOX_HEAD_DIM
    for j in range(FOX_HEADS // 2):
        sl = slice(j * LANES, (j + 1) * LANES)
        qblk = fq[:, sl] * LOG2E
        kblk = fk[:, sl]
        for h0, qb_, kb_ in ((2 * j, qblk, kblk),
                             (2 * j + 1, pltpu.roll(qblk, FOX_HEAD_DIM, 1),
                              pltpu.roll(kblk, FOX_HEAD_DIM, 1))):
            own = (lane >= AUG_LO + AUG_W * h0) & (lane < AUG_LO + AUG_W * (h0 + 1))
            qa_ref[0, MLA_HEADS + h0] = jnp.where(low, qb_, aug_q).astype(BF16)
            ka_ref[0, MLA_HEADS + h0] = jnp.where(low, kb_, jnp.where(own, aug_k, 0.0)).astype(BF16)

    fv = proj[:, C_FV0:C_FV0 + FOX_W]
    vt_ref[0, 0, 0:MLA_HEADS * MLA_V, :] = v_mla.T.astype(BF16)
    vt_ref[0, 0, MLA_HEADS * MLA_V:, :] = fv.T.astype(BF16)


def _inproj(x, mod_l, g, win, qg, kvg, wuqa, wukv, fb_row, ctab, stab, tri, eqk, *, ts):
    B, S, D = x.shape
    nst = S // ts
    const2 = lambda b, s: (0, 0)
    kern = functools.partial(_inproj_kernel, ts=ts)
    return pl.pallas_call(
        kern,
        grid=(B, nst),
        in_specs=[
            pl.BlockSpec((1, ts, D), lambda b, s: (b, s, 0)),
            pl.BlockSpec((1, 6, D), lambda b, s: (b, 0, 0)),
            pl.BlockSpec((1, D), const2),
            pl.BlockSpec(win.shape, const2),
            pl.BlockSpec(qg.shape, const2),
            pl.BlockSpec(kvg.shape, const2),
            pl.BlockSpec(wuqa.shape, const2),
            pl.BlockSpec(wukv.shape, const2),
            pl.BlockSpec(fb_row.shape, const2),
            pl.BlockSpec((1, ts, LANES), lambda b, s: (b, s, 0)),
            pl.BlockSpec((1, ts, LANES), lambda b, s: (b, s, 0)),
            pl.BlockSpec(tri.shape, const2),
            pl.BlockSpec(eqk.shape, const2),
        ],
        out_specs=[
            pl.BlockSpec((1, N_HEADS, ts, LANES), lambda b, s: (b, 0, s, 0)),
            pl.BlockSpec((1, N_HEADS, ts, LANES), lambda b, s: (b, 0, s, 0)),
            pl.BlockSpec((1, 1, N_HEADS * MLA_V, ts), lambda b, s: (b, s, 0, 0)),
        ],
        out_shape=[
            jax.ShapeDtypeStruct((B, N_HEADS, S, LANES), BF16),
            jax.ShapeDtypeStruct((B, N_HEADS, S, LANES), BF16),
            jax.ShapeDtypeStruct((B, nst, N_HEADS * MLA_V, ts), BF16),
        ],
        scratch_shapes=[pltpu.VMEM((1, LANES), F32)],
        compiler_params=pltpu.CompilerParams(
            dimension_semantics=("arbitrary", "arbitrary"), vmem_limit_bytes=VMEM_LIMIT),
    )(x, mod_l, g, win, qg, kvg, wuqa, wukv, fb_row, ctab, stab, tri, eqk)


def _colmax8(st, groups):
    tk, tq = st.shape
    v = st.reshape(groups, tk // (8 * groups), 8, tq)
    return jnp.max(jnp.max(v, axis=1), axis=0)


def _attn_kernel(q_ref, k_ref, vt_ref, o_ref, s_scr, tmax_scr, m_scr, acc_scr, *, tq, hp, nq):
    tk = tq
    half = tq // 2
    dv = MLA_V
    groups = 4
    ones = jnp.ones((ONES_ROWS, tk), BF16)
    nt_dims = (((1,), (1,)), ((), ()))

    def qk(qi, j):
        q0 = pl.multiple_of(qi * tq, tq)
        return [lax.dot_general(k_ref[0, h, pl.ds(pl.multiple_of(j * tk, tk), tk), :],
                                q_ref[0, h, pl.ds(q0, tq), :], nt_dims,
                                preferred_element_type=F32) for h in range(hp)]

    def park(sts):
        for h in range(hp):
            s_scr[h] = sts[h]
            tmax_scr[h] = _colmax8(sts[h], groups)

    def qk_diag(qi):
        q0 = pl.multiple_of(qi * tq, tq)
        q1 = pl.multiple_of(qi * tq + half, half)
        out = []
        for h in range(hp):
            top = lax.dot_general(k_ref[0, h, pl.ds(q0, half), :], q_ref[0, h, pl.ds(q0, tq), :],
                                  nt_dims, preferred_element_type=F32)
            bot = lax.dot_general(k_ref[0, h, pl.ds(q1, half), :], q_ref[0, h, pl.ds(q1, half), :],
                                  nt_dims, preferred_element_type=F32)
            out.append((top, bot))
        return out

    def park_diag(sts):
        causal = (lax.broadcasted_iota(jnp.int32, (half, half), 0)
                  <= lax.broadcasted_iota(jnp.int32, (half, half), 1))
        for h in range(hp):
            top, bot = sts[h]
            top_l = jnp.where(causal, top[:, :half], NEG_BIG)
            bot_r = jnp.where(causal, bot, NEG_BIG)
            s_scr[h, 0:half, 0:half] = top_l
            s_scr[h, 0:half, half:] = top[:, half:]
            s_scr[h, half:, half:] = bot_r
            tmax_scr[h, :, 0:half] = _colmax8(top_l, groups)
            tmax_scr[h, :, half:] = jnp.maximum(_colmax8(top[:, half:], groups),
                                                _colmax8(bot_r, groups))

    def update(j):
        for h in range(hp):
            m = m_scr[h]
            m_new = jnp.maximum(m, jnp.max(tmax_scr[h], axis=0, keepdims=True))
            alpha = jnp.exp2(m - m_new)
            p = jnp.exp2((s_scr[h] - m_new).astype(BF16))
            vt1 = jnp.concatenate([vt_ref[0, j, h * dv:(h + 1) * dv, :], ones], axis=0)
            acc_scr[h] = alpha * acc_scr[h] + jnp.dot(vt1, p, preferred_element_type=F32)
            m_scr[h] = m_new

    def update_diag(j):
        for h in range(hp):
            m = m_scr[h]
            m_new = jnp.maximum(m, jnp.max(tmax_scr[h], axis=0, keepdims=True))
            alpha = jnp.exp2(m - m_new)
            p_top = jnp.exp2((s_scr[h, 0:half, :] - m_new).astype(BF16))
            p_bot = jnp.exp2((s_scr[h, half:, half:] - m_new[:, half:]).astype(BF16))
            vt1 = jnp.concatenate([vt_ref[0, j, h * dv:(h + 1) * dv, :], ones], axis=0)
            acc = alpha * acc_scr[h] + jnp.dot(vt1[:, 0:half], p_top, preferred_element_type=F32)
            acc_scr[h, :, 0:half] = acc[:, 0:half]
            acc_scr[h, :, half:] = acc[:, half:] + jnp.dot(vt1[:, half:], p_bot,
                                                           preferred_element_type=F32)
            m_scr[h] = m_new

    def reset_state():
        m_scr[...] = jnp.full(m_scr.shape, NEG_BIG, F32)
        acc_scr[...] = jnp.zeros(acc_scr.shape, F32)

    def finish(qi):
        q0 = pl.multiple_of(qi * tq, tq)
        for h in range(hp):
            acc = acc_scr[h]
            o_ref[0, h * dv:(h + 1) * dv, pl.ds(q0, tq)] = (
                acc[0:dv, :] / acc[dv:dv + 1, :]).astype(o_ref.dtype)
        reset_state()

    reset_state()
    park_diag(qk_diag(0))

    def q_tile(qi, c):
        def below_diagonal(t):
            sts = qk(qi, t + 1)
            update(t)
            park(sts)

        def two_below(u, c2):
            below_diagonal(2 * u)
            below_diagonal(2 * u + 1)
            return c2

        lax.fori_loop(0, (qi - 1) // 2, two_below, 0)

        @pl.when((qi > 1) & ((qi - 1) % 2 == 1))
        def _():
            below_diagonal(qi - 2)

        @pl.when(qi > 0)
        def _():
            sts = qk_diag(qi)
            update(qi - 1)
            park_diag(sts)

        @pl.when(qi + 1 < nq)
        def _():
            sts = qk(qi + 1, 0)
            update_diag(qi)
            finish(qi)
            park(sts)

        @pl.when(qi + 1 == nq)
        def _():
            update_diag(qi)
            finish(qi)

        return c

    lax.fori_loop(0, nq, q_tile, 0)


def _attention(qa, ka, vt, *, tq, tk, hp):
    B, H, S, _ = qa.shape
    nkt = vt.shape[1]
    assert S % tq == 0 and nkt * tk == S and H % hp == 0 and tk == tq
    kern = functools.partial(_attn_kernel, tq=tq, hp=hp, nq=S // tq)
    return pl.pallas_call(
        kern,
        grid=(B, H // hp),
        in_specs=[
            pl.BlockSpec((1, hp, S, LANES), lambda b, g: (b, g, 0, 0)),
            pl.BlockSpec((1, hp, S, LANES), lambda b, g: (b, g, 0, 0)),
            pl.BlockSpec((1, nkt, hp * MLA_V, tk), lambda b, g: (b, 0, g, 0)),
        ],
        out_specs=pl.BlockSpec((1, hp * MLA_V, S), lambda b, g: (b, g, 0)),
        out_shape=jax.ShapeDtypeStruct((B, H * MLA_V, S), BF16),
        scratch_shapes=[
            pltpu.VMEM((hp, tk, tq), F32),
            pltpu.VMEM((hp, 8, tq), F32),
            pltpu.VMEM((hp, 1, tq), F32),
            pltpu.VMEM((hp, MLA_V + ONES_ROWS, tq), F32),
        ],
        compiler_params=pltpu.CompilerParams(
            dimension_semantics=("arbitrary", "arbitrary"), vmem_limit_bytes=VMEM_LIMIT),
    )(qa, ka, vt)


def _top2_gates(logits, lane):
    lg = jnp.where(lane < N_EXPERTS, logits, NEG_BIG)
    m1 = jnp.max(lg, axis=1, keepdims=True)
    i1 = jnp.min(jnp.where(lg == m1, lane, LANES), axis=1, keepdims=True)
    lg2 = jnp.where(lane == i1, NEG_BIG, lg)
    m2 = jnp.max(lg2, axis=1, keepdims=True)
    i2 = jnp.min(jnp.where(lg2 == m2, lane, LANES), axis=1, keepdims=True)
    e2 = jnp.exp(m2 - m1)
    den = 1.0 + e2
    return jnp.where(lane == i1, 1.0 / den, 0.0) + jnp.where(lane == i2, e2 / den, 0.0)


def _outproj_kernel(*refs, ts, with_router):
    if with_router:
        (ot_ref, x_ref, mod_ref, mg_ref, fg_ref, wo_ref, ng_ref, rw_ref,
         x1_ref, h2_ref, gates_ref) = refs
    else:
        ot_ref, x_ref, mod_ref, mg_ref, fg_ref, wo_ref, ng_ref, x1_ref, h2_ref = refs
    half = MLA_HEADS * MLA_V
    om = ot_ref[0, 0:half, :].astype(F32).T
    of = ot_ref[0, half:, :].astype(F32).T
    on = jnp.concatenate([_rms(om) * mg_ref[...], _rms(of) * fg_ref[...]], axis=1).astype(BF16)
    mix = jnp.dot(on, wo_ref[...], preferred_element_type=F32)
    x1 = x_ref[0] + mod_ref[0, 2:3, :] * mix
    x1_ref[0] = x1
    h2 = ((_rms(x1) * ng_ref[...]) * (1.0 + mod_ref[0, 4:5, :]) + mod_ref[0, 3:4, :]).astype(BF16)
    h2_ref[0] = h2
    if with_router:
        logits = jnp.dot(h2, rw_ref[...], preferred_element_type=F32)
        lane = lax.broadcasted_iota(jnp.int32, (ts, LANES), 1)
        gates_ref[0] = _top2_gates(logits, lane)


def _outproj(ot, x, mod_l, mg, fg, wo, ng, rw, *, ts):
    B, S, D = x.shape
    with_router = rw is not None
    const2 = lambda b, s: (0, 0)
    in_specs = [
        pl.BlockSpec((1, ot.shape[1], ts), lambda b, s: (b, 0, s)),
        pl.BlockSpec((1, ts, D), lambda b, s: (b, s, 0)),
        pl.BlockSpec((1, 6, D), lambda b, s: (b, 0, 0)),
        pl.BlockSpec(mg.shape, const2),
        pl.BlockSpec(fg.shape, const2),
        pl.BlockSpec(wo.shape, const2),
        pl.BlockSpec(ng.shape, const2),
    ]
    args = [ot, x, mod_l, mg, fg, wo, ng]
    out_specs = [pl.BlockSpec((1, ts, D), lambda b, s: (b, s, 0)),
                 pl.BlockSpec((1, ts, D), lambda b, s: (b, s, 0))]
    out_shape = [jax.ShapeDtypeStruct((B, S, D), F32), jax.ShapeDtypeStruct((B, S, D), BF16)]
    if with_router:
        in_specs.append(pl.BlockSpec(rw.shape, const2))
        args.append(rw)
        out_specs.append(pl.BlockSpec((1, ts, LANES), lambda b, s: (b, s, 0)))
        out_shape.append(jax.ShapeDtypeStruct((B, S, LANES), F32))
    kern = functools.partial(_outproj_kernel, ts=ts, with_router=with_router)
    return pl.pallas_call(
        kern,
        grid=(B, S // ts),
        in_specs=in_specs,
        out_specs=out_specs,
        out_shape=out_shape,
        compiler_params=pltpu.CompilerParams(
            dimension_semantics=("arbitrary", "arbitrary"), vmem_limit_bytes=VMEM_LIMIT),
    )(*args)


def _swiglu_tile(x, wg, wu):
    g = jnp.dot(x, wg, preferred_element_type=F32)
    u = jnp.dot(x, wu, preferred_element_type=F32)
    return g * _sigmoid(g) * u


def _ffn_kernel(h_ref, wg_ref, wu_ref, wd_ref, x_ref, mod_ref, fin_ref, o_ref, acc_ref,
                *, final_norm):
    f = pl.program_id(1)

    @pl.when(f == 0)
    def _():
        acc_ref[...] = jnp.zeros_like(acc_ref)

    a = _swiglu_tile(h_ref[...], wg_ref[...], wu_ref[...])
    acc_ref[...] += jnp.dot(a.astype(BF16), wd_ref[...], preferred_element_type=F32)

    @pl.when(f == pl.num_programs(1) - 1)
    def _():
        x2 = x_ref[...] + mod_ref[0, 5:6, :] * acc_ref[...]
        if final_norm:
            x2 = _rms(x2) * fin_ref[...]
        o_ref[...] = x2


def _ffn(h2, wg, wu, wd, x1, mod_l, fin_g, *, tm, tf, seq, final_norm):
    T, D = h2.shape
    F = wd.shape[0]
    tiles_per_batch = seq // tm
    kern = functools.partial(_ffn_kernel, final_norm=final_norm)
    return pl.pallas_call(
        kern,
        grid=(T // tm, F // tf),
        in_specs=[
            pl.BlockSpec((tm, D), lambda i, f: (i, 0)),
            pl.BlockSpec((D, tf), lambda i, f: (0, f)),
            pl.BlockSpec((D, tf), lambda i, f: (0, f)),
            pl.BlockSpec((tf, D), lambda i, f: (f, 0)),
            pl.BlockSpec((tm, D), lambda i, f: (i, 0)),
            pl.BlockSpec((1, 6, D), lambda i, f: (i // tiles_per_batch, 0, 0)),
            pl.BlockSpec((1, D), lambda i, f: (0, 0)),
        ],
        out_specs=pl.BlockSpec((tm, D), lambda i, f: (i, 0)),
        out_shape=jax.ShapeDtypeStruct((T, D), F32),
        scratch_shapes=[pltpu.VMEM((tm, D), F32)],
        compiler_params=pltpu.CompilerParams(
            dimension_semantics=("arbitrary", "arbitrary"), vmem_limit_bytes=VMEM_LIMIT),
    )(h2, wg, wu, wd, x1, mod_l, fin_g)


RC = 64
VT = 16
SPB = 8


def _route_kernel(gates_ref, rank_ref, rankt_ref, gatest_ref, cnt_ref, *, tm):
    gates = gates_ref[...]
    sel = gates > 0.0
    self32 = jnp.where(sel, 1.0, 0.0)
    earlier = (lax.broadcasted_iota(jnp.int32, (tm, tm), 1)
               < lax.broadcasted_iota(jnp.int32, (tm, tm), 0))
    rank = jnp.dot(jnp.where(earlier, 1.0, 0.0).astype(BF16), self32.astype(BF16),
                   preferred_element_type=F32)
    rank = jnp.where(sel, rank, -1.0)
    rank_ref[...] = rank
    rankt_ref[0] = rank.T
    gatest_ref[0] = gates.T
    cnt_ref[0] = jnp.sum(self32, axis=0, keepdims=True)


def _route(gates, *, tm):
    T = gates.shape[0]
    nt = T // tm
    return pl.pallas_call(
        functools.partial(_route_kernel, tm=tm),
        grid=(nt,),
        in_specs=[pl.BlockSpec((tm, LANES), lambda i: (i, 0))],
        out_specs=[
            pl.BlockSpec((tm, LANES), lambda i: (i, 0)),
            pl.BlockSpec((1, LANES, tm), lambda i: (i, 0, 0)),
            pl.BlockSpec((1, LANES, tm), lambda i: (i, 0, 0)),
            pl.BlockSpec((1, 1, LANES), lambda i: (i, 0, 0)),
        ],
        out_shape=[
            jax.ShapeDtypeStruct((T, LANES), F32),
            jax.ShapeDtypeStruct((nt, LANES, tm), F32),
            jax.ShapeDtypeStruct((nt, LANES, tm), F32),
            jax.ShapeDtypeStruct((nt, 1, LANES), F32),
        ],
        compiler_params=pltpu.CompilerParams(
            dimension_semantics=("arbitrary",), vmem_limit_bytes=VMEM_LIMIT),
    )(gates)


def _slot_tables(cnt, nt, ns, n_vt):
    E = N_EXPERTS
    nch = (cnt + (RC - 1)) // RC
    nct = jnp.sum(nch, axis=1)
    ccum = jnp.cumsum(nch, axis=1)
    j = jnp.arange(ns, dtype=jnp.int32)
    slot_e = jnp.sum((ccum[:, None, :] <= j[None, :, None]).astype(jnp.int32), axis=2)
    slot_e = jnp.minimum(slot_e, E - 1)
    first = jnp.take_along_axis(ccum - nch, slot_e, axis=1)
    valid = j[None, :] < nct[:, None]
    slot_k = jnp.where(valid, j[None, :] - first, 0)
    nce = jnp.sum(nch, axis=0)
    padded = (nce + (VT - 1)) // VT * VT
    base = jnp.cumsum(padded) - padded
    cid0 = base[None, :] + jnp.cumsum(nch, axis=0) - nch
    cid = jnp.take_along_axis(cid0, slot_e, axis=1) + slot_k
    cid = jnp.where(valid, cid, cid[:, 0:1])
    n_chunks = n_vt * VT
    slot_id = jnp.arange(nt, dtype=jnp.int32)[:, None] * ns + j[None, :]
    pos = jnp.full((n_chunks,), -1, jnp.int32).at[
        jnp.where(valid, cid, n_chunks).reshape(-1)].set(slot_id.reshape(-1), mode="drop")
    vcum = jnp.cumsum(padded // VT)
    nv = vcum[-1]
    v = jnp.minimum(jnp.arange(n_vt, dtype=jnp.int32), nv - 1)
    v_exp = jnp.minimum(jnp.searchsorted(vcum, v, side="right"), E - 1).astype(jnp.int32)
    i32 = lambda a: a.reshape(-1).astype(jnp.int32)
    return dict(slot_e=i32(slot_e), slot_k=i32(slot_k), nct=i32(nct), cid=i32(cid),
                pos=jnp.maximum(pos, 0), real=(pos >= 0).astype(jnp.int32),
                v_exp=v_exp, nv=i32(nv))


def _compact_kernel(se_ref, sk_ref, nct_ref, h_ref, rankt_ref, gatest_ref, xs_ref, gc_ref, oh_scr,
                    *, tm, ns):
    t = pl.program_id(0)
    n = nct_ref[t]

    def build(j, c):
        e = se_ref[t * ns + j]
        rrow = rankt_ref[0, pl.ds(e, 1), :].astype(jnp.int32)
        grow = gatest_ref[0, pl.ds(e, 1), :]
        row = lax.broadcasted_iota(jnp.int32, (RC, tm), 0) + sk_ref[t * ns + j] * RC
        hit = (rrow == row) & (j < n)
        oh_scr[j] = jnp.where(hit, 1.0, 0.0).astype(BF16)
        gcol = jnp.sum(jnp.where(hit, grow, 0.0), axis=1, keepdims=True)
        gc_ref[pl.ds(pl.multiple_of(j * RC, RC), RC), :] = jnp.broadcast_to(gcol, (RC, LANES))
        return c

    lax.fori_loop(0, ns, build, 0)
    rows = SPB * RC
    for b in range(ns // SPB):
        oh = oh_scr[b * SPB:(b + 1) * SPB].reshape(rows, tm)
        xs_ref[b * rows:(b + 1) * rows, :] = jnp.dot(
            oh, h_ref[...], preferred_element_type=F32).astype(BF16)


def _compact(tabs, h2, rankt, gatest, *, tm, ns):
    T, D = h2.shape
    nt = T // tm
    grid_spec = pltpu.PrefetchScalarGridSpec(
        num_scalar_prefetch=3,
        grid=(nt,),
        in_specs=[
            pl.BlockSpec((tm, D), lambda t, *_: (t, 0)),
            pl.BlockSpec((1, LANES, tm), lambda t, *_: (t, 0, 0)),
            pl.BlockSpec((1, LANES, tm), lambda t, *_: (t, 0, 0)),
        ],
        out_specs=[
            pl.BlockSpec((ns * RC, D), lambda t, *_: (t, 0)),
            pl.BlockSpec((ns * RC, LANES), lambda t, *_: (t, 0)),
        ],
        scratch_shapes=[pltpu.VMEM((ns, RC, tm), BF16)],
    )
    return pl.pallas_call(
        functools.partial(_compact_kernel, tm=tm, ns=ns),
        grid_spec=grid_spec,
        out_shape=[jax.ShapeDtypeStruct((nt * ns * RC, D), BF16),
                   jax.ShapeDtypeStruct((nt * ns * RC, LANES), F32)],
        compiler_params=pltpu.CompilerParams(
            dimension_semantics=("arbitrary",), vmem_limit_bytes=VMEM_LIMIT),
    )(tabs["slot_e"], tabs["slot_k"], tabs["nct"], h2, rankt, gatest)


def _slot_experts_kernel(vexp_ref, nv_ref, pos_ref, real_ref, *refs):
    xs_refs = refs[0:VT]
    gc_refs = refs[VT:2 * VT]
    wg_ref, wu_ref, wd_ref, ys_ref, x_scr, g_scr, acc_ref = refs[2 * VT:]
    v = pl.program_id(0)
    f = pl.program_id(1)

    @pl.when(v < nv_ref[0])
    def _():
        @pl.when(f == 0)
        def _():
            for j in range(VT):
                keep = real_ref[v * VT + j] == 1
                x_scr[j * RC:(j + 1) * RC, :] = jnp.where(keep, xs_refs[j][...], jnp.zeros_like(xs_refs[j]))
                g_scr[j * RC:(j + 1) * RC, :] = jnp.where(keep, gc_refs[j][...], 0.0)
            acc_ref[...] = jnp.zeros_like(acc_ref)

        a = _swiglu_tile(x_scr[...], wg_ref[0], wu_ref[0]) * g_scr[:, 0:1]
        acc_ref[...] += jnp.dot(a.astype(BF16), wd_ref[0], preferred_element_type=F32)

        @pl.when(f == pl.num_programs(1) - 1)
        def _():
            ys_ref[...] = acc_ref[...].astype(BF16)


def _slot_experts(tabs, xs, gc, wg, wu, wd, *, tf, n_vt):
    D = xs.shape[1]
    NF = wd.shape[1] // tf
    rows = VT * RC

    def live(v, nv):
        return jnp.minimum(v, nv[0] - 1)

    def hidden(v, f, nv):
        return jnp.where(v < nv[0], f, NF - 1)

    def chunk(j):
        return lambda v, f, vexp, nv, pos, real: (pos[live(v, nv) * VT + j], 0)

    grid_spec = pltpu.PrefetchScalarGridSpec(
        num_scalar_prefetch=4,
        grid=(n_vt, NF),
        in_specs=(
            [pl.BlockSpec((RC, D), chunk(j)) for j in range(VT)]
            + [pl.BlockSpec((RC, LANES), chunk(j)) for j in range(VT)]
            + [pl.BlockSpec((1, D, tf),
                            lambda v, f, vexp, nv, pos, real: (vexp[v], 0, hidden(v, f, nv)))] * 2
            + [pl.BlockSpec((1, tf, D),
                            lambda v, f, vexp, nv, pos, real: (vexp[v], hidden(v, f, nv), 0))]),
        out_specs=pl.BlockSpec((rows, D), lambda v, f, vexp, nv, pos, real: (live(v, nv), 0)),
        scratch_shapes=[pltpu.VMEM((rows, D), BF16), pltpu.VMEM((rows, LANES), F32),
                        pltpu.VMEM((rows, D), F32)],
    )
    return pl.pallas_call(
        _slot_experts_kernel,
        grid_spec=grid_spec,
        out_shape=jax.ShapeDtypeStruct((n_vt * rows, D), BF16),
        compiler_params=pltpu.CompilerParams(
            dimension_semantics=("arbitrary", "arbitrary"), vmem_limit_bytes=VMEM_LIMIT),
    )(tabs["v_exp"], tabs["nv"], tabs["pos"], tabs["real"], *([xs] * VT), *([gc] * VT), wg, wu, wd)


def _uncompact_kernel(se_ref, sk_ref, nct_ref, cid_ref, *refs, tm, ns, final_norm):
    ys_refs = refs[0:ns]
    rank_ref, x_ref, mod_ref, fin_ref, o_ref, oh_scr = refs[ns:]
    t = pl.program_id(0)
    n = nct_ref[t]
    lane = lax.broadcasted_iota(jnp.int32, (tm, LANES), 1)
    right = lane >= RC
    col = jnp.where(right, lane - RC, lane)
    rank = rank_ref[...]

    def slot_rank(j):
        return jnp.sum(jnp.where(lane == se_ref[t * ns + j], rank, 0.0), axis=1,
                       keepdims=True).astype(jnp.int32)

    for jp in range(ns // 2):
        ja, jb = 2 * jp, 2 * jp + 1
        rcol = jnp.where(right, slot_rank(jb), slot_rank(ja))
        want = col + jnp.where(right, sk_ref[t * ns + jb], sk_ref[t * ns + ja]) * RC
        used = jnp.where(right, jb, ja) < n
        oh_scr[:, jp * LANES:(jp + 1) * LANES] = jnp.where(
            (rcol == want) & used, 1.0, 0.0).astype(BF16)
    y = None
    rows = SPB * RC
    for b in range(ns // SPB):
        ys = jnp.concatenate([ys_refs[b * SPB + j][...] for j in range(SPB)], axis=0)
        part = jnp.dot(oh_scr[:, b * rows:(b + 1) * rows], ys, preferred_element_type=F32)
        y = part if y is None else y + part
    x2 = x_ref[...] + mod_ref[0, 5:6, :] * y
    if final_norm:
        x2 = _rms(x2) * fin_ref[...]
    o_ref[...] = x2


def _uncompact(tabs, ys, rank, x1, mod_l, fin_g, *, tm, ns, seq, final_norm):
    T, D = x1.shape
    tpb = seq // tm

    def chunk(j):
        return lambda t, se, sk, nct, cid: (cid[t * ns + j], 0)

    grid_spec = pltpu.PrefetchScalarGridSpec(
        num_scalar_prefetch=4,
        grid=(T // tm,),
        in_specs=(
            [pl.BlockSpec((RC, D), chunk(j)) for j in range(ns)]
            + [pl.BlockSpec((tm, LANES), lambda t, *_: (t, 0)),
               pl.BlockSpec((tm, D), lambda t, *_: (t, 0)),
               pl.BlockSpec((1, 6, D), lambda t, *_: (t // tpb, 0, 0)),
               pl.BlockSpec((1, D), lambda t, *_: (0, 0))]),
        out_specs=pl.BlockSpec((tm, D), lambda t, *_: (t, 0)),
        scratch_shapes=[pltpu.VMEM((tm, ns * RC), BF16)],
    )
    return pl.pallas_call(
        functools.partial(_uncompact_kernel, tm=tm, ns=ns, final_norm=final_norm),
        grid_spec=grid_spec,
        out_shape=jax.ShapeDtypeStruct((T, D), F32),
        compiler_params=pltpu.CompilerParams(
            dimension_semantics=("arbitrary",), vmem_limit_bytes=VMEM_LIMIT),
    )(tabs["slot_e"], tabs["slot_k"], tabs["nct"], tabs["cid"], *([ys] * ns),
      rank, x1, mod_l, fin_g)


def _routed_experts(h2, gates, w_gate, w_up, w_down, x1, mod_l, fin_g, *, tm, tf, seq, final_norm):
    T, D = h2.shape
    E = N_EXPERTS
    nt = T // tm
    ns = 2 * tm // RC + E
    assert ns % SPB == 0 and 2 * RC == LANES
    n_vt = -(-(nt * ns + E * (VT - 1)) // VT)
    rank, rankt, gatest, cnt = _route(gates, tm=tm)
    tabs = _slot_tables(cnt[:, 0, :E].astype(jnp.int32), nt, ns, n_vt)
    xs, gc = _compact(tabs, h2, rankt, gatest, tm=tm, ns=ns)
    ys = _slot_experts(tabs, xs, gc, w_gate.astype(BF16), w_up.astype(BF16), w_down.astype(BF16),
                       tf=tf, n_vt=n_vt)
    return _uncompact(tabs, ys, rank, x1, mod_l, fin_g, tm=tm, ns=ns, seq=seq,
                      final_norm=final_norm)


def _tiles(S):
    ts = min(512, S)
    tq = min(512, S)
    tk = ts
    tm = min(1024, S)
    tf = D_FF // 2
    return ts, tq, tk, tm, tf


def kernel(x, c, positions, ada_w, ada_b, attn_norm_g, w_in, q_norm_g, w_uq, kv_norm_g, w_ukv,
           fox_forget_b, mla_out_g, fox_out_g, w_o, ffn_norm_g, dense_w_gate, dense_w_up,
           dense_w_down, router_w, moe_w_gate, moe_w_up, moe_w_down, final_norm_g):
    B, S, D = x.shape
    L = ada_w.shape[0]
    ts, tq, tk, tm, tf = _tiles(S)

    mod = _adaln(c, ada_w, ada_b).reshape(L, B, 6, D)
    ctab, stab = _rope_tables(positions)

    idx_in = _perm_w_in()
    idx_qa = _perm_w_uq()
    idx_kv = _perm_w_ukv()
    col_scale = np.ones((C_END,), np.float32)
    col_scale[C_FQ0:C_FQ0 + FOX_W] = FOX_HEAD_DIM ** -0.5
    tri = jnp.asarray(np.tril(np.ones((ts, ts), np.float32)), BF16)
    eqk = jnp.asarray(_aug_placement(), BF16)

    for l in range(L):
        win = (_take_cols(w_in[l], idx_in) * col_scale).astype(BF16)
        wuqa = _take_cols(w_uq[l], idx_qa).astype(BF16)
        wukv = _take_cols(w_ukv[l], idx_kv).astype(BF16)
        fb_row = jnp.zeros((1, LANES), F32).at[0, CTRL_LOGIT:CTRL_LOGIT + FOX_HEADS].set(
            fox_forget_b[l].astype(F32))
        qa, ka, vt = _inproj(
            x, mod[l], attn_norm_g[l].reshape(1, D), win, q_norm_g[l].reshape(1, -1),
            kv_norm_g[l].reshape(1, -1), wuqa, wukv, fb_row, ctab, stab, tri, eqk, ts=ts)
        ot = _attention(qa, ka, vt, tq=tq, tk=tk, hp=4)

        j = l // 2
        is_moe = (l % 2 == 1)
        rw = None
        if is_moe:
            rw = jnp.pad(router_w[j], ((0, 0), (0, LANES - N_EXPERTS))).astype(BF16)
        outs = _outproj(ot, x, mod[l], mla_out_g[l].reshape(1, -1), fox_out_g[l].reshape(1, -1),
                        w_o[l].astype(BF16), ffn_norm_g[l].reshape(1, D), rw, ts=ts)
        x1, h2 = outs[0].reshape(B * S, D), outs[1].reshape(B * S, D)
        fin_g = final_norm_g.reshape(1, D)
        last = l == L - 1
        if is_moe:
            x = _routed_experts(h2, outs[2].reshape(B * S, LANES), moe_w_gate[j], moe_w_up[j],
                                moe_w_down[j], x1, mod[l], fin_g,
                                tm=tm, tf=tf, seq=S, final_norm=last)
        else:
            x = _ffn(h2, dense_w_gate[j].astype(BF16), dense_w_up[j].astype(BF16),
                     dense_w_down[j].astype(BF16), x1, mod[l], fin_g,
                     tm=tm, tf=tf, seq=S, final_norm=last)
        x = x.reshape(B, S, D)
    return x
```

```python
import functools

import numpy as np
import jax
import jax.numpy as jnp
from jax import lax
from jax.experimental import pallas as pl
from jax.experimental.pallas import tpu as pltpu

F32 = jnp.float32
BF16 = jnp.bfloat16

D_MODEL = 1024
MLA_HEADS = 8
MLA_V = 64
MLA_NOPE = 64
MLA_ROPE = 32
MLA_Q_LORA = 256
MLA_KV_LORA = 128
FOX_HEADS = 8
FOX_HEAD_DIM = 64
FOX_W = FOX_HEADS * FOX_HEAD_DIM
ROPE_THETA = 10000.0
D_FF = 3584
N_EXPERTS = 8
EPS = 1e-6
N_HEADS = MLA_HEADS + FOX_HEADS
HALF_ROPE = MLA_ROPE // 2

LANES = 128
VMEM_LIMIT = 56 * 1024 * 1024

ROPE_LO = MLA_NOPE
AUG_LO = FOX_HEAD_DIM
AUG_W = 6
CTRL_LOGIT = 96
CTRL_ONE = 120

C_Q0 = 0
C_KV0 = C_Q0 + MLA_Q_LORA
C_KRA0 = C_KV0 + MLA_KV_LORA
C_FQ0 = C_KRA0 + LANES
C_FK0 = C_FQ0 + FOX_W
C_FV0 = C_FK0 + FOX_W
C_END = C_FV0 + FOX_W

NEG_BIG = -1e30
LOG2E = 1.4426950408889634
ONES_ROWS = 16


def _rms(v):
    return v * lax.rsqrt(jnp.mean(v * v, axis=-1, keepdims=True) + EPS)


def _sigmoid(v):
    return 1.0 / (1.0 + jnp.exp(-v))


def _split3(v):
    hi = v.astype(BF16).astype(F32)
    r = v - hi
    mid = r.astype(BF16).astype(F32)
    lo = (r - mid).astype(BF16).astype(F32)
    return hi, mid, lo


def _adaln_kernel(c_ref, w_ref, b_ref, o_ref):
    c = c_ref[...]
    ca = (c * _sigmoid(c)).astype(BF16)
    o_ref[0] = jnp.dot(ca, w_ref[0].astype(BF16), preferred_element_type=F32) + b_ref[0]


def _adaln(c, ada_w, ada_b):
    L, D, N = ada_w.shape
    B = c.shape[0]
    tn = 1536
    return pl.pallas_call(
        _adaln_kernel,
        grid=(L, N // tn),
        in_specs=[
            pl.BlockSpec((B, D), lambda l, j: (0, 0)),
            pl.BlockSpec((1, D, tn), lambda l, j: (l, 0, j)),
            pl.BlockSpec((1, 1, tn), lambda l, j: (l, 0, j)),
        ],
        out_specs=pl.BlockSpec((1, B, tn), lambda l, j: (l, 0, j)),
        out_shape=jax.ShapeDtypeStruct((L, B, N), F32),
        compiler_params=pltpu.CompilerParams(
            dimension_semantics=("arbitrary", "arbitrary"), vmem_limit_bytes=VMEM_LIMIT),
    )(c, ada_w, ada_b.reshape(L, 1, N))


def _rope_tab_kernel(pos_ref, freq_ref, cos_ref, sin_ref):
    ang = freq_ref[...] * pos_ref[0]
    cos_ref[0] = jnp.cos(ang)
    sin_ref[0] = jnp.sin(ang)


def _rope_tables(positions):
    B, S = positions.shape
    half = HALF_ROPE
    inv_freq = ROPE_THETA ** (-jnp.arange(half, dtype=F32) / half)
    pos = positions.astype(F32).reshape(B, 1, S)
    cos_t, sin_t = pl.pallas_call(
        _rope_tab_kernel,
        grid=(B,),
        in_specs=[
            pl.BlockSpec((1, 1, S), lambda b: (b, 0, 0)),
            pl.BlockSpec((half, 1), lambda b: (0, 0)),
        ],
        out_specs=[pl.BlockSpec((1, half, S), lambda b: (b, 0, 0))] * 2,
        out_shape=[jax.ShapeDtypeStruct((B, half, S), F32)] * 2,
        compiler_params=pltpu.CompilerParams(dimension_semantics=("arbitrary",)),
    )(pos, inv_freq.reshape(half, 1))
    cos = jnp.transpose(cos_t, (0, 2, 1))
    sin = jnp.transpose(sin_t, (0, 2, 1))
    ones = jnp.ones((B, S, MLA_NOPE), F32)
    z_lo = jnp.zeros((B, S, MLA_NOPE), F32)
    z_hi = jnp.zeros((B, S, LANES - MLA_NOPE - MLA_ROPE), F32)
    ctab = jnp.concatenate([ones, cos, cos, z_hi], axis=-1)
    stab = jnp.concatenate([z_lo, -sin, sin, z_hi], axis=-1)
    return ctab, stab


def _take_cols(w, idx):
    pieces, i, n = [], 0, len(idx)
    while i < n:
        j = i + 1
        if idx[i] < 0:
            while j < n and idx[j] < 0:
                j += 1
            pieces.append(jnp.zeros((w.shape[0], j - i), w.dtype))
        else:
            while j < n and idx[j] == idx[j - 1] + 1:
                j += 1
            pieces.append(w[:, int(idx[i]):int(idx[i]) + (j - i)])
        i = j
    return jnp.concatenate(pieces, axis=1)


def _perm_w_in():
    cq = 0
    ckv = cq + MLA_Q_LORA
    kr = ckv + MLA_KV_LORA
    fq = kr + MLA_ROPE
    fk = fq + FOX_W
    fv = fk + FOX_W
    fl = fv + FOX_W
    idx = -np.ones((C_END,), np.int64)
    idx[C_Q0:C_Q0 + MLA_Q_LORA] = cq + np.arange(MLA_Q_LORA)
    idx[C_KV0:C_KV0 + MLA_KV_LORA] = ckv + np.arange(MLA_KV_LORA)
    idx[C_KRA0 + ROPE_LO:C_KRA0 + ROPE_LO + MLA_ROPE] = kr + np.arange(MLA_ROPE)
    idx[C_KRA0 + CTRL_LOGIT:C_KRA0 + CTRL_LOGIT + FOX_HEADS] = fl + np.arange(FOX_HEADS)
    idx[C_FQ0:C_FQ0 + FOX_W] = fq + np.arange(FOX_W)
    idx[C_FK0:C_FK0 + FOX_W] = fk + np.arange(FOX_W)
    idx[C_FV0:C_FV0 + FOX_W] = fv + np.arange(FOX_W)
    return idx


def _perm_w_uq():
    per = MLA_NOPE + MLA_ROPE
    ia = -np.ones((MLA_HEADS * LANES,), np.int64)
    for h in range(MLA_HEADS):
        ia[h * LANES:h * LANES + per] = h * per + np.arange(per)
    return ia


def _perm_w_ukv():
    per = MLA_NOPE + MLA_V
    ik = -np.ones((MLA_HEADS * LANES,), np.int64)
    iv = np.zeros((MLA_HEADS * MLA_V,), np.int64)
    for h in range(MLA_HEADS):
        ik[h * LANES:h * LANES + MLA_NOPE] = h * per + np.arange(MLA_NOPE)
        iv[h * MLA_V:(h + 1) * MLA_V] = h * per + MLA_NOPE + np.arange(MLA_V)
    return np.concatenate([ik, iv])


def _aug_placement():
    e = np.zeros((LANES, 2 * LANES), np.float32)
    for h in range(FOX_HEADS):
        lo = AUG_LO + AUG_W * h
        for p in range(3):
            src = CTRL_LOGIT + 8 * p + h
            e[src, lo + p] = 1.0
            e[CTRL_ONE, lo + 3 + p] = -1.0
            e[CTRL_ONE, LANES + lo + p] = 1.0
            e[src, LANES + lo + 3 + p] = 1.0
    return e


def _inproj_kernel(x_ref, mod_ref, g_ref, win_ref, qg_ref, kvg_ref, wuqa_ref, wukv_ref,
                   fb_ref, ct_ref, st_ref, tri_ref, eqk_ref,
                   qa_ref, ka_ref, vt_ref, carry_ref, *, ts):
    si = pl.program_id(1)

    @pl.when(si == 0)
    def _():
        carry_ref[...] = jnp.zeros_like(carry_ref)

    x = x_ref[0]
    shift = mod_ref[0, 0:1, :]
    scale = mod_ref[0, 1:2, :]
    h = ((_rms(x) * g_ref[...]) * (1.0 + scale) + shift).astype(BF16)
    proj = jnp.dot(h, win_ref[...], preferred_element_type=F32)

    ctab = ct_ref[0]
    stab = st_ref[0]
    lane = lax.broadcasted_iota(jnp.int32, (ts, LANES), 1)

    c_q = proj[:, C_Q0:C_Q0 + MLA_Q_LORA]
    cqn = (_rms(c_q) * qg_ref[...]).astype(BF16)
    qa = jnp.dot(cqn, wuqa_ref[...], preferred_element_type=F32)
    mla_scale = (MLA_NOPE + MLA_ROPE) ** -0.5 * LOG2E
    first_half = lane < ROPE_LO + HALF_ROPE

    def rope(blk):
        swapped = jnp.where(first_half, pltpu.roll(blk, LANES - HALF_ROPE, 1),
                            pltpu.roll(blk, HALF_ROPE, 1))
        return blk * ctab + swapped * stab

    for hh in range(MLA_HEADS):
        qa_ref[0, hh] = (rope(qa[:, hh * LANES:(hh + 1) * LANES]) * mla_scale).astype(BF16)

    c_kv = proj[:, C_KV0:C_KV0 + MLA_KV_LORA]
    ckvn = (_rms(c_kv) * kvg_ref[...]).astype(BF16)
    kv = jnp.dot(ckvn, wukv_ref[...], preferred_element_type=F32)
    kra = proj[:, C_KRA0:C_KRA0 + LANES]
    krope = rope(kra)
    for hh in range(MLA_HEADS):
        ka_ref[0, hh] = (kv[:, hh * LANES:(hh + 1) * LANES] + krope).astype(BF16)
    v_mla = kv[:, MLA_HEADS * LANES:]

    ctrl = (lane >= CTRL_LOGIT) & (lane < CTRL_LOGIT + FOX_HEADS)
    fl = kra + fb_ref[...]
    lsig = jnp.minimum(fl, 0.0) - jnp.log1p(jnp.exp(-jnp.abs(fl)))
    lf = jnp.where(ctrl, lsig, 0.0)
    hi, mid, lo = _split3(lf)
    p1 = (hi + pltpu.roll(mid, 8, 1) + pltpu.roll(lo, 16, 1)).astype(BF16)
    cs = jnp.dot(tri_ref[...], p1, preferred_element_type=F32)
    cs = cs + pltpu.roll(cs, LANES - 8, 1) + pltpu.roll(cs, LANES - 16, 1)
    fcum = jnp.where(ctrl, cs, 0.0) + carry_ref[...]
    carry_ref[...] = fcum[ts - 1:ts, :]

    hi, mid, lo = _split3(fcum * LOG2E)
    p2 = hi + pltpu.roll(mid, 8, 1) + pltpu.roll(lo, 16, 1)
    p2 = jnp.where(lane == CTRL_ONE, 1.0, p2).astype(BF16)
    aug = jnp.dot(p2, eqk_ref[...], preferred_element_type=F32)
    aug_q = aug[:, :LANES]
    aug_k = aug[:, LANES:]

    fq = proj[:, C_FQ0:C_FQ0 + FOX_W]
    fk = proj[:, C_FK0:C_FK0 + FOX_W]
    low = lane < FOX_HEAD_DIM
    for j in range(FOX_HEADS // 2):
        sl = slice(j * LANES, (j + 1) * LANES)
        qblk = fq[:, sl] * LOG2E
        kblk = fk[:, sl]
        for h0, qb_, kb_ in ((2 * j, qblk, kblk),
                             (2 * j + 1, pltpu.roll(qblk, FOX_HEAD_DIM, 1),
                              pltpu.roll(kblk, FOX_HEAD_DIM, 1))):
            own = (lane >= AUG_LO + AUG_W * h0) & (lane < AUG_LO + AUG_W * (h0 + 1))
            qa_ref[0, MLA_HEADS + h0] = jnp.where(low, qb_, aug_q).astype(BF16)
            ka_ref[0, MLA_HEADS + h0] = jnp.where(low, kb_, jnp.where(own, aug_k, 0.0)).astype(BF16)

    fv = proj[:, C_FV0:C_FV0 + FOX_W]
    vt_ref[0, 0, 0:MLA_HEADS * MLA_V, :] = v_mla.T.astype(BF16)
    vt_ref[0, 0, MLA_HEADS * MLA_V:, :] = fv.T.astype(BF16)


def _inproj(x, mod_l, g, win, qg, kvg, wuqa, wukv, fb_row, ctab, stab, tri, eqk, *, ts):
    B, S, D = x.shape
    nst = S // ts
    const2 = lambda b, s: (0, 0)
    kern = functools.partial(_inproj_kernel, ts=ts)
    return pl.pallas_call(
        kern,
        grid=(B, nst),
        in_specs=[
            pl.BlockSpec((1, ts, D), lambda b, s: (b, s, 0)),
            pl.BlockSpec((1, 6, D), lambda b, s: (b, 0, 0)),
            pl.BlockSpec((1, D), const2),
            pl.BlockSpec(win.shape, const2),
            pl.BlockSpec(qg.shape, const2),
            pl.BlockSpec(kvg.shape, const2),
            pl.BlockSpec(wuqa.shape, const2),
            pl.BlockSpec(wukv.shape, const2),
            pl.BlockSpec(fb_row.shape, const2),
            pl.BlockSpec((1, ts, LANES), lambda b, s: (b, s, 0)),
            pl.BlockSpec((1, ts, LANES), lambda b, s: (b, s, 0)),
            pl.BlockSpec(tri.shape, const2),
            pl.BlockSpec(eqk.shape, const2),
        ],
        out_specs=[
            pl.BlockSpec((1, N_HEADS, ts, LANES), lambda b, s: (b, 0, s, 0)),
            pl.BlockSpec((1, N_HEADS, ts, LANES), lambda b, s: (b, 0, s, 0)),
            pl.BlockSpec((1, 1, N_HEADS * MLA_V, ts), lambda b, s: (b, s, 0, 0)),
        ],
        out_shape=[
            jax.ShapeDtypeStruct((B, N_HEADS, S, LANES), BF16),
            jax.ShapeDtypeStruct((B, N_HEADS, S, LANES), BF16),
            jax.ShapeDtypeStruct((B, nst, N_HEADS * MLA_V, ts), BF16),
        ],
        scratch_shapes=[pltpu.VMEM((1, LANES), F32)],
        compiler_params=pltpu.CompilerParams(
            dimension_semantics=("arbitrary", "arbitrary"), vmem_limit_bytes=VMEM_LIMIT),
    )(x, mod_l, g, win, qg, kvg, wuqa, wukv, fb_row, ctab, stab, tri, eqk)


def _colmax8(st, groups):
    tk, tq = st.shape
    v = st.reshape(groups, tk // (8 * groups), 8, tq)
    return jnp.max(jnp.max(v, axis=1), axis=0)


def _attn_kernel(q_ref, k_ref, vt_ref, o_ref, s_scr, tmax_scr, m_scr, acc_scr, *, tq, hp, nq):
    tk = tq
    half = tq // 2
    dv = MLA_V
    groups = 4
    ones = jnp.ones((ONES_ROWS, tk), BF16)
    nt_dims = (((1,), (1,)), ((), ()))

    def qk(qi, j):
        q0 = pl.multiple_of(qi * tq, tq)
        return [lax.dot_general(k_ref[0, h, pl.ds(pl.multiple_of(j * tk, tk), tk), :],
                                q_ref[0, h, pl.ds(q0, tq), :], nt_dims,
                                preferred_element_type=F32) for h in range(hp)]

    def park(sts):
        for h in range(hp):
            s_scr[h] = sts[h]
            tmax_scr[h] = _colmax8(sts[h], groups)

    def qk_diag(qi):
        q0 = pl.multiple_of(qi * tq, tq)
        q1 = pl.multiple_of(qi * tq + half, half)
        out = []
        for h in range(hp):
            top = lax.dot_general(k_ref[0, h, pl.ds(q0, half), :], q_ref[0, h, pl.ds(q0, tq), :],
                                  nt_dims, preferred_element_type=F32)
            bot = lax.dot_general(k_ref[0, h, pl.ds(q1, half), :], q_ref[0, h, pl.ds(q1, half), :],
                                  nt_dims, preferred_element_type=F32)
            out.append((top, bot))
        return out

    def park_diag(sts):
        causal = (lax.broadcasted_iota(jnp.int32, (half, half), 0)
                  <= lax.broadcasted_iota(jnp.int32, (half, half), 1))
        for h in range(hp):
            top, bot = sts[h]
            top_l = jnp.where(causal, top[:, :half], NEG_BIG)
            bot_r = jnp.where(causal, bot, NEG_BIG)
            s_scr[h, 0:half, 0:half] = top_l
            s_scr[h, 0:half, half:] = top[:, half:]
            s_scr[h, half:, half:] = bot_r
            tmax_scr[h, :, 0:half] = _colmax8(top_l, groups)
            tmax_scr[h, :, half:] = jnp.maximum(_colmax8(top[:, half:], groups),
                                                _colmax8(bot_r, groups))

    def update(j):
        for h in range(hp):
            m = m_scr[h]
            m_new = jnp.maximum(m, jnp.max(tmax_scr[h], axis=0, keepdims=True))
            alpha = jnp.exp2(m - m_new)
            p = jnp.exp2((s_scr[h] - m_new).astype(BF16))
            vt1 = jnp.concatenate([vt_ref[0, j, h * dv:(h + 1) * dv, :], ones], axis=0)
            acc_scr[h] = alpha * acc_scr[h] + jnp.dot(vt1, p, preferred_element_type=F32)
            m_scr[h] = m_new

    def update_diag(j):
        for h in range(hp):
            m = m_scr[h]
            m_new = jnp.maximum(m, jnp.max(tmax_scr[h], axis=0, keepdims=True))
            alpha = jnp.exp2(m - m_new)
            p_top = jnp.exp2((s_scr[h, 0:half, :] - m_new).astype(BF16))
            p_bot = jnp.exp2((s_scr[h, half:, half:] - m_new[:, half:]).astype(BF16))
            vt1 = jnp.concatenate([vt_ref[0, j, h * dv:(h + 1) * dv, :], ones], axis=0)
            acc = alpha * acc_scr[h] + jnp.dot(vt1[:, 0:half], p_top, preferred_element_type=F32)
            acc_scr[h, :, 0:half] = acc[:, 0:half]
            acc_scr[h, :, half:] = acc[:, half:] + jnp.dot(vt1[:, half:], p_bot,
                                                           preferred_element_type=F32)
            m_scr[h] = m_new

    def reset_state():
        m_scr[...] = jnp.full(m_scr.shape, NEG_BIG, F32)
        acc_scr[...] = jnp.zeros(acc_scr.shape, F32)

    def finish(qi):
        q0 = pl.multiple_of(qi * tq, tq)
        for h in range(hp):
            acc = acc_scr[h]
            o_ref[0, h * dv:(h + 1) * dv, pl.ds(q0, tq)] = (
                acc[0:dv, :] / acc[dv:dv + 1, :]).astype(o_ref.dtype)
        reset_state()

    reset_state()
    park_diag(qk_diag(0))

    def q_tile(qi, c):
        def below_diagonal(t):
            sts = qk(qi, t + 1)
            update(t)
            park(sts)

        def two_below(u, c2):
            below_diagonal(2 * u)
            below_diagonal(2 * u + 1)
            return c2

        lax.fori_loop(0, (qi - 1) // 2, two_below, 0)

        @pl.when((qi > 1) & ((qi - 1) % 2 == 1))
        def _():
            below_diagonal(qi - 2)

        @pl.when(qi > 0)
        def _():
            sts = qk_diag(qi)
            update(qi - 1)
            park_diag(sts)

        @pl.when(qi + 1 < nq)
        def _():
            sts = qk(qi + 1, 0)
            update_diag(qi)
            finish(qi)
            park(sts)

        @pl.when(qi + 1 == nq)
        def _():
            update_diag(qi)
            finish(qi)

        return c

    lax.fori_loop(0, nq, q_tile, 0)


def _attention(qa, ka, vt, *, tq, tk, hp):
    B, H, S, _ = qa.shape
    nkt = vt.shape[1]
    assert S % tq == 0 and nkt * tk == S and H % hp == 0 and tk == tq
    kern = functools.partial(_attn_kernel, tq=tq, hp=hp, nq=S // tq)
    return pl.pallas_call(
        kern,
        grid=(B, H // hp),
        in_specs=[
            pl.BlockSpec((1, hp, S, LANES), lambda b, g: (b, g, 0, 0)),
            pl.BlockSpec((1, hp, S, LANES), lambda b, g: (b, g, 0, 0)),
            pl.BlockSpec((1, nkt, hp * MLA_V, tk), lambda b, g: (b, 0, g, 0)),
        ],
        out_specs=pl.BlockSpec((1, hp * MLA_V, S), lambda b, g: (b, g, 0)),
        out_shape=jax.ShapeDtypeStruct((B, H * MLA_V, S), BF16),
        scratch_shapes=[
            pltpu.VMEM((hp, tk, tq), F32),
            pltpu.VMEM((hp, 8, tq), F32),
            pltpu.VMEM((hp, 1, tq), F32),
            pltpu.VMEM((hp, MLA_V + ONES_ROWS, tq), F32),
        ],
        compiler_params=pltpu.CompilerParams(
            dimension_semantics=("arbitrary", "arbitrary"), vmem_limit_bytes=VMEM_LIMIT),
    )(qa, ka, vt)


def _top2_gates(logits, lane):
    lg = jnp.where(lane < N_EXPERTS, logits, NEG_BIG)
    m1 = jnp.max(lg, axis=1, keepdims=True)
    i1 = jnp.min(jnp.where(lg == m1, lane, LANES), axis=1, keepdims=True)
    lg2 = jnp.where(lane == i1, NEG_BIG, lg)
    m2 = jnp.max(lg2, axis=1, keepdims=True)
    i2 = jnp.min(jnp.where(lg2 == m2, lane, LANES), axis=1, keepdims=True)
    e2 = jnp.exp(m2 - m1)
    den = 1.0 + e2
    return jnp.where(lane == i1, 1.0 / den, 0.0) + jnp.where(lane == i2, e2 / den, 0.0)


def _outproj_kernel(*refs, ts, with_router):
    if with_router:
        (ot_ref, x_ref, mod_ref, mg_ref, fg_ref, wo_ref, ng_ref, rw_ref,
         x1_ref, h2_ref, gates_ref) = refs
    else:
        ot_ref, x_ref, mod_ref, mg_ref, fg_ref, wo_ref, ng_ref, x1_ref, h2_ref = refs
    half = MLA_HEADS * MLA_V
    om = ot_ref[0, 0:half, :].astype(F32).T
    of = ot_ref[0, half:, :].astype(F32).T
    on = jnp.concatenate([_rms(om) * mg_ref[...], _rms(of) * fg_ref[...]], axis=1).astype(BF16)
    mix = jnp.dot(on, wo_ref[...], preferred_element_type=F32)
    x1 = x_ref[0] + mod_ref[0, 2:3, :] * mix
    x1_ref[0] = x1
    h2 = ((_rms(x1) * ng_ref[...]) * (1.0 + mod_ref[0, 4:5, :]) + mod_ref[0, 3:4, :]).astype(BF16)
    h2_ref[0] = h2
    if with_router:
        logits = jnp.dot(h2, rw_ref[...], preferred_element_type=F32)
        lane = lax.broadcasted_iota(jnp.int32, (ts, LANES), 1)
        gates_ref[0] = _top2_gates(logits, lane)


def _outproj(ot, x, mod_l, mg, fg, wo, ng, rw, *, ts):
    B, S, D = x.shape
    with_router = rw is not None
    const2 = lambda b, s: (0, 0)
    in_specs = [
        pl.BlockSpec((1, ot.shape[1], ts), lambda b, s: (b, 0, s)),
        pl.BlockSpec((1, ts, D), lambda b, s: (b, s, 0)),
        pl.BlockSpec((1, 6, D), lambda b, s: (b, 0, 0)),
        pl.BlockSpec(mg.shape, const2),
        pl.BlockSpec(fg.shape, const2),
        pl.BlockSpec(wo.shape, const2),
        pl.BlockSpec(ng.shape, const2),
    ]
    args = [ot, x, mod_l, mg, fg, wo, ng]
    out_specs = [pl.BlockSpec((1, ts, D), lambda b, s: (b, s, 0)),
                 pl.BlockSpec((1, ts, D), lambda b, s: (b, s, 0))]
    out_shape = [jax.ShapeDtypeStruct((B, S, D), F32), jax.ShapeDtypeStruct((B, S, D), BF16)]
    if with_router:
        in_specs.append(pl.BlockSpec(rw.shape, const2))
        args.append(rw)
        out_specs.append(pl.BlockSpec((1, ts, LANES), lambda b, s: (b, s, 0)))
        out_shape.append(jax.ShapeDtypeStruct((B, S, LANES), F32))
    kern = functools.partial(_outproj_kernel, ts=ts, with_router=with_router)
    return pl.pallas_call(
        kern,
        grid=(B, S // ts),
        in_specs=in_specs,
        out_specs=out_specs,
        out_shape=out_shape,
        compiler_params=pltpu.CompilerParams(
            dimension_semantics=("arbitrary", "arbitrary"), vmem_limit_bytes=VMEM_LIMIT),
    )(*args)


def _swiglu_tile(x, wg, wu):
    g = jnp.dot(x, wg, preferred_element_type=F32)
    u = jnp.dot(x, wu, preferred_element_type=F32)
    return g * _sigmoid(g) * u


def _ffn_kernel(h_ref, wg_ref, wu_ref, wd_ref, x_ref, mod_ref, fin_ref, o_ref, acc_ref,
                *, final_norm):
    f = pl.program_id(1)

    @pl.when(f == 0)
    def _():
        acc_ref[...] = jnp.zeros_like(acc_ref)

    a = _swiglu_tile(h_ref[...], wg_ref[...], wu_ref[...])
    acc_ref[...] += jnp.dot(a.astype(BF16), wd_ref[...], preferred_element_type=F32)

    @pl.when(f == pl.num_programs(1) - 1)
    def _():
        x2 = x_ref[...] + mod_ref[0, 5:6, :] * acc_ref[...]
        if final_norm:
            x2 = _rms(x2) * fin_ref[...]
        o_ref[...] = x2


def _ffn(h2, wg, wu, wd, x1, mod_l, fin_g, *, tm, tf, seq, final_norm):
    T, D = h2.shape
    F = wd.shape[0]
    tiles_per_batch = seq // tm
    kern = functools.partial(_ffn_kernel, final_norm=final_norm)
    return pl.pallas_call(
        kern,
        grid=(T // tm, F // tf),
        in_specs=[
            pl.BlockSpec((tm, D), lambda i, f: (i, 0)),
            pl.BlockSpec((D, tf), lambda i, f: (0, f)),
            pl.BlockSpec((D, tf), lambda i, f: (0, f)),
            pl.BlockSpec((tf, D), lambda i, f: (f, 0)),
            pl.BlockSpec((tm, D), lambda i, f: (i, 0)),
            pl.BlockSpec((1, 6, D), lambda i, f: (i // tiles_per_batch, 0, 0)),
            pl.BlockSpec((1, D), lambda i, f: (0, 0)),
        ],
        out_specs=pl.BlockSpec((tm, D), lambda i, f: (i, 0)),
        out_shape=jax.ShapeDtypeStruct((T, D), F32),
        scratch_shapes=[pltpu.VMEM((tm, D), F32)],
        compiler_params=pltpu.CompilerParams(
            dimension_semantics=("arbitrary", "arbitrary"), vmem_limit_bytes=VMEM_LIMIT),
    )(h2, wg, wu, wd, x1, mod_l, fin_g)


RC = 64
VT = 16
SPB = 8


def _route_kernel(gates_ref, rank_ref, rankt_ref, gatest_ref, cnt_ref, *, tm):
    gates = gates_ref[...]
    sel = gates > 0.0
    self32 = jnp.where(sel, 1.0, 0.0)
    earlier = (lax.broadcasted_iota(jnp.int32, (tm, tm), 1)
               < lax.broadcasted_iota(jnp.int32, (tm, tm), 0))
    rank = jnp.dot(jnp.where(earlier, 1.0, 0.0).astype(BF16), self32.astype(BF16),
                   preferred_element_type=F32)
    rank = jnp.where(sel, rank, -1.0)
    rank_ref[...] = rank
    rankt_ref[0] = rank.T
    gatest_ref[0] = gates.T
    cnt_ref[0] = jnp.sum(self32, axis=0, keepdims=True)


def _route(gates, *, tm):
    T = gates.shape[0]
    nt = T // tm
    return pl.pallas_call(
        functools.partial(_route_kernel, tm=tm),
        grid=(nt,),
        in_specs=[pl.BlockSpec((tm, LANES), lambda i: (i, 0))],
        out_specs=[
            pl.BlockSpec((tm, LANES), lambda i: (i, 0)),
            pl.BlockSpec((1, LANES, tm), lambda i: (i, 0, 0)),
            pl.BlockSpec((1, LANES, tm), lambda i: (i, 0, 0)),
            pl.BlockSpec((1, 1, LANES), lambda i: (i, 0, 0)),
        ],
        out_shape=[
            jax.ShapeDtypeStruct((T, LANES), F32),
            jax.ShapeDtypeStruct((nt, LANES, tm), F32),
            jax.ShapeDtypeStruct((nt, LANES, tm), F32),
            jax.ShapeDtypeStruct((nt, 1, LANES), F32),
        ],
        compiler_params=pltpu.CompilerParams(
            dimension_semantics=("arbitrary",), vmem_limit_bytes=VMEM_LIMIT),
    )(gates)


def _slot_tables(cnt, nt, ns, n_vt):
    E = N_EXPERTS
    nch = (cnt + (RC - 1)) // RC
    nct = jnp.sum(nch, axis=1)
    ccum = jnp.cumsum(nch, axis=1)
    j = jnp.arange(ns, dtype=jnp.int32)
    slot_e = jnp.sum((ccum[:, None, :] <= j[None, :, None]).astype(jnp.int32), axis=2)
    slot_e = jnp.minimum(slot_e, E - 1)
    first = jnp.take_along_axis(ccum - nch, slot_e, axis=1)
    valid = j[None, :] < nct[:, None]
    slot_k = jnp.where(valid, j[None, :] - first, 0)
    nce = jnp.sum(nch, axis=0)
    padded = (nce + (VT - 1)) // VT * VT
    base = jnp.cumsum(padded) - padded
    cid0 = base[None, :] + jnp.cumsum(nch, axis=0) - nch
    cid = jnp.take_along_axis(cid0, slot_e, axis=1) + slot_k
    cid = jnp.where(valid, cid, cid[:, 0:1])
    n_chunks = n_vt * VT
    slot_id = jnp.arange(nt, dtype=jnp.int32)[:, None] * ns + j[None, :]
    pos = jnp.full((n_chunks,), -1, jnp.int32).at[
        jnp.where(valid, cid, n_chunks).reshape(-1)].set(slot_id.reshape(-1), mode="drop")
    vcum = jnp.cumsum(padded // VT)
    nv = vcum[-1]
    v = jnp.minimum(jnp.arange(n_vt, dtype=jnp.int32), nv - 1)
    v_exp = jnp.sum((vcum[None, :] <= v[:, None]).astype(jnp.int32), axis=1)
    v_exp = jnp.minimum(v_exp, E - 1)
    i32 = lambda a: a.reshape(-1).astype(jnp.int32)
    return dict(slot_e=i32(slot_e), slot_k=i32(slot_k), nct=i32(nct), cid=i32(cid),
                pos=jnp.maximum(pos, 0), real=(pos >= 0).astype(jnp.int32),
                v_exp=v_exp, nv=i32(nv))


def _compact_kernel(se_ref, sk_ref, nct_ref, h_ref, rankt_ref, gatest_ref, xs_ref, gc_ref, oh_scr,
                    *, tm, ns):
    t = pl.program_id(0)
    n = nct_ref[t]

    def build(j, c):
        e = se_ref[t * ns + j]
        rrow = rankt_ref[0, pl.ds(e, 1), :].astype(jnp.int32)
        grow = gatest_ref[0, pl.ds(e, 1), :]
        row = lax.broadcasted_iota(jnp.int32, (RC, tm), 0) + sk_ref[t * ns + j] * RC
        hit = (rrow == row) & (j < n)
        oh_scr[j] = jnp.where(hit, 1.0, 0.0).astype(BF16)
        gcol = jnp.sum(jnp.where(hit, grow, 0.0), axis=1, keepdims=True)
        gc_ref[pl.ds(pl.multiple_of(j * RC, RC), RC), :] = jnp.broadcast_to(gcol, (RC, LANES))
        return c

    lax.fori_loop(0, ns, build, 0)
    rows = SPB * RC
    for b in range(ns // SPB):
        oh = oh_scr[b * SPB:(b + 1) * SPB].reshape(rows, tm)
        xs_ref[b * rows:(b + 1) * rows, :] = jnp.dot(
            oh, h_ref[...], preferred_element_type=F32).astype(BF16)


def _compact(tabs, h2, rankt, gatest, *, tm, ns):
    T, D = h2.shape
    nt = T // tm
    grid_spec = pltpu.PrefetchScalarGridSpec(
        num_scalar_prefetch=3,
        grid=(nt,),
        in_specs=[
            pl.BlockSpec((tm, D), lambda t, *_: (t, 0)),
            pl.BlockSpec((1, LANES, tm), lambda t, *_: (t, 0, 0)),
            pl.BlockSpec((1, LANES, tm), lambda t, *_: (t, 0, 0)),
        ],
        out_specs=[
            pl.BlockSpec((ns * RC, D), lambda t, *_: (t, 0)),
            pl.BlockSpec((ns * RC, LANES), lambda t, *_: (t, 0)),
        ],
        scratch_shapes=[pltpu.VMEM((ns, RC, tm), BF16)],
    )
    return pl.pallas_call(
        functools.partial(_compact_kernel, tm=tm, ns=ns),
        grid_spec=grid_spec,
        out_shape=[jax.ShapeDtypeStruct((nt * ns * RC, D), BF16),
                   jax.ShapeDtypeStruct((nt * ns * RC, LANES), F32)],
        compiler_params=pltpu.CompilerParams(
            dimension_semantics=("arbitrary",), vmem_limit_bytes=VMEM_LIMIT),
    )(tabs["slot_e"], tabs["slot_k"], tabs["nct"], h2, rankt, gatest)


def _slot_experts_kernel(vexp_ref, nv_ref, pos_ref, real_ref, *refs):
    xs_refs = refs[0:VT]
    gc_refs = refs[VT:2 * VT]
    wg_ref, wu_ref, wd_ref, ys_ref, x_scr, g_scr, acc_ref = refs[2 * VT:]
    v = pl.program_id(0)
    f = pl.program_id(1)

    @pl.when(v < nv_ref[0])
    def _():
        @pl.when(f == 0)
        def _():
            for j in range(VT):
                keep = real_ref[v * VT + j] == 1
                x_scr[j * RC:(j + 1) * RC, :] = jnp.where(keep, xs_refs[j][...], jnp.zeros_like(xs_refs[j]))
                g_scr[j * RC:(j + 1) * RC, :] = jnp.where(keep, gc_refs[j][...], 0.0)
            acc_ref[...] = jnp.zeros_like(acc_ref)

        a = _swiglu_tile(x_scr[...], wg_ref[0], wu_ref[0]) * g_scr[:, 0:1]
        acc_ref[...] += jnp.dot(a.astype(BF16), wd_ref[0], preferred_element_type=F32)

        @pl.when(f == pl.num_programs(1) - 1)
        def _():
            ys_ref[...] = acc_ref[...].astype(BF16)


def _slot_experts(tabs, xs, gc, wg, wu, wd, *, tf, n_vt):
    D = xs.shape[1]
    NF = wd.shape[1] // tf
    rows = VT * RC

    def live(v, nv):
        return jnp.minimum(v, nv[0] - 1)

    def hidden(v, f, nv):
        return jnp.where(v < nv[0], f, NF - 1)

    def chunk(j):
        return lambda v, f, vexp, nv, pos, real: (pos[live(v, nv) * VT + j], 0)

    grid_spec = pltpu.PrefetchScalarGridSpec(
        num_scalar_prefetch=4,
        grid=(n_vt, NF),
        in_specs=(
            [pl.BlockSpec((RC, D), chunk(j)) for j in range(VT)]
            + [pl.BlockSpec((RC, LANES), chunk(j)) for j in range(VT)]
            + [pl.BlockSpec((1, D, tf),
                            lambda v, f, vexp, nv, pos, real: (vexp[v], 0, hidden(v, f, nv)))] * 2
            + [pl.BlockSpec((1, tf, D),
                            lambda v, f, vexp, nv, pos, real: (vexp[v], hidden(v, f, nv), 0))]),
        out_specs=pl.BlockSpec((rows, D), lambda v, f, vexp, nv, pos, real: (live(v, nv), 0)),
        scratch_shapes=[pltpu.VMEM((rows, D), BF16), pltpu.VMEM((rows, LANES), F32),
                        pltpu.VMEM((rows, D), F32)],
    )
    return pl.pallas_call(
        _slot_experts_kernel,
        grid_spec=grid_spec,
        out_shape=jax.ShapeDtypeStruct((n_vt * rows, D), BF16),
        compiler_params=pltpu.CompilerParams(
            dimension_semantics=("arbitrary", "arbitrary"), vmem_limit_bytes=VMEM_LIMIT),
    )(tabs["v_exp"], tabs["nv"], tabs["pos"], tabs["real"], *([xs] * VT), *([gc] * VT), wg, wu, wd)


def _uncompact_kernel(se_ref, sk_ref, nct_ref, cid_ref, *refs, tm, ns, final_norm):
    ys_refs = refs[0:ns]
    rank_ref, x_ref, mod_ref, fin_ref, o_ref, oh_scr = refs[ns:]
    t = pl.program_id(0)
    n = nct_ref[t]
    lane = lax.broadcasted_iota(jnp.int32, (tm, LANES), 1)
    right = lane >= RC
    col = jnp.where(right, lane - RC, lane)
    rank = rank_ref[...]

    def slot_rank(j):
        return jnp.sum(jnp.where(lane == se_ref[t * ns + j], rank, 0.0), axis=1,
                       keepdims=True).astype(jnp.int32)

    for jp in range(ns // 2):
        ja, jb = 2 * jp, 2 * jp + 1
        rcol = jnp.where(right, slot_rank(jb), slot_rank(ja))
        want = col + jnp.where(right, sk_ref[t * ns + jb], sk_ref[t * ns + ja]) * RC
        used = jnp.where(right, jb, ja) < n
        oh_scr[:, jp * LANES:(jp + 1) * LANES] = jnp.where(
            (rcol == want) & used, 1.0, 0.0).astype(BF16)
    y = None
    rows = SPB * RC
    for b in range(ns // SPB):
        ys = jnp.concatenate([ys_refs[b * SPB + j][...] for j in range(SPB)], axis=0)
        part = jnp.dot(oh_scr[:, b * rows:(b + 1) * rows], ys, preferred_element_type=F32)
        y = part if y is None else y + part
    x2 = x_ref[...] + mod_ref[0, 5:6, :] * y
    if final_norm:
        x2 = _rms(x2) * fin_ref[...]
    o_ref[...] = x2


def _uncompact(tabs, ys, rank, x1, mod_l, fin_g, *, tm, ns, seq, final_norm):
    T, D = x1.shape
    tpb = seq // tm

    def chunk(j):
        return lambda t, se, sk, nct, cid: (cid[t * ns + j], 0)

    grid_spec = pltpu.PrefetchScalarGridSpec(
        num_scalar_prefetch=4,
        grid=(T // tm,),
        in_specs=(
            [pl.BlockSpec((RC, D), chunk(j)) for j in range(ns)]
            + [pl.BlockSpec((tm, LANES), lambda t, *_: (t, 0)),
               pl.BlockSpec((tm, D), lambda t, *_: (t, 0)),
               pl.BlockSpec((1, 6, D), lambda t, *_: (t // tpb, 0, 0)),
               pl.BlockSpec((1, D), lambda t, *_: (0, 0))]),
        out_specs=pl.BlockSpec((tm, D), lambda t, *_: (t, 0)),
        scratch_shapes=[pltpu.VMEM((tm, ns * RC), BF16)],
    )
    return pl.pallas_call(
        functools.partial(_uncompact_kernel, tm=tm, ns=ns, final_norm=final_norm),
        grid_spec=grid_spec,
        out_shape=jax.ShapeDtypeStruct((T, D), F32),
        compiler_params=pltpu.CompilerParams(
            dimension_semantics=("arbitrary",), vmem_limit_bytes=VMEM_LIMIT),
    )(tabs["slot_e"], tabs["slot_k"], tabs["nct"], tabs["cid"], *([ys] * ns),
      rank, x1, mod_l, fin_g)


def _routed_experts(h2, gates, w_gate, w_up, w_down, x1, mod_l, fin_g, *, tm, tf, seq, final_norm):
    T, D = h2.shape
    E = N_EXPERTS
    nt = T // tm
    ns = 2 * tm // RC + E
    assert ns % SPB == 0 and 2 * RC == LANES
    n_vt = -(-(nt * ns + E * (VT - 1)) // VT)
    rank, rankt, gatest, cnt = _route(gates, tm=tm)
    tabs = _slot_tables(cnt[:, 0, :E].astype(jnp.int32), nt, ns, n_vt)
    xs, gc = _compact(tabs, h2, rankt, gatest, tm=tm, ns=ns)
    ys = _slot_experts(tabs, xs, gc, w_gate.astype(BF16), w_up.astype(BF16), w_down.astype(BF16),
                       tf=tf, n_vt=n_vt)
    return _uncompact(tabs, ys, rank, x1, mod_l, fin_g, tm=tm, ns=ns, seq=seq,
                      final_norm=final_norm)


def _tiles(S):
    ts = min(512, S)
    tq = min(512, S)
    tk = ts
    tm = min(1024, S)
    tf = D_FF // 2
    return ts, tq, tk, tm, tf


def kernel(x, c, positions, ada_w, ada_b, attn_norm_g, w_in, q_norm_g, w_uq, kv_norm_g, w_ukv,
           fox_forget_b, mla_out_g, fox_out_g, w_o, ffn_norm_g, dense_w_gate, dense_w_up,
           dense_w_down, router_w, moe_w_gate, moe_w_up, moe_w_down, final_norm_g):
    B, S, D = x.shape
    L = ada_w.shape[0]
    ts, tq, tk, tm, tf = _tiles(S)

    mod = _adaln(c, ada_w, ada_b).reshape(L, B, 6, D)
    ctab, stab = _rope_tables(positions)

    idx_in = _perm_w_in()
    idx_qa = _perm_w_uq()
    idx_kv = _perm_w_ukv()
    col_scale = np.ones((C_END,), np.float32)
    col_scale[C_FQ0:C_FQ0 + FOX_W] = FOX_HEAD_DIM ** -0.5
    tri = jnp.asarray(np.tril(np.ones((ts, ts), np.float32)), BF16)
    eqk = jnp.asarray(_aug_placement(), BF16)

    for l in range(L):
        win = (_take_cols(w_in[l], idx_in) * col_scale).astype(BF16)
        wuqa = _take_cols(w_uq[l], idx_qa).astype(BF16)
        wukv = _take_cols(w_ukv[l], idx_kv).astype(BF16)
        fb_row = jnp.zeros((1, LANES), F32).at[0, CTRL_LOGIT:CTRL_LOGIT + FOX_HEADS].set(
            fox_forget_b[l].astype(F32))
        qa, ka, vt = _inproj(
            x, mod[l], attn_norm_g[l].reshape(1, D), win, q_norm_g[l].reshape(1, -1),
            kv_norm_g[l].reshape(1, -1), wuqa, wukv, fb_row, ctab, stab, tri, eqk, ts=ts)
        ot = _attention(qa, ka, vt, tq=tq, tk=tk, hp=4)

        j = l // 2
        is_moe = (l % 2 == 1)
        rw = None
        if is_moe:
            rw = jnp.pad(router_w[j], ((0, 0), (0, LANES - N_EXPERTS))).astype(BF16)
        outs = _outproj(ot, x, mod[l], mla_out_g[l].reshape(1, -1), fox_out_g[l].reshape(1, -1),
                        w_o[l].astype(BF16), ffn_norm_g[l].reshape(1, D), rw, ts=ts)
        x1, h2 = outs[0].reshape(B * S, D), outs[1].reshape(B * S, D)
        fin_g = final_norm_g.reshape(1, D)
        last = l == L - 1
        if is_moe:
            x = _routed_experts(h2, outs[2].reshape(B * S, LANES), moe_w_gate[j], moe_w_up[j],
                                moe_w_down[j], x1, mod[l], fin_g,
                                tm=tm, tf=tf, seq=S, final_norm=last)
        else:
            x = _ffn(h2, dense_w_gate[j].astype(BF16), dense_w_up[j].astype(BF16),
                     dense_w_down[j].astype(BF16), x1, mod[l], fin_g,
                     tm=tm, tf=tf, seq=S, final_norm=last)
        x = x.reshape(B, S, D)
    return x
```

```python
import functools

import numpy as np
import jax
import jax.numpy as jnp
from jax import lax
from jax.experimental import pallas as pl
from jax.experimental.pallas import tpu as pltpu

F32 = jnp.float32
BF16 = jnp.bfloat16

D_MODEL = 1024
MLA_HEADS = 8
MLA_V = 64
MLA_NOPE = 64
MLA_ROPE = 32
MLA_Q_LORA = 256
MLA_KV_LORA = 128
FOX_HEADS = 8
FOX_HEAD_DIM = 64
FOX_W = FOX_HEADS * FOX_HEAD_DIM
ROPE_THETA = 10000.0
D_FF = 3584
N_EXPERTS = 8
EPS = 1e-6
N_HEADS = MLA_HEADS + FOX_HEADS
HALF_ROPE = MLA_ROPE // 2

LANES = 128
VMEM_LIMIT = 56 * 1024 * 1024

ROPE_LO = MLA_NOPE
AUG_LO = FOX_HEAD_DIM
AUG_W = 6
CTRL_LOGIT = 96
CTRL_ONE = 120

C_Q0 = 0
C_KV0 = C_Q0 + MLA_Q_LORA
C_KRA0 = C_KV0 + MLA_KV_LORA
C_FQ0 = C_KRA0 + LANES
C_FK0 = C_FQ0 + FOX_W
C_FV0 = C_FK0 + FOX_W
C_END = C_FV0 + FOX_W

NEG_BIG = -1e30
LOG2E = 1.4426950408889634
ONES_ROWS = 16


def _rms(v):
    return v * lax.rsqrt(jnp.mean(v * v, axis=-1, keepdims=True) + EPS)


def _sigmoid(v):
    return 1.0 / (1.0 + jnp.exp(-v))


def _split3(v):
    hi = v.astype(BF16).astype(F32)
    r = v - hi
    mid = r.astype(BF16).astype(F32)
    lo = (r - mid).astype(BF16).astype(F32)
    return hi, mid, lo


def _adaln_kernel(c_ref, w_ref, b_ref, o_ref):
    c = c_ref[...]
    ca = (c * _sigmoid(c)).astype(BF16)
    o_ref[0] = jnp.dot(ca, w_ref[0].astype(BF16), preferred_element_type=F32) + b_ref[0]


def _adaln(c, ada_w, ada_b):
    L, D, N = ada_w.shape
    B = c.shape[0]
    tn = 1536
    return pl.pallas_call(
        _adaln_kernel,
        grid=(L, N // tn),
        in_specs=[
            pl.BlockSpec((B, D), lambda l, j: (0, 0)),
            pl.BlockSpec((1, D, tn), lambda l, j: (l, 0, j)),
            pl.BlockSpec((1, 1, tn), lambda l, j: (l, 0, j)),
        ],
        out_specs=pl.BlockSpec((1, B, tn), lambda l, j: (l, 0, j)),
        out_shape=jax.ShapeDtypeStruct((L, B, N), F32),
        compiler_params=pltpu.CompilerParams(
            dimension_semantics=("arbitrary", "arbitrary"), vmem_limit_bytes=VMEM_LIMIT),
    )(c, ada_w, ada_b.reshape(L, 1, N))


def _rope_tab_kernel(pos_ref, freq_ref, cos_ref, sin_ref):
    ang = freq_ref[...] * pos_ref[0]
    cos_ref[0] = jnp.cos(ang)
    sin_ref[0] = jnp.sin(ang)


def _rope_tables(positions):
    B, S = positions.shape
    half = HALF_ROPE
    inv_freq = ROPE_THETA ** (-jnp.arange(half, dtype=F32) / half)
    pos = positions.astype(F32).reshape(B, 1, S)
    cos_t, sin_t = pl.pallas_call(
        _rope_tab_kernel,
        grid=(B,),
        in_specs=[
            pl.BlockSpec((1, 1, S), lambda b: (b, 0, 0)),
            pl.BlockSpec((half, 1), lambda b: (0, 0)),
        ],
        out_specs=[pl.BlockSpec((1, half, S), lambda b: (b, 0, 0))] * 2,
        out_shape=[jax.ShapeDtypeStruct((B, half, S), F32)] * 2,
        compiler_params=pltpu.CompilerParams(dimension_semantics=("arbitrary",)),
    )(pos, inv_freq.reshape(half, 1))
    cos = jnp.transpose(cos_t, (0, 2, 1))
    sin = jnp.transpose(sin_t, (0, 2, 1))
    ones = jnp.ones((B, S, MLA_NOPE), F32)
    z_lo = jnp.zeros((B, S, MLA_NOPE), F32)
    z_hi = jnp.zeros((B, S, LANES - MLA_NOPE - MLA_ROPE), F32)
    ctab = jnp.concatenate([ones, cos, cos, z_hi], axis=-1)
    stab = jnp.concatenate([z_lo, -sin, sin, z_hi], axis=-1)
    return ctab, stab


def _take_cols(w, idx):
    pieces, i, n = [], 0, len(idx)
    while i < n:
        j = i + 1
        if idx[i] < 0:
            while j < n and idx[j] < 0:
                j += 1
            pieces.append(jnp.zeros((w.shape[0], j - i), w.dtype))
        else:
            while j < n and idx[j] == idx[j - 1] + 1:
                j += 1
            pieces.append(w[:, int(idx[i]):int(idx[i]) + (j - i)])
        i = j
    return jnp.concatenate(pieces, axis=1)


def _perm_w_in():
    cq = 0
    ckv = cq + MLA_Q_LORA
    kr = ckv + MLA_KV_LORA
    fq = kr + MLA_ROPE
    fk = fq + FOX_W
    fv = fk + FOX_W
    fl = fv + FOX_W
    idx = -np.ones((C_END,), np.int64)
    idx[C_Q0:C_Q0 + MLA_Q_LORA] = cq + np.arange(MLA_Q_LORA)
    idx[C_KV0:C_KV0 + MLA_KV_LORA] = ckv + np.arange(MLA_KV_LORA)
    idx[C_KRA0 + ROPE_LO:C_KRA0 + ROPE_LO + MLA_ROPE] = kr + np.arange(MLA_ROPE)
    idx[C_KRA0 + CTRL_LOGIT:C_KRA0 + CTRL_LOGIT + FOX_HEADS] = fl + np.arange(FOX_HEADS)
    idx[C_FQ0:C_FQ0 + FOX_W] = fq + np.arange(FOX_W)
    idx[C_FK0:C_FK0 + FOX_W] = fk + np.arange(FOX_W)
    idx[C_FV0:C_FV0 + FOX_W] = fv + np.arange(FOX_W)
    return idx


def _perm_w_uq():
    per = MLA_NOPE + MLA_ROPE
    ia = -np.ones((MLA_HEADS * LANES,), np.int64)
    for h in range(MLA_HEADS):
        ia[h * LANES:h * LANES + per] = h * per + np.arange(per)
    return ia


def _perm_w_ukv():
    per = MLA_NOPE + MLA_V
    ik = -np.ones((MLA_HEADS * LANES,), np.int64)
    iv = np.zeros((MLA_HEADS * MLA_V,), np.int64)
    for h in range(MLA_HEADS):
        ik[h * LANES:h * LANES + MLA_NOPE] = h * per + np.arange(MLA_NOPE)
        iv[h * MLA_V:(h + 1) * MLA_V] = h * per + MLA_NOPE + np.arange(MLA_V)
    return np.concatenate([ik, iv])


def _aug_placement():
    e = np.zeros((LANES, 2 * LANES), np.float32)
    for h in range(FOX_HEADS):
        lo = AUG_LO + AUG_W * h
        for p in range(3):
            src = CTRL_LOGIT + 8 * p + h
            e[src, lo + p] = 1.0
            e[CTRL_ONE, lo + 3 + p] = -1.0
            e[CTRL_ONE, LANES + lo + p] = 1.0
            e[src, LANES + lo + 3 + p] = 1.0
    return e


def _inproj_kernel(x_ref, mod_ref, g_ref, win_ref, qg_ref, kvg_ref, wuqa_ref, wukv_ref,
                   fb_ref, ct_ref, st_ref, tri_ref, eqk_ref,
                   qa_ref, ka_ref, vt_ref, carry_ref, *, ts):
    si = pl.program_id(1)

    @pl.when(si == 0)
    def _():
        carry_ref[...] = jnp.zeros_like(carry_ref)

    x = x_ref[0]
    shift = mod_ref[0, 0:1, :]
    scale = mod_ref[0, 1:2, :]
    h = ((_rms(x) * g_ref[...]) * (1.0 + scale) + shift).astype(BF16)
    proj = jnp.dot(h, win_ref[...], preferred_element_type=F32)

    ctab = ct_ref[0]
    stab = st_ref[0]
    lane = lax.broadcasted_iota(jnp.int32, (ts, LANES), 1)

    c_q = proj[:, C_Q0:C_Q0 + MLA_Q_LORA]
    cqn = (_rms(c_q) * qg_ref[...]).astype(BF16)
    qa = jnp.dot(cqn, wuqa_ref[...], preferred_element_type=F32)
    mla_scale = (MLA_NOPE + MLA_ROPE) ** -0.5 * LOG2E
    first_half = lane < ROPE_LO + HALF_ROPE

    def rope(blk):
        swapped = jnp.where(first_half, pltpu.roll(blk, LANES - HALF_ROPE, 1),
                            pltpu.roll(blk, HALF_ROPE, 1))
        return blk * ctab + swapped * stab

    for hh in range(MLA_HEADS):
        qa_ref[0, hh] = (rope(qa[:, hh * LANES:(hh + 1) * LANES]) * mla_scale).astype(BF16)

    c_kv = proj[:, C_KV0:C_KV0 + MLA_KV_LORA]
    ckvn = (_rms(c_kv) * kvg_ref[...]).astype(BF16)
    kv = jnp.dot(ckvn, wukv_ref[...], preferred_element_type=F32)
    kra = proj[:, C_KRA0:C_KRA0 + LANES]
    krope = rope(kra)
    for hh in range(MLA_HEADS):
        ka_ref[0, hh] = (kv[:, hh * LANES:(hh + 1) * LANES] + krope).astype(BF16)
    v_mla = kv[:, MLA_HEADS * LANES:]

    ctrl = (lane >= CTRL_LOGIT) & (lane < CTRL_LOGIT + FOX_HEADS)
    fl = kra + fb_ref[...]
    lsig = jnp.minimum(fl, 0.0) - jnp.log1p(jnp.exp(-jnp.abs(fl)))
    lf = jnp.where(ctrl, lsig, 0.0)
    hi, mid, lo = _split3(lf)
    p1 = (hi + pltpu.roll(mid, 8, 1) + pltpu.roll(lo, 16, 1)).astype(BF16)
    cs = jnp.dot(tri_ref[...], p1, preferred_element_type=F32)
    cs = cs + pltpu.roll(cs, LANES - 8, 1) + pltpu.roll(cs, LANES - 16, 1)
    fcum = jnp.where(ctrl, cs, 0.0) + carry_ref[...]
    carry_ref[...] = fcum[ts - 1:ts, :]

    hi, mid, lo = _split3(fcum * LOG2E)
    p2 = hi + pltpu.roll(mid, 8, 1) + pltpu.roll(lo, 16, 1)
    p2 = jnp.where(lane == CTRL_ONE, 1.0, p2).astype(BF16)
    aug = jnp.dot(p2, eqk_ref[...], preferred_element_type=F32)
    aug_q = aug[:, :LANES]
    aug_k = aug[:, LANES:]

    fq = proj[:, C_FQ0:C_FQ0 + FOX_W]
    fk = proj[:, C_FK0:C_FK0 + FOX_W]
    low = lane < FOX_HEAD_DIM
    for j in range(FOX_HEADS // 2):
        sl = slice(j * LANES, (j + 1) * LANES)
        qblk = fq[:, sl] * LOG2E
        kblk = fk[:, sl]
        for h0, qb_, kb_ in ((2 * j, qblk, kblk),
                             (2 * j + 1, pltpu.roll(qblk, FOX_HEAD_DIM, 1),
                              pltpu.roll(kblk, FOX_HEAD_DIM, 1))):
            own = (lane >= AUG_LO + AUG_W * h0) & (lane < AUG_LO + AUG_W * (h0 + 1))
            qa_ref[0, MLA_HEADS + h0] = jnp.where(low, qb_, aug_q).astype(BF16)
            ka_ref[0, MLA_HEADS + h0] = jnp.where(low, kb_, jnp.where(own, aug_k, 0.0)).astype(BF16)

    fv = proj[:, C_FV0:C_FV0 + FOX_W]
    vt_ref[0, 0, 0:MLA_HEADS * MLA_V, :] = v_mla.T.astype(BF16)
    vt_ref[0, 0, MLA_HEADS * MLA_V:, :] = fv.T.astype(BF16)


def _inproj(x, mod_l, g, win, qg, kvg, wuqa, wukv, fb_row, ctab, stab, tri, eqk, *, ts):
    B, S, D = x.shape
    nst = S // ts
    const2 = lambda b, s: (0, 0)
    kern = functools.partial(_inproj_kernel, ts=ts)
    return pl.pallas_call(
        kern,
        grid=(B, nst),
        in_specs=[
            pl.BlockSpec((1, ts, D), lambda b, s: (b, s, 0)),
            pl.BlockSpec((1, 6, D), lambda b, s: (b, 0, 0)),
            pl.BlockSpec((1, D), const2),
            pl.BlockSpec(win.shape, const2),
            pl.BlockSpec(qg.shape, const2),
            pl.BlockSpec(kvg.shape, const2),
            pl.BlockSpec(wuqa.shape, const2),
            pl.BlockSpec(wukv.shape, const2),
            pl.BlockSpec(fb_row.shape, const2),
            pl.BlockSpec((1, ts, LANES), lambda b, s: (b, s, 0)),
            pl.BlockSpec((1, ts, LANES), lambda b, s: (b, s, 0)),
            pl.BlockSpec(tri.shape, const2),
            pl.BlockSpec(eqk.shape, const2),
        ],
        out_specs=[
            pl.BlockSpec((1, N_HEADS, ts, LANES), lambda b, s: (b, 0, s, 0)),
            pl.BlockSpec((1, N_HEADS, ts, LANES), lambda b, s: (b, 0, s, 0)),
            pl.BlockSpec((1, 1, N_HEADS * MLA_V, ts), lambda b, s: (b, s, 0, 0)),
        ],
        out_shape=[
            jax.ShapeDtypeStruct((B, N_HEADS, S, LANES), BF16),
            jax.ShapeDtypeStruct((B, N_HEADS, S, LANES), BF16),
            jax.ShapeDtypeStruct((B, nst, N_HEADS * MLA_V, ts), BF16),
        ],
        scratch_shapes=[pltpu.VMEM((1, LANES), F32)],
        compiler_params=pltpu.CompilerParams(
            dimension_semantics=("arbitrary", "arbitrary"), vmem_limit_bytes=VMEM_LIMIT),
    )(x, mod_l, g, win, qg, kvg, wuqa, wukv, fb_row, ctab, stab, tri, eqk)


def _colmax8(st, groups):
    tk, tq = st.shape
    v = st.reshape(groups, tk // (8 * groups), 8, tq)
    return jnp.max(jnp.max(v, axis=1), axis=0)


def _attn_kernel(q_ref, k_ref, vt_ref, o_ref, s_scr, tmax_scr, m_scr, acc_scr, *, tq, hp, nq):
    tk = tq
    half = tq // 2
    dv = MLA_V
    groups = 4
    ones = jnp.ones((ONES_ROWS, tk), BF16)
    nt_dims = (((1,), (1,)), ((), ()))

    def qk(qi, j, h):
        q0 = pl.multiple_of(qi * tq, tq)
        return lax.dot_general(k_ref[0, h, pl.ds(pl.multiple_of(j * tk, tk), tk), :],
                               q_ref[0, h, pl.ds(q0, tq), :], nt_dims,
                               preferred_element_type=F32)

    def park(st, h):
        s_scr[h] = st
        tmax_scr[h] = _colmax8(st, groups)

    def qk_diag(qi, h):
        q0 = pl.multiple_of(qi * tq, tq)
        q1 = pl.multiple_of(qi * tq + half, half)
        top = lax.dot_general(k_ref[0, h, pl.ds(q0, half), :], q_ref[0, h, pl.ds(q0, tq), :],
                              nt_dims, preferred_element_type=F32)
        bot = lax.dot_general(k_ref[0, h, pl.ds(q1, half), :], q_ref[0, h, pl.ds(q1, half), :],
                              nt_dims, preferred_element_type=F32)
        return top, bot

    def park_diag(st, h):
        causal = (lax.broadcasted_iota(jnp.int32, (half, half), 0)
                  <= lax.broadcasted_iota(jnp.int32, (half, half), 1))
        top, bot = st
        top_l = jnp.where(causal, top[:, :half], NEG_BIG)
        bot_r = jnp.where(causal, bot, NEG_BIG)
        s_scr[h, 0:half, 0:half] = top_l
        s_scr[h, 0:half, half:] = top[:, half:]
        s_scr[h, half:, half:] = bot_r
        tmax_scr[h, :, 0:half] = _colmax8(top_l, groups)
        tmax_scr[h, :, half:] = jnp.maximum(_colmax8(top[:, half:], groups),
                                            _colmax8(bot_r, groups))

    def update(j, h):
        m = m_scr[h]
        m_new = jnp.maximum(m, jnp.max(tmax_scr[h], axis=0, keepdims=True))
        alpha = jnp.exp2(m - m_new)
        p = jnp.exp2((s_scr[h] - m_new).astype(BF16))
        vt1 = jnp.concatenate([vt_ref[0, j, h * dv:(h + 1) * dv, :], ones], axis=0)
        acc_scr[h] = alpha * acc_scr[h] + jnp.dot(vt1, p, preferred_element_type=F32)
        m_scr[h] = m_new

    def update_diag(j, h):
        m = m_scr[h]
        m_new = jnp.maximum(m, jnp.max(tmax_scr[h], axis=0, keepdims=True))
        alpha = jnp.exp2(m - m_new)
        p_top = jnp.exp2((s_scr[h, 0:half, :] - m_new).astype(BF16))
        p_bot = jnp.exp2((s_scr[h, half:, half:] - m_new[:, half:]).astype(BF16))
        vt1 = jnp.concatenate([vt_ref[0, j, h * dv:(h + 1) * dv, :], ones], axis=0)
        acc = alpha * acc_scr[h] + jnp.dot(vt1[:, 0:half], p_top, preferred_element_type=F32)
        acc_scr[h, :, 0:half] = acc[:, 0:half]
        acc_scr[h, :, half:] = acc[:, half:] + jnp.dot(vt1[:, half:], p_bot,
                                                       preferred_element_type=F32)
        m_scr[h] = m_new

    def reset_state():
        m_scr[...] = jnp.full(m_scr.shape, NEG_BIG, F32)
        acc_scr[...] = jnp.zeros(acc_scr.shape, F32)

    def finish(qi):
        q0 = pl.multiple_of(qi * tq, tq)
        for h in range(hp):
            acc = acc_scr[h]
            o_ref[0, h * dv:(h + 1) * dv, pl.ds(q0, tq)] = (
                acc[0:dv, :] / acc[dv:dv + 1, :]).astype(o_ref.dtype)
        reset_state()

    reset_state()
    for h in range(hp):
        park_diag(qk_diag(0, h), h)

    def q_tile(qi, c):
        def below_diagonal(t):
            for h in range(hp):
                st = qk(qi, t + 1, h)
                update(t, h)
                park(st, h)

        def two_below(u, c2):
            below_diagonal(2 * u)
            below_diagonal(2 * u + 1)
            return c2

        lax.fori_loop(0, (qi - 1) // 2, two_below, 0)

        @pl.when((qi > 1) & ((qi - 1) % 2 == 1))
        def _():
            below_diagonal(qi - 2)

        @pl.when(qi > 0)
        def _():
            for h in range(hp):
                st = qk_diag(qi, h)
                update(qi - 1, h)
                park_diag(st, h)

        @pl.when(qi + 1 < nq)
        def _():
            sts = []
            for h in range(hp):
                sts.append(qk(qi + 1, 0, h))
                update_diag(qi, h)
            finish(qi)
            for h in range(hp):
                park(sts[h], h)

        @pl.when(qi + 1 == nq)
        def _():
            for h in range(hp):
                update_diag(qi, h)
            finish(qi)

        return c

    lax.fori_loop(0, nq, q_tile, 0)


def _attention(qa, ka, vt, *, tq, tk, hp):
    B, H, S, _ = qa.shape
    nkt = vt.shape[1]
    assert S % tq == 0 and nkt * tk == S and H % hp == 0 and tk == tq
    kern = functools.partial(_attn_kernel, tq=tq, hp=hp, nq=S // tq)
    return pl.pallas_call(
        kern,
        grid=(B, H // hp),
        in_specs=[
            pl.BlockSpec((1, hp, S, LANES), lambda b, g: (b, g, 0, 0)),
            pl.BlockSpec((1, hp, S, LANES), lambda b, g: (b, g, 0, 0)),
            pl.BlockSpec((1, nkt, hp * MLA_V, tk), lambda b, g: (b, 0, g, 0)),
        ],
        out_specs=pl.BlockSpec((1, hp * MLA_V, S), lambda b, g: (b, g, 0)),
        out_shape=jax.ShapeDtypeStruct((B, H * MLA_V, S), BF16),
        scratch_shapes=[
            pltpu.VMEM((hp, tk, tq), F32),
            pltpu.VMEM((hp, 8, tq), F32),
            pltpu.VMEM((hp, 1, tq), F32),
            pltpu.VMEM((hp, MLA_V + ONES_ROWS, tq), F32),
        ],
        compiler_params=pltpu.CompilerParams(
            dimension_semantics=("arbitrary", "arbitrary"), vmem_limit_bytes=VMEM_LIMIT),
    )(qa, ka, vt)


def _top2_gates(logits, lane):
    lg = jnp.where(lane < N_EXPERTS, logits, NEG_BIG)
    m1 = jnp.max(lg, axis=1, keepdims=True)
    i1 = jnp.min(jnp.where(lg == m1, lane, LANES), axis=1, keepdims=True)
    lg2 = jnp.where(lane == i1, NEG_BIG, lg)
    m2 = jnp.max(lg2, axis=1, keepdims=True)
    i2 = jnp.min(jnp.where(lg2 == m2, lane, LANES), axis=1, keepdims=True)
    e2 = jnp.exp(m2 - m1)
    den = 1.0 + e2
    return jnp.where(lane == i1, 1.0 / den, 0.0) + jnp.where(lane == i2, e2 / den, 0.0)


def _outproj_kernel(*refs, ts, with_router):
    if with_router:
        (ot_ref, x_ref, mod_ref, mg_ref, fg_ref, wo_ref, ng_ref, rw_ref,
         x1_ref, h2_ref, gates_ref) = refs
    else:
        ot_ref, x_ref, mod_ref, mg_ref, fg_ref, wo_ref, ng_ref, x1_ref, h2_ref = refs
    half = MLA_HEADS * MLA_V
    om = ot_ref[0, 0:half, :].astype(F32).T
    of = ot_ref[0, half:, :].astype(F32).T
    on = jnp.concatenate([_rms(om) * mg_ref[...], _rms(of) * fg_ref[...]], axis=1).astype(BF16)
    mix = jnp.dot(on, wo_ref[...], preferred_element_type=F32)
    x1 = x_ref[0] + mod_ref[0, 2:3, :] * mix
    x1_ref[0] = x1
    h2 = ((_rms(x1) * ng_ref[...]) * (1.0 + mod_ref[0, 4:5, :]) + mod_ref[0, 3:4, :]).astype(BF16)
    h2_ref[0] = h2
    if with_router:
        logits = jnp.dot(h2, rw_ref[...], preferred_element_type=F32)
        lane = lax.broadcasted_iota(jnp.int32, (ts, LANES), 1)
        gates_ref[0] = _top2_gates(logits, lane)


def _outproj(ot, x, mod_l, mg, fg, wo, ng, rw, *, ts):
    B, S, D = x.shape
    with_router = rw is not None
    const2 = lambda b, s: (0, 0)
    in_specs = [
        pl.BlockSpec((1, ot.shape[1], ts), lambda b, s: (b, 0, s)),
        pl.BlockSpec((1, ts, D), lambda b, s: (b, s, 0)),
        pl.BlockSpec((1, 6, D), lambda b, s: (b, 0, 0)),
        pl.BlockSpec(mg.shape, const2),
        pl.BlockSpec(fg.shape, const2),
        pl.BlockSpec(wo.shape, const2),
        pl.BlockSpec(ng.shape, const2),
    ]
    args = [ot, x, mod_l, mg, fg, wo, ng]
    out_specs = [pl.BlockSpec((1, ts, D), lambda b, s: (b, s, 0)),
                 pl.BlockSpec((1, ts, D), lambda b, s: (b, s, 0))]
    out_shape = [jax.ShapeDtypeStruct((B, S, D), F32), jax.ShapeDtypeStruct((B, S, D), BF16)]
    if with_router:
        in_specs.append(pl.BlockSpec(rw.shape, const2))
        args.append(rw)
        out_specs.append(pl.BlockSpec((1, ts, LANES), lambda b, s: (b, s, 0)))
        out_shape.append(jax.ShapeDtypeStruct((B, S, LANES), F32))
    kern = functools.partial(_outproj_kernel, ts=ts, with_router=with_router)
    return pl.pallas_call(
        kern,
        grid=(B, S // ts),
        in_specs=in_specs,
        out_specs=out_specs,
        out_shape=out_shape,
        compiler_params=pltpu.CompilerParams(
            dimension_semantics=("arbitrary", "arbitrary"), vmem_limit_bytes=VMEM_LIMIT),
    )(*args)


def _swiglu_tile(x, wg, wu):
    g = jnp.dot(x, wg, preferred_element_type=F32)
    u = jnp.dot(x, wu, preferred_element_type=F32)
    return g * _sigmoid(g) * u


def _ffn_kernel(h_ref, wg_ref, wu_ref, wd_ref, x_ref, mod_ref, fin_ref, o_ref, acc_ref,
                *, final_norm):
    f = pl.program_id(1)

    @pl.when(f == 0)
    def _():
        acc_ref[...] = jnp.zeros_like(acc_ref)

    a = _swiglu_tile(h_ref[...], wg_ref[...], wu_ref[...])
    acc_ref[...] += jnp.dot(a.astype(BF16), wd_ref[...], preferred_element_type=F32)

    @pl.when(f == pl.num_programs(1) - 1)
    def _():
        x2 = x_ref[...] + mod_ref[0, 5:6, :] * acc_ref[...]
        if final_norm:
            x2 = _rms(x2) * fin_ref[...]
        o_ref[...] = x2


def _ffn(h2, wg, wu, wd, x1, mod_l, fin_g, *, tm, tf, seq, final_norm):
    T, D = h2.shape
    F = wd.shape[0]
    tiles_per_batch = seq // tm
    kern = functools.partial(_ffn_kernel, final_norm=final_norm)
    return pl.pallas_call(
        kern,
        grid=(T // tm, F // tf),
        in_specs=[
            pl.BlockSpec((tm, D), lambda i, f: (i, 0)),
            pl.BlockSpec((D, tf), lambda i, f: (0, f)),
            pl.BlockSpec((D, tf), lambda i, f: (0, f)),
            pl.BlockSpec((tf, D), lambda i, f: (f, 0)),
            pl.BlockSpec((tm, D), lambda i, f: (i, 0)),
            pl.BlockSpec((1, 6, D), lambda i, f: (i // tiles_per_batch, 0, 0)),
            pl.BlockSpec((1, D), lambda i, f: (0, 0)),
        ],
        out_specs=pl.BlockSpec((tm, D), lambda i, f: (i, 0)),
        out_shape=jax.ShapeDtypeStruct((T, D), F32),
        scratch_shapes=[pltpu.VMEM((tm, D), F32)],
        compiler_params=pltpu.CompilerParams(
            dimension_semantics=("arbitrary", "arbitrary"), vmem_limit_bytes=VMEM_LIMIT),
    )(h2, wg, wu, wd, x1, mod_l, fin_g)


RC = 64
VT = 16
SPB = 8


def _route_kernel(gates_ref, rank_ref, rankt_ref, gatest_ref, cnt_ref, *, tm):
    gates = gates_ref[...]
    sel = gates > 0.0
    self32 = jnp.where(sel, 1.0, 0.0)
    earlier = (lax.broadcasted_iota(jnp.int32, (tm, tm), 1)
               < lax.broadcasted_iota(jnp.int32, (tm, tm), 0))
    rank = jnp.dot(jnp.where(earlier, 1.0, 0.0).astype(BF16), self32.astype(BF16),
                   preferred_element_type=F32)
    rank = jnp.where(sel, rank, -1.0)
    rank_ref[...] = rank
    rankt_ref[0] = rank.T
    gatest_ref[0] = gates.T
    cnt_ref[0] = jnp.sum(self32, axis=0, keepdims=True)


def _route(gates, *, tm):
    T = gates.shape[0]
    nt = T // tm
    return pl.pallas_call(
        functools.partial(_route_kernel, tm=tm),
        grid=(nt,),
        in_specs=[pl.BlockSpec((tm, LANES), lambda i: (i, 0))],
        out_specs=[
            pl.BlockSpec((tm, LANES), lambda i: (i, 0)),
            pl.BlockSpec((1, LANES, tm), lambda i: (i, 0, 0)),
            pl.BlockSpec((1, LANES, tm), lambda i: (i, 0, 0)),
            pl.BlockSpec((1, 1, LANES), lambda i: (i, 0, 0)),
        ],
        out_shape=[
            jax.ShapeDtypeStruct((T, LANES), F32),
            jax.ShapeDtypeStruct((nt, LANES, tm), F32),
            jax.ShapeDtypeStruct((nt, LANES, tm), F32),
            jax.ShapeDtypeStruct((nt, 1, LANES), F32),
        ],
        compiler_params=pltpu.CompilerParams(
            dimension_semantics=("arbitrary",), vmem_limit_bytes=VMEM_LIMIT),
    )(gates)


def _slot_tables(cnt, nt, ns, n_vt):
    E = N_EXPERTS
    nch = (cnt + (RC - 1)) // RC
    nct = jnp.sum(nch, axis=1)
    ccum = jnp.cumsum(nch, axis=1)
    j = jnp.arange(ns, dtype=jnp.int32)
    slot_e = jnp.sum((ccum[:, None, :] <= j[None, :, None]).astype(jnp.int32), axis=2)
    slot_e = jnp.minimum(slot_e, E - 1)
    first = jnp.take_along_axis(ccum - nch, slot_e, axis=1)
    valid = j[None, :] < nct[:, None]
    slot_k = jnp.where(valid, j[None, :] - first, 0)
    nce = jnp.sum(nch, axis=0)
    padded = (nce + (VT - 1)) // VT * VT
    base = jnp.cumsum(padded) - padded
    cid0 = base[None, :] + jnp.cumsum(nch, axis=0) - nch
    cid = jnp.take_along_axis(cid0, slot_e, axis=1) + slot_k
    cid = jnp.where(valid, cid, cid[:, 0:1])
    n_chunks = n_vt * VT
    slot_id = jnp.arange(nt, dtype=jnp.int32)[:, None] * ns + j[None, :]
    pos = jnp.full((n_chunks,), -1, jnp.int32).at[
        jnp.where(valid, cid, n_chunks).reshape(-1)].set(slot_id.reshape(-1), mode="drop")
    vcum = jnp.cumsum(padded // VT)
    nv = vcum[-1]
    v = jnp.minimum(jnp.arange(n_vt, dtype=jnp.int32), nv - 1)
    v_exp = jnp.sum((vcum[None, :] <= v[:, None]).astype(jnp.int32), axis=1)
    v_exp = jnp.minimum(v_exp, E - 1)
    i32 = lambda a: a.reshape(-1).astype(jnp.int32)
    return dict(slot_e=i32(slot_e), slot_k=i32(slot_k), nct=i32(nct), cid=i32(cid),
                pos=jnp.maximum(pos, 0), real=(pos >= 0).astype(jnp.int32),
                v_exp=v_exp, nv=i32(nv))


def _compact_kernel(se_ref, sk_ref, nct_ref, h_ref, rankt_ref, gatest_ref, xs_ref, gc_ref, oh_scr,
                    *, tm, ns):
    t = pl.program_id(0)
    n = nct_ref[t]

    def build(j, c):
        e = se_ref[t * ns + j]
        rrow = rankt_ref[0, pl.ds(e, 1), :].astype(jnp.int32)
        grow = gatest_ref[0, pl.ds(e, 1), :]
        row = lax.broadcasted_iota(jnp.int32, (RC, tm), 0) + sk_ref[t * ns + j] * RC
        hit = (rrow == row) & (j < n)
        oh_scr[j] = jnp.where(hit, 1.0, 0.0).astype(BF16)
        gcol = jnp.sum(jnp.where(hit, grow, 0.0), axis=1, keepdims=True)
        gc_ref[pl.ds(pl.multiple_of(j * RC, RC), RC), :] = jnp.broadcast_to(gcol, (RC, LANES))
        return c

    lax.fori_loop(0, ns, build, 0)
    rows = SPB * RC
    for b in range(ns // SPB):
        oh = oh_scr[b * SPB:(b + 1) * SPB].reshape(rows, tm)
        xs_ref[b * rows:(b + 1) * rows, :] = jnp.dot(
            oh, h_ref[...], preferred_element_type=F32).astype(BF16)


def _compact(tabs, h2, rankt, gatest, *, tm, ns):
    T, D = h2.shape
    nt = T // tm
    grid_spec = pltpu.PrefetchScalarGridSpec(
        num_scalar_prefetch=3,
        grid=(nt,),
        in_specs=[
            pl.BlockSpec((tm, D), lambda t, *_: (t, 0)),
            pl.BlockSpec((1, LANES, tm), lambda t, *_: (t, 0, 0)),
            pl.BlockSpec((1, LANES, tm), lambda t, *_: (t, 0, 0)),
        ],
        out_specs=[
            pl.BlockSpec((ns * RC, D), lambda t, *_: (t, 0)),
            pl.BlockSpec((ns * RC, LANES), lambda t, *_: (t, 0)),
        ],
        scratch_shapes=[pltpu.VMEM((ns, RC, tm), BF16)],
    )
    return pl.pallas_call(
        functools.partial(_compact_kernel, tm=tm, ns=ns),
        grid_spec=grid_spec,
        out_shape=[jax.ShapeDtypeStruct((nt * ns * RC, D), BF16),
                   jax.ShapeDtypeStruct((nt * ns * RC, LANES), F32)],
        compiler_params=pltpu.CompilerParams(
            dimension_semantics=("arbitrary",), vmem_limit_bytes=VMEM_LIMIT),
    )(tabs["slot_e"], tabs["slot_k"], tabs["nct"], h2, rankt, gatest)


def _slot_experts_kernel(vexp_ref, nv_ref, pos_ref, real_ref, *refs):
    xs_refs = refs[0:VT]
    gc_refs = refs[VT:2 * VT]
    wg_ref, wu_ref, wd_ref, ys_ref, x_scr, g_scr, acc_ref = refs[2 * VT:]
    v = pl.program_id(0)
    f = pl.program_id(1)

    @pl.when(v < nv_ref[0])
    def _():
        @pl.when(f == 0)
        def _():
            for j in range(VT):
                keep = real_ref[v * VT + j] == 1
                x_scr[j * RC:(j + 1) * RC, :] = jnp.where(keep, xs_refs[j][...], jnp.zeros_like(xs_refs[j]))
                g_scr[j * RC:(j + 1) * RC, :] = jnp.where(keep, gc_refs[j][...], 0.0)
            acc_ref[...] = jnp.zeros_like(acc_ref)

        a = _swiglu_tile(x_scr[...], wg_ref[0], wu_ref[0]) * g_scr[:, 0:1]
        acc_ref[...] += jnp.dot(a.astype(BF16), wd_ref[0], preferred_element_type=F32)

        @pl.when(f == pl.num_programs(1) - 1)
        def _():
            ys_ref[...] = acc_ref[...].astype(BF16)


def _slot_experts(tabs, xs, gc, wg, wu, wd, *, tf, n_vt):
    D = xs.shape[1]
    NF = wd.shape[1] // tf
    rows = VT * RC

    def live(v, nv):
        return jnp.minimum(v, nv[0] - 1)

    def hidden(v, f, nv):
        return jnp.where(v < nv[0], f, NF - 1)

    def chunk(j):
        return lambda v, f, vexp, nv, pos, real: (pos[live(v, nv) * VT + j], 0)

    grid_spec = pltpu.PrefetchScalarGridSpec(
        num_scalar_prefetch=4,
        grid=(n_vt, NF),
        in_specs=(
            [pl.BlockSpec((RC, D), chunk(j)) for j in range(VT)]
            + [pl.BlockSpec((RC, LANES), chunk(j)) for j in range(VT)]
            + [pl.BlockSpec((1, D, tf),
                            lambda v, f, vexp, nv, pos, real: (vexp[v], 0, hidden(v, f, nv)))] * 2
            + [pl.BlockSpec((1, tf, D),
                            lambda v, f, vexp, nv, pos, real: (vexp[v], hidden(v, f, nv), 0))]),
        out_specs=pl.BlockSpec((rows, D), lambda v, f, vexp, nv, pos, real: (live(v, nv), 0)),
        scratch_shapes=[pltpu.VMEM((rows, D), BF16), pltpu.VMEM((rows, LANES), F32),
                        pltpu.VMEM((rows, D), F32)],
    )
    return pl.pallas_call(
        _slot_experts_kernel,
        grid_spec=grid_spec,
        out_shape=jax.ShapeDtypeStruct((n_vt * rows, D), BF16),
        compiler_params=pltpu.CompilerParams(
            dimension_semantics=("arbitrary", "arbitrary"), vmem_limit_bytes=VMEM_LIMIT),
    )(tabs["v_exp"], tabs["nv"], tabs["pos"], tabs["real"], *([xs] * VT), *([gc] * VT), wg, wu, wd)


def _uncompact_kernel(se_ref, sk_ref, nct_ref, cid_ref, *refs, tm, ns, final_norm):
    ys_refs = refs[0:ns]
    rank_ref, x_ref, mod_ref, fin_ref, o_ref, oh_scr = refs[ns:]
    t = pl.program_id(0)
    n = nct_ref[t]
    lane = lax.broadcasted_iota(jnp.int32, (tm, LANES), 1)
    right = lane >= RC
    col = jnp.where(right, lane - RC, lane)
    rank = rank_ref[...]

    def slot_rank(j):
        return jnp.sum(jnp.where(lane == se_ref[t * ns + j], rank, 0.0), axis=1,
                       keepdims=True).astype(jnp.int32)

    for jp in range(ns // 2):
        ja, jb = 2 * jp, 2 * jp + 1
        rcol = jnp.where(right, slot_rank(jb), slot_rank(ja))
        want = col + jnp.where(right, sk_ref[t * ns + jb], sk_ref[t * ns + ja]) * RC
        used = jnp.where(right, jb, ja) < n
        oh_scr[:, jp * LANES:(jp + 1) * LANES] = jnp.where(
            (rcol == want) & used, 1.0, 0.0).astype(BF16)
    y = None
    rows = SPB * RC
    for b in range(ns // SPB):
        ys = jnp.concatenate([ys_refs[b * SPB + j][...] for j in range(SPB)], axis=0)
        part = jnp.dot(oh_scr[:, b * rows:(b + 1) * rows], ys, preferred_element_type=F32)
        y = part if y is None else y + part
    x2 = x_ref[...] + mod_ref[0, 5:6, :] * y
    if final_norm:
        x2 = _rms(x2) * fin_ref[...]
    o_ref[...] = x2


def _uncompact(tabs, ys, rank, x1, mod_l, fin_g, *, tm, ns, seq, final_norm):
    T, D = x1.shape
    tpb = seq // tm

    def chunk(j):
        return lambda t, se, sk, nct, cid: (cid[t * ns + j], 0)

    grid_spec = pltpu.PrefetchScalarGridSpec(
        num_scalar_prefetch=4,
        grid=(T // tm,),
        in_specs=(
            [pl.BlockSpec((RC, D), chunk(j)) for j in range(ns)]
            + [pl.BlockSpec((tm, LANES), lambda t, *_: (t, 0)),
               pl.BlockSpec((tm, D), lambda t, *_: (t, 0)),
               pl.BlockSpec((1, 6, D), lambda t, *_: (t // tpb, 0, 0)),
               pl.BlockSpec((1, D), lambda t, *_: (0, 0))]),
        out_specs=pl.BlockSpec((tm, D), lambda t, *_: (t, 0)),
        scratch_shapes=[pltpu.VMEM((tm, ns * RC), BF16)],
    )
    return pl.pallas_call(
        functools.partial(_uncompact_kernel, tm=tm, ns=ns, final_norm=final_norm),
        grid_spec=grid_spec,
        out_shape=jax.ShapeDtypeStruct((T, D), F32),
        compiler_params=pltpu.CompilerParams(
            dimension_semantics=("arbitrary",), vmem_limit_bytes=VMEM_LIMIT),
    )(tabs["slot_e"], tabs["slot_k"], tabs["nct"], tabs["cid"], *([ys] * ns),
      rank, x1, mod_l, fin_g)


def _routed_experts(h2, gates, w_gate, w_up, w_down, x1, mod_l, fin_g, *, tm, tf, seq, final_norm):
    T, D = h2.shape
    E = N_EXPERTS
    nt = T // tm
    ns = 2 * tm // RC + E
    assert ns % SPB == 0 and 2 * RC == LANES
    n_vt = -(-(nt * ns + E * (VT - 1)) // VT)
    rank, rankt, gatest, cnt = _route(gates, tm=tm)
    tabs = _slot_tables(cnt[:, 0, :E].astype(jnp.int32), nt, ns, n_vt)
    xs, gc = _compact(tabs, h2, rankt, gatest, tm=tm, ns=ns)
    ys = _slot_experts(tabs, xs, gc, w_gate.astype(BF16), w_up.astype(BF16), w_down.astype(BF16),
                       tf=tf, n_vt=n_vt)
    return _uncompact(tabs, ys, rank, x1, mod_l, fin_g, tm=tm, ns=ns, seq=seq,
                      final_norm=final_norm)


def _tiles(S):
    ts = min(512, S)
    tq = min(512, S)
    tk = ts
    tm = min(1024, S)
    tf = D_FF // 2
    return ts, tq, tk, tm, tf


def kernel(x, c, positions, ada_w, ada_b, attn_norm_g, w_in, q_norm_g, w_uq, kv_norm_g, w_ukv,
           fox_forget_b, mla_out_g, fox_out_g, w_o, ffn_norm_g, dense_w_gate, dense_w_up,
           dense_w_down, router_w, moe_w_gate, moe_w_up, moe_w_down, final_norm_g):
    B, S, D = x.shape
    L = ada_w.shape[0]
    ts, tq, tk, tm, tf = _tiles(S)

    mod = _adaln(c, ada_w, ada_b).reshape(L, B, 6, D)
    ctab, stab = _rope_tables(positions)

    idx_in = _perm_w_in()
    idx_qa = _perm_w_uq()
    idx_kv = _perm_w_ukv()
    col_scale = np.ones((C_END,), np.float32)
    col_scale[C_FQ0:C_FQ0 + FOX_W] = FOX_HEAD_DIM ** -0.5
    tri = jnp.asarray(np.tril(np.ones((ts, ts), np.float32)), BF16)
    eqk = jnp.asarray(_aug_placement(), BF16)

    for l in range(L):
        win = (_take_cols(w_in[l], idx_in) * col_scale).astype(BF16)
        wuqa = _take_cols(w_uq[l], idx_qa).astype(BF16)
        wukv = _take_cols(w_ukv[l], idx_kv).astype(BF16)
        fb_row = jnp.zeros((1, LANES), F32).at[0, CTRL_LOGIT:CTRL_LOGIT + FOX_HEADS].set(
            fox_forget_b[l].astype(F32))
        qa, ka, vt = _inproj(
            x, mod[l], attn_norm_g[l].reshape(1, D), win, q_norm_g[l].reshape(1, -1),
            kv_norm_g[l].reshape(1, -1), wuqa, wukv, fb_row, ctab, stab, tri, eqk, ts=ts)
        ot = _attention(qa, ka, vt, tq=tq, tk=tk, hp=4)

        j = l // 2
        is_moe = (l % 2 == 1)
        rw = None
        if is_moe:
            rw = jnp.pad(router_w[j], ((0, 0), (0, LANES - N_EXPERTS))).astype(BF16)
        outs = _outproj(ot, x, mod[l], mla_out_g[l].reshape(1, -1), fox_out_g[l].reshape(1, -1),
                        w_o[l].astype(BF16), ffn_norm_g[l].reshape(1, D), rw, ts=ts)
        x1, h2 = outs[0].reshape(B * S, D), outs[1].reshape(B * S, D)
        fin_g = final_norm_g.reshape(1, D)
        last = l == L - 1
        if is_moe:
            x = _routed_experts(h2, outs[2].reshape(B * S, LANES), moe_w_gate[j], moe_w_up[j],
                                moe_w_down[j], x1, mod[l], fin_g,
                                tm=tm, tf=tf, seq=S, final_norm=last)
        else:
            x = _ffn(h2, dense_w_gate[j].astype(BF16), dense_w_up[j].astype(BF16),
                     dense_w_down[j].astype(BF16), x1, mod[l], fin_g,
                     tm=tm, tf=tf, seq=S, final_norm=last)
        x = x.reshape(B, S, D)
    return x
```

```python
import functools

import numpy as np
import jax
import jax.numpy as jnp
from jax import lax
from jax.experimental import pallas as pl
from jax.experimental.pallas import tpu as pltpu

F32 = jnp.float32
BF16 = jnp.bfloat16

D_MODEL = 1024
MLA_HEADS = 8
MLA_V = 64
MLA_NOPE = 64
MLA_ROPE = 32
MLA_Q_LORA = 256
MLA_KV_LORA = 128
FOX_HEADS = 8
FOX_HEAD_DIM = 64
FOX_W = FOX_HEADS * FOX_HEAD_DIM
ROPE_THETA = 10000.0
D_FF = 3584
N_EXPERTS = 8
EPS = 1e-6
N_HEADS = MLA_HEADS + FOX_HEADS
HALF_ROPE = MLA_ROPE // 2

LANES = 128
VMEM_LIMIT = 56 * 1024 * 1024

ROPE_LO = MLA_NOPE
AUG_LO = FOX_HEAD_DIM
AUG_W = 6
CTRL_LOGIT = 96
CTRL_ONE = 120

C_Q0 = 0
C_KV0 = C_Q0 + MLA_Q_LORA
C_KRA0 = C_KV0 + MLA_KV_LORA
C_FQ0 = C_KRA0 + LANES
C_FK0 = C_FQ0 + FOX_W
C_FV0 = C_FK0 + FOX_W
C_END = C_FV0 + FOX_W

NEG_BIG = -1e30
LOG2E = 1.4426950408889634
ONES_ROWS = 16


def _rms(v):
    return v * lax.rsqrt(jnp.mean(v * v, axis=-1, keepdims=True) + EPS)


def _sigmoid(v):
    return 1.0 / (1.0 + jnp.exp(-v))


def _split3(v):
    hi = v.astype(BF16).astype(F32)
    r = v - hi
    mid = r.astype(BF16).astype(F32)
    lo = (r - mid).astype(BF16).astype(F32)
    return hi, mid, lo


def _adaln_kernel(c_ref, w_ref, b_ref, o_ref):
    c = c_ref[...]
    ca = (c * _sigmoid(c)).astype(BF16)
    o_ref[0] = jnp.dot(ca, w_ref[0].astype(BF16), preferred_element_type=F32) + b_ref[0]


def _adaln(c, ada_w, ada_b):
    L, D, N = ada_w.shape
    B = c.shape[0]
    tn = 1536
    return pl.pallas_call(
        _adaln_kernel,
        grid=(L, N // tn),
        in_specs=[
            pl.BlockSpec((B, D), lambda l, j: (0, 0)),
            pl.BlockSpec((1, D, tn), lambda l, j: (l, 0, j)),
            pl.BlockSpec((1, 1, tn), lambda l, j: (l, 0, j)),
        ],
        out_specs=pl.BlockSpec((1, B, tn), lambda l, j: (l, 0, j)),
        out_shape=jax.ShapeDtypeStruct((L, B, N), F32),
        compiler_params=pltpu.CompilerParams(
            dimension_semantics=("arbitrary", "arbitrary"), vmem_limit_bytes=VMEM_LIMIT),
    )(c, ada_w, ada_b.reshape(L, 1, N))


def _rope_tab_kernel(pos_ref, freq_ref, cos_ref, sin_ref):
    ang = freq_ref[...] * pos_ref[0]
    cos_ref[0] = jnp.cos(ang)
    sin_ref[0] = jnp.sin(ang)


def _rope_tables(positions):
    B, S = positions.shape
    half = HALF_ROPE
    inv_freq = ROPE_THETA ** (-jnp.arange(half, dtype=F32) / half)
    pos = positions.astype(F32).reshape(B, 1, S)
    cos_t, sin_t = pl.pallas_call(
        _rope_tab_kernel,
        grid=(B,),
        in_specs=[
            pl.BlockSpec((1, 1, S), lambda b: (b, 0, 0)),
            pl.BlockSpec((half, 1), lambda b: (0, 0)),
        ],
        out_specs=[pl.BlockSpec((1, half, S), lambda b: (b, 0, 0))] * 2,
        out_shape=[jax.ShapeDtypeStruct((B, half, S), F32)] * 2,
        compiler_params=pltpu.CompilerParams(dimension_semantics=("arbitrary",)),
    )(pos, inv_freq.reshape(half, 1))
    cos = jnp.transpose(cos_t, (0, 2, 1))
    sin = jnp.transpose(sin_t, (0, 2, 1))
    ones = jnp.ones((B, S, MLA_NOPE), F32)
    z_lo = jnp.zeros((B, S, MLA_NOPE), F32)
    z_hi = jnp.zeros((B, S, LANES - MLA_NOPE - MLA_ROPE), F32)
    ctab = jnp.concatenate([ones, cos, cos, z_hi], axis=-1)
    stab = jnp.concatenate([z_lo, -sin, sin, z_hi], axis=-1)
    return ctab, stab


def _take_cols(w, idx):
    pieces, i, n = [], 0, len(idx)
    while i < n:
        j = i + 1
        if idx[i] < 0:
            while j < n and idx[j] < 0:
                j += 1
            pieces.append(jnp.zeros((w.shape[0], j - i), w.dtype))
        else:
            while j < n and idx[j] == idx[j - 1] + 1:
                j += 1
            pieces.append(w[:, int(idx[i]):int(idx[i]) + (j - i)])
        i = j
    return jnp.concatenate(pieces, axis=1)


def _perm_w_in():
    cq = 0
    ckv = cq + MLA_Q_LORA
    kr = ckv + MLA_KV_LORA
    fq = kr + MLA_ROPE
    fk = fq + FOX_W
    fv = fk + FOX_W
    fl = fv + FOX_W
    idx = -np.ones((C_END,), np.int64)
    idx[C_Q0:C_Q0 + MLA_Q_LORA] = cq + np.arange(MLA_Q_LORA)
    idx[C_KV0:C_KV0 + MLA_KV_LORA] = ckv + np.arange(MLA_KV_LORA)
    idx[C_KRA0 + ROPE_LO:C_KRA0 + ROPE_LO + MLA_ROPE] = kr + np.arange(MLA_ROPE)
    idx[C_KRA0 + CTRL_LOGIT:C_KRA0 + CTRL_LOGIT + FOX_HEADS] = fl + np.arange(FOX_HEADS)
    idx[C_FQ0:C_FQ0 + FOX_W] = fq + np.arange(FOX_W)
    idx[C_FK0:C_FK0 + FOX_W] = fk + np.arange(FOX_W)
    idx[C_FV0:C_FV0 + FOX_W] = fv + np.arange(FOX_W)
    return idx


def _perm_w_uq():
    per = MLA_NOPE + MLA_ROPE
    ia = -np.ones((MLA_HEADS * LANES,), np.int64)
    for h in range(MLA_HEADS):
        ia[h * LANES:h * LANES + per] = h * per + np.arange(per)
    return ia


def _perm_w_ukv():
    per = MLA_NOPE + MLA_V
    ik = -np.ones((MLA_HEADS * LANES,), np.int64)
    iv = np.zeros((MLA_HEADS * MLA_V,), np.int64)
    for h in range(MLA_HEADS):
        ik[h * LANES:h * LANES + MLA_NOPE] = h * per + np.arange(MLA_NOPE)
        iv[h * MLA_V:(h + 1) * MLA_V] = h * per + MLA_NOPE + np.arange(MLA_V)
    return np.concatenate([ik, iv])


def _aug_placement():
    e = np.zeros((LANES, 2 * LANES), np.float32)
    for h in range(FOX_HEADS):
        lo = AUG_LO + AUG_W * h
        for p in range(3):
            src = CTRL_LOGIT + 8 * p + h
            e[src, lo + p] = 1.0
            e[CTRL_ONE, lo + 3 + p] = -1.0
            e[CTRL_ONE, LANES + lo + p] = 1.0
            e[src, LANES + lo + 3 + p] = 1.0
    return e


def _inproj_kernel(x_ref, mod_ref, g_ref, win_ref, qg_ref, kvg_ref, wuqa_ref, wukv_ref,
                   fb_ref, ct_ref, st_ref, tri_ref, eqk_ref,
                   qa_ref, ka_ref, vt_ref, carry_ref, *, ts):
    si = pl.program_id(1)

    @pl.when(si == 0)
    def _():
        carry_ref[...] = jnp.zeros_like(carry_ref)

    x = x_ref[0]
    shift = mod_ref[0, 0:1, :]
    scale = mod_ref[0, 1:2, :]
    h = ((_rms(x) * g_ref[...]) * (1.0 + scale) + shift).astype(BF16)
    proj = jnp.dot(h, win_ref[...], preferred_element_type=F32)

    ctab = ct_ref[0]
    stab = st_ref[0]
    lane = lax.broadcasted_iota(jnp.int32, (ts, LANES), 1)

    c_q = proj[:, C_Q0:C_Q0 + MLA_Q_LORA]
    cqn = (_rms(c_q) * qg_ref[...]).astype(BF16)
    qa = jnp.dot(cqn, wuqa_ref[...], preferred_element_type=F32)
    mla_scale = (MLA_NOPE + MLA_ROPE) ** -0.5 * LOG2E
    first_half = lane < ROPE_LO + HALF_ROPE

    def rope(blk):
        swapped = jnp.where(first_half, pltpu.roll(blk, LANES - HALF_ROPE, 1),
                            pltpu.roll(blk, HALF_ROPE, 1))
        return blk * ctab + swapped * stab

    for hh in range(MLA_HEADS):
        qa_ref[0, hh] = (rope(qa[:, hh * LANES:(hh + 1) * LANES]) * mla_scale).astype(BF16)

    c_kv = proj[:, C_KV0:C_KV0 + MLA_KV_LORA]
    ckvn = (_rms(c_kv) * kvg_ref[...]).astype(BF16)
    kv = jnp.dot(ckvn, wukv_ref[...], preferred_element_type=F32)
    kra = proj[:, C_KRA0:C_KRA0 + LANES]
    krope = rope(kra)
    for hh in range(MLA_HEADS):
        ka_ref[0, hh] = (kv[:, hh * LANES:(hh + 1) * LANES] + krope).astype(BF16)
    v_mla = kv[:, MLA_HEADS * LANES:]

    ctrl = (lane >= CTRL_LOGIT) & (lane < CTRL_LOGIT + FOX_HEADS)
    fl = kra + fb_ref[...]
    lsig = jnp.minimum(fl, 0.0) - jnp.log1p(jnp.exp(-jnp.abs(fl)))
    lf = jnp.where(ctrl, lsig, 0.0)
    hi, mid, lo = _split3(lf)
    p1 = (hi + pltpu.roll(mid, 8, 1) + pltpu.roll(lo, 16, 1)).astype(BF16)
    cs = jnp.dot(tri_ref[...], p1, preferred_element_type=F32)
    cs = cs + pltpu.roll(cs, LANES - 8, 1) + pltpu.roll(cs, LANES - 16, 1)
    fcum = jnp.where(ctrl, cs, 0.0) + carry_ref[...]
    carry_ref[...] = fcum[ts - 1:ts, :]

    hi, mid, lo = _split3(fcum * LOG2E)
    p2 = hi + pltpu.roll(mid, 8, 1) + pltpu.roll(lo, 16, 1)
    p2 = jnp.where(lane == CTRL_ONE, 1.0, p2).astype(BF16)
    aug = jnp.dot(p2, eqk_ref[...], preferred_element_type=F32)
    aug_q = aug[:, :LANES]
    aug_k = aug[:, LANES:]

    fq = proj[:, C_FQ0:C_FQ0 + FOX_W]
    fk = proj[:, C_FK0:C_FK0 + FOX_W]
    low = lane < FOX_HEAD_DIM
    for j in range(FOX_HEADS // 2):
        sl = slice(j * LANES, (j + 1) * LANES)
        qblk = fq[:, sl] * LOG2E
        kblk = fk[:, sl]
        for h0, qb_, kb_ in ((2 * j, qblk, kblk),
                             (2 * j + 1, pltpu.roll(qblk, FOX_HEAD_DIM, 1),
                              pltpu.roll(kblk, FOX_HEAD_DIM, 1))):
            own = (lane >= AUG_LO + AUG_W * h0) & (lane < AUG_LO + AUG_W * (h0 + 1))
            qa_ref[0, MLA_HEADS + h0] = jnp.where(low, qb_, aug_q).astype(BF16)
            ka_ref[0, MLA_HEADS + h0] = jnp.where(low, kb_, jnp.where(own, aug_k, 0.0)).astype(BF16)

    fv = proj[:, C_FV0:C_FV0 + FOX_W]
    vt_ref[0, 0, 0:MLA_HEADS * MLA_V, :] = v_mla.T.astype(BF16)
    vt_ref[0, 0, MLA_HEADS * MLA_V:, :] = fv.T.astype(BF16)


def _inproj(x, mod_l, g, win, qg, kvg, wuqa, wukv, fb_row, ctab, stab, tri, eqk, *, ts):
    B, S, D = x.shape
    nst = S // ts
    const2 = lambda b, s: (0, 0)
    kern = functools.partial(_inproj_kernel, ts=ts)
    return pl.pallas_call(
        kern,
        grid=(B, nst),
        in_specs=[
            pl.BlockSpec((1, ts, D), lambda b, s: (b, s, 0)),
            pl.BlockSpec((1, 6, D), lambda b, s: (b, 0, 0)),
            pl.BlockSpec((1, D), const2),
            pl.BlockSpec(win.shape, const2),
            pl.BlockSpec(qg.shape, const2),
            pl.BlockSpec(kvg.shape, const2),
            pl.BlockSpec(wuqa.shape, const2),
            pl.BlockSpec(wukv.shape, const2),
            pl.BlockSpec(fb_row.shape, const2),
            pl.BlockSpec((1, ts, LANES), lambda b, s: (b, s, 0)),
            pl.BlockSpec((1, ts, LANES), lambda b, s: (b, s, 0)),
            pl.BlockSpec(tri.shape, const2),
            pl.BlockSpec(eqk.shape, const2),
        ],
        out_specs=[
            pl.BlockSpec((1, N_HEADS, ts, LANES), lambda b, s: (b, 0, s, 0)),
            pl.BlockSpec((1, N_HEADS, ts, LANES), lambda b, s: (b, 0, s, 0)),
            pl.BlockSpec((1, 1, N_HEADS * MLA_V, ts), lambda b, s: (b, s, 0, 0)),
        ],
        out_shape=[
            jax.ShapeDtypeStruct((B, N_HEADS, S, LANES), BF16),
            jax.ShapeDtypeStruct((B, N_HEADS, S, LANES), BF16),
            jax.ShapeDtypeStruct((B, nst, N_HEADS * MLA_V, ts), BF16),
        ],
        scratch_shapes=[pltpu.VMEM((1, LANES), F32)],
        compiler_params=pltpu.CompilerParams(
            dimension_semantics=("arbitrary", "arbitrary"), vmem_limit_bytes=VMEM_LIMIT),
    )(x, mod_l, g, win, qg, kvg, wuqa, wukv, fb_row, ctab, stab, tri, eqk)


def _colmax8(st, groups):
    tk, tq = st.shape
    v = st.reshape(groups, tk // (8 * groups), 8, tq)
    return jnp.max(jnp.max(v, axis=1), axis=0)


def _attn_kernel(q_ref, k_ref, vt_ref, o_ref, s_scr, tmax_scr, m_scr, acc_scr, *, tq, hp, nq):
    tk = tq
    half = tq // 2
    dv = MLA_V
    groups = 4
    ones = jnp.ones((ONES_ROWS, tk), BF16)
    nt_dims = (((1,), (1,)), ((), ()))

    def qk(qi, j):
        q0 = pl.multiple_of(qi * tq, tq)
        return [lax.dot_general(k_ref[0, h, pl.ds(pl.multiple_of(j * tk, tk), tk), :],
                                q_ref[0, h, pl.ds(q0, tq), :], nt_dims,
                                preferred_element_type=F32) for h in range(hp)]

    def park(sts):
        for h in range(hp):
            s_scr[h] = sts[h]
            tmax_scr[h] = _colmax8(sts[h], groups)

    def qk_diag(qi):
        q0 = pl.multiple_of(qi * tq, tq)
        q1 = pl.multiple_of(qi * tq + half, half)
        out = []
        for h in range(hp):
            top = lax.dot_general(k_ref[0, h, pl.ds(q0, half), :], q_ref[0, h, pl.ds(q0, tq), :],
                                  nt_dims, preferred_element_type=F32)
            bot = lax.dot_general(k_ref[0, h, pl.ds(q1, half), :], q_ref[0, h, pl.ds(q1, half), :],
                                  nt_dims, preferred_element_type=F32)
            out.append((top, bot))
        return out

    def park_diag(sts):
        causal = (lax.broadcasted_iota(jnp.int32, (half, half), 0)
                  <= lax.broadcasted_iota(jnp.int32, (half, half), 1))
        for h in range(hp):
            top, bot = sts[h]
            top_l = jnp.where(causal, top[:, :half], NEG_BIG)
            bot_r = jnp.where(causal, bot, NEG_BIG)
            s_scr[h, 0:half, 0:half] = top_l
            s_scr[h, 0:half, half:] = top[:, half:]
            s_scr[h, half:, half:] = bot_r
            tmax_scr[h, :, 0:half] = _colmax8(top_l, groups)
            tmax_scr[h, :, half:] = jnp.maximum(_colmax8(top[:, half:], groups),
                                                _colmax8(bot_r, groups))

    def update(j):
        for h in range(hp):
            m = m_scr[h]
            m_new = jnp.maximum(m, jnp.max(tmax_scr[h], axis=0, keepdims=True))
            alpha = jnp.exp2(m - m_new)
            p = jnp.exp2((s_scr[h] - m_new).astype(BF16))
            vt1 = jnp.concatenate([vt_ref[0, j, h * dv:(h + 1) * dv, :], ones], axis=0)
            acc_scr[h] = alpha * acc_scr[h] + jnp.dot(vt1, p, preferred_element_type=F32)
            m_scr[h] = m_new

    def update_diag(j):
        for h in range(hp):
            m = m_scr[h]
            m_new = jnp.maximum(m, jnp.max(tmax_scr[h], axis=0, keepdims=True))
            alpha = jnp.exp2(m - m_new)
            p_top = jnp.exp2((s_scr[h, 0:half, :] - m_new).astype(BF16))
            p_bot = jnp.exp2((s_scr[h, half:, half:] - m_new[:, half:]).astype(BF16))
            vt1 = jnp.concatenate([vt_ref[0, j, h * dv:(h + 1) * dv, :], ones], axis=0)
            acc = alpha * acc_scr[h] + jnp.dot(vt1[:, 0:half], p_top, preferred_element_type=F32)
            acc_scr[h, :, 0:half] = acc[:, 0:half]
            acc_scr[h, :, half:] = acc[:, half:] + jnp.dot(vt1[:, half:], p_bot,
                                                           preferred_element_type=F32)
            m_scr[h] = m_new

    def reset_state():
        m_scr[...] = jnp.full(m_scr.shape, NEG_BIG, F32)
        acc_scr[...] = jnp.zeros(acc_scr.shape, F32)

    def finish(qi):
        q0 = pl.multiple_of(qi * tq, tq)
        for h in range(hp):
            acc = acc_scr[h]
            o_ref[0, h * dv:(h + 1) * dv, pl.ds(q0, tq)] = (
                acc[0:dv, :] / acc[dv:dv + 1, :]).astype(o_ref.dtype)
        reset_state()

    reset_state()
    park_diag(qk_diag(0))

    def q_tile(qi, c):
        def below_diagonal(t):
            sts = qk(qi, t + 1)
            update(t)
            park(sts)

        def two_below(u, c2):
            below_diagonal(2 * u)
            below_diagonal(2 * u + 1)
            return c2

        lax.fori_loop(0, (qi - 1) // 2, two_below, 0)

        @pl.when((qi > 1) & ((qi - 1) % 2 == 1))
        def _():
            below_diagonal(qi - 2)

        @pl.when(qi > 0)
        def _():
            sts = qk_diag(qi)
            update(qi - 1)
            park_diag(sts)

        @pl.when(qi + 1 < nq)
        def _():
            sts = qk(qi + 1, 0)
            update_diag(qi)
            finish(qi)
            park(sts)

        @pl.when(qi + 1 == nq)
        def _():
            update_diag(qi)
            finish(qi)

        return c

    lax.fori_loop(0, nq, q_tile, 0)


def _attention(qa, ka, vt, *, tq, tk, hp):
    B, H, S, _ = qa.shape
    nkt = vt.shape[1]
    assert S % tq == 0 and nkt * tk == S and H % hp == 0 and tk == tq
    kern = functools.partial(_attn_kernel, tq=tq, hp=hp, nq=S // tq)
    return pl.pallas_call(
        kern,
        grid=(B, H // hp),
        in_specs=[
            pl.BlockSpec((1, hp, S, LANES), lambda b, g: (b, g, 0, 0)),
            pl.BlockSpec((1, hp, S, LANES), lambda b, g: (b, g, 0, 0)),
            pl.BlockSpec((1, nkt, hp * MLA_V, tk), lambda b, g: (b, 0, g, 0)),
        ],
        out_specs=pl.BlockSpec((1, hp * MLA_V, S), lambda b, g: (b, g, 0)),
        out_shape=jax.ShapeDtypeStruct((B, H * MLA_V, S), BF16),
        scratch_shapes=[
            pltpu.VMEM((hp, tk, tq), F32),
            pltpu.VMEM((hp, 8, tq), F32),
            pltpu.VMEM((hp, 1, tq), F32),
            pltpu.VMEM((hp, MLA_V + ONES_ROWS, tq), F32),
        ],
        compiler_params=pltpu.CompilerParams(
            dimension_semantics=("arbitrary", "arbitrary"), vmem_limit_bytes=VMEM_LIMIT),
    )(qa, ka, vt)


def _top2_gates(logits, lane):
    lg = jnp.where(lane < N_EXPERTS, logits, NEG_BIG)
    m1 = jnp.max(lg, axis=1, keepdims=True)
    i1 = jnp.min(jnp.where(lg == m1, lane, LANES), axis=1, keepdims=True)
    lg2 = jnp.where(lane == i1, NEG_BIG, lg)
    m2 = jnp.max(lg2, axis=1, keepdims=True)
    i2 = jnp.min(jnp.where(lg2 == m2, lane, LANES), axis=1, keepdims=True)
    e2 = jnp.exp(m2 - m1)
    den = 1.0 + e2
    return jnp.where(lane == i1, 1.0 / den, 0.0) + jnp.where(lane == i2, e2 / den, 0.0)


def _outproj_kernel(*refs, ts, with_router):
    if with_router:
        (ot_ref, x_ref, mod_ref, mg_ref, fg_ref, wo_ref, ng_ref, rw_ref,
         x1_ref, h2_ref, gates_ref) = refs
    else:
        ot_ref, x_ref, mod_ref, mg_ref, fg_ref, wo_ref, ng_ref, x1_ref, h2_ref = refs
    half = MLA_HEADS * MLA_V
    om = ot_ref[0, 0:half, :].astype(F32).T
    of = ot_ref[0, half:, :].astype(F32).T
    on = jnp.concatenate([_rms(om) * mg_ref[...], _rms(of) * fg_ref[...]], axis=1).astype(BF16)
    mix = jnp.dot(on, wo_ref[...], preferred_element_type=F32)
    x1 = x_ref[0] + mod_ref[0, 2:3, :] * mix
    x1_ref[0] = x1
    h2 = ((_rms(x1) * ng_ref[...]) * (1.0 + mod_ref[0, 4:5, :]) + mod_ref[0, 3:4, :]).astype(BF16)
    h2_ref[0] = h2
    if with_router:
        logits = jnp.dot(h2, rw_ref[...], preferred_element_type=F32)
        lane = lax.broadcasted_iota(jnp.int32, (ts, LANES), 1)
        gates_ref[0] = _top2_gates(logits, lane)


def _outproj(ot, x, mod_l, mg, fg, wo, ng, rw, *, ts):
    B, S, D = x.shape
    with_router = rw is not None
    const2 = lambda b, s: (0, 0)
    in_specs = [
        pl.BlockSpec((1, ot.shape[1], ts), lambda b, s: (b, 0, s)),
        pl.BlockSpec((1, ts, D), lambda b, s: (b, s, 0)),
        pl.BlockSpec((1, 6, D), lambda b, s: (b, 0, 0)),
        pl.BlockSpec(mg.shape, const2),
        pl.BlockSpec(fg.shape, const2),
        pl.BlockSpec(wo.shape, const2),
        pl.BlockSpec(ng.shape, const2),
    ]
    args = [ot, x, mod_l, mg, fg, wo, ng]
    out_specs = [pl.BlockSpec((1, ts, D), lambda b, s: (b, s, 0)),
                 pl.BlockSpec((1, ts, D), lambda b, s: (b, s, 0))]
    out_shape = [jax.ShapeDtypeStruct((B, S, D), F32), jax.ShapeDtypeStruct((B, S, D), BF16)]
    if with_router:
        in_specs.append(pl.BlockSpec(rw.shape, const2))
        args.append(rw)
        out_specs.append(pl.BlockSpec((1, ts, LANES), lambda b, s: (b, s, 0)))
        out_shape.append(jax.ShapeDtypeStruct((B, S, LANES), F32))
    kern = functools.partial(_outproj_kernel, ts=ts, with_router=with_router)
    return pl.pallas_call(
        kern,
        grid=(B, S // ts),
        in_specs=in_specs,
        out_specs=out_specs,
        out_shape=out_shape,
        compiler_params=pltpu.CompilerParams(
            dimension_semantics=("arbitrary", "arbitrary"), vmem_limit_bytes=VMEM_LIMIT),
    )(*args)


def _swiglu_tile(x, wg, wu):
    g = jnp.dot(x, wg, preferred_element_type=F32)
    u = jnp.dot(x, wu, preferred_element_type=F32)
    return g * _sigmoid(g) * u


def _ffn_kernel(h_ref, wg_ref, wu_ref, wd_ref, x_ref, mod_ref, fin_ref, o_ref, acc_ref,
                *, final_norm):
    f = pl.program_id(1)

    @pl.when(f == 0)
    def _():
        acc_ref[...] = jnp.zeros_like(acc_ref)

    a = _swiglu_tile(h_ref[...], wg_ref[...], wu_ref[...])
    acc_ref[...] += jnp.dot(a.astype(BF16), wd_ref[...], preferred_element_type=F32)

    @pl.when(f == pl.num_programs(1) - 1)
    def _():
        x2 = x_ref[...] + mod_ref[0, 5:6, :] * acc_ref[...]
        if final_norm:
            x2 = _rms(x2) * fin_ref[...]
        o_ref[...] = x2


def _ffn(h2, wg, wu, wd, x1, mod_l, fin_g, *, tm, tf, seq, final_norm):
    T, D = h2.shape
    F = wd.shape[0]
    tiles_per_batch = seq // tm
    kern = functools.partial(_ffn_kernel, final_norm=final_norm)
    return pl.pallas_call(
        kern,
        grid=(T // tm, F // tf),
        in_specs=[
            pl.BlockSpec((tm, D), lambda i, f: (i, 0)),
            pl.BlockSpec((D, tf), lambda i, f: (0, f)),
            pl.BlockSpec((D, tf), lambda i, f: (0, f)),
            pl.BlockSpec((tf, D), lambda i, f: (f, 0)),
            pl.BlockSpec((tm, D), lambda i, f: (i, 0)),
            pl.BlockSpec((1, 6, D), lambda i, f: (i // tiles_per_batch, 0, 0)),
            pl.BlockSpec((1, D), lambda i, f: (0, 0)),
        ],
        out_specs=pl.BlockSpec((tm, D), lambda i, f: (i, 0)),
        out_shape=jax.ShapeDtypeStruct((T, D), F32),
        scratch_shapes=[pltpu.VMEM((tm, D), F32)],
        compiler_params=pltpu.CompilerParams(
            dimension_semantics=("arbitrary", "arbitrary"), vmem_limit_bytes=VMEM_LIMIT),
    )(h2, wg, wu, wd, x1, mod_l, fin_g)


RC = 64
VT = 16
SPB = 8


def _route_kernel(gates_ref, rank_ref, rankt_ref, gatest_ref, cnt_ref, *, tm):
    gates = gates_ref[...]
    sel = gates > 0.0
    self32 = jnp.where(sel, 1.0, 0.0)
    earlier = (lax.broadcasted_iota(jnp.int32, (tm, tm), 1)
               < lax.broadcasted_iota(jnp.int32, (tm, tm), 0))
    rank = jnp.dot(jnp.where(earlier, 1.0, 0.0).astype(BF16), self32.astype(BF16),
                   preferred_element_type=F32)
    rank = jnp.where(sel, rank, -1.0)
    rank_ref[...] = rank
    rankt_ref[0] = rank.T
    gatest_ref[0] = gates.T
    cnt_ref[0] = jnp.sum(self32, axis=0, keepdims=True)


def _route(gates, *, tm):
    T = gates.shape[0]
    nt = T // tm
    return pl.pallas_call(
        functools.partial(_route_kernel, tm=tm),
        grid=(nt,),
        in_specs=[pl.BlockSpec((tm, LANES), lambda i: (i, 0))],
        out_specs=[
            pl.BlockSpec((tm, LANES), lambda i: (i, 0)),
            pl.BlockSpec((1, LANES, tm), lambda i: (i, 0, 0)),
            pl.BlockSpec((1, LANES, tm), lambda i: (i, 0, 0)),
            pl.BlockSpec((1, 1, LANES), lambda i: (i, 0, 0)),
        ],
        out_shape=[
            jax.ShapeDtypeStruct((T, LANES), F32),
            jax.ShapeDtypeStruct((nt, LANES, tm), F32),
            jax.ShapeDtypeStruct((nt, LANES, tm), F32),
            jax.ShapeDtypeStruct((nt, 1, LANES), F32),
        ],
        compiler_params=pltpu.CompilerParams(
            dimension_semantics=("arbitrary",), vmem_limit_bytes=VMEM_LIMIT),
    )(gates)


def _slot_tables(cnt, nt, ns, n_vt):
    E = N_EXPERTS
    nch = (cnt + (RC - 1)) // RC
    nct = jnp.sum(nch, axis=1)
    ccum = jnp.cumsum(nch, axis=1)
    j = jnp.arange(ns, dtype=jnp.int32)
    slot_e = jnp.sum((ccum[:, None, :] <= j[None, :, None]).astype(jnp.int32), axis=2)
    slot_e = jnp.minimum(slot_e, E - 1)
    first = jnp.take_along_axis(ccum - nch, slot_e, axis=1)
    valid = j[None, :] < nct[:, None]
    slot_k = jnp.where(valid, j[None, :] - first, 0)
    nce = jnp.sum(nch, axis=0)
    padded = (nce + (VT - 1)) // VT * VT
    base = jnp.cumsum(padded) - padded
    cid0 = base[None, :] + jnp.cumsum(nch, axis=0) - nch
    cid = jnp.take_along_axis(cid0, slot_e, axis=1) + slot_k
    cid = jnp.where(valid, cid, cid[:, 0:1])
    n_chunks = n_vt * VT
    slot_id = jnp.arange(nt, dtype=jnp.int32)[:, None] * ns + j[None, :]
    pos = jnp.full((n_chunks,), -1, jnp.int32).at[
        jnp.where(valid, cid, n_chunks).reshape(-1)].set(slot_id.reshape(-1), mode="drop")
    vcum = jnp.cumsum(padded // VT)
    nv = vcum[-1]
    v = jnp.minimum(jnp.arange(n_vt, dtype=jnp.int32), nv - 1)
    v_exp = jnp.sum((vcum[None, :] <= v[:, None]).astype(jnp.int32), axis=1)
    v_exp = jnp.minimum(v_exp, E - 1)
    i32 = lambda a: a.reshape(-1).astype(jnp.int32)
    return dict(slot_e=i32(slot_e), slot_k=i32(slot_k), nct=i32(nct), cid=i32(cid),
                pos=jnp.maximum(pos, 0), real=(pos >= 0).astype(jnp.int32),
                v_exp=v_exp, nv=i32(nv))


def _compact_kernel(se_ref, sk_ref, nct_ref, h_ref, rankt_ref, gatest_ref, xs_ref, gc_ref, oh_scr,
                    *, tm, ns):
    t = pl.program_id(0)
    n = nct_ref[t]

    def build(j, c):
        e = se_ref[t * ns + j]
        rrow = rankt_ref[0, pl.ds(e, 1), :].astype(jnp.int32)
        grow = gatest_ref[0, pl.ds(e, 1), :]
        row = lax.broadcasted_iota(jnp.int32, (RC, tm), 0) + sk_ref[t * ns + j] * RC
        hit = (rrow == row) & (j < n)
        oh_scr[j] = jnp.where(hit, 1.0, 0.0).astype(BF16)
        gcol = jnp.sum(jnp.where(hit, grow, 0.0), axis=1, keepdims=True)
        gc_ref[pl.ds(pl.multiple_of(j * RC, RC), RC), :] = jnp.broadcast_to(gcol, (RC, LANES))
        return c

    lax.fori_loop(0, ns, build, 0, unroll=SPB)
    rows = SPB * RC
    for b in range(ns // SPB):
        oh = oh_scr[b * SPB:(b + 1) * SPB].reshape(rows, tm)
        xs_ref[b * rows:(b + 1) * rows, :] = jnp.dot(
            oh, h_ref[...], preferred_element_type=F32).astype(BF16)


def _compact(tabs, h2, rankt, gatest, *, tm, ns):
    T, D = h2.shape
    nt = T // tm
    grid_spec = pltpu.PrefetchScalarGridSpec(
        num_scalar_prefetch=3,
        grid=(nt,),
        in_specs=[
            pl.BlockSpec((tm, D), lambda t, *_: (t, 0)),
            pl.BlockSpec((1, LANES, tm), lambda t, *_: (t, 0, 0)),
            pl.BlockSpec((1, LANES, tm), lambda t, *_: (t, 0, 0)),
        ],
        out_specs=[
            pl.BlockSpec((ns * RC, D), lambda t, *_: (t, 0)),
            pl.BlockSpec((ns * RC, LANES), lambda t, *_: (t, 0)),
        ],
        scratch_shapes=[pltpu.VMEM((ns, RC, tm), BF16)],
    )
    return pl.pallas_call(
        functools.partial(_compact_kernel, tm=tm, ns=ns),
        grid_spec=grid_spec,
        out_shape=[jax.ShapeDtypeStruct((nt * ns * RC, D), BF16),
                   jax.ShapeDtypeStruct((nt * ns * RC, LANES), F32)],
        compiler_params=pltpu.CompilerParams(
            dimension_semantics=("arbitrary",), vmem_limit_bytes=VMEM_LIMIT),
    )(tabs["slot_e"], tabs["slot_k"], tabs["nct"], h2, rankt, gatest)


def _slot_experts_kernel(vexp_ref, nv_ref, pos_ref, real_ref, *refs):
    xs_refs = refs[0:VT]
    gc_refs = refs[VT:2 * VT]
    wg_ref, wu_ref, wd_ref, ys_ref, x_scr, g_scr, acc_ref = refs[2 * VT:]
    v = pl.program_id(0)
    f = pl.program_id(1)

    @pl.when(v < nv_ref[0])
    def _():
        @pl.when(f == 0)
        def _():
            for j in range(VT):
                keep = real_ref[v * VT + j] == 1
                x_scr[j * RC:(j + 1) * RC, :] = jnp.where(keep, xs_refs[j][...], jnp.zeros_like(xs_refs[j]))
                g_scr[j * RC:(j + 1) * RC, :] = jnp.where(keep, gc_refs[j][...], 0.0)
            acc_ref[...] = jnp.zeros_like(acc_ref)

        a = _swiglu_tile(x_scr[...], wg_ref[0], wu_ref[0]) * g_scr[:, 0:1]
        acc_ref[...] += jnp.dot(a.astype(BF16), wd_ref[0], preferred_element_type=F32)

        @pl.when(f == pl.num_programs(1) - 1)
        def _():
            ys_ref[...] = acc_ref[...].astype(BF16)


def _slot_experts(tabs, xs, gc, wg, wu, wd, *, tf, n_vt):
    D = xs.shape[1]
    NF = wd.shape[1] // tf
    rows = VT * RC

    def live(v, nv):
        return jnp.minimum(v, nv[0] - 1)

    def hidden(v, f, nv):
        return jnp.where(v < nv[0], f, NF - 1)

    def chunk(j):
        return lambda v, f, vexp, nv, pos, real: (pos[live(v, nv) * VT + j], 0)

    grid_spec = pltpu.PrefetchScalarGridSpec(
        num_scalar_prefetch=4,
        grid=(n_vt, NF),
        in_specs=(
            [pl.BlockSpec((RC, D), chunk(j)) for j in range(VT)]
            + [pl.BlockSpec((RC, LANES), chunk(j)) for j in range(VT)]
            + [pl.BlockSpec((1, D, tf),
                            lambda v, f, vexp, nv, pos, real: (vexp[v], 0, hidden(v, f, nv)))] * 2
            + [pl.BlockSpec((1, tf, D),
                            lambda v, f, vexp, nv, pos, real: (vexp[v], hidden(v, f, nv), 0))]),
        out_specs=pl.BlockSpec((rows, D), lambda v, f, vexp, nv, pos, real: (live(v, nv), 0)),
        scratch_shapes=[pltpu.VMEM((rows, D), BF16), pltpu.VMEM((rows, LANES), F32),
                        pltpu.VMEM((rows, D), F32)],
    )
    return pl.pallas_call(
        _slot_experts_kernel,
        grid_spec=grid_spec,
        out_shape=jax.ShapeDtypeStruct((n_vt * rows, D), BF16),
        compiler_params=pltpu.CompilerParams(
            dimension_semantics=("arbitrary", "arbitrary"), vmem_limit_bytes=VMEM_LIMIT),
    )(tabs["v_exp"], tabs["nv"], tabs["pos"], tabs["real"], *([xs] * VT), *([gc] * VT), wg, wu, wd)


def _uncompact_kernel(se_ref, sk_ref, nct_ref, cid_ref, *refs, tm, ns, final_norm):
    ys_refs = refs[0:ns]
    rank_ref, x_ref, mod_ref, fin_ref, o_ref, oh_scr = refs[ns:]
    t = pl.program_id(0)
    n = nct_ref[t]
    lane = lax.broadcasted_iota(jnp.int32, (tm, LANES), 1)
    right = lane >= RC
    col = jnp.where(right, lane - RC, lane)
    rank = rank_ref[...]

    def slot_rank(j):
        return jnp.sum(jnp.where(lane == se_ref[t * ns + j], rank, 0.0), axis=1,
                       keepdims=True).astype(jnp.int32)

    for jp in range(ns // 2):
        ja, jb = 2 * jp, 2 * jp + 1
        rcol = jnp.where(right, slot_rank(jb), slot_rank(ja))
        want = col + jnp.where(right, sk_ref[t * ns + jb], sk_ref[t * ns + ja]) * RC
        used = jnp.where(right, jb, ja) < n
        oh_scr[:, jp * LANES:(jp + 1) * LANES] = jnp.where(
            (rcol == want) & used, 1.0, 0.0).astype(BF16)
    y = None
    rows = SPB * RC
    for b in range(ns // SPB):
        ys = jnp.concatenate([ys_refs[b * SPB + j][...] for j in range(SPB)], axis=0)
        part = jnp.dot(oh_scr[:, b * rows:(b + 1) * rows], ys, preferred_element_type=F32)
        y = part if y is None else y + part
    x2 = x_ref[...] + mod_ref[0, 5:6, :] * y
    if final_norm:
        x2 = _rms(x2) * fin_ref[...]
    o_ref[...] = x2


def _uncompact(tabs, ys, rank, x1, mod_l, fin_g, *, tm, ns, seq, final_norm):
    T, D = x1.shape
    tpb = seq // tm

    def chunk(j):
        return lambda t, se, sk, nct, cid: (cid[t * ns + j], 0)

    grid_spec = pltpu.PrefetchScalarGridSpec(
        num_scalar_prefetch=4,
        grid=(T // tm,),
        in_specs=(
            [pl.BlockSpec((RC, D), chunk(j)) for j in range(ns)]
            + [pl.BlockSpec((tm, LANES), lambda t, *_: (t, 0)),
               pl.BlockSpec((tm, D), lambda t, *_: (t, 0)),
               pl.BlockSpec((1, 6, D), lambda t, *_: (t // tpb, 0, 0)),
               pl.BlockSpec((1, D), lambda t, *_: (0, 0))]),
        out_specs=pl.BlockSpec((tm, D), lambda t, *_: (t, 0)),
        scratch_shapes=[pltpu.VMEM((tm, ns * RC), BF16)],
    )
    return pl.pallas_call(
        functools.partial(_uncompact_kernel, tm=tm, ns=ns, final_norm=final_norm),
        grid_spec=grid_spec,
        out_shape=jax.ShapeDtypeStruct((T, D), F32),
        compiler_params=pltpu.CompilerParams(
            dimension_semantics=("arbitrary",), vmem_limit_bytes=VMEM_LIMIT),
    )(tabs["slot_e"], tabs["slot_k"], tabs["nct"], tabs["cid"], *([ys] * ns),
      rank, x1, mod_l, fin_g)


def _routed_experts(h2, gates, w_gate, w_up, w_down, x1, mod_l, fin_g, *, tm, tf, seq, final_norm):
    T, D = h2.shape
    E = N_EXPERTS
    nt = T // tm
    ns = 2 * tm // RC + E
    assert ns % SPB == 0 and 2 * RC == LANES
    n_vt = -(-(nt * ns + E * (VT - 1)) // VT)
    rank, rankt, gatest, cnt = _route(gates, tm=tm)
    tabs = _slot_tables(cnt[:, 0, :E].astype(jnp.int32), nt, ns, n_vt)
    xs, gc = _compact(tabs, h2, rankt, gatest, tm=tm, ns=ns)
    ys = _slot_experts(tabs, xs, gc, w_gate.astype(BF16), w_up.astype(BF16), w_down.astype(BF16),
                       tf=tf, n_vt=n_vt)
    return _uncompact(tabs, ys, rank, x1, mod_l, fin_g, tm=tm, ns=ns, seq=seq,
                      final_norm=final_norm)


def _tiles(S):
    ts = min(512, S)
    tq = min(512, S)
    tk = ts
    tm = min(1024, S)
    tf = D_FF // 2
    return ts, tq, tk, tm, tf


def kernel(x, c, positions, ada_w, ada_b, attn_norm_g, w_in, q_norm_g, w_uq, kv_norm_g, w_ukv,
           fox_forget_b, mla_out_g, fox_out_g, w_o, ffn_norm_g, dense_w_gate, dense_w_up,
           dense_w_down, router_w, moe_w_gate, moe_w_up, moe_w_down, final_norm_g):
    B, S, D = x.shape
    L = ada_w.shape[0]
    ts, tq, tk, tm, tf = _tiles(S)

    mod = _adaln(c, ada_w, ada_b).reshape(L, B, 6, D)
    ctab, stab = _rope_tables(positions)

    idx_in = _perm_w_in()
    idx_qa = _perm_w_uq()
    idx_kv = _perm_w_ukv()
    col_scale = np.ones((C_END,), np.float32)
    col_scale[C_FQ0:C_FQ0 + FOX_W] = FOX_HEAD_DIM ** -0.5
    tri = jnp.asarray(np.tril(np.ones((ts, ts), np.float32)), BF16)
    eqk = jnp.asarray(_aug_placement(), BF16)

    for l in range(L):
        win = (_take_cols(w_in[l], idx_in) * col_scale).astype(BF16)
        wuqa = _take_cols(w_uq[l], idx_qa).astype(BF16)
        wukv = _take_cols(w_ukv[l], idx_kv).astype(BF16)
        fb_row = jnp.zeros((1, LANES), F32).at[0, CTRL_LOGIT:CTRL_LOGIT + FOX_HEADS].set(
            fox_forget_b[l].astype(F32))
        qa, ka, vt = _inproj(
            x, mod[l], attn_norm_g[l].reshape(1, D), win, q_norm_g[l].reshape(1, -1),
            kv_norm_g[l].reshape(1, -1), wuqa, wukv, fb_row, ctab, stab, tri, eqk, ts=ts)
        ot = _attention(qa, ka, vt, tq=tq, tk=tk, hp=4)

        j = l // 2
        is_moe = (l % 2 == 1)
        rw = None
        if is_moe:
            rw = jnp.pad(router_w[j], ((0, 0), (0, LANES - N_EXPERTS))).astype(BF16)
        outs = _outproj(ot, x, mod[l], mla_out_g[l].reshape(1, -1), fox_out_g[l].reshape(1, -1),
                        w_o[l].astype(BF16), ffn_norm_g[l].reshape(1, D), rw, ts=ts)
        x1, h2 = outs[0].reshape(B * S, D), outs[1].reshape(B * S, D)
        fin_g = final_norm_g.reshape(1, D)
        last = l == L - 1
        if is_moe:
            x = _routed_experts(h2, outs[2].reshape(B * S, LANES), moe_w_gate[j], moe_w_up[j],
                                moe_w_down[j], x1, mod[l], fin_g,
                                tm=tm, tf=tf, seq=S, final_norm=last)
        else:
            x = _ffn(h2, dense_w_gate[j].astype(BF16), dense_w_up[j].astype(BF16),
                     dense_w_down[j].astype(BF16), x1, mod[l], fin_g,
                     tm=tm, tf=tf, seq=S, final_norm=last)
        x = x.reshape(B, S, D)
    return x
```

```python
import functools

import numpy as np
import jax
import jax.numpy as jnp
from jax import lax
from jax.experimental import pallas as pl
from jax.experimental.pallas import tpu as pltpu

F32 = jnp.float32
BF16 = jnp.bfloat16

D_MODEL = 1024
MLA_HEADS = 8
MLA_V = 64
MLA_NOPE = 64
MLA_ROPE = 32
MLA_Q_LORA = 256
MLA_KV_LORA = 128
FOX_HEADS = 8
FOX_HEAD_DIM = 64
FOX_W = FOX_HEADS * FOX_HEAD_DIM
ROPE_THETA = 10000.0
D_FF = 3584
N_EXPERTS = 8
EPS = 1e-6
N_HEADS = MLA_HEADS + FOX_HEADS
HALF_ROPE = MLA_ROPE // 2

LANES = 128
VMEM_LIMIT = 56 * 1024 * 1024

ROPE_LO = MLA_NOPE
AUG_LO = FOX_HEAD_DIM
AUG_W = 6
CTRL_LOGIT = 96
CTRL_ONE = 120

C_Q0 = 0
C_KV0 = C_Q0 + MLA_Q_LORA
C_KRA0 = C_KV0 + MLA_KV_LORA
C_FQ0 = C_KRA0 + LANES
C_FK0 = C_FQ0 + FOX_W
C_FV0 = C_FK0 + FOX_W
C_END = C_FV0 + FOX_W

NEG_BIG = -1e30
LOG2E = 1.4426950408889634
ONES_ROWS = 16


def _rms(v):
    return v * lax.rsqrt(jnp.mean(v * v, axis=-1, keepdims=True) + EPS)


def _sigmoid(v):
    return 1.0 / (1.0 + jnp.exp(-v))


def _split3(v):
    hi = v.astype(BF16).astype(F32)
    r = v - hi
    mid = r.astype(BF16).astype(F32)
    lo = (r - mid).astype(BF16).astype(F32)
    return hi, mid, lo


def _adaln_kernel(c_ref, w_ref, b_ref, o_ref):
    c = c_ref[...]
    ca = (c * _sigmoid(c)).astype(BF16)
    o_ref[0] = jnp.dot(ca, w_ref[0].astype(BF16), preferred_element_type=F32) + b_ref[0]


def _adaln(c, ada_w, ada_b):
    L, D, N = ada_w.shape
    B = c.shape[0]
    tn = 1536
    return pl.pallas_call(
        _adaln_kernel,
        grid=(L, N // tn),
        in_specs=[
            pl.BlockSpec((B, D), lambda l, j: (0, 0)),
            pl.BlockSpec((1, D, tn), lambda l, j: (l, 0, j)),
            pl.BlockSpec((1, 1, tn), lambda l, j: (l, 0, j)),
        ],
        out_specs=pl.BlockSpec((1, B, tn), lambda l, j: (l, 0, j)),
        out_shape=jax.ShapeDtypeStruct((L, B, N), F32),
        compiler_params=pltpu.CompilerParams(
            dimension_semantics=("arbitrary", "arbitrary"), vmem_limit_bytes=VMEM_LIMIT),
    )(c, ada_w, ada_b.reshape(L, 1, N))


def _rope_tab_kernel(pos_ref, freq_ref, cos_ref, sin_ref):
    ang = freq_ref[...] * pos_ref[0]
    cos_ref[0] = jnp.cos(ang)
    sin_ref[0] = jnp.sin(ang)


def _rope_tables(positions):
    B, S = positions.shape
    half = HALF_ROPE
    inv_freq = ROPE_THETA ** (-jnp.arange(half, dtype=F32) / half)
    pos = positions.astype(F32).reshape(B, 1, S)
    cos_t, sin_t = pl.pallas_call(
        _rope_tab_kernel,
        grid=(B,),
        in_specs=[
            pl.BlockSpec((1, 1, S), lambda b: (b, 0, 0)),
            pl.BlockSpec((half, 1), lambda b: (0, 0)),
        ],
        out_specs=[pl.BlockSpec((1, half, S), lambda b: (b, 0, 0))] * 2,
        out_shape=[jax.ShapeDtypeStruct((B, half, S), F32)] * 2,
        compiler_params=pltpu.CompilerParams(dimension_semantics=("arbitrary",)),
    )(pos, inv_freq.reshape(half, 1))
    cos = jnp.transpose(cos_t, (0, 2, 1))
    sin = jnp.transpose(sin_t, (0, 2, 1))
    ones = jnp.ones((B, S, MLA_NOPE), F32)
    z_lo = jnp.zeros((B, S, MLA_NOPE), F32)
    z_hi = jnp.zeros((B, S, LANES - MLA_NOPE - MLA_ROPE), F32)
    ctab = jnp.concatenate([ones, cos, cos, z_hi], axis=-1)
    stab = jnp.concatenate([z_lo, -sin, sin, z_hi], axis=-1)
    return ctab, stab


def _take_cols(w, idx):
    pieces, i, n = [], 0, len(idx)
    while i < n:
        j = i + 1
        if idx[i] < 0:
            while j < n and idx[j] < 0:
                j += 1
            pieces.append(jnp.zeros((w.shape[0], j - i), w.dtype))
        else:
            while j < n and idx[j] == idx[j - 1] + 1:
                j += 1
            pieces.append(w[:, int(idx[i]):int(idx[i]) + (j - i)])
        i = j
    return jnp.concatenate(pieces, axis=1)


def _perm_w_in():
    cq = 0
    ckv = cq + MLA_Q_LORA
    kr = ckv + MLA_KV_LORA
    fq = kr + MLA_ROPE
    fk = fq + FOX_W
    fv = fk + FOX_W
    fl = fv + FOX_W
    idx = -np.ones((C_END,), np.int64)
    idx[C_Q0:C_Q0 + MLA_Q_LORA] = cq + np.arange(MLA_Q_LORA)
    idx[C_KV0:C_KV0 + MLA_KV_LORA] = ckv + np.arange(MLA_KV_LORA)
    idx[C_KRA0 + ROPE_LO:C_KRA0 + ROPE_LO + MLA_ROPE] = kr + np.arange(MLA_ROPE)
    idx[C_KRA0 + CTRL_LOGIT:C_KRA0 + CTRL_LOGIT + FOX_HEADS] = fl + np.arange(FOX_HEADS)
    idx[C_FQ0:C_FQ0 + FOX_W] = fq + np.arange(FOX_W)
    idx[C_FK0:C_FK0 + FOX_W] = fk + np.arange(FOX_W)
    idx[C_FV0:C_FV0 + FOX_W] = fv + np.arange(FOX_W)
    return idx


def _perm_w_uq():
    per = MLA_NOPE + MLA_ROPE
    ia = -np.ones((MLA_HEADS * LANES,), np.int64)
    for h in range(MLA_HEADS):
        ia[h * LANES:h * LANES + per] = h * per + np.arange(per)
    return ia


def _perm_w_ukv():
    per = MLA_NOPE + MLA_V
    ik = -np.ones((MLA_HEADS * LANES,), np.int64)
    iv = np.zeros((MLA_HEADS * MLA_V,), np.int64)
    for h in range(MLA_HEADS):
        ik[h * LANES:h * LANES + MLA_NOPE] = h * per + np.arange(MLA_NOPE)
        iv[h * MLA_V:(h + 1) * MLA_V] = h * per + MLA_NOPE + np.arange(MLA_V)
    return np.concatenate([ik, iv])


def _aug_placement():
    e = np.zeros((LANES, 2 * LANES), np.float32)
    for h in range(FOX_HEADS):
        lo = AUG_LO + AUG_W * h
        for p in range(3):
            src = CTRL_LOGIT + 8 * p + h
            e[src, lo + p] = 1.0
            e[CTRL_ONE, lo + 3 + p] = -1.0
            e[CTRL_ONE, LANES + lo + p] = 1.0
            e[src, LANES + lo + 3 + p] = 1.0
    return e


def _inproj_kernel(x_ref, mod_ref, g_ref, win_ref, qg_ref, kvg_ref, wuqa_ref, wukv_ref,
                   fb_ref, ct_ref, st_ref, tri_ref, eqk_ref,
                   qa_ref, ka_ref, vt_ref, carry_ref, *, ts):
    si = pl.program_id(1)

    @pl.when(si == 0)
    def _():
        carry_ref[...] = jnp.zeros_like(carry_ref)

    x = x_ref[0]
    shift = mod_ref[0, 0:1, :]
    scale = mod_ref[0, 1:2, :]
    h = ((_rms(x) * g_ref[...]) * (1.0 + scale) + shift).astype(BF16)
    proj = jnp.dot(h, win_ref[...], preferred_element_type=F32)

    ctab = ct_ref[0]
    stab = st_ref[0]
    lane = lax.broadcasted_iota(jnp.int32, (ts, LANES), 1)

    c_q = proj[:, C_Q0:C_Q0 + MLA_Q_LORA]
    cqn = (_rms(c_q) * qg_ref[...]).astype(BF16)
    qa = jnp.dot(cqn, wuqa_ref[...], preferred_element_type=F32)
    mla_scale = (MLA_NOPE + MLA_ROPE) ** -0.5 * LOG2E
    first_half = lane < ROPE_LO + HALF_ROPE

    def rope(blk):
        swapped = jnp.where(first_half, pltpu.roll(blk, LANES - HALF_ROPE, 1),
                            pltpu.roll(blk, HALF_ROPE, 1))
        return blk * ctab + swapped * stab

    for hh in range(MLA_HEADS):
        qa_ref[0, hh] = (rope(qa[:, hh * LANES:(hh + 1) * LANES]) * mla_scale).astype(BF16)

    c_kv = proj[:, C_KV0:C_KV0 + MLA_KV_LORA]
    ckvn = (_rms(c_kv) * kvg_ref[...]).astype(BF16)
    kv = jnp.dot(ckvn, wukv_ref[...], preferred_element_type=F32)
    kra = proj[:, C_KRA0:C_KRA0 + LANES]
    krope = rope(kra)
    for hh in range(MLA_HEADS):
        ka_ref[0, hh] = (kv[:, hh * LANES:(hh + 1) * LANES] + krope).astype(BF16)
    v_mla = kv[:, MLA_HEADS * LANES:]

    ctrl = (lane >= CTRL_LOGIT) & (lane < CTRL_LOGIT + FOX_HEADS)
    fl = kra + fb_ref[...]
    lsig = jnp.minimum(fl, 0.0) - jnp.log1p(jnp.exp(-jnp.abs(fl)))
    lf = jnp.where(ctrl, lsig, 0.0)
    hi, mid, lo = _split3(lf)
    p1 = (hi + pltpu.roll(mid, 8, 1) + pltpu.roll(lo, 16, 1)).astype(BF16)
    cs = jnp.dot(tri_ref[...], p1, preferred_element_type=F32)
    cs = cs + pltpu.roll(cs, LANES - 8, 1) + pltpu.roll(cs, LANES - 16, 1)
    fcum = jnp.where(ctrl, cs, 0.0) + carry_ref[...]
    carry_ref[...] = fcum[ts - 1:ts, :]

    hi, mid, lo = _split3(fcum * LOG2E)
    p2 = hi + pltpu.roll(mid, 8, 1) + pltpu.roll(lo, 16, 1)
    p2 = jnp.where(lane == CTRL_ONE, 1.0, p2).astype(BF16)
    aug = jnp.dot(p2, eqk_ref[...], preferred_element_type=F32)
    aug_q = aug[:, :LANES]
    aug_k = aug[:, LANES:]

    fq = proj[:, C_FQ0:C_FQ0 + FOX_W]
    fk = proj[:, C_FK0:C_FK0 + FOX_W]
    low = lane < FOX_HEAD_DIM
    for j in range(FOX_HEADS // 2):
        sl = slice(j * LANES, (j + 1) * LANES)
        qblk = fq[:, sl] * LOG2E
        kblk = fk[:, sl]
        for h0, qb_, kb_ in ((2 * j, qblk, kblk),
                             (2 * j + 1, pltpu.roll(qblk, FOX_HEAD_DIM, 1),
                              pltpu.roll(kblk, FOX_HEAD_DIM, 1))):
            own = (lane >= AUG_LO + AUG_W * h0) & (lane < AUG_LO + AUG_W * (h0 + 1))
            qa_ref[0, MLA_HEADS + h0] = jnp.where(low, qb_, aug_q).astype(BF16)
            ka_ref[0, MLA_HEADS + h0] = jnp.where(low, kb_, jnp.where(own, aug_k, 0.0)).astype(BF16)

    fv = proj[:, C_FV0:C_FV0 + FOX_W]
    vt_ref[0, 0, 0:MLA_HEADS * MLA_V, :] = v_mla.T.astype(BF16)
    vt_ref[0, 0, MLA_HEADS * MLA_V:, :] = fv.T.astype(BF16)


def _inproj(x, mod_l, g, win, qg, kvg, wuqa, wukv, fb_row, ctab, stab, tri, eqk, *, ts):
    B, S, D = x.shape
    nst = S // ts
    const2 = lambda b, s: (0, 0)
    kern = functools.partial(_inproj_kernel, ts=ts)
    return pl.pallas_call(
        kern,
        grid=(B, nst),
        in_specs=[
            pl.BlockSpec((1, ts, D), lambda b, s: (b, s, 0)),
            pl.BlockSpec((1, 6, D), lambda b, s: (b, 0, 0)),
            pl.BlockSpec((1, D), const2),
            pl.BlockSpec(win.shape, const2),
            pl.BlockSpec(qg.shape, const2),
            pl.BlockSpec(kvg.shape, const2),
            pl.BlockSpec(wuqa.shape, const2),
            pl.BlockSpec(wukv.shape, const2),
            pl.BlockSpec(fb_row.shape, const2),
            pl.BlockSpec((1, ts, LANES), lambda b, s: (b, s, 0)),
            pl.BlockSpec((1, ts, LANES), lambda b, s: (b, s, 0)),
            pl.BlockSpec(tri.shape, const2),
            pl.BlockSpec(eqk.shape, const2),
        ],
        out_specs=[
            pl.BlockSpec((1, N_HEADS, ts, LANES), lambda b, s: (b, 0, s, 0)),
            pl.BlockSpec((1, N_HEADS, ts, LANES), lambda b, s: (b, 0, s, 0)),
            pl.BlockSpec((1, 1, N_HEADS * MLA_V, ts), lambda b, s: (b, s, 0, 0)),
        ],
        out_shape=[
            jax.ShapeDtypeStruct((B, N_HEADS, S, LANES), BF16),
            jax.ShapeDtypeStruct((B, N_HEADS, S, LANES), BF16),
            jax.ShapeDtypeStruct((B, nst, N_HEADS * MLA_V, ts), BF16),
        ],
        scratch_shapes=[pltpu.VMEM((1, LANES), F32)],
        compiler_params=pltpu.CompilerParams(
            dimension_semantics=("arbitrary", "arbitrary"), vmem_limit_bytes=VMEM_LIMIT),
    )(x, mod_l, g, win, qg, kvg, wuqa, wukv, fb_row, ctab, stab, tri, eqk)


def _colmax8(st, groups):
    tk, tq = st.shape
    v = st.reshape(groups, tk // (8 * groups), 8, tq)
    return jnp.max(jnp.max(v, axis=1), axis=0)


def _attn_kernel(q_ref, k_ref, vt_ref, o_ref, s_scr, tmax_scr, m_scr, acc_scr, *, tq, hp, nq):
    tk = tq
    half = tq // 2
    dv = MLA_V
    groups = 4
    ones = jnp.ones((ONES_ROWS, tk), BF16)
    nt_dims = (((1,), (1,)), ((), ()))

    def qk(qi, j):
        q0 = pl.multiple_of(qi * tq, tq)
        return [lax.dot_general(k_ref[0, h, pl.ds(pl.multiple_of(j * tk, tk), tk), :],
                                q_ref[0, h, pl.ds(q0, tq), :], nt_dims,
                                preferred_element_type=F32) for h in range(hp)]

    def park(sts):
        for h in range(hp):
            s_scr[h] = sts[h]
            tmax_scr[h] = _colmax8(sts[h], groups)

    def qk_diag(qi):
        q0 = pl.multiple_of(qi * tq, tq)
        q1 = pl.multiple_of(qi * tq + half, half)
        out = []
        for h in range(hp):
            top = lax.dot_general(k_ref[0, h, pl.ds(q0, half), :], q_ref[0, h, pl.ds(q0, tq), :],
                                  nt_dims, preferred_element_type=F32)
            bot = lax.dot_general(k_ref[0, h, pl.ds(q1, half), :], q_ref[0, h, pl.ds(q1, half), :],
                                  nt_dims, preferred_element_type=F32)
            out.append((top, bot))
        return out

    def park_diag(sts):
        causal = (lax.broadcasted_iota(jnp.int32, (half, half), 0)
                  <= lax.broadcasted_iota(jnp.int32, (half, half), 1))
        for h in range(hp):
            top, bot = sts[h]
            top_l = jnp.where(causal, top[:, :half], NEG_BIG)
            bot_r = jnp.where(causal, bot, NEG_BIG)
            s_scr[h, 0:half, 0:half] = top_l
            s_scr[h, 0:half, half:] = top[:, half:]
            s_scr[h, half:, half:] = bot_r
            tmax_scr[h, :, 0:half] = _colmax8(top_l, groups)
            tmax_scr[h, :, half:] = jnp.maximum(_colmax8(top[:, half:], groups),
                                                _colmax8(bot_r, groups))

    def update(j):
        for h in range(hp):
            m = m_scr[h]
            m_new = jnp.maximum(m, jnp.max(tmax_scr[h], axis=0, keepdims=True))
            alpha = jnp.exp2(m - m_new)
            p = jnp.exp2((s_scr[h] - m_new).astype(BF16))
            vt1 = jnp.concatenate([vt_ref[0, j, h * dv:(h + 1) * dv, :], ones], axis=0)
            acc_scr[h] = alpha * acc_scr[h] + jnp.dot(vt1, p, preferred_element_type=F32)
            m_scr[h] = m_new

    def update_diag(j):
        for h in range(hp):
            m = m_scr[h]
            m_new = jnp.maximum(m, jnp.max(tmax_scr[h], axis=0, keepdims=True))
            alpha = jnp.exp2(m - m_new)
            p_top = jnp.exp2((s_scr[h, 0:half, :] - m_new).astype(BF16))
            p_bot = jnp.exp2((s_scr[h, half:, half:] - m_new[:, half:]).astype(BF16))
            vt1 = jnp.concatenate([vt_ref[0, j, h * dv:(h + 1) * dv, :], ones], axis=0)
            acc = alpha * acc_scr[h] + jnp.dot(vt1[:, 0:half], p_top, preferred_element_type=F32)
            acc_scr[h, :, 0:half] = acc[:, 0:half]
            acc_scr[h, :, half:] = acc[:, half:] + jnp.dot(vt1[:, half:], p_bot,
                                                           preferred_element_type=F32)
            m_scr[h] = m_new

    def reset_state():
        m_scr[...] = jnp.full(m_scr.shape, NEG_BIG, F32)
        acc_scr[...] = jnp.zeros(acc_scr.shape, F32)

    def finish(qi):
        q0 = pl.multiple_of(qi * tq, tq)
        for h in range(hp):
            acc = acc_scr[h]
            o_ref[0, h * dv:(h + 1) * dv, pl.ds(q0, tq)] = (
                acc[0:dv, :] / acc[dv:dv + 1, :]).astype(o_ref.dtype)
        reset_state()

    reset_state()
    park_diag(qk_diag(0))

    def q_tile(qi, c):
        def below_diagonal(t):
            sts = qk(qi, t + 1)
            update(t)
            park(sts)

        def two_below(u, c2):
            below_diagonal(2 * u)
            below_diagonal(2 * u + 1)
            return c2

        lax.fori_loop(0, (qi - 1) // 2, two_below, 0)

        @pl.when((qi > 1) & ((qi - 1) % 2 == 1))
        def _():
            below_diagonal(qi - 2)

        @pl.when(qi > 0)
        def _():
            sts = qk_diag(qi)
            update(qi - 1)
            park_diag(sts)

        @pl.when(qi + 1 < nq)
        def _():
            sts = qk(qi + 1, 0)
            update_diag(qi)
            finish(qi)
            park(sts)

        @pl.when(qi + 1 == nq)
        def _():
            update_diag(qi)
            finish(qi)

        return c

    lax.fori_loop(0, nq, q_tile, 0)


def _attention(qa, ka, vt, *, tq, tk, hp):
    B, H, S, _ = qa.shape
    nkt = vt.shape[1]
    assert S % tq == 0 and nkt * tk == S and H % hp == 0 and tk == tq
    kern = functools.partial(_attn_kernel, tq=tq, hp=hp, nq=S // tq)
    return pl.pallas_call(
        kern,
        grid=(B, H // hp),
        in_specs=[
            pl.BlockSpec((1, hp, S, LANES), lambda b, g: (b, g, 0, 0)),
            pl.BlockSpec((1, hp, S, LANES), lambda b, g: (b, g, 0, 0)),
            pl.BlockSpec((1, nkt, hp * MLA_V, tk), lambda b, g: (b, 0, g, 0)),
        ],
        out_specs=pl.BlockSpec((1, hp * MLA_V, S), lambda b, g: (b, g, 0)),
        out_shape=jax.ShapeDtypeStruct((B, H * MLA_V, S), BF16),
        scratch_shapes=[
            pltpu.VMEM((hp, tk, tq), F32),
            pltpu.VMEM((hp, 8, tq), F32),
            pltpu.VMEM((hp, 1, tq), F32),
            pltpu.VMEM((hp, MLA_V + ONES_ROWS, tq), F32),
        ],
        compiler_params=pltpu.CompilerParams(
            dimension_semantics=("arbitrary", "arbitrary"), vmem_limit_bytes=VMEM_LIMIT),
    )(qa, ka, vt)


def _top2_gates(logits, lane):
    lg = jnp.where(lane < N_EXPERTS, logits, NEG_BIG)
    m1 = jnp.max(lg, axis=1, keepdims=True)
    i1 = jnp.min(jnp.where(lg == m1, lane, LANES), axis=1, keepdims=True)
    lg2 = jnp.where(lane == i1, NEG_BIG, lg)
    m2 = jnp.max(lg2, axis=1, keepdims=True)
    i2 = jnp.min(jnp.where(lg2 == m2, lane, LANES), axis=1, keepdims=True)
    e2 = jnp.exp(m2 - m1)
    den = 1.0 + e2
    return jnp.where(lane == i1, 1.0 / den, 0.0) + jnp.where(lane == i2, e2 / den, 0.0)


def _outproj_kernel(*refs, ts, with_router):
    if with_router:
        (ot_ref, x_ref, mod_ref, mg_ref, fg_ref, wo_ref, ng_ref, rw_ref,
         x1_ref, h2_ref, gates_ref) = refs
    else:
        ot_ref, x_ref, mod_ref, mg_ref, fg_ref, wo_ref, ng_ref, x1_ref, h2_ref = refs
    half = MLA_HEADS * MLA_V
    om = ot_ref[0, 0:half, :].astype(F32).T
    of = ot_ref[0, half:, :].astype(F32).T
    on = jnp.concatenate([_rms(om) * mg_ref[...], _rms(of) * fg_ref[...]], axis=1).astype(BF16)
    mix = jnp.dot(on, wo_ref[...], preferred_element_type=F32)
    x1 = x_ref[0] + mod_ref[0, 2:3, :] * mix
    x1_ref[0] = x1
    h2 = ((_rms(x1) * ng_ref[...]) * (1.0 + mod_ref[0, 4:5, :]) + mod_ref[0, 3:4, :]).astype(BF16)
    h2_ref[0] = h2
    if with_router:
        logits = jnp.dot(h2, rw_ref[...], preferred_element_type=F32)
        lane = lax.broadcasted_iota(jnp.int32, (ts, LANES), 1)
        gates_ref[0] = _top2_gates(logits, lane)


def _outproj(ot, x, mod_l, mg, fg, wo, ng, rw, *, ts):
    B, S, D = x.shape
    with_router = rw is not None
    const2 = lambda b, s: (0, 0)
    in_specs = [
        pl.BlockSpec((1, ot.shape[1], ts), lambda b, s: (b, 0, s)),
        pl.BlockSpec((1, ts, D), lambda b, s: (b, s, 0)),
        pl.BlockSpec((1, 6, D), lambda b, s: (b, 0, 0)),
        pl.BlockSpec(mg.shape, const2),
        pl.BlockSpec(fg.shape, const2),
        pl.BlockSpec(wo.shape, const2),
        pl.BlockSpec(ng.shape, const2),
    ]
    args = [ot, x, mod_l, mg, fg, wo, ng]
    out_specs = [pl.BlockSpec((1, ts, D), lambda b, s: (b, s, 0)),
                 pl.BlockSpec((1, ts, D), lambda b, s: (b, s, 0))]
    out_shape = [jax.ShapeDtypeStruct((B, S, D), F32), jax.ShapeDtypeStruct((B, S, D), BF16)]
    if with_router:
        in_specs.append(pl.BlockSpec(rw.shape, const2))
        args.append(rw)
        out_specs.append(pl.BlockSpec((1, ts, LANES), lambda b, s: (b, s, 0)))
        out_shape.append(jax.ShapeDtypeStruct((B, S, LANES), F32))
    kern = functools.partial(_outproj_kernel, ts=ts, with_router=with_router)
    return pl.pallas_call(
        kern,
        grid=(B, S // ts),
        in_specs=in_specs,
        out_specs=out_specs,
        out_shape=out_shape,
        compiler_params=pltpu.CompilerParams(
            dimension_semantics=("arbitrary", "arbitrary"), vmem_limit_bytes=VMEM_LIMIT),
    )(*args)


def _swiglu_tile(x, wg, wu):
    g = jnp.dot(x, wg, preferred_element_type=F32)
    u = jnp.dot(x, wu, preferred_element_type=F32)
    return g * _sigmoid(g) * u


def _ffn_kernel(h_ref, wg_ref, wu_ref, wd_ref, x_ref, mod_ref, fin_ref, o_ref, acc_ref,
                *, final_norm):
    f = pl.program_id(1)

    @pl.when(f == 0)
    def _():
        acc_ref[...] = jnp.zeros_like(acc_ref)

    a = _swiglu_tile(h_ref[...], wg_ref[...], wu_ref[...])
    acc_ref[...] += jnp.dot(a.astype(BF16), wd_ref[...], preferred_element_type=F32)

    @pl.when(f == pl.num_programs(1) - 1)
    def _():
        x2 = x_ref[...] + mod_ref[0, 5:6, :] * acc_ref[...]
        if final_norm:
            x2 = _rms(x2) * fin_ref[...]
        o_ref[...] = x2


def _ffn(h2, wg, wu, wd, x1, mod_l, fin_g, *, tm, tf, seq, final_norm):
    T, D = h2.shape
    F = wd.shape[0]
    tiles_per_batch = seq // tm
    kern = functools.partial(_ffn_kernel, final_norm=final_norm)
    return pl.pallas_call(
        kern,
        grid=(T // tm, F // tf),
        in_specs=[
            pl.BlockSpec((tm, D), lambda i, f: (i, 0)),
            pl.BlockSpec((D, tf), lambda i, f: (0, f)),
            pl.BlockSpec((D, tf), lambda i, f: (0, f)),
            pl.BlockSpec((tf, D), lambda i, f: (f, 0)),
            pl.BlockSpec((tm, D), lambda i, f: (i, 0)),
            pl.BlockSpec((1, 6, D), lambda i, f: (i // tiles_per_batch, 0, 0)),
            pl.BlockSpec((1, D), lambda i, f: (0, 0)),
        ],
        out_specs=pl.BlockSpec((tm, D), lambda i, f: (i, 0)),
        out_shape=jax.ShapeDtypeStruct((T, D), F32),
        scratch_shapes=[pltpu.VMEM((tm, D), F32)],
        compiler_params=pltpu.CompilerParams(
            dimension_semantics=("arbitrary", "arbitrary"), vmem_limit_bytes=VMEM_LIMIT),
    )(h2, wg, wu, wd, x1, mod_l, fin_g)


RC = 64
VT = 16
SPB = 8


def _route_kernel(gates_ref, rank_ref, rankt_ref, gatest_ref, cnt_ref, *, tm):
    gates = gates_ref[...]
    sel = gates > 0.0
    self32 = jnp.where(sel, 1.0, 0.0)
    earlier = (lax.broadcasted_iota(jnp.int32, (tm, tm), 1)
               < lax.broadcasted_iota(jnp.int32, (tm, tm), 0))
    rank = jnp.dot(jnp.where(earlier, 1.0, 0.0).astype(BF16), self32.astype(BF16),
                   preferred_element_type=F32)
    rank = jnp.where(sel, rank, -1.0)
    rank_ref[...] = rank
    rankt_ref[0] = rank.T
    gatest_ref[0] = gates.T
    cnt_ref[0] = jnp.sum(self32, axis=0, keepdims=True)


def _route(gates, *, tm):
    T = gates.shape[0]
    nt = T // tm
    return pl.pallas_call(
        functools.partial(_route_kernel, tm=tm),
        grid=(nt,),
        in_specs=[pl.BlockSpec((tm, LANES), lambda i: (i, 0))],
        out_specs=[
            pl.BlockSpec((tm, LANES), lambda i: (i, 0)),
            pl.BlockSpec((1, LANES, tm), lambda i: (i, 0, 0)),
            pl.BlockSpec((1, LANES, tm), lambda i: (i, 0, 0)),
            pl.BlockSpec((1, 1, LANES), lambda i: (i, 0, 0)),
        ],
        out_shape=[
            jax.ShapeDtypeStruct((T, LANES), F32),
            jax.ShapeDtypeStruct((nt, LANES, tm), F32),
            jax.ShapeDtypeStruct((nt, LANES, tm), F32),
            jax.ShapeDtypeStruct((nt, 1, LANES), F32),
        ],
        compiler_params=pltpu.CompilerParams(
            dimension_semantics=("arbitrary",), vmem_limit_bytes=VMEM_LIMIT),
    )(gates)


def _slot_tables(cnt, nt, ns, n_vt):
    E = N_EXPERTS
    nch = (cnt + (RC - 1)) // RC
    nct = jnp.sum(nch, axis=1)
    ccum = jnp.cumsum(nch, axis=1)
    j = jnp.arange(ns, dtype=jnp.int32)
    slot_e = jnp.sum((ccum[:, None, :] <= j[None, :, None]).astype(jnp.int32), axis=2)
    slot_e = jnp.minimum(slot_e, E - 1)
    first = jnp.take_along_axis(ccum - nch, slot_e, axis=1)
    valid = j[None, :] < nct[:, None]
    slot_k = jnp.where(valid, j[None, :] - first, 0)
    nce = jnp.sum(nch, axis=0)
    padded = (nce + (VT - 1)) // VT * VT
    base = jnp.cumsum(padded) - padded
    cid0 = base[None, :] + jnp.cumsum(nch, axis=0) - nch
    cid = jnp.take_along_axis(cid0, slot_e, axis=1) + slot_k
    cid = jnp.where(valid, cid, cid[:, 0:1])
    n_chunks = n_vt * VT
    slot_id = jnp.arange(nt, dtype=jnp.int32)[:, None] * ns + j[None, :]
    pos = jnp.full((n_chunks,), -1, jnp.int32).at[
        jnp.where(valid, cid, n_chunks).reshape(-1)].set(slot_id.reshape(-1), mode="drop")
    vcum = jnp.cumsum(padded // VT)
    nv = vcum[-1]
    v = jnp.minimum(jnp.arange(n_vt, dtype=jnp.int32), nv - 1)
    v_exp = jnp.sum((vcum[None, :] <= v[:, None]).astype(jnp.int32), axis=1)
    v_exp = jnp.minimum(v_exp, E - 1)
    i32 = lambda a: a.reshape(-1).astype(jnp.int32)
    return dict(slot_e=i32(slot_e), slot_k=i32(slot_k), nct=i32(nct), cid=i32(cid),
                pos=jnp.maximum(pos, 0), real=(pos >= 0).astype(jnp.int32),
                v_exp=v_exp, nv=i32(nv))


def _compact_kernel(se_ref, sk_ref, nct_ref, h_ref, rankt_ref, gatest_ref, xs_ref, gc_ref, oh_scr,
                    *, tm, ns):
    t = pl.program_id(0)
    n = nct_ref[t]

    def build(j, c):
        e = se_ref[t * ns + j]
        rrow = rankt_ref[0, pl.ds(e, 1), :].astype(jnp.int32)
        grow = gatest_ref[0, pl.ds(e, 1), :]
        row = lax.broadcasted_iota(jnp.int32, (RC, tm), 0) + sk_ref[t * ns + j] * RC
        hit = (rrow == row) & (j < n)
        oh_scr[j] = jnp.where(hit, 1.0, 0.0).astype(BF16)
        gcol = jnp.sum(jnp.where(hit, grow, 0.0), axis=1, keepdims=True)
        gc_ref[pl.ds(pl.multiple_of(j * RC, RC), RC), :] = jnp.broadcast_to(gcol, (RC, LANES))
        return c

    lax.fori_loop(0, ns, build, 0, unroll=SPB)
    rows = SPB * RC
    for b in range(ns // SPB):
        oh = oh_scr[b * SPB:(b + 1) * SPB].reshape(rows, tm)
        xs_ref[b * rows:(b + 1) * rows, :] = jnp.dot(
            oh, h_ref[...], preferred_element_type=F32).astype(BF16)


def _compact(tabs, h2, rankt, gatest, *, tm, ns):
    T, D = h2.shape
    nt = T // tm
    grid_spec = pltpu.PrefetchScalarGridSpec(
        num_scalar_prefetch=3,
        grid=(nt,),
        in_specs=[
            pl.BlockSpec((tm, D), lambda t, *_: (t, 0)),
            pl.BlockSpec((1, LANES, tm), lambda t, *_: (t, 0, 0)),
            pl.BlockSpec((1, LANES, tm), lambda t, *_: (t, 0, 0)),
        ],
        out_specs=[
            pl.BlockSpec((ns * RC, D), lambda t, *_: (t, 0)),
            pl.BlockSpec((ns * RC, LANES), lambda t, *_: (t, 0)),
        ],
        scratch_shapes=[pltpu.VMEM((ns, RC, tm), BF16)],
    )
    return pl.pallas_call(
        functools.partial(_compact_kernel, tm=tm, ns=ns),
        grid_spec=grid_spec,
        out_shape=[jax.ShapeDtypeStruct((nt * ns * RC, D), BF16),
                   jax.ShapeDtypeStruct((nt * ns * RC, LANES), F32)],
        compiler_params=pltpu.CompilerParams(
            dimension_semantics=("arbitrary",), vmem_limit_bytes=VMEM_LIMIT),
    )(tabs["slot_e"], tabs["slot_k"], tabs["nct"], h2, rankt, gatest)


def _slot_experts_kernel(vexp_ref, nv_ref, pos_ref, real_ref, *refs):
    xs_refs = refs[0:VT]
    gc_refs = refs[VT:2 * VT]
    wg_ref, wu_ref, wd_ref, ys_ref, x_scr, g_scr, acc_ref = refs[2 * VT:]
    v = pl.program_id(0)
    f = pl.program_id(1)

    @pl.when(v < nv_ref[0])
    def _():
        @pl.when(f == 0)
        def _():
            for j in range(VT):
                keep = real_ref[v * VT + j] == 1
                x_scr[j * RC:(j + 1) * RC, :] = jnp.where(keep, xs_refs[j][...], jnp.zeros_like(xs_refs[j]))
                g_scr[j * RC:(j + 1) * RC, :] = jnp.where(keep, gc_refs[j][...], 0.0)
            acc_ref[...] = jnp.zeros_like(acc_ref)

        a = _swiglu_tile(x_scr[...], wg_ref[0], wu_ref[0]) * g_scr[:, 0:1]
        acc_ref[...] += jnp.dot(a.astype(BF16), wd_ref[0], preferred_element_type=F32)

        @pl.when(f == pl.num_programs(1) - 1)
        def _():
            ys_ref[...] = acc_ref[...].astype(BF16)

    @pl.when(v >= nv_ref[0])
    def _():
        ys_ref[...] = jnp.zeros_like(ys_ref)


def _slot_experts(tabs, xs, gc, wg, wu, wd, *, tf, n_vt):
    D = xs.shape[1]
    NF = wd.shape[1] // tf
    rows = VT * RC

    def live(v, nv):
        return jnp.minimum(v, nv[0] - 1)

    def hidden(v, f, nv):
        return jnp.where(v < nv[0], f, NF - 1)

    def chunk(j):
        return lambda v, f, vexp, nv, pos, real: (pos[live(v, nv) * VT + j], 0)

    grid_spec = pltpu.PrefetchScalarGridSpec(
        num_scalar_prefetch=4,
        grid=(n_vt, NF),
        in_specs=(
            [pl.BlockSpec((RC, D), chunk(j)) for j in range(VT)]
            + [pl.BlockSpec((RC, LANES), chunk(j)) for j in range(VT)]
            + [pl.BlockSpec((1, D, tf),
                            lambda v, f, vexp, nv, pos, real: (vexp[v], 0, hidden(v, f, nv)))] * 2
            + [pl.BlockSpec((1, tf, D),
                            lambda v, f, vexp, nv, pos, real: (vexp[v], hidden(v, f, nv), 0))]),
        out_specs=pl.BlockSpec((rows, D), lambda v, f, vexp, nv, pos, real: (v, 0)),
        scratch_shapes=[pltpu.VMEM((rows, D), BF16), pltpu.VMEM((rows, LANES), F32),
                        pltpu.VMEM((rows, D), F32)],
    )
    return pl.pallas_call(
        _slot_experts_kernel,
        grid_spec=grid_spec,
        out_shape=jax.ShapeDtypeStruct((n_vt * rows, D), BF16),
        compiler_params=pltpu.CompilerParams(
            dimension_semantics=("arbitrary", "arbitrary"), vmem_limit_bytes=VMEM_LIMIT),
    )(tabs["v_exp"], tabs["nv"], tabs["pos"], tabs["real"], *([xs] * VT), *([gc] * VT), wg, wu, wd)


def _uncompact_kernel(se_ref, sk_ref, nct_ref, cid_ref, *refs, tm, ns, final_norm):
    ys_refs = refs[0:ns]
    rank_ref, x_ref, mod_ref, fin_ref, o_ref, oh_scr = refs[ns:]
    t = pl.program_id(0)
    n = nct_ref[t]
    lane = lax.broadcasted_iota(jnp.int32, (tm, LANES), 1)
    right = lane >= RC
    col = jnp.where(right, lane - RC, lane)
    rank = rank_ref[...]

    def slot_rank(j):
        return jnp.sum(jnp.where(lane == se_ref[t * ns + j], rank, 0.0), axis=1,
                       keepdims=True).astype(jnp.int32)

    for jp in range(ns // 2):
        ja, jb = 2 * jp, 2 * jp + 1
        rcol = jnp.where(right, slot_rank(jb), slot_rank(ja))
        want = col + jnp.where(right, sk_ref[t * ns + jb], sk_ref[t * ns + ja]) * RC
        used = jnp.where(right, jb, ja) < n
        oh_scr[:, jp * LANES:(jp + 1) * LANES] = jnp.where(
            (rcol == want) & used, 1.0, 0.0).astype(BF16)
    y = None
    rows = SPB * RC
    for b in range(ns // SPB):
        ys = jnp.concatenate([ys_refs[b * SPB + j][...] for j in range(SPB)], axis=0)
        part = jnp.dot(oh_scr[:, b * rows:(b + 1) * rows], ys, preferred_element_type=F32)
        y = part if y is None else y + part
    x2 = x_ref[...] + mod_ref[0, 5:6, :] * y
    if final_norm:
        x2 = _rms(x2) * fin_ref[...]
    o_ref[...] = x2


def _uncompact(tabs, ys, rank, x1, mod_l, fin_g, *, tm, ns, seq, final_norm):
    T, D = x1.shape
    tpb = seq // tm

    def chunk(j):
        return lambda t, se, sk, nct, cid: (cid[t * ns + j], 0)

    grid_spec = pltpu.PrefetchScalarGridSpec(
        num_scalar_prefetch=4,
        grid=(T // tm,),
        in_specs=(
            [pl.BlockSpec((RC, D), chunk(j)) for j in range(ns)]
            + [pl.BlockSpec((tm, LANES), lambda t, *_: (t, 0)),
               pl.BlockSpec((tm, D), lambda t, *_: (t, 0)),
               pl.BlockSpec((1, 6, D), lambda t, *_: (t // tpb, 0, 0)),
               pl.BlockSpec((1, D), lambda t, *_: (0, 0))]),
        out_specs=pl.BlockSpec((tm, D), lambda t, *_: (t, 0)),
        scratch_shapes=[pltpu.VMEM((tm, ns * RC), BF16)],
    )
    return pl.pallas_call(
        functools.partial(_uncompact_kernel, tm=tm, ns=ns, final_norm=final_norm),
        grid_spec=grid_spec,
        out_shape=jax.ShapeDtypeStruct((T, D), F32),
        compiler_params=pltpu.CompilerParams(
            dimension_semantics=("arbitrary",), vmem_limit_bytes=VMEM_LIMIT),
    )(tabs["slot_e"], tabs["slot_k"], tabs["nct"], tabs["cid"], *([ys] * ns),
      rank, x1, mod_l, fin_g)


def _routed_experts(h2, gates, w_gate, w_up, w_down, x1, mod_l, fin_g, *, tm, tf, seq, final_norm):
    T, D = h2.shape
    E = N_EXPERTS
    nt = T // tm
    ns = 2 * tm // RC + E
    assert ns % SPB == 0 and 2 * RC == LANES
    n_vt = -(-(nt * ns + E * (VT - 1)) // VT)
    rank, rankt, gatest, cnt = _route(gates, tm=tm)
    tabs = _slot_tables(cnt[:, 0, :E].astype(jnp.int32), nt, ns, n_vt)
    xs, gc = _compact(tabs, h2, rankt, gatest, tm=tm, ns=ns)
    ys = _slot_experts(tabs, xs, gc, w_gate.astype(BF16), w_up.astype(BF16), w_down.astype(BF16),
                       tf=tf, n_vt=n_vt)
    return _uncompact(tabs, ys, rank, x1, mod_l, fin_g, tm=tm, ns=ns, seq=seq,
                      final_norm=final_norm)


def _tiles(S):
    ts = min(512, S)
    tq = min(512, S)
    tk = ts
    tm = min(1024, S)
    tf = D_FF // 2
    return ts, tq, tk, tm, tf


def kernel(x, c, positions, ada_w, ada_b, attn_norm_g, w_in, q_norm_g, w_uq, kv_norm_g, w_ukv,
           fox_forget_b, mla_out_g, fox_out_g, w_o, ffn_norm_g, dense_w_gate, dense_w_up,
           dense_w_down, router_w, moe_w_gate, moe_w_up, moe_w_down, final_norm_g):
    B, S, D = x.shape
    L = ada_w.shape[0]
    ts, tq, tk, tm, tf = _tiles(S)

    mod = _adaln(c, ada_w, ada_b).reshape(L, B, 6, D)
    ctab, stab = _rope_tables(positions)

    idx_in = _perm_w_in()
    idx_qa = _perm_w_uq()
    idx_kv = _perm_w_ukv()
    col_scale = np.ones((C_END,), np.float32)
    col_scale[C_FQ0:C_FQ0 + FOX_W] = FOX_HEAD_DIM ** -0.5
    tri = jnp.asarray(np.tril(np.ones((ts, ts), np.float32)), BF16)
    eqk = jnp.asarray(_aug_placement(), BF16)

    for l in range(L):
        win = (_take_cols(w_in[l], idx_in) * col_scale).astype(BF16)
        wuqa = _take_cols(w_uq[l], idx_qa).astype(BF16)
        wukv = _take_cols(w_ukv[l], idx_kv).astype(BF16)
        fb_row = jnp.zeros((1, LANES), F32).at[0, CTRL_LOGIT:CTRL_LOGIT + FOX_HEADS].set(
            fox_forget_b[l].astype(F32))
        qa, ka, vt = _inproj(
            x, mod[l], attn_norm_g[l].reshape(1, D), win, q_norm_g[l].reshape(1, -1),
            kv_norm_g[l].reshape(1, -1), wuqa, wukv, fb_row, ctab, stab, tri, eqk, ts=ts)
        ot = _attention(qa, ka, vt, tq=tq, tk=tk, hp=4)

        j = l // 2
        is_moe = (l % 2 == 1)
        rw = None
        if is_moe:
            rw = jnp.pad(router_w[j], ((0, 0), (0, LANES - N_EXPERTS))).astype(BF16)
        outs = _outproj(ot, x, mod[l], mla_out_g[l].reshape(1, -1), fox_out_g[l].reshape(1, -1),
                        w_o[l].astype(BF16), ffn_norm_g[l].reshape(1, D), rw, ts=ts)
        x1, h2 = outs[0].reshape(B * S, D), outs[1].reshape(B * S, D)
        fin_g = final_norm_g.reshape(1, D)
        last = l == L - 1
        if is_moe:
            x = _routed_experts(h2, outs[2].reshape(B * S, LANES), moe_w_gate[j], moe_w_up[j],
                                moe_w_down[j], x1, mod[l], fin_g,
                                tm=tm, tf=tf, seq=S, final_norm=last)
        else:
            x = _ffn(h2, dense_w_gate[j].astype(BF16), dense_w_up[j].astype(BF16),
                     dense_w_down[j].astype(BF16), x1, mod[l], fin_g,
                     tm=tm, tf=tf, seq=S, final_norm=last)
        x = x.reshape(B, S, D)
    return x
```
